```python
import math
import jax, jax.numpy as jnp
from jax import lax
import numpy as np

D_MODEL = 1024
BATCH = 8
SEQ = 2048
DEPTH = 1
DEC_BATCH = 128
DEC_SEQ = 4
PAST_LEN = 16384
PAGE_SIZE = 128

D_CONV = D_MODEL // 2
D_SSM = D_MODEL - D_CONV
D_MIX = D_CONV + D_SSM
CONV_WIDTH = 31
CONV_GROUP = 64
N_CONV_GROUPS = D_CONV // CONV_GROUP
SSM_GROUP = 16
N_SSM_GROUPS = D_SSM // SSM_GROUP
SSM_STATE = 64
N_MEM = 256
N_MEM_HEADS = 4
MEM_HEAD_DIM = D_MODEL // N_MEM_HEADS
D_FF = ((8 * D_MODEL + 3 * 256 - 1) // (3 * 256)) * 256
N_NORMS = 6
RMS_EPS = 1e-6
LN_EPS = 1e-5

kernel_name = "hymba_conformer_s5_memxattn_decode_step"


def rms_norm(x, g):
    xf = x.astype(jnp.float32)
    y = xf * lax.rsqrt(jnp.mean(xf * xf, axis=-1, keepdims=True) + RMS_EPS)
    return (y * g.astype(jnp.float32)).astype(x.dtype)


def layer_norm(x, g, b):
    xf = x.astype(jnp.float32)
    xc = xf - jnp.mean(xf, axis=-1, keepdims=True)
    var = jnp.mean(xc * xc, axis=-1, keepdims=True)
    y = xc * lax.rsqrt(var + LN_EPS) * g.astype(jnp.float32) + b.astype(jnp.float32)
    return y.astype(x.dtype)


def causal_depthwise_conv(v_ext, w_dw, b_dw):
    c = v_ext.shape[-1]
    out = lax.conv_general_dilated(
        v_ext, w_dw.astype(v_ext.dtype)[:, None, :], window_strides=(1,), padding="VALID",
        dimension_numbers=("NWC", "WIO", "NWC"), feature_group_count=c)
    return out + b_dw.astype(v_ext.dtype)


def _ssm_combine(left, right):
    a_l, b_l = left
    a_r, b_r = right
    return a_r * a_l, a_r * b_l + b_r


def s5_layer(u, h0, lam_re, lam_im, log_dt, b_re, b_im, c_re, c_im, d_skip, w_glu):
    f32 = jnp.float32
    bsz, t, _ = u.shape
    lam = lax.complex(lam_re.astype(f32), lam_im.astype(f32))
    dt = jnp.exp(log_dt.astype(f32))[:, None]
    a_bar = jnp.exp(lam * dt)
    b_mat = lax.complex(b_re.astype(f32), b_im.astype(f32))
    b_bar = ((a_bar - 1.0) / lam)[:, :, None] * b_mat
    c_mat = lax.complex(c_re.astype(f32), c_im.astype(f32))
    uf = u.astype(f32).reshape(bsz, t, N_SSM_GROUPS, SSM_GROUP)
    bu = jnp.einsum("btgp,gnp->btgn", uf.astype(jnp.complex64), b_bar)
    bu = bu.at[:, 0].add(a_bar * h0)
    a_seq = jnp.broadcast_to(a_bar, bu.shape)
    _, hs = lax.associative_scan(_ssm_combine, (a_seq, bu), axis=1)
    y = jnp.einsum("btgn,gpn->btgp", hs, c_mat).real
    y = y + d_skip.astype(f32).reshape(N_SSM_GROUPS, SSM_GROUP) * uf
    y = jax.nn.gelu(y.reshape(bsz, t, D_SSM))
    out = y * jax.nn.sigmoid(y @ w_glu.astype(f32))
    return out, hs[:, -1]


def token_mixer(h, conv_buf, ssm_h0, w_in, w_dw, b_dw, ln_g, ln_b, lam_re, lam_im, log_dt,
                b_re, b_im, c_re, c_im, d_skip, w_glu, w_out):
    z = h @ w_in
    a = z[..., :D_CONV]
    g = z[..., D_CONV:2 * D_CONV]
    u = z[..., 2 * D_CONV:]
    v = a * jax.nn.sigmoid(g)
    v_ext = jnp.concatenate([conv_buf.astype(v.dtype), v], axis=1)
    conv_new = v_ext[:, v_ext.shape[1] - (CONV_WIDTH - 1):]
    c = jax.nn.silu(layer_norm(causal_depthwise_conv(v_ext, w_dw, b_dw), ln_g, ln_b))
    s, h_last = s5_layer(u, ssm_h0, lam_re, lam_im, log_dt, b_re, b_im, c_re, c_im, d_skip, w_glu)
    mix = jnp.concatenate([c, s.astype(c.dtype)], axis=-1)
    return mix @ w_out, conv_new, h_last


def mem_kv(mem, g_mem, w_k, w_v):
    bsz = mem.shape[0]
    m = rms_norm(mem, g_mem)
    k = (m @ w_k).reshape(bsz, N_MEM, N_MEM_HEADS, MEM_HEAD_DIM)
    v = (m @ w_v).reshape(bsz, N_MEM, N_MEM_HEADS, MEM_HEAD_DIM)
    return k, v


def mem_attend(h, mem_k, mem_v, w_q, w_o):
    bsz, t, _ = h.shape
    q = (h @ w_q).reshape(bsz, t, N_MEM_HEADS, MEM_HEAD_DIM).astype(jnp.float32)
    s = jnp.einsum("bthd,bmhd->bhtm", q, mem_k.astype(jnp.float32)) * (MEM_HEAD_DIM ** -0.5)
    p = jax.nn.softmax(s, axis=-1)
    o = jnp.einsum("bhtm,bmhd->bthd", p, mem_v.astype(jnp.float32))
    return o.astype(h.dtype).reshape(bsz, t, D_MODEL) @ w_o


def swiglu(h, w_gate, w_up, w_down):
    return (jax.nn.silu(h @ w_gate) * (h @ w_up)) @ w_down


def block(x, mem_k, mem_v, conv_buf, ssm_h0, norm_g, mix_w, attn_w, ffn_w):
    h = rms_norm(x, norm_g[0])
    m, conv_new, h_last = token_mixer(h, conv_buf, ssm_h0, *mix_w)
    x = x + rms_norm(m, norm_g[1])
    h = rms_norm(x, norm_g[2])
    x = x + rms_norm(mem_attend(h, mem_k, mem_v, *attn_w), norm_g[3])
    h = rms_norm(x, norm_g[4])
    x = x + rms_norm(swiglu(h, *ffn_w), norm_g[5])
    return x, conv_new, h_last


def setup_inputs(seed: int = 0) -> dict:
    key = jax.random.key(seed)
    ks = jax.random.split(key, 40)
    f32 = jnp.float32
    nrm = lambda k, shape, s: s * jax.random.normal(k, shape, f32)
    L = DEPTH
    lam_im_base = math.pi * jnp.arange(SSM_STATE, dtype=f32)
    return {
        "x_prompt": nrm(ks[0], (BATCH, SEQ, D_MODEL), 1.0),
        "x_sample": nrm(ks[1], (DEC_BATCH, DEC_SEQ, D_MODEL), 1.0),
        "mem_prompt": nrm(ks[2], (BATCH, N_MEM, D_MODEL), 1.0),
        "cache_mem_k": nrm(ks[3], (L, DEC_BATCH, N_MEM, N_MEM_HEADS, MEM_HEAD_DIM), 1.0),
        "cache_mem_v": nrm(ks[4], (L, DEC_BATCH, N_MEM, N_MEM_HEADS, MEM_HEAD_DIM), 1.0),
        "state_conv": nrm(ks[5], (L, DEC_BATCH, CONV_WIDTH - 1, D_CONV), 0.5),
        "state_ssm_re": nrm(ks[6], (L, DEC_BATCH, N_SSM_GROUPS, SSM_STATE), 0.1),
        "state_ssm_im": nrm(ks[7], (L, DEC_BATCH, N_SSM_GROUPS, SSM_STATE), 0.1),
        "norm_g": 1.0 + nrm(ks[8], (L, N_NORMS, D_MODEL), 0.05),
        "mem_norm_g": 1.0 + nrm(ks[9], (L, D_MODEL), 0.05),
        "w_in": nrm(ks[10], (L, D_MODEL, 2 * D_CONV + D_SSM), D_MODEL ** -0.5),
        "w_dw": nrm(ks[11], (L, CONV_WIDTH, D_CONV), CONV_WIDTH ** -0.5),
        "b_dw": nrm(ks[12], (L, D_CONV), 0.01),
        "ln_g": 1.0 + nrm(ks[13], (L, D_CONV), 0.05),
        "ln_b": nrm(ks[14], (L, D_CONV), 0.01),
        "lam_re": -0.5 + nrm(ks[15], (L, N_SSM_GROUPS, SSM_STATE), 0.01),
        "lam_im": lam_im_base + nrm(ks[16], (L, N_SSM_GROUPS, SSM_STATE), 0.01),
        "log_dt": jax.random.uniform(ks[17], (L, N_SSM_GROUPS), f32, math.log(1e-3), math.log(1e-1)),
        "b_re": nrm(ks[18], (L, N_SSM_GROUPS, SSM_STATE, SSM_GROUP), (2 * SSM_GROUP) ** -0.5),
        "b_im": nrm(ks[19], (L, N_SSM_GROUPS, SSM_STATE, SSM_GROUP), (2 * SSM_GROUP) ** -0.5),
        "c_re": nrm(ks[20], (L, N_SSM_GROUPS, SSM_GROUP, SSM_STATE), (2 * SSM_STATE) ** -0.5),
        "c_im": nrm(ks[21], (L, N_SSM_GROUPS, SSM_GROUP, SSM_STATE), (2 * SSM_STATE) ** -0.5),
        "d_skip": nrm(ks[22], (L, D_SSM), 1.0),
        "w_glu": nrm(ks[23], (L, D_SSM, D_SSM), D_SSM ** -0.5),
        "w_out": nrm(ks[24], (L, D_MIX, D_MODEL), D_MIX ** -0.5),
        "w_q": nrm(ks[25], (L, D_MODEL, D_MODEL), D_MODEL ** -0.5),
        "w_k": nrm(ks[26], (L, D_MODEL, D_MODEL), D_MODEL ** -0.5),
        "w_v": nrm(ks[27], (L, D_MODEL, D_MODEL), D_MODEL ** -0.5),
        "w_o": nrm(ks[28], (L, D_MODEL, D_MODEL), D_MODEL ** -0.5),
        "w_gate": nrm(ks[29], (L, D_MODEL, D_FF), D_MODEL ** -0.5),
        "w_up": nrm(ks[30], (L, D_MODEL, D_FF), D_MODEL ** -0.5),
        "w_down": nrm(ks[31], (L, D_FF, D_MODEL), D_FF ** -0.5),
    }


def reference(x_prompt, x_sample, mem_prompt, cache_mem_k, cache_mem_v, state_conv,
              state_ssm_re, state_ssm_im, norm_g, mem_norm_g, w_in, w_dw, b_dw, ln_g, ln_b,
              lam_re, lam_im, log_dt, b_re, b_im, c_re, c_im, d_skip, w_glu, w_out,
              w_q, w_k, w_v, w_o, w_gate, w_up, w_down):
    f32 = jnp.float32
    yp, ys = x_prompt, x_sample
    kps, vps, cps, hps_re, hps_im, css, hss_re, hss_im = [], [], [], [], [], [], [], []
    for l in range(DEPTH):
        mix_w = (w_in[l], w_dw[l], b_dw[l], ln_g[l], ln_b[l], lam_re[l], lam_im[l], log_dt[l],
                 b_re[l], b_im[l], c_re[l], c_im[l], d_skip[l], w_glu[l], w_out[l])
        attn_w = (w_q[l], w_o[l])
        ffn_w = (w_gate[l], w_up[l], w_down[l])
        kp, vp = mem_kv(mem_prompt, mem_norm_g[l], w_k[l], w_v[l])
        conv0 = jnp.zeros((yp.shape[0], CONV_WIDTH - 1, D_CONV), yp.dtype)
        h0 = jnp.zeros((yp.shape[0], N_SSM_GROUPS, SSM_STATE), jnp.complex64)
        yp, cp, hp = block(yp, kp, vp, conv0, h0, norm_g[l], mix_w, attn_w, ffn_w)
        hs0 = lax.complex(state_ssm_re[l].astype(f32), state_ssm_im[l].astype(f32))
        ys, cs, hs = block(ys, cache_mem_k[l], cache_mem_v[l], state_conv[l], hs0,
                           norm_g[l], mix_w, attn_w, ffn_w)
        kps.append(kp)
        vps.append(vp)
        cps.append(cp)
        hps_re.append(hp.real)
        hps_im.append(hp.imag)
        css.append(cs)
        hss_re.append(hs.real)
        hss_im.append(hs.imag)
    return (yp, ys, jnp.stack(kps), jnp.stack(vps), jnp.stack(cps), jnp.stack(hps_re),
            jnp.stack(hps_im), jnp.stack(css), jnp.stack(hss_re), jnp.stack(hss_im))
```

```python
import functools
import math

import jax
import jax.numpy as jnp
from jax import lax
from jax.experimental import pallas as pl
from jax.experimental.pallas import tpu as pltpu

F32 = jnp.float32
BF16 = jnp.bfloat16

D_MODEL = 1024
D_CONV = 512
D_SSM = 512
CONV_WIDTH = 31
N_GROUPS = 32
P_GROUP = 16
N_STATE = 64
N_MEM = 256
N_HEADS = 4
HEAD_DIM = 256
D_FF = 2816
RMS_EPS = 1e-6
LN_EPS = 1e-5

LANES = 128
GROUPS_PER_BLOCK = LANES // P_GROUP
N_LANE_BLOCKS = D_SSM // LANES
STATE_HALF = GROUPS_PER_BLOCK * N_STATE
STATE_W = 2 * STATE_HALF
MXU_N = 256
VMEM_LIMIT = 56 * 1024 * 1024


def _params(n_axes, vmem=VMEM_LIMIT):
    return pltpu.CompilerParams(dimension_semantics=("arbitrary",) * n_axes, vmem_limit_bytes=vmem)


def _const(shape):
    nd = len(shape)
    return pl.BlockSpec(shape, lambda *_: (0,) * nd, pipeline_mode=pl.Buffered(1))


def _rms(x, g):
    return x * lax.rsqrt(jnp.mean(x * x, axis=-1, keepdims=True) + RMS_EPS) * g


def _bdot(a, b):
    return jnp.dot(a.astype(BF16), b.astype(BF16), preferred_element_type=F32)


def _inproj_kernel(x_ref, g_ref, w_ref, v_ref, u_ref, *, ns, rt):
    for s in range(ns):
        x = x_ref[:, s * D_MODEL:(s + 1) * D_MODEL]
        h = _rms(x, g_ref[...])
        z = jnp.dot(h.astype(BF16), w_ref[...], preferred_element_type=F32)
        a = z[:, :D_CONV]
        g = z[:, D_CONV:2 * D_CONV]
        u = z[:, 2 * D_CONV:]
        v_ref[s] = a * jax.nn.sigmoid(g)
        for j in range(N_LANE_BLOCKS):
            u_ref[j, :, s * LANES:(s + 1) * LANES] = u[:, j * LANES:(j + 1) * LANES].astype(BF16)


def _inproj(x2, g0, w_in, *, R, L, ns, rt):
    grid = (R // rt, L // ns)
    return pl.pallas_call(
        functools.partial(_inproj_kernel, ns=ns, rt=rt),
        grid=grid,
        in_specs=[
            pl.BlockSpec((rt, ns * D_MODEL), lambda i, s: (i, s)),
            _const((1, D_MODEL)),
            _const((D_MODEL, 2 * D_CONV + D_SSM)),
        ],
        out_specs=[
            pl.BlockSpec((ns, rt, D_CONV), lambda i, s: (s, i, 0)),
            pl.BlockSpec((N_LANE_BLOCKS, rt, ns * LANES), lambda i, s: (0, i, s)),
        ],
        out_shape=[
            jax.ShapeDtypeStruct((L, R, D_CONV), F32),
            jax.ShapeDtypeStruct((N_LANE_BLOCKS, R, L * LANES), BF16),
        ],
        compiler_params=_params(2),
        name="inproj",
    )(x2, g0, w_in)


def _ln_silu(acc, g, b):
    mu = jnp.mean(acc, axis=-1, keepdims=True)
    xc = acc - mu
    var = jnp.mean(xc * xc, axis=-1, keepdims=True)
    y = xc * lax.rsqrt(var + LN_EPS) * g + b
    return y * jax.nn.sigmoid(y)


def _conv_prompt_kernel(v_ref, w_ref, b_ref, g_ref, bb_ref, o_ref, v1_ref, v2_ref, *, L, C, rc):
    row = lax.broadcasted_iota(jnp.int32, (C, D_CONV), 0)
    for s in range(L):
        x = v_ref[s]
        v1_ref[s] = jnp.where(row >= 1, pltpu.roll(x, 1, 0), 0.0)
        v2_ref[s] = jnp.where(row >= 2, pltpu.roll(x, 2, 0), 0.0)
    srcs = (v_ref, v1_ref, v2_ref)

    def chunk(i, carry):
        r0 = pl.multiple_of(i * rc, rc)
        for s in range(L):
            acc = jnp.broadcast_to(b_ref[...], (rc, D_CONV))
            for d in range(CONV_WIDTH):
                blk = (s - d) % L
                shift = (d - s + L - 1) // L if d > s else 0
                k = CONV_WIDTH - 1 - d
                acc = acc + w_ref[k:k + 1, :] * srcs[shift][blk, pl.ds(r0, rc), :]
            o_ref[s, pl.ds(r0, rc), :] = _ln_silu(acc, g_ref[...], bb_ref[...]).astype(o_ref.dtype)
        return carry

    lax.fori_loop(0, C // rc, chunk, 0)


def _conv_prompt(v, w_dw, b_dw, ln_g, ln_b, *, R, L, C):
    rc = 32
    return pl.pallas_call(
        functools.partial(_conv_prompt_kernel, L=L, C=C, rc=rc),
        grid=(R // C,),
        in_specs=[
            pl.BlockSpec((L, C, D_CONV), lambda b: (0, b, 0)),
            _const((CONV_WIDTH, D_CONV)),
            _const((1, D_CONV)),
            _const((1, D_CONV)),
            _const((1, D_CONV)),
        ],
        out_specs=pl.BlockSpec((L, C, D_CONV), lambda b: (0, b, 0)),
        out_shape=jax.ShapeDtypeStruct((L, R, D_CONV), BF16),
        scratch_shapes=[pltpu.VMEM((L, C, D_CONV), F32), pltpu.VMEM((L, C, D_CONV), F32)],
        compiler_params=_params(1),
        name="conv_prompt",
    )(v, w_dw, b_dw, ln_g, ln_b)


def _conv_sample_kernel(v_ref, st_ref, w_ref, b_ref, g_ref, bb_ref, o_ref, new_ref, *, L, H):
    def ext(i):
        if i < H:
            return st_ref[:, i * D_CONV:(i + 1) * D_CONV]
        return v_ref[i - H]

    for t in range(L):
        acc = jnp.broadcast_to(b_ref[...], v_ref.shape[1:])
        for k in range(CONV_WIDTH):
            acc = acc + w_ref[k:k + 1, :] * ext(t + k)
        o_ref[t] = _ln_silu(acc, g_ref[...], bb_ref[...]).astype(o_ref.dtype)
    for i in range(H):
        new_ref[:, i * D_CONV:(i + 1) * D_CONV] = ext(i + L)


def _conv_sample(v, state2, w_dw, b_dw, ln_g, ln_b, *, R, L):
    H = CONV_WIDTH - 1
    return pl.pallas_call(
        functools.partial(_conv_sample_kernel, L=L, H=H),
        grid=(1,),
        in_specs=[
            _const((L, R, D_CONV)),
            _const((R, H * D_CONV)),
            _const((CONV_WIDTH, D_CONV)),
            _const((1, D_CONV)),
            _const((1, D_CONV)),
            _const((1, D_CONV)),
        ],
        out_specs=[
            pl.BlockSpec((L, R, D_CONV), lambda i: (0, 0, 0)),
            pl.BlockSpec((R, H * D_CONV), lambda i: (0, 0)),
        ],
        out_shape=[
            jax.ShapeDtypeStruct((L, R, D_CONV), BF16),
            jax.ShapeDtypeStruct((R, H * D_CONV), F32),
        ],
        compiler_params=_params(1),
        name="conv_sample",
    )(v, state2, w_dw, b_dw, ln_g, ln_b)


def _ssm_kernel(u_ref, w_ref, wb_ref, wc_ref, a_ref, d_ref, h0_ref, y_ref, hl_ref, s_scr, hp_scr,
                *, L, C, bt):
    nk = STATE_W // LANES
    half = nk // 2
    inc_all = jnp.dot(u_ref[...], wb_ref[...], preferred_element_type=F32)
    for k in range(nk):
        s_scr[k] = inc_all[:, k * LANES:(k + 1) * LANES]
    h0 = h0_ref[...]
    a = a_ref[...]

    def piece(x, k):
        return x[:, k * LANES:(k + 1) * LANES]

    ar = [piece(a[0:1], k) for k in range(half)]
    ai = [piece(a[1:2], k) for k in range(half)]

    def step(c, carry):
        rows = pl.ds(c, bt, stride=C) if C > 1 else pl.ds(0, bt)
        out = []
        for k in range(half):
            hr, hi = carry[k], carry[half + k]
            hp_scr[k, rows, :] = hr
            hp_scr[half + k, rows, :] = hi
            out.append(ar[k] * hr - ai[k] * hi + s_scr[k, rows, :])
        for k in range(half):
            hr, hi = carry[k], carry[half + k]
            out.append(ar[k] * hi + ai[k] * hr + s_scr[half + k, rows, :])
        return tuple(out)

    fin = lax.fori_loop(0, C, step, tuple(piece(h0, k) for k in range(nk)))
    for k in range(nk):
        hl_ref[:, k * LANES:(k + 1) * LANES] = fin[k]

    hp = jnp.concatenate([hp_scr[k] for k in range(nk)], axis=-1).astype(BF16)
    d = d_ref[...]
    per_tile = MXU_N // LANES
    for n in range(L * LANES // MXU_N):
        k = (n + 1) * MXU_N
        cols = slice(n * MXU_N, (n + 1) * MXU_N)
        y = jnp.dot(u_ref[:, :k], w_ref[:k, cols], preferred_element_type=F32)
        y = y + jnp.dot(hp, wc_ref[:, cols], preferred_element_type=F32)
        for h in range(per_tile):
            s = n * per_tile + h
            us = u_ref[:, s * LANES:(s + 1) * LANES].astype(F32)
            y_ref[s] = y[:, h * LANES:(h + 1) * LANES] + d * us


def _ssm(u4, w, wb, wc, a_pow, d_skip, h0, *, R, L, C, bt):
    rows = bt * C
    nb = R // rows
    return pl.pallas_call(
        functools.partial(_ssm_kernel, L=L, C=C, bt=bt),
        grid=(N_LANE_BLOCKS, nb),
        in_specs=[
            pl.BlockSpec((None, rows, L * LANES), lambda j, b: (j, b, 0)),
            pl.BlockSpec((None, L * LANES, L * LANES), lambda j, b: (j, 0, 0), pipeline_mode=pl.Buffered(1)),
            pl.BlockSpec((None, L * LANES, STATE_W), lambda j, b: (j, 0, 0), pipeline_mode=pl.Buffered(1)),
            pl.BlockSpec((None, STATE_W, L * LANES), lambda j, b: (j, 0, 0), pipeline_mode=pl.Buffered(1)),
            pl.BlockSpec((None, 2, STATE_HALF), lambda j, b: (j, 0, 0)),
            pl.BlockSpec((None, 1, LANES), lambda j, b: (j, 0, 0)),
            pl.BlockSpec((None, None, bt, STATE_W), lambda j, b: (j, b, 0, 0)),
        ],
        out_specs=[
            pl.BlockSpec((L, rows, LANES), lambda j, b: (0, b, j)),
            pl.BlockSpec((None, None, bt, STATE_W), lambda j, b: (j, b, 0, 0)),
        ],
        out_shape=[
            jax.ShapeDtypeStruct((L, R, D_SSM), F32),
            jax.ShapeDtypeStruct((N_LANE_BLOCKS, nb, bt, STATE_W), F32),
        ],
        scratch_shapes=[pltpu.VMEM((STATE_W // LANES, rows, LANES), F32),
                        pltpu.VMEM((STATE_W // LANES, rows, LANES), F32)],
        compiler_params=_params(2),
        name="ssm",
    )(u4, w, wb, wc, a_pow, d_skip, h0)


def _ssm_weights(lam_re, lam_im, log_dt, b_re, b_im, c_re, c_im, L):
    hp = lax.Precision.HIGHEST
    dt = jnp.exp(log_dt)[:, None]
    zr = lam_re * dt
    zi = lam_im * dt

    def apow(n):
        mag = jnp.exp(zr * n)
        return mag * jnp.cos(zi * n), mag * jnp.sin(zi * n)

    a1r, a1i = apow(1.0)
    den = lam_re * lam_re + lam_im * lam_im
    qr = ((a1r - 1.0) * lam_re + a1i * lam_im) / den
    qi = (a1i * lam_re - (a1r - 1.0) * lam_im) / den
    bbr = qr[:, :, None] * b_re - qi[:, :, None] * b_im
    bbi = qr[:, :, None] * b_im + qi[:, :, None] * b_re

    taus = jnp.arange(L, dtype=F32)
    pr = jnp.stack([apow(t)[0] for t in range(L + 1)], 0)
    pi = jnp.stack([apow(t)[1] for t in range(L + 1)], 0)
    del taus

    car = c_re[None] * pr[:L, :, None, :] - c_im[None] * pi[:L, :, None, :]
    cai = c_re[None] * pi[:L, :, None, :] + c_im[None] * pr[:L, :, None, :]
    taps = (jnp.einsum("tgqn,gnp->gtqp", car, bbr, precision=hp)
            - jnp.einsum("tgqn,gnp->gtqp", cai, bbi, precision=hp))

    eye = jnp.eye(GROUPS_PER_BLOCK, dtype=F32)
    nj = N_LANE_BLOCKS
    g8 = GROUPS_PER_BLOCK

    s_idx = jnp.arange(L)
    lag = s_idx[None, :] - s_idx[:, None]
    toe = taps[:, jnp.clip(lag, 0, L - 1)]
    toe = jnp.where((lag >= 0)[None, :, :, None, None], toe, 0.0)
    toe = toe.reshape(nj, g8, L, L, P_GROUP, P_GROUP)
    w = jnp.einsum("jgsSqp,gh->jsgpShq", toe, eye).reshape(nj, L * LANES, L * LANES)

    er = pr[L - 1 - s_idx][:, :, :, None] * bbr[None] - pi[L - 1 - s_idx][:, :, :, None] * bbi[None]
    ei = pr[L - 1 - s_idx][:, :, :, None] * bbi[None] + pi[L - 1 - s_idx][:, :, :, None] * bbr[None]
    e = jnp.stack([er, ei], 0).reshape(2, L, nj, g8, N_STATE, P_GROUP)
    wb = jnp.einsum("asjgnp,gh->jsgpahn", e, eye).reshape(nj, L * LANES, STATE_W)

    mr = c_re[None] * pr[1:, :, None, :] - c_im[None] * pi[1:, :, None, :]
    mi = c_re[None] * pi[1:, :, None, :] + c_im[None] * pr[1:, :, None, :]
    m = jnp.stack([mr, -mi], 0).reshape(2, L, nj, g8, P_GROUP, N_STATE)
    wc = jnp.einsum("aSjgqn,gh->jagnShq", m, eye).reshape(nj, STATE_W, L * LANES)

    a_pow = jnp.stack([pr[L].reshape(nj, STATE_HALF), pi[L].reshape(nj, STATE_HALF)], 1)
    return w.astype(BF16), wb.astype(BF16), wc.astype(BF16), a_pow


def _pack_state(re, im):
    b = re.shape[0]
    r = re.reshape(b, N_LANE_BLOCKS, STATE_HALF)
    i = im.reshape(b, N_LANE_BLOCKS, STATE_HALF)
    return jnp.transpose(jnp.concatenate([r, i], -1), (1, 0, 2))


def _unpack_state(h):
    b = h.shape[1]
    h = jnp.transpose(h, (1, 0, 2))
    re = h[:, :, :STATE_HALF].reshape(b, N_GROUPS, N_STATE)
    im = h[:, :, STATE_HALF:].reshape(b, N_GROUPS, N_STATE)
    return re, im


def _mixout_kernel(c_ref, y_ref, x_ref, wglu_ref, wout_ref, g_ref, o_ref, *, ns, rt):
    for s in range(ns):
        gy = jax.nn.gelu(y_ref[s])
        sg = gy * jax.nn.sigmoid(jnp.dot(gy.astype(BF16), wglu_ref[...], preferred_element_type=F32))
        m = jnp.dot(c_ref[s], wout_ref[0:D_CONV, :], preferred_element_type=F32)
        m = m + jnp.dot(sg.astype(BF16), wout_ref[D_CONV:, :], preferred_element_type=F32)
        x = x_ref[:, s * D_MODEL:(s + 1) * D_MODEL]
        o_ref[s] = x + _rms(m, g_ref[...])


def _mixout(c, y, x2, w_glu, w_out, g1, *, R, L, ns, rt):
    return pl.pallas_call(
        functools.partial(_mixout_kernel, ns=ns, rt=rt),
        grid=(L // ns, R // rt),
        in_specs=[
            pl.BlockSpec((ns, rt, D_CONV), lambda s, i: (s, i, 0)),
            pl.BlockSpec((ns, rt, D_SSM), lambda s, i: (s, i, 0)),
            pl.BlockSpec((rt, ns * D_MODEL), lambda s, i: (i, s)),
            _const((D_SSM, D_SSM)),
            _const((D_CONV + D_SSM, D_MODEL)),
            _const((1, D_MODEL)),
        ],
        out_specs=pl.BlockSpec((ns, rt, D_MODEL), lambda s, i: (s, i, 0)),
        out_shape=jax.ShapeDtypeStruct((L, R, D_MODEL), F32),
        compiler_params=_params(2),
        name="mixout",
    )(c, y, x2, w_glu, w_out, g1)


def _memkv_kernel(m_ref, g_ref, wk_ref, wv_ref, k_ref, v_ref):
    m = _rms(m_ref[...], g_ref[...]).astype(BF16)
    k_ref[...] = jnp.dot(m, wk_ref[...], preferred_element_type=F32)
    v_ref[...] = jnp.dot(m, wv_ref[...], preferred_element_type=F32)


def _memkv(mem2, g_mem, w_k, w_v):
    rows = mem2.shape[0]
    rt = 512
    return pl.pallas_call(
        _memkv_kernel,
        grid=(rows // rt,),
        in_specs=[
            pl.BlockSpec((rt, D_MODEL), lambda i: (i, 0)),
            _const((1, D_MODEL)),
            _const((D_MODEL, D_MODEL)),
            _const((D_MODEL, D_MODEL)),
        ],
        out_specs=[pl.BlockSpec((rt, D_MODEL), lambda i: (i, 0))] * 2,
        out_shape=[jax.ShapeDtypeStruct((rows, D_MODEL), F32)] * 2,
        compiler_params=_params(1),
        name="memkv",
    )(mem2, g_mem, w_k, w_v)


def _softmax_rows(s):
    s = s - jnp.max(s, axis=-1, keepdims=True)
    e = jnp.exp(s)
    return e / jnp.sum(e, axis=-1, keepdims=True)


def _attn_prompt_kernel(x_ref, k_ref, v_ref, wq_ref, wo_ref, gq_ref, go_ref, o_ref, *, ns, rt):
    rows = ns * rt
    x = x_ref[...].reshape(rows, D_MODEL)
    h = _rms(x, gq_ref[...])
    q = jnp.dot(h.astype(BF16), wq_ref[...], preferred_element_type=F32).astype(BF16)
    kb = k_ref[...].astype(BF16)
    vb = v_ref[...].astype(BF16)
    heads = []
    for hd in range(N_HEADS):
        sl = slice(hd * HEAD_DIM, (hd + 1) * HEAD_DIM)
        sc = lax.dot_general(q[:, sl], kb[:, sl], (((1,), (1,)), ((), ())), preferred_element_type=F32)
        p = _softmax_rows(sc * (HEAD_DIM ** -0.5))
        heads.append(jnp.dot(p.astype(BF16), vb[:, sl], preferred_element_type=F32))
    o = jnp.concatenate(heads, axis=-1)
    a = jnp.dot(o.astype(BF16), wo_ref[...], preferred_element_type=F32)
    o_ref[...] = (x + _rms(a, go_ref[...])).reshape(ns, rt, D_MODEL)


def _attn_prompt(x1, k, v, w_q, w_o, g2, g3, *, R, L, C):
    ns = 4
    return pl.pallas_call(
        functools.partial(_attn_prompt_kernel, ns=ns, rt=C),
        grid=(R // C, L // ns),
        in_specs=[
            pl.BlockSpec((ns, C, D_MODEL), lambda b, s: (s, b, 0)),
            pl.BlockSpec((None, N_MEM, D_MODEL), lambda b, s: (b, 0, 0)),
            pl.BlockSpec((None, N_MEM, D_MODEL), lambda b, s: (b, 0, 0)),
            _const((D_MODEL, D_MODEL)),
            _const((D_MODEL, D_MODEL)),
            _const((1, D_MODEL)),
            _const((1, D_MODEL)),
        ],
        out_specs=pl.BlockSpec((ns, C, D_MODEL), lambda b, s: (s, b, 0)),
        out_shape=jax.ShapeDtypeStruct((L, R, D_MODEL), F32),
        compiler_params=_params(2),
        name="attn_prompt",
    )(x1, k, v, w_q, w_o, g2, g3)


def _qproj_kernel(x_ref, wq_ref, g_ref, q_ref):
    h = _rms(x_ref[...], g_ref[...])
    q_ref[...] = jnp.dot(h.astype(BF16), wq_ref[...], preferred_element_type=F32)


def _qproj(x1f, w_q, g2):
    rows = x1f.shape[0]
    return pl.pallas_call(
        _qproj_kernel,
        grid=(1,),
        in_specs=[_const((rows, D_MODEL)), _const((D_MODEL, D_MODEL)), _const((1, D_MODEL))],
        out_specs=pl.BlockSpec((rows, D_MODEL), lambda i: (0, 0)),
        out_shape=jax.ShapeDtypeStruct((rows, D_MODEL), F32),
        compiler_params=_params(1),
        name="qproj_sample",
    )(x1f, w_q, g2)


def _attn_sample_kernel(q_ref, k_ref, v_ref, o_ref, *, L, bb):
    rows = L * bb
    q = q_ref[...].reshape(rows, D_MODEL).astype(BF16)
    owner = lax.broadcasted_iota(jnp.int32, (rows, HEAD_DIM), 0) % bb
    for hd in range(N_HEADS):
        sl = slice(hd * HEAD_DIM, (hd + 1) * HEAD_DIM)
        acc = jnp.zeros((rows, HEAD_DIM), F32)
        for b in range(bb):
            kb = k_ref[b, :, sl].astype(BF16)
            vb = v_ref[b, :, sl].astype(BF16)
            sc = lax.dot_general(q[:, sl], kb, (((1,), (1,)), ((), ())), preferred_element_type=F32)
            p = _softmax_rows(sc * (HEAD_DIM ** -0.5))
            o = jnp.dot(p.astype(BF16), vb, preferred_element_type=F32)
            acc = jnp.where(owner == b, o, acc)
        o_ref[:, :, sl] = acc.reshape(L, bb, HEAD_DIM)


def _attn_sample(q3, k3, v3, *, R, L):
    bb = 8
    return pl.pallas_call(
        functools.partial(_attn_sample_kernel, L=L, bb=bb),
        grid=(R // bb,),
        in_specs=[
            pl.BlockSpec((L, bb, D_MODEL), lambda i: (0, i, 0)),
            pl.BlockSpec((bb, N_MEM, D_MODEL), lambda i: (i, 0, 0)),
            pl.BlockSpec((bb, N_MEM, D_MODEL), lambda i: (i, 0, 0)),
        ],
        out_specs=pl.BlockSpec((L, bb, D_MODEL), lambda i: (0, i, 0)),
        out_shape=jax.ShapeDtypeStruct((L, R, D_MODEL), F32),
        compiler_params=_params(1),
        name="attn_sample",
    )(q3, k3, v3)


def _oproj_kernel(o_ref, x_ref, wo_ref, g_ref, y_ref):
    a = jnp.dot(o_ref[...].astype(BF16), wo_ref[...], preferred_element_type=F32)
    y_ref[...] = x_ref[...] + _rms(a, g_ref[...])


def _oproj(of, x1f, w_o, g3):
    rows = of.shape[0]
    return pl.pallas_call(
        _oproj_kernel,
        grid=(1,),
        in_specs=[_const((rows, D_MODEL)), _const((rows, D_MODEL)), _const((D_MODEL, D_MODEL)),
                  _const((1, D_MODEL))],
        out_specs=pl.BlockSpec((rows, D_MODEL), lambda i: (0, 0)),
        out_shape=jax.ShapeDtypeStruct((rows, D_MODEL), F32),
        compiler_params=_params(1),
        name="oproj_sample",
    )(of, x1f, w_o, g3)


def _ffn_kernel(x_ref, wg_ref, wu_ref, wd_ref, gi_ref, go_ref, o_ref, *, ns, rt):
    rows = ns * rt
    x = x_ref[...].reshape(rows, D_MODEL)
    h = _rms(x, gi_ref[...]).astype(BF16)
    gate = jnp.dot(h, wg_ref[...], preferred_element_type=F32)
    up = jnp.dot(h, wu_ref[...], preferred_element_type=F32)
    act = (gate * jax.nn.sigmoid(gate) * up).astype(BF16)
    dn = jnp.dot(act, wd_ref[...], preferred_element_type=F32)
    y = x + _rms(dn, go_ref[...])
    for s in range(ns):
        o_ref[:, s * D_MODEL:(s + 1) * D_MODEL] = y[s * rt:(s + 1) * rt]


def _ffn(x2, w_gate, w_up, w_down, g4, g5, *, R, L, ns, rt):
    return pl.pallas_call(
        functools.partial(_ffn_kernel, ns=ns, rt=rt),
        grid=(L // ns, R // rt),
        in_specs=[
            pl.BlockSpec((ns, rt, D_MODEL), lambda s, i: (s, i, 0)),
            _const((D_MODEL, D_FF)),
            _const((D_MODEL, D_FF)),
            _const((D_FF, D_MODEL)),
            _const((1, D_MODEL)),
            _const((1, D_MODEL)),
        ],
        out_specs=pl.BlockSpec((rt, ns * D_MODEL), lambda s, i: (i, s)),
        out_shape=jax.ShapeDtypeStruct((R, L * D_MODEL), F32),
        compiler_params=_params(2),
        name="ffn",
    )(x2, w_gate, w_up, w_down, g4, g5)


def _layer(x, mem_k, mem_v, conv_state, h0_packed, wts, *, C, L, ns, rt, bt, sample):
    bq, t, _ = x.shape
    R = bq * C
    x2 = x.reshape(R, L * D_MODEL)
    ng = wts["norm_g"]
    g = [ng[i:i + 1] for i in range(6)]

    v, u4 = _inproj(x2, g[0], wts["w_in"], R=R, L=L, ns=ns, rt=rt)
    if sample:
        cact, conv_new = _conv_sample(v, conv_state.reshape(R, -1), wts["w_dw"], wts["b_dw"], wts["ln_g"],
                                      wts["ln_b"], R=R, L=L)
        conv_new = conv_new.reshape(bq, CONV_WIDTH - 1, D_CONV)
    else:
        cact = _conv_prompt(v, wts["w_dw"], wts["b_dw"], wts["ln_g"], wts["ln_b"], R=R, L=L, C=C)
        tail = v.reshape(L, bq, C, D_CONV)[:, :, C - 2:, :]
        tail = jnp.transpose(tail, (1, 2, 0, 3)).reshape(bq, 2 * L, D_CONV)
        conv_new = tail[:, 2 * L - (CONV_WIDTH - 1):, :]

    sw = wts["ssm_L%d" % L]
    nb = R // (bt * C)
    y, hl = _ssm(u4, sw[0], sw[1], sw[2], sw[3], wts["d_skip"], h0_packed.reshape(N_LANE_BLOCKS, nb, bt, STATE_W),
                 R=R, L=L, C=C, bt=bt)
    hl = hl.reshape(N_LANE_BLOCKS, bq, STATE_W)

    x1 = _mixout(cact, y, x2, wts["w_glu"], wts["w_out"], g[1], R=R, L=L, ns=ns, rt=rt)

    if sample:
        q = _qproj(x1.reshape(L * R, D_MODEL), wts["w_q"], g[2])
        o = _attn_sample(q.reshape(L, R, D_MODEL), mem_k, mem_v, R=R, L=L)
        x2a = _oproj(o.reshape(L * R, D_MODEL), x1.reshape(L * R, D_MODEL), wts["w_o"], g[3])
        x2a = x2a.reshape(L, R, D_MODEL)
    else:
        x2a = _attn_prompt(x1, mem_k, mem_v, wts["w_q"], wts["w_o"], g[2], g[3], R=R, L=L, C=C)

    out = _ffn(x2a, wts["w_gate"], wts["w_up"], wts["w_down"], g[4], g[5], R=R, L=L, ns=ns, rt=rt)
    return out.reshape(bq, t, D_MODEL), conv_new, hl


PROMPT_L = 16


def kernel(x_prompt, x_sample, mem_prompt, cache_mem_k, cache_mem_v, state_conv, state_ssm_re, state_ssm_im,
           norm_g, mem_norm_g, w_in, w_dw, b_dw, ln_g, ln_b, lam_re, lam_im, log_dt, b_re, b_im, c_re, c_im,
           d_skip, w_glu, w_out, w_q, w_k, w_v, w_o, w_gate, w_up, w_down):
    depth = w_in.shape[0]
    bp, tp, _ = x_prompt.shape
    bs, ts, _ = x_sample.shape
    assert tp % PROMPT_L == 0 and tp >= CONV_WIDTH - 1

    yp, ys = x_prompt, x_sample
    outs = [[] for _ in range(8)]
    for l in range(depth):
        wts = {
            "norm_g": norm_g[l],
            "w_in": w_in[l].astype(BF16),
            "w_dw": w_dw[l],
            "b_dw": b_dw[l][None],
            "ln_g": ln_g[l][None],
            "ln_b": ln_b[l][None],
            "d_skip": d_skip[l].reshape(N_LANE_BLOCKS, 1, LANES),
            "w_glu": w_glu[l].astype(BF16),
            "w_out": w_out[l].astype(BF16),
            "w_q": w_q[l].astype(BF16),
            "w_o": w_o[l].astype(BF16),
            "w_gate": w_gate[l].astype(BF16),
            "w_up": w_up[l].astype(BF16),
            "w_down": w_down[l].astype(BF16),
        }
        ssm_args = (lam_re[l], lam_im[l], log_dt[l], b_re[l], b_im[l], c_re[l], c_im[l])
        wts["ssm_L%d" % PROMPT_L] = _ssm_weights(*ssm_args, PROMPT_L)
        wts["ssm_L%d" % ts] = _ssm_weights(*ssm_args, ts)

        kp, vp = _memkv(mem_prompt.reshape(bp * N_MEM, D_MODEL), mem_norm_g[l][None],
                        w_k[l].astype(BF16), w_v[l].astype(BF16))
        kp = kp.reshape(bp, N_MEM, D_MODEL)
        vp = vp.reshape(bp, N_MEM, D_MODEL)
        h0p = jnp.zeros((N_LANE_BLOCKS, bp, STATE_W), F32)
        yp, cp, hp = _layer(yp, kp, vp, None, h0p, wts, C=tp // PROMPT_L, L=PROMPT_L, ns=1, rt=512, bt=4,
                            sample=False)
        h0s = _pack_state(state_ssm_re[l], state_ssm_im[l])
        ys, cs, hs = _layer(ys, cache_mem_k[l].reshape(bs, N_MEM, D_MODEL), cache_mem_v[l].reshape(bs, N_MEM, D_MODEL),
                            state_conv[l], h0s, wts, C=1, L=ts, ns=ts, rt=bs, bt=bs, sample=True)

        hp_re, hp_im = _unpack_state(hp)
        hs_re, hs_im = _unpack_state(hs)
        for lst, val in zip(outs, (kp.reshape(bp, N_MEM, N_HEADS, HEAD_DIM), vp.reshape(bp, N_MEM, N_HEADS, HEAD_DIM),
                                   cp, hp_re, hp_im, cs, hs_re, hs_im)):
            lst.append(val)
    return (yp, ys) + tuple(jnp.stack(o) for o in outs)
```

```python
import functools

import jax
import jax.numpy as jnp
from jax import lax
from jax.experimental import pallas as pl
from jax.experimental.pallas import tpu as pltpu

F32 = jnp.float32
BF16 = jnp.bfloat16

D_MODEL = 1024
D_CONV = 512
D_SSM = 512
CONV_WIDTH = 31
N_GROUPS = 32
P_GROUP = 16
N_STATE = 64
N_MEM = 256
N_HEADS = 4
HEAD_DIM = 256
D_FF = 2816
RMS_EPS = 1e-6
LN_EPS = 1e-5

LANES = 128
GROUPS_PER_BLOCK = LANES // P_GROUP
N_LANE_BLOCKS = D_SSM // LANES
STATE_HALF = GROUPS_PER_BLOCK * N_STATE
STATE_W = 2 * STATE_HALF
MXU_N = 256
VMEM_LIMIT = 56 * 1024 * 1024


def _params(n_axes, vmem=VMEM_LIMIT):
    return pltpu.CompilerParams(dimension_semantics=("arbitrary",) * n_axes, vmem_limit_bytes=vmem)


def _const(shape):
    nd = len(shape)
    return pl.BlockSpec(shape, lambda *_: (0,) * nd, pipeline_mode=pl.Buffered(1))


def _rms(x, g):
    return x * lax.rsqrt(jnp.mean(x * x, axis=-1, keepdims=True) + RMS_EPS) * g


def _gather_rows(x_ref, L):
    return jnp.concatenate([x_ref[:, s, :] for s in range(L)], axis=0)


def _nat_spec(ct, L):
    return pl.BlockSpec((None, ct, L, D_MODEL), lambda b, i: (b, i, 0, 0))


def _tp_spec(ct, L, width, nc):
    return pl.BlockSpec((L, ct, width), lambda b, i: (0, b * nc + i, 0))


def _inproj_kernel(x_ref, g_ref, w_ref, v_ref, u_ref, *, L, ct):
    x = _gather_rows(x_ref, L)
    h = _rms(x, g_ref[...])
    z = jnp.dot(h.astype(BF16), w_ref[...], preferred_element_type=F32)
    a = z[:, :D_CONV]
    g = z[:, D_CONV:2 * D_CONV]
    u = z[:, 2 * D_CONV:].astype(BF16)
    v_ref[...] = (a * jax.nn.sigmoid(g)).reshape(L, ct, D_CONV)
    for s in range(L):
        for j in range(N_LANE_BLOCKS):
            u_ref[j, :, s * LANES:(s + 1) * LANES] = u[s * ct:(s + 1) * ct, j * LANES:(j + 1) * LANES]


def _inproj(x4, g0, w_in, *, L, ct):
    bq, c = x4.shape[:2]
    nc = c // ct
    R = bq * c
    return pl.pallas_call(
        functools.partial(_inproj_kernel, L=L, ct=ct),
        grid=(bq, nc),
        in_specs=[_nat_spec(ct, L), _const((1, D_MODEL)), _const((D_MODEL, 2 * D_CONV + D_SSM))],
        out_specs=[
            _tp_spec(ct, L, D_CONV, nc),
            pl.BlockSpec((N_LANE_BLOCKS, ct, L * LANES), lambda b, i: (0, b * nc + i, 0)),
        ],
        out_shape=[
            jax.ShapeDtypeStruct((L, R, D_CONV), F32),
            jax.ShapeDtypeStruct((N_LANE_BLOCKS, R, L * LANES), BF16),
        ],
        compiler_params=_params(2),
        name="inproj",
    )(x4, g0, w_in)


def _ln_silu(acc, g, b):
    mu = jnp.mean(acc, axis=-1, keepdims=True)
    xc = acc - mu
    var = jnp.mean(xc * xc, axis=-1, keepdims=True)
    y = xc * lax.rsqrt(var + LN_EPS) * g + b
    return y * jax.nn.sigmoid(y)


def _conv_prompt_kernel(v_ref, w_ref, b_ref, g_ref, bb_ref, o_ref, v1_ref, v2_ref, *, L, C, rc):
    row = lax.broadcasted_iota(jnp.int32, (C, D_CONV), 0)
    for s in range(L):
        x = v_ref[s]
        v1_ref[s] = jnp.where(row >= 1, pltpu.roll(x, 1, 0), 0.0)
        v2_ref[s] = jnp.where(row >= 2, pltpu.roll(x, 2, 0), 0.0)
    srcs = (v_ref, v1_ref, v2_ref)

    def chunk(i, carry):
        r0 = pl.multiple_of(i * rc, rc)
        for s in range(L):
            acc = jnp.broadcast_to(b_ref[...], (rc, D_CONV))
            for d in range(CONV_WIDTH):
                blk = (s - d) % L
                shift = (d - s + L - 1) // L if d > s else 0
                k = CONV_WIDTH - 1 - d
                acc = acc + w_ref[k:k + 1, :] * srcs[shift][blk, pl.ds(r0, rc), :]
            o_ref[s, pl.ds(r0, rc), :] = _ln_silu(acc, g_ref[...], bb_ref[...]).astype(o_ref.dtype)
        return carry

    lax.fori_loop(0, C // rc, chunk, 0)


def _conv_prompt(v, w_dw, b_dw, ln_g, ln_b, *, L, C):
    R = v.shape[1]
    rc = 32
    return pl.pallas_call(
        functools.partial(_conv_prompt_kernel, L=L, C=C, rc=rc),
        grid=(R // C,),
        in_specs=[
            pl.BlockSpec((L, C, D_CONV), lambda b: (0, b, 0)),
            _const((CONV_WIDTH, D_CONV)),
            _const((1, D_CONV)),
            _const((1, D_CONV)),
            _const((1, D_CONV)),
        ],
        out_specs=pl.BlockSpec((L, C, D_CONV), lambda b: (0, b, 0)),
        out_shape=jax.ShapeDtypeStruct((L, R, D_CONV), BF16),
        scratch_shapes=[pltpu.VMEM((L, C, D_CONV), F32), pltpu.VMEM((L, C, D_CONV), F32)],
        compiler_params=_params(1),
        name="conv_prompt",
    )(v, w_dw, b_dw, ln_g, ln_b)


def _conv_sample_kernel(v_ref, st_ref, w_ref, b_ref, g_ref, bb_ref, o_ref, new_ref, *, L, H):
    def ext(i):
        return st_ref[i] if i < H else v_ref[i - H]

    for t in range(L):
        acc = jnp.broadcast_to(b_ref[...], v_ref.shape[1:])
        for k in range(CONV_WIDTH):
            acc = acc + w_ref[k:k + 1, :] * ext(t + k)
        o_ref[t] = _ln_silu(acc, g_ref[...], bb_ref[...]).astype(o_ref.dtype)
    for i in range(H):
        new_ref[i] = ext(i + L)


def _conv_sample(v, state_t, w_dw, b_dw, ln_g, ln_b):
    L, R, _ = v.shape
    H = CONV_WIDTH - 1
    return pl.pallas_call(
        functools.partial(_conv_sample_kernel, L=L, H=H),
        grid=(1,),
        in_specs=[
            _const((L, R, D_CONV)),
            _const((H, R, D_CONV)),
            _const((CONV_WIDTH, D_CONV)),
            _const((1, D_CONV)),
            _const((1, D_CONV)),
            _const((1, D_CONV)),
        ],
        out_specs=[
            pl.BlockSpec((L, R, D_CONV), lambda i: (0, 0, 0)),
            pl.BlockSpec((H, R, D_CONV), lambda i: (0, 0, 0)),
        ],
        out_shape=[
            jax.ShapeDtypeStruct((L, R, D_CONV), BF16),
            jax.ShapeDtypeStruct((H, R, D_CONV), F32),
        ],
        compiler_params=_params(1),
        name="conv_sample",
    )(v, state_t, w_dw, b_dw, ln_g, ln_b)


def _expand_block_diag(d2):
    tiled = jnp.concatenate([d2] * (STATE_HALF // LANES), axis=-1)
    r = lax.broadcasted_iota(jnp.int32, tiled.shape, 0) // P_GROUP
    c = lax.broadcasted_iota(jnp.int32, tiled.shape, 1) // N_STATE
    return jnp.where(r == c, tiled, jnp.zeros_like(tiled))


def _ssm_kernel(u_ref, taps_ref, wbc_ref, wcc_ref, a_ref, d_ref, h0_ref, y_ref, hl_ref,
                w_scr, wb_scr, wct_scr, s_scr, hp_scr, *, L, C, bt):
    per_tile = MXU_N // LANES

    @pl.when(pl.program_id(1) == 0)
    def _():
        for sp in range(L):
            for s in range(L):
                rows = slice(s * LANES, (s + 1) * LANES)
                cols = slice(sp * LANES, (sp + 1) * LANES)
                if s <= sp:
                    w_scr[rows, cols] = taps_ref[sp - s]
                elif s // per_tile == sp // per_tile:
                    w_scr[rows, cols] = jnp.zeros((LANES, LANES), BF16)
        for s in range(L):
            rows = slice(s * LANES, (s + 1) * LANES)
            for a in range(2):
                cols = slice(a * STATE_HALF, (a + 1) * STATE_HALF)
                wb_scr[rows, cols] = _expand_block_diag(wbc_ref[s, a])
                wct_scr[rows, cols] = _expand_block_diag(wcc_ref[s, a])

    nk = STATE_W // LANES
    half = nk // 2
    inc_all = jnp.dot(u_ref[...], wb_scr[...], preferred_element_type=F32)
    for k in range(nk):
        s_scr[k] = inc_all[:, k * LANES:(k + 1) * LANES]
    h0 = h0_ref[...]
    a = a_ref[...]

    def piece(x, k):
        return x[:, k * LANES:(k + 1) * LANES]

    ar = [piece(a[0:1], k) for k in range(half)]
    ai = [piece(a[1:2], k) for k in range(half)]

    def step(c, carry):
        rows = pl.ds(c, bt, stride=C) if C > 1 else pl.ds(0, bt)
        out = []
        for k in range(half):
            hr, hi = carry[k], carry[half + k]
            hp_scr[k, rows, :] = hr
            hp_scr[half + k, rows, :] = hi
            out.append(ar[k] * hr - ai[k] * hi + s_scr[k, rows, :])
        for k in range(half):
            hr, hi = carry[k], carry[half + k]
            out.append(ar[k] * hi + ai[k] * hr + s_scr[half + k, rows, :])
        return tuple(out)

    fin = lax.fori_loop(0, C, step, tuple(piece(h0, k) for k in range(nk)))
    for k in range(nk):
        hl_ref[:, k * LANES:(k + 1) * LANES] = fin[k]

    hp = jnp.concatenate([hp_scr[k] for k in range(nk)], axis=-1).astype(BF16)
    d = d_ref[...]
    for n in range(L // per_tile):
        k = (n + 1) * MXU_N
        cols = slice(n * MXU_N, (n + 1) * MXU_N)
        y = jnp.dot(u_ref[:, :k], w_scr[:k, cols], preferred_element_type=F32)
        y = y + lax.dot_general(hp, wct_scr[cols, :], (((1,), (1,)), ((), ())), preferred_element_type=F32)
        for h in range(per_tile):
            s = n * per_tile + h
            us = u_ref[:, s * LANES:(s + 1) * LANES].astype(F32)
            y_ref[s] = y[:, h * LANES:(h + 1) * LANES] + d * us


def _ssm(u4, taps, wbc, wcc, a_pow, d_skip, h0, *, L, C, bt):
    R = u4.shape[1]
    rows = bt * C
    nb = R // rows
    lw = L * LANES
    return pl.pallas_call(
        functools.partial(_ssm_kernel, L=L, C=C, bt=bt),
        grid=(N_LANE_BLOCKS, nb),
        in_specs=[
            pl.BlockSpec((None, rows, lw), lambda j, b: (j, b, 0)),
            pl.BlockSpec((None, L, LANES, LANES), lambda j, b: (j, 0, 0, 0)),
            pl.BlockSpec((None, L, 2, LANES, LANES), lambda j, b: (j, 0, 0, 0, 0)),
            pl.BlockSpec((None, L, 2, LANES, LANES), lambda j, b: (j, 0, 0, 0, 0)),
            pl.BlockSpec((None, 2, STATE_HALF), lambda j, b: (j, 0, 0)),
            pl.BlockSpec((None, 1, LANES), lambda j, b: (j, 0, 0)),
            pl.BlockSpec((None, None, bt, STATE_W), lambda j, b: (j, b, 0, 0)),
        ],
        out_specs=[
            pl.BlockSpec((L, rows, LANES), lambda j, b: (0, b, j)),
            pl.BlockSpec((None, None, bt, STATE_W), lambda j, b: (j, b, 0, 0)),
        ],
        out_shape=[
            jax.ShapeDtypeStruct((L, R, D_SSM), F32),
            jax.ShapeDtypeStruct((N_LANE_BLOCKS, nb, bt, STATE_W), F32),
        ],
        scratch_shapes=[
            pltpu.VMEM((lw, lw), BF16),
            pltpu.VMEM((lw, STATE_W), BF16),
            pltpu.VMEM((lw, STATE_W), BF16),
            pltpu.VMEM((STATE_W // LANES, rows, LANES), F32),
            pltpu.VMEM((STATE_W // LANES, rows, LANES), F32),
        ],
        compiler_params=_params(2),
        name="ssm",
    )(u4, taps, wbc, wcc, a_pow, d_skip, h0)


def _ssm_weights(lam_re, lam_im, log_dt, b_re, b_im, c_re, c_im, L):
    hp = lax.Precision.HIGHEST
    dt = jnp.exp(log_dt)[:, None]
    zr = lam_re * dt
    zi = lam_im * dt
    n_pow = jnp.arange(L + 1, dtype=F32)[:, None, None]
    mag = jnp.exp(zr[None] * n_pow)
    pr = mag * jnp.cos(zi[None] * n_pow)
    pi = mag * jnp.sin(zi[None] * n_pow)
    a1r, a1i = pr[1], pi[1]
    den = lam_re * lam_re + lam_im * lam_im
    qr = ((a1r - 1.0) * lam_re + a1i * lam_im) / den
    qi = (a1i * lam_re - (a1r - 1.0) * lam_im) / den
    bbr = qr[:, :, None] * b_re - qi[:, :, None] * b_im
    bbi = qr[:, :, None] * b_im + qi[:, :, None] * b_re

    nj, g8 = N_LANE_BLOCKS, GROUPS_PER_BLOCK
    eye = jnp.eye(g8, dtype=F32)

    car = c_re[None] * pr[:L, :, None, :] - c_im[None] * pi[:L, :, None, :]
    cai = c_re[None] * pi[:L, :, None, :] + c_im[None] * pr[:L, :, None, :]
    taps = (jnp.einsum("tgqn,gnp->gtqp", car, bbr, precision=hp)
            - jnp.einsum("tgqn,gnp->gtqp", cai, bbi, precision=hp))
    taps = taps.reshape(nj, g8, L, P_GROUP, P_GROUP)
    taps = jnp.einsum("jgtqp,gh->jtgphq", taps, eye).reshape(nj, L, LANES, LANES)

    n_rev = (L - 1.0) - jnp.arange(L, dtype=F32)[:, None, None]
    mag_rev = jnp.exp(zr[None] * n_rev)
    rev = mag_rev * jnp.cos(zi[None] * n_rev), mag_rev * jnp.sin(zi[None] * n_rev)
    er = rev[0][:, :, :, None] * bbr[None] - rev[1][:, :, :, None] * bbi[None]
    ei = rev[0][:, :, :, None] * bbi[None] + rev[1][:, :, :, None] * bbr[None]
    e = jnp.stack([er, ei], 1).reshape(L, 2, nj, g8, N_STATE, P_GROUP)
    e = jnp.transpose(e, (2, 0, 1, 3, 5, 4)).reshape(nj, L, 2, LANES, N_STATE)
    wbc = jnp.concatenate([e, e], axis=-1)

    mr = c_re[None] * pr[1:, :, None, :] - c_im[None] * pi[1:, :, None, :]
    mi = c_re[None] * pi[1:, :, None, :] + c_im[None] * pr[1:, :, None, :]
    m = jnp.stack([mr, -mi], 1).reshape(L, 2, nj, g8, P_GROUP, N_STATE)
    m = jnp.transpose(m, (2, 0, 1, 3, 4, 5)).reshape(nj, L, 2, LANES, N_STATE)
    wcc = jnp.concatenate([m, m], axis=-1)

    a_pow = jnp.stack([pr[L].reshape(nj, STATE_HALF), pi[L].reshape(nj, STATE_HALF)], 1)
    return taps.astype(BF16), wbc.astype(BF16), wcc.astype(BF16), a_pow


def _pack_state(re, im):
    b = re.shape[0]
    r = re.reshape(b, N_LANE_BLOCKS, STATE_HALF)
    i = im.reshape(b, N_LANE_BLOCKS, STATE_HALF)
    return jnp.transpose(jnp.concatenate([r, i], -1), (1, 0, 2))


def _unpack_state(h):
    b = h.shape[1]
    h = jnp.transpose(h, (1, 0, 2))
    re = h[:, :, :STATE_HALF].reshape(b, N_GROUPS, N_STATE)
    im = h[:, :, STATE_HALF:].reshape(b, N_GROUPS, N_STATE)
    return re, im


def _mixout_kernel(c_ref, y_ref, x_ref, wglu_ref, wout_ref, g_ref, o_ref, *, L, ct):
    rows = L * ct
    gy = jax.nn.gelu(y_ref[...].reshape(rows, D_SSM))
    sg = gy * jax.nn.sigmoid(jnp.dot(gy.astype(BF16), wglu_ref[...], preferred_element_type=F32))
    m = jnp.dot(c_ref[...].reshape(rows, D_CONV), wout_ref[0:D_CONV, :], preferred_element_type=F32)
    m = m + jnp.dot(sg.astype(BF16), wout_ref[D_CONV:, :], preferred_element_type=F32)
    x = _gather_rows(x_ref, L)
    o_ref[...] = (x + _rms(m, g_ref[...])).reshape(L, ct, D_MODEL)


def _mixout(c, y, x4, w_glu, w_out, g1, *, L, ct):
    bq, cc = x4.shape[:2]
    nc = cc // ct
    R = bq * cc
    return pl.pallas_call(
        functools.partial(_mixout_kernel, L=L, ct=ct),
        grid=(bq, nc),
        in_specs=[
            _tp_spec(ct, L, D_CONV, nc),
            _tp_spec(ct, L, D_SSM, nc),
            _nat_spec(ct, L),
            _const((D_SSM, D_SSM)),
            _const((D_CONV + D_SSM, D_MODEL)),
            _const((1, D_MODEL)),
        ],
        out_specs=_tp_spec(ct, L, D_MODEL, nc),
        out_shape=jax.ShapeDtypeStruct((L, R, D_MODEL), F32),
        compiler_params=_params(2),
        name="mixout",
    )(c, y, x4, w_glu, w_out, g1)


def _memkv_kernel(m_ref, g_ref, wk_ref, wv_ref, k_ref, v_ref):
    m = _rms(m_ref[...], g_ref[...]).astype(BF16)
    k_ref[...] = jnp.dot(m, wk_ref[...], preferred_element_type=F32)
    v_ref[...] = jnp.dot(m, wv_ref[...], preferred_element_type=F32)


def _memkv(mem2, g_mem, w_k, w_v):
    rows = mem2.shape[0]
    rt = 512
    return pl.pallas_call(
        _memkv_kernel,
        grid=(rows // rt,),
        in_specs=[
            pl.BlockSpec((rt, D_MODEL), lambda i: (i, 0)),
            _const((1, D_MODEL)),
            _const((D_MODEL, D_MODEL)),
            _const((D_MODEL, D_MODEL)),
        ],
        out_specs=[pl.BlockSpec((rt, D_MODEL), lambda i: (i, 0))] * 2,
        out_shape=[jax.ShapeDtypeStruct((rows, D_MODEL), F32)] * 2,
        compiler_params=_params(1),
        name="memkv",
    )(mem2, g_mem, w_k, w_v)


def _softmax_rows(s):
    s = s - jnp.max(s, axis=-1, keepdims=True)
    e = jnp.exp(s)
    return e / jnp.sum(e, axis=-1, keepdims=True)


def _attn_prompt_kernel(x_ref, k_ref, v_ref, wq_ref, wo_ref, gq_ref, go_ref, o_ref, *, L, ct):
    rows = L * ct
    x = x_ref[...].reshape(rows, D_MODEL)
    h = _rms(x, gq_ref[...])
    q = jnp.dot(h.astype(BF16), wq_ref[...], preferred_element_type=F32).astype(BF16)
    kb = k_ref[...].astype(BF16)
    vb = v_ref[...].astype(BF16)
    heads = []
    for hd in range(N_HEADS):
        sl = slice(hd * HEAD_DIM, (hd + 1) * HEAD_DIM)
        sc = lax.dot_general(q[:, sl], kb[:, sl], (((1,), (1,)), ((), ())), preferred_element_type=F32)
        p = _softmax_rows(sc * (HEAD_DIM ** -0.5))
        heads.append(jnp.dot(p.astype(BF16), vb[:, sl], preferred_element_type=F32))
    o = jnp.concatenate(heads, axis=-1)
    a = jnp.dot(o.astype(BF16), wo_ref[...], preferred_element_type=F32)
    o_ref[...] = (x + _rms(a, go_ref[...])).reshape(L, ct, D_MODEL)


def _attn_prompt(x1, k, v, w_q, w_o, g2, g3, *, C, ct):
    L, R, _ = x1.shape
    nc = C // ct
    return pl.pallas_call(
        functools.partial(_attn_prompt_kernel, L=L, ct=ct),
        grid=(R // C, nc),
        in_specs=[
            _tp_spec(ct, L, D_MODEL, nc),
            pl.BlockSpec((None, N_MEM, D_MODEL), lambda b, i: (b, 0, 0)),
            pl.BlockSpec((None, N_MEM, D_MODEL), lambda b, i: (b, 0, 0)),
            _const((D_MODEL, D_MODEL)),
            _const((D_MODEL, D_MODEL)),
            _const((1, D_MODEL)),
            _const((1, D_MODEL)),
        ],
        out_specs=_tp_spec(ct, L, D_MODEL, nc),
        out_shape=jax.ShapeDtypeStruct((L, R, D_MODEL), F32),
        compiler_params=_params(2),
        name="attn_prompt",
    )(x1, k, v, w_q, w_o, g2, g3)


def _qproj_kernel(x_ref, wq_ref, g_ref, q_ref):
    h = _rms(x_ref[...], g_ref[...])
    q_ref[...] = jnp.dot(h.astype(BF16), wq_ref[...], preferred_element_type=F32)


def _qproj(x1f, w_q, g2):
    rows = x1f.shape[0]
    return pl.pallas_call(
        _qproj_kernel,
        grid=(1,),
        in_specs=[_const((rows, D_MODEL)), _const((D_MODEL, D_MODEL)), _const((1, D_MODEL))],
        out_specs=pl.BlockSpec((rows, D_MODEL), lambda i: (0, 0)),
        out_shape=jax.ShapeDtypeStruct((rows, D_MODEL), F32),
        compiler_params=_params(1),
        name="qproj_sample",
    )(x1f, w_q, g2)


def _attn_sample_kernel(q_ref, k_ref, v_ref, o_ref, *, L, bb):
    rows = L * bb
    q = q_ref[...].reshape(rows, D_MODEL).astype(BF16)
    owner = lax.broadcasted_iota(jnp.int32, (rows, HEAD_DIM), 0) % bb
    for hd in range(N_HEADS):
        sl = slice(hd * HEAD_DIM, (hd + 1) * HEAD_DIM)
        acc = jnp.zeros((rows, HEAD_DIM), F32)
        for b in range(bb):
            kb = k_ref[b, :, hd, :].astype(BF16)
            vb = v_ref[b, :, hd, :].astype(BF16)
            sc = lax.dot_general(q[:, sl], kb, (((1,), (1,)), ((), ())), preferred_element_type=F32)
            p = _softmax_rows(sc * (HEAD_DIM ** -0.5))
            o = jnp.dot(p.astype(BF16), vb, preferred_element_type=F32)
            acc = jnp.where(owner == b, o, acc)
        o_ref[:, :, sl] = acc.reshape(L, bb, HEAD_DIM)


def _attn_sample(q3, k5, v5):
    L, R, _ = q3.shape
    bb = 8
    kv_spec = pl.BlockSpec((None, bb, N_MEM, N_HEADS, HEAD_DIM), lambda i: (0, i, 0, 0, 0))
    return pl.pallas_call(
        functools.partial(_attn_sample_kernel, L=L, bb=bb),
        grid=(R // bb,),
        in_specs=[pl.BlockSpec((L, bb, D_MODEL), lambda i: (0, i, 0)), kv_spec, kv_spec],
        out_specs=pl.BlockSpec((L, bb, D_MODEL), lambda i: (0, i, 0)),
        out_shape=jax.ShapeDtypeStruct((L, R, D_MODEL), F32),
        compiler_params=_params(1),
        name="attn_sample",
    )(q3, k5, v5)


def _oproj_kernel(o_ref, x_ref, wo_ref, g_ref, y_ref):
    a = jnp.dot(o_ref[...].astype(BF16), wo_ref[...], preferred_element_type=F32)
    y_ref[...] = x_ref[...] + _rms(a, g_ref[...])


def _oproj(of, x1f, w_o, g3):
    rows = of.shape[0]
    return pl.pallas_call(
        _oproj_kernel,
        grid=(1,),
        in_specs=[_const((rows, D_MODEL)), _const((rows, D_MODEL)), _const((D_MODEL, D_MODEL)),
                  _const((1, D_MODEL))],
        out_specs=pl.BlockSpec((rows, D_MODEL), lambda i: (0, 0)),
        out_shape=jax.ShapeDtypeStruct((rows, D_MODEL), F32),
        compiler_params=_params(1),
        name="oproj_sample",
    )(of, x1f, w_o, g3)


def _ffn_kernel(x_ref, wg_ref, wu_ref, wd_ref, gi_ref, go_ref, o_ref, *, L, ct):
    rows = L * ct
    x = x_ref[...].reshape(rows, D_MODEL)
    h = _rms(x, gi_ref[...]).astype(BF16)
    gate = jnp.dot(h, wg_ref[...], preferred_element_type=F32)
    up = jnp.dot(h, wu_ref[...], preferred_element_type=F32)
    act = (gate * jax.nn.sigmoid(gate) * up).astype(BF16)
    dn = jnp.dot(act, wd_ref[...], preferred_element_type=F32)
    y = x + _rms(dn, go_ref[...])
    for s in range(L):
        o_ref[:, s, :] = y[s * ct:(s + 1) * ct]


def _ffn(x2, w_gate, w_up, w_down, g4, g5, *, bq, C, ct):
    L = x2.shape[0]
    nc = C // ct
    return pl.pallas_call(
        functools.partial(_ffn_kernel, L=L, ct=ct),
        grid=(bq, nc),
        in_specs=[
            _tp_spec(ct, L, D_MODEL, nc),
            _const((D_MODEL, D_FF)),
            _const((D_MODEL, D_FF)),
            _const((D_FF, D_MODEL)),
            _const((1, D_MODEL)),
            _const((1, D_MODEL)),
        ],
        out_specs=_nat_spec(ct, L),
        out_shape=jax.ShapeDtypeStruct((bq, C, L, D_MODEL), F32),
        compiler_params=_params(2),
        name="ffn",
    )(x2, w_gate, w_up, w_down, g4, g5)


def _layer(x4, mem_k, mem_v, conv_state_t, h0_packed, wts, *, ct, sc, bt, sample):
    bq, C, L, _ = x4.shape
    R = bq * C
    ng = wts["norm_g"]
    g = [ng[i:i + 1] for i in range(6)]

    v, u4 = _inproj(x4, g[0], wts["w_in"], L=L, ct=ct)
    if sample:
        cact, conv_new = _conv_sample(v, conv_state_t, wts["w_dw"], wts["b_dw"], wts["ln_g"], wts["ln_b"])
    else:
        cact = _conv_prompt(v, wts["w_dw"], wts["b_dw"], wts["ln_g"], wts["ln_b"], L=L, C=C)
        tail = v.reshape(L, bq, C, D_CONV)[:, :, C - 2:, :]
        tail = jnp.transpose(tail, (1, 2, 0, 3)).reshape(bq, 2 * L, D_CONV)
        conv_new = tail[:, 2 * L - (CONV_WIDTH - 1):, :]

    sw = wts["ssm_L%d" % L]
    nb = R // (bt * sc)
    y, hl = _ssm(u4, sw[0], sw[1], sw[2], sw[3], wts["d_skip"], h0_packed.reshape(N_LANE_BLOCKS, nb, bt, STATE_W),
                 L=L, C=sc, bt=bt)
    hl = hl.reshape(N_LANE_BLOCKS, nb * bt, STATE_W)

    x1 = _mixout(cact, y, x4, wts["w_glu"], wts["w_out"], g[1], L=L, ct=ct)

    if sample:
        q = _qproj(x1.reshape(L * R, D_MODEL), wts["w_q"], g[2])
        o = _attn_sample(q.reshape(L, R, D_MODEL), mem_k, mem_v)
        x2 = _oproj(o.reshape(L * R, D_MODEL), x1.reshape(L * R, D_MODEL), wts["w_o"], g[3])
        x2 = x2.reshape(L, R, D_MODEL)
    else:
        x2 = _attn_prompt(x1, mem_k, mem_v, wts["w_q"], wts["w_o"], g[2], g[3], C=C, ct=ct)

    out = _ffn(x2, wts["w_gate"], wts["w_up"], wts["w_down"], g[4], g[5], bq=bq, C=C, ct=ct)
    return out, conv_new, hl


PROMPT_L = 16
PROMPT_CT = 32


def kernel(x_prompt, x_sample, mem_prompt, cache_mem_k, cache_mem_v, state_conv, state_ssm_re, state_ssm_im,
           norm_g, mem_norm_g, w_in, w_dw, b_dw, ln_g, ln_b, lam_re, lam_im, log_dt, b_re, b_im, c_re, c_im,
           d_skip, w_glu, w_out, w_q, w_k, w_v, w_o, w_gate, w_up, w_down):
    depth = w_in.shape[0]
    bp, tp, _ = x_prompt.shape
    bs, ts, _ = x_sample.shape
    assert tp % (PROMPT_L * PROMPT_CT) == 0 and tp >= CONV_WIDTH - 1

    yp = x_prompt.reshape(bp, tp // PROMPT_L, PROMPT_L, D_MODEL)
    ys = x_sample.reshape(1, bs, ts, D_MODEL)
    outs = [[] for _ in range(8)]
    for l in range(depth):
        wts = {
            "norm_g": norm_g[l],
            "w_in": w_in[l].astype(BF16),
            "w_dw": w_dw[l],
            "b_dw": b_dw[l][None],
            "ln_g": ln_g[l][None],
            "ln_b": ln_b[l][None],
            "d_skip": d_skip[l].reshape(N_LANE_BLOCKS, 1, LANES),
            "w_glu": w_glu[l].astype(BF16),
            "w_out": w_out[l].astype(BF16),
            "w_q": w_q[l].astype(BF16),
            "w_o": w_o[l].astype(BF16),
            "w_gate": w_gate[l].astype(BF16),
            "w_up": w_up[l].astype(BF16),
            "w_down": w_down[l].astype(BF16),
        }
        ssm_args = (lam_re[l], lam_im[l], log_dt[l], b_re[l], b_im[l], c_re[l], c_im[l])
        wts["ssm_L%d" % PROMPT_L] = _ssm_weights(*ssm_args, PROMPT_L)
        wts["ssm_L%d" % ts] = _ssm_weights(*ssm_args, ts)

        kp, vp = _memkv(mem_prompt.reshape(bp * N_MEM, D_MODEL), mem_norm_g[l][None],
                        w_k[l].astype(BF16), w_v[l].astype(BF16))
        kp = kp.reshape(bp, N_MEM, D_MODEL)
        vp = vp.reshape(bp, N_MEM, D_MODEL)
        h0p = jnp.zeros((N_LANE_BLOCKS, bp, STATE_W), F32)
        yp, cp, hp = _layer(yp, kp, vp, None, h0p, wts, ct=PROMPT_CT, sc=tp // PROMPT_L, bt=4, sample=False)
        h0s = _pack_state(state_ssm_re[l], state_ssm_im[l])
        ys, cs, hs = _layer(ys, cache_mem_k[l:l + 1], cache_mem_v[l:l + 1], jnp.transpose(state_conv[l], (1, 0, 2)),
                            h0s, wts, ct=bs, sc=1, bt=bs, sample=True)
        cs = jnp.transpose(cs, (1, 0, 2))

        hp_re, hp_im = _unpack_state(hp)
        hs_re, hs_im = _unpack_state(hs)
        for lst, val in zip(outs, (kp.reshape(bp, N_MEM, N_HEADS, HEAD_DIM), vp.reshape(bp, N_MEM, N_HEADS, HEAD_DIM),
                                   cp, hp_re, hp_im, cs, hs_re, hs_im)):
            lst.append(val)
    return (yp.reshape(bp, tp, D_MODEL), ys.reshape(bs, ts, D_MODEL)) + tuple(jnp.stack(o) for o in outs)
```

```python
import functools

import jax
import jax.numpy as jnp
from jax import lax
from jax.experimental import pallas as pl
from jax.experimental.pallas import tpu as pltpu

F32 = jnp.float32
BF16 = jnp.bfloat16

D_MODEL = 1024
D_CONV = 512
D_SSM = 512
CONV_WIDTH = 31
N_GROUPS = 32
P_GROUP = 16
N_STATE = 64
N_MEM = 256
N_HEADS = 4
HEAD_DIM = 256
D_FF = 2816
RMS_EPS = 1e-6
LN_EPS = 1e-5

LANES = 128
GROUPS_PER_BLOCK = LANES // P_GROUP
N_LANE_BLOCKS = D_SSM // LANES
STATE_HALF = GROUPS_PER_BLOCK * N_STATE
STATE_W = 2 * STATE_HALF
MXU_N = 256
VMEM_LIMIT = 56 * 1024 * 1024


def _params(n_axes, vmem=VMEM_LIMIT):
    return pltpu.CompilerParams(dimension_semantics=("arbitrary",) * n_axes, vmem_limit_bytes=vmem)


def _const(shape):
    nd = len(shape)
    return pl.BlockSpec(shape, lambda *_: (0,) * nd, pipeline_mode=pl.Buffered(1))


def _rms(x, g):
    return x * lax.rsqrt(jnp.mean(x * x, axis=-1, keepdims=True) + RMS_EPS) * g


def _gather_rows(x_ref, L):
    return jnp.concatenate([x_ref[:, s, :] for s in range(L)], axis=0)


def _nat_spec(ct, L):
    return pl.BlockSpec((None, ct, L, D_MODEL), lambda b, i: (b, i, 0, 0))


def _tp_spec(ct, L, width, nc):
    return pl.BlockSpec((L, ct, width), lambda b, i: (0, b * nc + i, 0))


def _inproj_kernel(x_ref, g_ref, w_ref, v_ref, u_ref, *, L, ct):
    x = _gather_rows(x_ref, L)
    h = _rms(x, g_ref[...])
    z = jnp.dot(h.astype(BF16), w_ref[...], preferred_element_type=F32)
    a = z[:, :D_CONV]
    g = z[:, D_CONV:2 * D_CONV]
    u = z[:, 2 * D_CONV:].astype(BF16)
    v_ref[...] = (a * jax.nn.sigmoid(g)).reshape(L, ct, D_CONV)
    for s in range(L):
        for j in range(N_LANE_BLOCKS):
            u_ref[j, :, s * LANES:(s + 1) * LANES] = u[s * ct:(s + 1) * ct, j * LANES:(j + 1) * LANES]


def _inproj(x4, g0, w_in, *, L, ct):
    bq, c = x4.shape[:2]
    nc = c // ct
    R = bq * c
    return pl.pallas_call(
        functools.partial(_inproj_kernel, L=L, ct=ct),
        grid=(bq, nc),
        in_specs=[_nat_spec(ct, L), _const((1, D_MODEL)), _const((D_MODEL, 2 * D_CONV + D_SSM))],
        out_specs=[
            _tp_spec(ct, L, D_CONV, nc),
            pl.BlockSpec((N_LANE_BLOCKS, ct, L * LANES), lambda b, i: (0, b * nc + i, 0)),
        ],
        out_shape=[
            jax.ShapeDtypeStruct((L, R, D_CONV), F32),
            jax.ShapeDtypeStruct((N_LANE_BLOCKS, R, L * LANES), BF16),
        ],
        compiler_params=_params(2),
        name="inproj",
    )(x4, g0, w_in)


def _ln_silu(acc, g, b):
    mu = jnp.mean(acc, axis=-1, keepdims=True)
    xc = acc - mu
    var = jnp.mean(xc * xc, axis=-1, keepdims=True)
    y = xc * lax.rsqrt(var + LN_EPS) * g + b
    return y * jax.nn.sigmoid(y)


def _conv_prompt_kernel(v_ref, w_ref, b_ref, g_ref, bb_ref, o_ref, v1_ref, v2_ref, *, L, C, rc):
    row = lax.broadcasted_iota(jnp.int32, (C, D_CONV), 0)
    for s in range(L):
        x = v_ref[s]
        v1_ref[s] = jnp.where(row >= 1, pltpu.roll(x, 1, 0), 0.0)
        v2_ref[s] = jnp.where(row >= 2, pltpu.roll(x, 2, 0), 0.0)
    srcs = (v_ref, v1_ref, v2_ref)

    def chunk(i, carry):
        r0 = pl.multiple_of(i * rc, rc)
        for s in range(L):
            acc = jnp.broadcast_to(b_ref[...], (rc, D_CONV))
            for d in range(CONV_WIDTH):
                blk = (s - d) % L
                shift = (d - s + L - 1) // L if d > s else 0
                k = CONV_WIDTH - 1 - d
                acc = acc + w_ref[k:k + 1, :] * srcs[shift][blk, pl.ds(r0, rc), :]
            o_ref[s, pl.ds(r0, rc), :] = _ln_silu(acc, g_ref[...], bb_ref[...]).astype(o_ref.dtype)
        return carry

    lax.fori_loop(0, C // rc, chunk, 0)


def _conv_prompt(v, w_dw, b_dw, ln_g, ln_b, *, L, C):
    R = v.shape[1]
    rc = 32
    return pl.pallas_call(
        functools.partial(_conv_prompt_kernel, L=L, C=C, rc=rc),
        grid=(R // C,),
        in_specs=[
            pl.BlockSpec((L, C, D_CONV), lambda b: (0, b, 0)),
            _const((CONV_WIDTH, D_CONV)),
            _const((1, D_CONV)),
            _const((1, D_CONV)),
            _const((1, D_CONV)),
        ],
        out_specs=pl.BlockSpec((L, C, D_CONV), lambda b: (0, b, 0)),
        out_shape=jax.ShapeDtypeStruct((L, R, D_CONV), BF16),
        scratch_shapes=[pltpu.VMEM((L, C, D_CONV), F32), pltpu.VMEM((L, C, D_CONV), F32)],
        compiler_params=_params(1),
        name="conv_prompt",
    )(v, w_dw, b_dw, ln_g, ln_b)


def _conv_sample_kernel(v_ref, st_ref, w_ref, b_ref, g_ref, bb_ref, o_ref, new_ref, *, L, H):
    def ext(i):
        return st_ref[i] if i < H else v_ref[i - H]

    for t in range(L):
        acc = jnp.broadcast_to(b_ref[...], v_ref.shape[1:])
        for k in range(CONV_WIDTH):
            acc = acc + w_ref[k:k + 1, :] * ext(t + k)
        o_ref[t] = _ln_silu(acc, g_ref[...], bb_ref[...]).astype(o_ref.dtype)
    for i in range(H):
        new_ref[i] = ext(i + L)


def _conv_sample(v, state_t, w_dw, b_dw, ln_g, ln_b):
    L, R, _ = v.shape
    H = CONV_WIDTH - 1
    return pl.pallas_call(
        functools.partial(_conv_sample_kernel, L=L, H=H),
        grid=(1,),
        in_specs=[
            _const((L, R, D_CONV)),
            _const((H, R, D_CONV)),
            _const((CONV_WIDTH, D_CONV)),
            _const((1, D_CONV)),
            _const((1, D_CONV)),
            _const((1, D_CONV)),
        ],
        out_specs=[
            pl.BlockSpec((L, R, D_CONV), lambda i: (0, 0, 0)),
            pl.BlockSpec((H, R, D_CONV), lambda i: (0, 0, 0)),
        ],
        out_shape=[
            jax.ShapeDtypeStruct((L, R, D_CONV), BF16),
            jax.ShapeDtypeStruct((H, R, D_CONV), F32),
        ],
        compiler_params=_params(1),
        name="conv_sample",
    )(v, state_t, w_dw, b_dw, ln_g, ln_b)


def _expand_block_diag(d2):
    tiled = jnp.concatenate([d2] * (STATE_HALF // LANES), axis=-1)
    r = lax.broadcasted_iota(jnp.int32, tiled.shape, 0) // P_GROUP
    c = lax.broadcasted_iota(jnp.int32, tiled.shape, 1) // N_STATE
    return jnp.where(r == c, tiled, jnp.zeros_like(tiled))


def _ssm_kernel(u_ref, taps_ref, wbc_ref, wcc_ref, a_ref, d_ref, h0_ref, y_ref, hl_ref,
                w_scr, wb_scr, wct_scr, s_scr, hp_scr, *, L, C, bt):
    per_tile = MXU_N // LANES

    @pl.when(pl.program_id(1) == 0)
    def _():
        for sp in range(L):
            for s in range(L):
                rows = slice(s * LANES, (s + 1) * LANES)
                cols = slice(sp * LANES, (sp + 1) * LANES)
                if s <= sp:
                    w_scr[rows, cols] = taps_ref[sp - s]
                elif s // per_tile == sp // per_tile:
                    w_scr[rows, cols] = jnp.zeros((LANES, LANES), BF16)
        for s in range(L):
            rows = slice(s * LANES, (s + 1) * LANES)
            for a in range(2):
                cols = slice(a * STATE_HALF, (a + 1) * STATE_HALF)
                wb_scr[rows, cols] = _expand_block_diag(wbc_ref[s, a])
                wct_scr[rows, cols] = _expand_block_diag(wcc_ref[s, a])

    nk = STATE_W // LANES
    half = nk // 2
    inc_all = jnp.dot(u_ref[...], wb_scr[...], preferred_element_type=F32)
    for k in range(nk):
        s_scr[k] = inc_all[:, k * LANES:(k + 1) * LANES]
    h0 = h0_ref[...]
    a = a_ref[...]

    def piece(x, k):
        return x[:, k * LANES:(k + 1) * LANES]

    ar = [piece(a[0:1], k) for k in range(half)]
    ai = [piece(a[1:2], k) for k in range(half)]

    def step(c, carry):
        rows = pl.ds(c, bt, stride=C) if C > 1 else pl.ds(0, bt)
        out = []
        for k in range(half):
            hr, hi = carry[k], carry[half + k]
            hp_scr[k, rows, :] = hr
            hp_scr[half + k, rows, :] = hi
            out.append(ar[k] * hr - ai[k] * hi + s_scr[k, rows, :])
        for k in range(half):
            hr, hi = carry[k], carry[half + k]
            out.append(ar[k] * hi + ai[k] * hr + s_scr[half + k, rows, :])
        return tuple(out)

    fin = lax.fori_loop(0, C, step, tuple(piece(h0, k) for k in range(nk)))
    for k in range(nk):
        hl_ref[:, k * LANES:(k + 1) * LANES] = fin[k]

    hp = jnp.concatenate([hp_scr[k] for k in range(nk)], axis=-1).astype(BF16)
    d = d_ref[...]
    for n in range(L // per_tile):
        k = (n + 1) * MXU_N
        cols = slice(n * MXU_N, (n + 1) * MXU_N)
        y = jnp.dot(u_ref[:, :k], w_scr[:k, cols], preferred_element_type=F32)
        y = y + lax.dot_general(hp, wct_scr[cols, :], (((1,), (1,)), ((), ())), preferred_element_type=F32)
        for h in range(per_tile):
            s = n * per_tile + h
            us = u_ref[:, s * LANES:(s + 1) * LANES].astype(F32)
            y_ref[s] = y[:, h * LANES:(h + 1) * LANES] + d * us


def _ssm(u4, taps, wbc, wcc, a_pow, d_skip, h0, *, L, C, bt):
    R = u4.shape[1]
    rows = bt * C
    nb = R // rows
    lw = L * LANES
    return pl.pallas_call(
        functools.partial(_ssm_kernel, L=L, C=C, bt=bt),
        grid=(N_LANE_BLOCKS, nb),
        in_specs=[
            pl.BlockSpec((None, rows, lw), lambda j, b: (j, b, 0)),
            pl.BlockSpec((None, L, LANES, LANES), lambda j, b: (j, 0, 0, 0)),
            pl.BlockSpec((None, L, 2, LANES, LANES), lambda j, b: (j, 0, 0, 0, 0)),
            pl.BlockSpec((None, L, 2, LANES, LANES), lambda j, b: (j, 0, 0, 0, 0)),
            pl.BlockSpec((None, 2, STATE_HALF), lambda j, b: (j, 0, 0)),
            pl.BlockSpec((None, 1, LANES), lambda j, b: (j, 0, 0)),
            pl.BlockSpec((None, None, bt, STATE_W), lambda j, b: (j, b, 0, 0)),
        ],
        out_specs=[
            pl.BlockSpec((L, rows, LANES), lambda j, b: (0, b, j)),
            pl.BlockSpec((None, None, bt, STATE_W), lambda j, b: (j, b, 0, 0)),
        ],
        out_shape=[
            jax.ShapeDtypeStruct((L, R, D_SSM), F32),
            jax.ShapeDtypeStruct((N_LANE_BLOCKS, nb, bt, STATE_W), F32),
        ],
        scratch_shapes=[
            pltpu.VMEM((lw, lw), BF16),
            pltpu.VMEM((lw, STATE_W), BF16),
            pltpu.VMEM((lw, STATE_W), BF16),
            pltpu.VMEM((STATE_W // LANES, rows, LANES), F32),
            pltpu.VMEM((STATE_W // LANES, rows, LANES), F32),
        ],
        compiler_params=_params(2),
        name="ssm",
    )(u4, taps, wbc, wcc, a_pow, d_skip, h0)


def _ssm_weights(lam_re, lam_im, log_dt, b_re, b_im, c_re, c_im, L):
    hp = lax.Precision.HIGHEST
    dt = jnp.exp(log_dt)[:, None]
    zr = lam_re * dt
    zi = lam_im * dt
    n_pow = jnp.arange(L + 1, dtype=F32)[:, None, None]
    mag = jnp.exp(zr[None] * n_pow)
    pr = mag * jnp.cos(zi[None] * n_pow)
    pi = mag * jnp.sin(zi[None] * n_pow)
    a1r, a1i = pr[1], pi[1]
    den = lam_re * lam_re + lam_im * lam_im
    qr = ((a1r - 1.0) * lam_re + a1i * lam_im) / den
    qi = (a1i * lam_re - (a1r - 1.0) * lam_im) / den
    bbr = qr[:, :, None] * b_re - qi[:, :, None] * b_im
    bbi = qr[:, :, None] * b_im + qi[:, :, None] * b_re

    nj, g8 = N_LANE_BLOCKS, GROUPS_PER_BLOCK
    eye = jnp.eye(g8, dtype=F32)

    car = c_re[None] * pr[:L, :, None, :] - c_im[None] * pi[:L, :, None, :]
    cai = c_re[None] * pi[:L, :, None, :] + c_im[None] * pr[:L, :, None, :]
    taps = (jnp.einsum("tgqn,gnp->gtqp", car, bbr, precision=hp)
            - jnp.einsum("tgqn,gnp->gtqp", cai, bbi, precision=hp))
    taps = taps.reshape(nj, g8, L, P_GROUP, P_GROUP)
    taps = jnp.einsum("jgtqp,gh->jtgphq", taps, eye).reshape(nj, L, LANES, LANES)

    n_rev = (L - 1.0) - jnp.arange(L, dtype=F32)[:, None, None]
    mag_rev = jnp.exp(zr[None] * n_rev)
    rev = mag_rev * jnp.cos(zi[None] * n_rev), mag_rev * jnp.sin(zi[None] * n_rev)
    er = rev[0][:, :, :, None] * bbr[None] - rev[1][:, :, :, None] * bbi[None]
    ei = rev[0][:, :, :, None] * bbi[None] + rev[1][:, :, :, None] * bbr[None]
    e = jnp.stack([er, ei], 1).reshape(L, 2, nj, g8, N_STATE, P_GROUP)
    e = jnp.transpose(e, (2, 0, 1, 3, 5, 4)).reshape(nj, L, 2, LANES, N_STATE)
    wbc = jnp.concatenate([e, e], axis=-1)

    mr = c_re[None] * pr[1:, :, None, :] - c_im[None] * pi[1:, :, None, :]
    mi = c_re[None] * pi[1:, :, None, :] + c_im[None] * pr[1:, :, None, :]
    m = jnp.stack([mr, -mi], 1).reshape(L, 2, nj, g8, P_GROUP, N_STATE)
    m = jnp.transpose(m, (2, 0, 1, 3, 4, 5)).reshape(nj, L, 2, LANES, N_STATE)
    wcc = jnp.concatenate([m, m], axis=-1)

    a_pow = jnp.stack([pr[L].reshape(nj, STATE_HALF), pi[L].reshape(nj, STATE_HALF)], 1)
    return taps.astype(BF16), wbc.astype(BF16), wcc.astype(BF16), a_pow


def _pack_state(re, im):
    b = re.shape[0]
    r = re.reshape(b, N_LANE_BLOCKS, STATE_HALF)
    i = im.reshape(b, N_LANE_BLOCKS, STATE_HALF)
    return jnp.transpose(jnp.concatenate([r, i], -1), (1, 0, 2))


def _unpack_state(h):
    b = h.shape[1]
    h = jnp.transpose(h, (1, 0, 2))
    re = h[:, :, :STATE_HALF].reshape(b, N_GROUPS, N_STATE)
    im = h[:, :, STATE_HALF:].reshape(b, N_GROUPS, N_STATE)
    return re, im


def _mixout_kernel(c_ref, y_ref, x_ref, wglu_ref, wout_ref, g_ref, o_ref, *, L, ct):
    rows = L * ct
    gy = jax.nn.gelu(y_ref[...].reshape(rows, D_SSM))
    sg = gy * jax.nn.sigmoid(jnp.dot(gy.astype(BF16), wglu_ref[...], preferred_element_type=F32))
    m = jnp.dot(c_ref[...].reshape(rows, D_CONV), wout_ref[0:D_CONV, :], preferred_element_type=F32)
    m = m + jnp.dot(sg.astype(BF16), wout_ref[D_CONV:, :], preferred_element_type=F32)
    x = _gather_rows(x_ref, L)
    o_ref[...] = (x + _rms(m, g_ref[...])).reshape(L, ct, D_MODEL)


def _mixout(c, y, x4, w_glu, w_out, g1, *, L, ct):
    bq, cc = x4.shape[:2]
    nc = cc // ct
    R = bq * cc
    return pl.pallas_call(
        functools.partial(_mixout_kernel, L=L, ct=ct),
        grid=(bq, nc),
        in_specs=[
            _tp_spec(ct, L, D_CONV, nc),
            _tp_spec(ct, L, D_SSM, nc),
            _nat_spec(ct, L),
            _const((D_SSM, D_SSM)),
            _const((D_CONV + D_SSM, D_MODEL)),
            _const((1, D_MODEL)),
        ],
        out_specs=_tp_spec(ct, L, D_MODEL, nc),
        out_shape=jax.ShapeDtypeStruct((L, R, D_MODEL), F32),
        compiler_params=_params(2),
        name="mixout",
    )(c, y, x4, w_glu, w_out, g1)


def _memkv_kernel(m_ref, g_ref, wk_ref, wv_ref, k_ref, v_ref):
    m = _rms(m_ref[...], g_ref[...]).astype(BF16)
    k_ref[...] = jnp.dot(m, wk_ref[...], preferred_element_type=F32)
    v_ref[...] = jnp.dot(m, wv_ref[...], preferred_element_type=F32)


def _memkv(mem2, g_mem, w_k, w_v):
    rows = mem2.shape[0]
    rt = 512
    return pl.pallas_call(
        _memkv_kernel,
        grid=(rows // rt,),
        in_specs=[
            pl.BlockSpec((rt, D_MODEL), lambda i: (i, 0)),
            _const((1, D_MODEL)),
            _const((D_MODEL, D_MODEL)),
            _const((D_MODEL, D_MODEL)),
        ],
        out_specs=[pl.BlockSpec((rt, D_MODEL), lambda i: (i, 0))] * 2,
        out_shape=[jax.ShapeDtypeStruct((rows, D_MODEL), F32)] * 2,
        compiler_params=_params(1),
        name="memkv",
    )(mem2, g_mem, w_k, w_v)


def _softmax_rows(s):
    s = s - jnp.max(s, axis=-1, keepdims=True)
    e = jnp.exp(s)
    return e / jnp.sum(e, axis=-1, keepdims=True)


def _attn_prompt_kernel(x_ref, k_ref, v_ref, wq_ref, wo_ref, gq_ref, go_ref, o_ref, *, L, ct):
    rows = L * ct
    x = x_ref[...].reshape(rows, D_MODEL)
    h = _rms(x, gq_ref[...])
    q = jnp.dot(h.astype(BF16), wq_ref[...], preferred_element_type=F32).astype(BF16)
    kb = k_ref[...].astype(BF16)
    vb = v_ref[...].astype(BF16)
    heads = []
    for hd in range(N_HEADS):
        sl = slice(hd * HEAD_DIM, (hd + 1) * HEAD_DIM)
        sc = lax.dot_general(q[:, sl], kb[:, sl], (((1,), (1,)), ((), ())), preferred_element_type=F32)
        p = _softmax_rows(sc * (HEAD_DIM ** -0.5))
        heads.append(jnp.dot(p.astype(BF16), vb[:, sl], preferred_element_type=F32))
    o = jnp.concatenate(heads, axis=-1)
    a = jnp.dot(o.astype(BF16), wo_ref[...], preferred_element_type=F32)
    o_ref[...] = (x + _rms(a, go_ref[...])).reshape(L, ct, D_MODEL)


def _attn_prompt(x1, k, v, w_q, w_o, g2, g3, *, C, ct):
    L, R, _ = x1.shape
    nc = C // ct
    return pl.pallas_call(
        functools.partial(_attn_prompt_kernel, L=L, ct=ct),
        grid=(R // C, nc),
        in_specs=[
            _tp_spec(ct, L, D_MODEL, nc),
            pl.BlockSpec((None, N_MEM, D_MODEL), lambda b, i: (b, 0, 0)),
            pl.BlockSpec((None, N_MEM, D_MODEL), lambda b, i: (b, 0, 0)),
            _const((D_MODEL, D_MODEL)),
            _const((D_MODEL, D_MODEL)),
            _const((1, D_MODEL)),
            _const((1, D_MODEL)),
        ],
        out_specs=_tp_spec(ct, L, D_MODEL, nc),
        out_shape=jax.ShapeDtypeStruct((L, R, D_MODEL), F32),
        compiler_params=_params(2),
        name="attn_prompt",
    )(x1, k, v, w_q, w_o, g2, g3)


def _qproj_kernel(x_ref, wq_ref, g_ref, q_ref):
    h = _rms(x_ref[...], g_ref[...])
    q_ref[...] = jnp.dot(h.astype(BF16), wq_ref[...], preferred_element_type=F32)


def _qproj(x1f, w_q, g2):
    rows = x1f.shape[0]
    return pl.pallas_call(
        _qproj_kernel,
        grid=(1,),
        in_specs=[_const((rows, D_MODEL)), _const((D_MODEL, D_MODEL)), _const((1, D_MODEL))],
        out_specs=pl.BlockSpec((rows, D_MODEL), lambda i: (0, 0)),
        out_shape=jax.ShapeDtypeStruct((rows, D_MODEL), F32),
        compiler_params=_params(1),
        name="qproj_sample",
    )(x1f, w_q, g2)


def _attn_sample_kernel(q_ref, k_ref, v_ref, o_ref, *, L, bb):
    rows = L * bb
    q2 = q_ref[...].reshape(rows, D_MODEL)
    qs = jnp.concatenate([q2[:, h * HEAD_DIM:(h + 1) * HEAD_DIM] for h in range(N_HEADS)], axis=0).astype(BF16)
    nr = N_HEADS * rows
    nc = N_MEM * N_HEADS
    row_head = lax.broadcasted_iota(jnp.int32, (nr, nc), 0) // rows
    col_head = lax.broadcasted_iota(jnp.int32, (nr, nc), 1) % N_HEADS
    same_head = row_head == col_head
    owner = lax.broadcasted_iota(jnp.int32, (nr, HEAD_DIM), 0) % bb
    acc = jnp.zeros((nr, HEAD_DIM), F32)
    for b in range(bb):
        ka = k_ref[b].reshape(nc, HEAD_DIM).astype(BF16)
        va = v_ref[b].reshape(nc, HEAD_DIM).astype(BF16)
        sc = lax.dot_general(qs, ka, (((1,), (1,)), ((), ())), preferred_element_type=F32)
        p = _softmax_rows(jnp.where(same_head, sc * (HEAD_DIM ** -0.5), -1e30))
        o = jnp.dot(p.astype(BF16), va, preferred_element_type=F32)
        acc = jnp.where(owner == b, o, acc)
    for h in range(N_HEADS):
        o_ref[:, :, h * HEAD_DIM:(h + 1) * HEAD_DIM] = acc[h * rows:(h + 1) * rows].reshape(L, bb, HEAD_DIM)


def _attn_sample(q3, k5, v5):
    L, R, _ = q3.shape
    bb = 8
    kv_spec = pl.BlockSpec((None, bb, N_MEM, N_HEADS, HEAD_DIM), lambda i: (0, i, 0, 0, 0))
    return pl.pallas_call(
        functools.partial(_attn_sample_kernel, L=L, bb=bb),
        grid=(R // bb,),
        in_specs=[pl.BlockSpec((L, bb, D_MODEL), lambda i: (0, i, 0)), kv_spec, kv_spec],
        out_specs=pl.BlockSpec((L, bb, D_MODEL), lambda i: (0, i, 0)),
        out_shape=jax.ShapeDtypeStruct((L, R, D_MODEL), F32),
        compiler_params=_params(1),
        name="attn_sample",
    )(q3, k5, v5)


def _oproj_kernel(o_ref, x_ref, wo_ref, g_ref, y_ref):
    a = jnp.dot(o_ref[...].astype(BF16), wo_ref[...], preferred_element_type=F32)
    y_ref[...] = x_ref[...] + _rms(a, g_ref[...])


def _oproj(of, x1f, w_o, g3):
    rows = of.shape[0]
    return pl.pallas_call(
        _oproj_kernel,
        grid=(1,),
        in_specs=[_const((rows, D_MODEL)), _const((rows, D_MODEL)), _const((D_MODEL, D_MODEL)),
                  _const((1, D_MODEL))],
        out_specs=pl.BlockSpec((rows, D_MODEL), lambda i: (0, 0)),
        out_shape=jax.ShapeDtypeStruct((rows, D_MODEL), F32),
        compiler_params=_params(1),
        name="oproj_sample",
    )(of, x1f, w_o, g3)


def _ffn_kernel(x_ref, wg_ref, wu_ref, wd_ref, gi_ref, go_ref, o_ref, *, L, ct):
    rows = L * ct
    x = x_ref[...].reshape(rows, D_MODEL)
    h = _rms(x, gi_ref[...]).astype(BF16)
    gate = jnp.dot(h, wg_ref[...], preferred_element_type=F32)
    up = jnp.dot(h, wu_ref[...], preferred_element_type=F32)
    act = (gate * jax.nn.sigmoid(gate) * up).astype(BF16)
    dn = jnp.dot(act, wd_ref[...], preferred_element_type=F32)
    y = x + _rms(dn, go_ref[...])
    for s in range(L):
        o_ref[:, s, :] = y[s * ct:(s + 1) * ct]


def _ffn(x2, w_gate, w_up, w_down, g4, g5, *, bq, C, ct):
    L = x2.shape[0]
    nc = C // ct
    return pl.pallas_call(
        functools.partial(_ffn_kernel, L=L, ct=ct),
        grid=(bq, nc),
        in_specs=[
            _tp_spec(ct, L, D_MODEL, nc),
            _const((D_MODEL, D_FF)),
            _const((D_MODEL, D_FF)),
            _const((D_FF, D_MODEL)),
            _const((1, D_MODEL)),
            _const((1, D_MODEL)),
        ],
        out_specs=_nat_spec(ct, L),
        out_shape=jax.ShapeDtypeStruct((bq, C, L, D_MODEL), F32),
        compiler_params=_params(2),
        name="ffn",
    )(x2, w_gate, w_up, w_down, g4, g5)


def _layer(x4, mem_k, mem_v, conv_state_t, h0_packed, wts, *, ct, sc, bt, sample):
    bq, C, L, _ = x4.shape
    R = bq * C
    ng = wts["norm_g"]
    g = [ng[i:i + 1] for i in range(6)]

    v, u4 = _inproj(x4, g[0], wts["w_in"], L=L, ct=ct)
    if sample:
        cact, conv_new = _conv_sample(v, conv_state_t, wts["w_dw"], wts["b_dw"], wts["ln_g"], wts["ln_b"])
    else:
        cact = _conv_prompt(v, wts["w_dw"], wts["b_dw"], wts["ln_g"], wts["ln_b"], L=L, C=C)
        tail = v.reshape(L, bq, C, D_CONV)[:, :, C - 2:, :]
        tail = jnp.transpose(tail, (1, 2, 0, 3)).reshape(bq, 2 * L, D_CONV)
        conv_new = tail[:, 2 * L - (CONV_WIDTH - 1):, :]

    sw = wts["ssm_L%d" % L]
    nb = R // (bt * sc)
    y, hl = _ssm(u4, sw[0], sw[1], sw[2], sw[3], wts["d_skip"], h0_packed.reshape(N_LANE_BLOCKS, nb, bt, STATE_W),
                 L=L, C=sc, bt=bt)
    hl = hl.reshape(N_LANE_BLOCKS, nb * bt, STATE_W)

    x1 = _mixout(cact, y, x4, wts["w_glu"], wts["w_out"], g[1], L=L, ct=ct)

    if sample:
        q = _qproj(x1.reshape(L * R, D_MODEL), wts["w_q"], g[2])
        o = _attn_sample(q.reshape(L, R, D_MODEL), mem_k, mem_v)
        x2 = _oproj(o.reshape(L * R, D_MODEL), x1.reshape(L * R, D_MODEL), wts["w_o"], g[3])
        x2 = x2.reshape(L, R, D_MODEL)
    else:
        x2 = _attn_prompt(x1, mem_k, mem_v, wts["w_q"], wts["w_o"], g[2], g[3], C=C, ct=ct)

    out = _ffn(x2, wts["w_gate"], wts["w_up"], wts["w_down"], g[4], g[5], bq=bq, C=C, ct=ct)
    return out, conv_new, hl


PROMPT_L = 16
PROMPT_CT = 32


def kernel(x_prompt, x_sample, mem_prompt, cache_mem_k, cache_mem_v, state_conv, state_ssm_re, state_ssm_im,
           norm_g, mem_norm_g, w_in, w_dw, b_dw, ln_g, ln_b, lam_re, lam_im, log_dt, b_re, b_im, c_re, c_im,
           d_skip, w_glu, w_out, w_q, w_k, w_v, w_o, w_gate, w_up, w_down):
    depth = w_in.shape[0]
    bp, tp, _ = x_prompt.shape
    bs, ts, _ = x_sample.shape
    assert tp % (PROMPT_L * PROMPT_CT) == 0 and tp >= CONV_WIDTH - 1

    yp = x_prompt.reshape(bp, tp // PROMPT_L, PROMPT_L, D_MODEL)
    ys = x_sample.reshape(1, bs, ts, D_MODEL)
    outs = [[] for _ in range(8)]
    for l in range(depth):
        wts = {
            "norm_g": norm_g[l],
            "w_in": w_in[l].astype(BF16),
            "w_dw": w_dw[l],
            "b_dw": b_dw[l][None],
            "ln_g": ln_g[l][None],
            "ln_b": ln_b[l][None],
            "d_skip": d_skip[l].reshape(N_LANE_BLOCKS, 1, LANES),
            "w_glu": w_glu[l].astype(BF16),
            "w_out": w_out[l].astype(BF16),
            "w_q": w_q[l].astype(BF16),
            "w_o": w_o[l].astype(BF16),
            "w_gate": w_gate[l].astype(BF16),
            "w_up": w_up[l].astype(BF16),
            "w_down": w_down[l].astype(BF16),
        }
        ssm_args = (lam_re[l], lam_im[l], log_dt[l], b_re[l], b_im[l], c_re[l], c_im[l])
        wts["ssm_L%d" % PROMPT_L] = _ssm_weights(*ssm_args, PROMPT_L)
        wts["ssm_L%d" % ts] = _ssm_weights(*ssm_args, ts)

        kp, vp = _memkv(mem_prompt.reshape(bp * N_MEM, D_MODEL), mem_norm_g[l][None],
                        w_k[l].astype(BF16), w_v[l].astype(BF16))
        kp = kp.reshape(bp, N_MEM, D_MODEL)
        vp = vp.reshape(bp, N_MEM, D_MODEL)
        h0p = jnp.zeros((N_LANE_BLOCKS, bp, STATE_W), F32)
        yp, cp, hp = _layer(yp, kp, vp, None, h0p, wts, ct=PROMPT_CT, sc=tp // PROMPT_L, bt=4, sample=False)
        h0s = _pack_state(state_ssm_re[l], state_ssm_im[l])
        ys, cs, hs = _layer(ys, cache_mem_k[l:l + 1], cache_mem_v[l:l + 1], jnp.transpose(state_conv[l], (1, 0, 2)),
                            h0s, wts, ct=bs, sc=1, bt=bs, sample=True)
        cs = jnp.transpose(cs, (1, 0, 2))

        hp_re, hp_im = _unpack_state(hp)
        hs_re, hs_im = _unpack_state(hs)
        for lst, val in zip(outs, (kp.reshape(bp, N_MEM, N_HEADS, HEAD_DIM), vp.reshape(bp, N_MEM, N_HEADS, HEAD_DIM),
                                   cp, hp_re, hp_im, cs, hs_re, hs_im)):
            lst.append(val)
    return (yp.reshape(bp, tp, D_MODEL), ys.reshape(bs, ts, D_MODEL)) + tuple(jnp.stack(o) for o in outs)
```

```python
import functools

import jax
import jax.numpy as jnp
from jax import lax
from jax.experimental import pallas as pl
from jax.experimental.pallas import tpu as pltpu

F32 = jnp.float32
BF16 = jnp.bfloat16

D_MODEL = 1024
D_CONV = 512
D_SSM = 512
CONV_WIDTH = 31
N_GROUPS = 32
P_GROUP = 16
N_STATE = 64
N_MEM = 256
N_HEADS = 4
HEAD_DIM = 256
D_FF = 2816
RMS_EPS = 1e-6
LN_EPS = 1e-5

LANES = 128
SUBLANES = 8
GROUPS_PER_BLOCK = LANES // P_GROUP
N_LANE_BLOCKS = D_SSM // LANES
STATE_HALF = GROUPS_PER_BLOCK * N_STATE
STATE_W = 2 * STATE_HALF
MXU_N = 256
VMEM_LIMIT = 56 * 1024 * 1024


def _params(n_axes, vmem=VMEM_LIMIT):
    return pltpu.CompilerParams(dimension_semantics=("arbitrary",) * n_axes, vmem_limit_bytes=vmem)


def _const(shape):
    nd = len(shape)
    return pl.BlockSpec(shape, lambda *_: (0,) * nd, pipeline_mode=pl.Buffered(1))


def _rms(x, g):
    return x * lax.rsqrt(jnp.mean(x * x, axis=-1, keepdims=True) + RMS_EPS) * g


def _gather_rows(x_ref, L, c0, n):
    return jnp.concatenate([x_ref[c0:c0 + n, s, :] for s in range(L)], axis=0)


def _sub_tiles(ct, sub):
    return [(c0, sub) for c0 in range(0, ct, sub)]


def _nat_spec(ct, L):
    return pl.BlockSpec((None, ct, L, D_MODEL), lambda b, i: (b, i, 0, 0))


def _tp_spec(ct, L, width, nc):
    return pl.BlockSpec((L, ct, width), lambda b, i: (0, b * nc + i, 0))


def _inproj_kernel(x_ref, g_ref, w_ref, v_ref, u_ref, *, L, ct, sub):
    for c0, n in _sub_tiles(ct, sub):
        x = _gather_rows(x_ref, L, c0, n)
        h = _rms(x, g_ref[...])
        z = jnp.dot(h.astype(BF16), w_ref[...], preferred_element_type=F32)
        a = z[:, :D_CONV]
        g = z[:, D_CONV:2 * D_CONV]
        u = z[:, 2 * D_CONV:].astype(BF16)
        v_ref[:, c0:c0 + n, :] = (a * jax.nn.sigmoid(g)).reshape(L, n, D_CONV)
        for s in range(L):
            for j in range(N_LANE_BLOCKS):
                u_ref[j, c0:c0 + n, s * LANES:(s + 1) * LANES] = u[s * n:(s + 1) * n, j * LANES:(j + 1) * LANES]


def _inproj(x4, g0, w_in, *, L, ct, sub):
    bq, c = x4.shape[:2]
    nc = c // ct
    R = bq * c
    return pl.pallas_call(
        functools.partial(_inproj_kernel, L=L, ct=ct, sub=sub),
        grid=(bq, nc),
        in_specs=[_nat_spec(ct, L), _const((1, D_MODEL)), _const((D_MODEL, 2 * D_CONV + D_SSM))],
        out_specs=[
            _tp_spec(ct, L, D_CONV, nc),
            pl.BlockSpec((N_LANE_BLOCKS, ct, L * LANES), lambda b, i: (0, b * nc + i, 0)),
        ],
        out_shape=[
            jax.ShapeDtypeStruct((L, R, D_CONV), F32),
            jax.ShapeDtypeStruct((N_LANE_BLOCKS, R, L * LANES), BF16),
        ],
        compiler_params=_params(2),
        name="inproj",
    )(x4, g0, w_in)


def _ln_silu(acc, g, b):
    mu = jnp.mean(acc, axis=-1, keepdims=True)
    xc = acc - mu
    var = jnp.mean(xc * xc, axis=-1, keepdims=True)
    y = xc * lax.rsqrt(var + LN_EPS) * g + b
    return y * jax.nn.sigmoid(y)


def _conv_prompt_kernel(v_ref, w_ref, b_ref, g_ref, bb_ref, o_ref, v1_ref, v2_ref, *, L, C, rc):
    row = lax.broadcasted_iota(jnp.int32, (C, D_CONV), 0)
    for s in range(L):
        x = v_ref[s]
        v1_ref[s] = jnp.where(row >= 1, pltpu.roll(x, 1, 0), 0.0)
        v2_ref[s] = jnp.where(row >= 2, pltpu.roll(x, 2, 0), 0.0)
    srcs = (v_ref, v1_ref, v2_ref)

    sub8 = rc // SUBLANES

    def chunk(i, carry):
        r0 = pl.multiple_of(i * rc, rc)
        for s in range(L):
            acc = jnp.broadcast_to(b_ref[...][None], (sub8, SUBLANES, D_CONV))
            for d in range(CONV_WIDTH):
                blk = (s - d) % L
                shift = (d - s + L - 1) // L if d > s else 0
                k = CONV_WIDTH - 1 - d
                src = srcs[shift][blk, pl.ds(r0, rc), :].reshape(sub8, SUBLANES, D_CONV)
                acc = acc + w_ref[k][None] * src
            y = _ln_silu(acc.reshape(rc, D_CONV), g_ref[...], bb_ref[...])
            o_ref[s, pl.ds(r0, rc), :] = y.astype(o_ref.dtype)
        return carry

    lax.fori_loop(0, C // rc, chunk, 0)


def _conv_prompt(v, w_dw, b_dw, ln_g, ln_b, *, L, C):
    R = v.shape[1]
    rc = 32
    return pl.pallas_call(
        functools.partial(_conv_prompt_kernel, L=L, C=C, rc=rc),
        grid=(R // C,),
        in_specs=[
            pl.BlockSpec((L, C, D_CONV), lambda b: (0, b, 0)),
            _const((CONV_WIDTH, SUBLANES, D_CONV)),
            _const((SUBLANES, D_CONV)),
            _const((1, D_CONV)),
            _const((1, D_CONV)),
        ],
        out_specs=pl.BlockSpec((L, C, D_CONV), lambda b: (0, b, 0)),
        out_shape=jax.ShapeDtypeStruct((L, R, D_CONV), BF16),
        scratch_shapes=[pltpu.VMEM((L, C, D_CONV), F32), pltpu.VMEM((L, C, D_CONV), F32)],
        compiler_params=_params(1),
        name="conv_prompt",
    )(v, jnp.broadcast_to(w_dw[:, None, :], (CONV_WIDTH, SUBLANES, D_CONV)),
      jnp.broadcast_to(b_dw, (SUBLANES, D_CONV)), ln_g, ln_b)


def _conv_sample_kernel(v_ref, st_ref, w_ref, b_ref, g_ref, bb_ref, o_ref, new_ref, *, L, H):
    def ext(i):
        return st_ref[i] if i < H else v_ref[i - H]

    for t in range(L):
        acc = jnp.broadcast_to(b_ref[...], v_ref.shape[1:])
        for k in range(CONV_WIDTH):
            acc = acc + w_ref[k:k + 1, :] * ext(t + k)
        o_ref[t] = _ln_silu(acc, g_ref[...], bb_ref[...]).astype(o_ref.dtype)
    for i in range(H):
        new_ref[i] = ext(i + L)


def _conv_sample(v, state_t, w_dw, b_dw, ln_g, ln_b):
    L, R, _ = v.shape
    H = CONV_WIDTH - 1
    return pl.pallas_call(
        functools.partial(_conv_sample_kernel, L=L, H=H),
        grid=(1,),
        in_specs=[
            _const((L, R, D_CONV)),
            _const((H, R, D_CONV)),
            _const((CONV_WIDTH, D_CONV)),
            _const((1, D_CONV)),
            _const((1, D_CONV)),
            _const((1, D_CONV)),
        ],
        out_specs=[
            pl.BlockSpec((L, R, D_CONV), lambda i: (0, 0, 0)),
            pl.BlockSpec((H, R, D_CONV), lambda i: (0, 0, 0)),
        ],
        out_shape=[
            jax.ShapeDtypeStruct((L, R, D_CONV), BF16),
            jax.ShapeDtypeStruct((H, R, D_CONV), F32),
        ],
        compiler_params=_params(1),
        name="conv_sample",
    )(v, state_t, w_dw, b_dw, ln_g, ln_b)


def _expand_block_diag(d2):
    tiled = jnp.concatenate([d2] * (STATE_HALF // LANES), axis=-1)
    r = lax.broadcasted_iota(jnp.int32, tiled.shape, 0) // P_GROUP
    c = lax.broadcasted_iota(jnp.int32, tiled.shape, 1) // N_STATE
    return jnp.where(r == c, tiled, jnp.zeros_like(tiled))


def _ssm_kernel(u_ref, taps_ref, wbc_ref, wcc_ref, a_ref, d_ref, h0_ref, y_ref, hl_ref,
                w_scr, wb_scr, wct_scr, s_scr, hp_scr, *, L, C, bt):
    per_tile = MXU_N // LANES

    @pl.when(pl.program_id(1) == 0)
    def _():
        for sp in range(L):
            for s in range(L):
                rows = slice(s * LANES, (s + 1) * LANES)
                cols = slice(sp * LANES, (sp + 1) * LANES)
                if s <= sp:
                    w_scr[rows, cols] = taps_ref[sp - s]
                elif s // per_tile == sp // per_tile:
                    w_scr[rows, cols] = jnp.zeros((LANES, LANES), BF16)
        for s in range(L):
            rows = slice(s * LANES, (s + 1) * LANES)
            for a in range(2):
                cols = slice(a * STATE_HALF, (a + 1) * STATE_HALF)
                wb_scr[rows, cols] = _expand_block_diag(wbc_ref[s, a])
                wct_scr[rows, cols] = _expand_block_diag(wcc_ref[s, a])

    nk = STATE_W // LANES
    half = nk // 2
    inc_all = jnp.dot(u_ref[...], wb_scr[...], preferred_element_type=F32)
    for k in range(nk):
        s_scr[k] = inc_all[:, k * LANES:(k + 1) * LANES]

    d = d_ref[...]
    for n in range(L // per_tile):
        k = (n + 1) * MXU_N
        y = jnp.dot(u_ref[:, :k], w_scr[:k, n * MXU_N:(n + 1) * MXU_N], preferred_element_type=F32)
        for h in range(per_tile):
            s = n * per_tile + h
            us = u_ref[:, s * LANES:(s + 1) * LANES].astype(F32)
            y_ref[s] = y[:, h * LANES:(h + 1) * LANES] + d * us

    h0 = h0_ref[...]
    a = a_ref[...]

    def piece(x, k):
        return x[:, k * LANES:(k + 1) * LANES]

    ar = [piece(a[0:1], k) for k in range(half)]
    ai = [piece(a[1:2], k) for k in range(half)]
    state = [piece(h0, k) for k in range(nk)]
    for c in range(C):
        rows = pl.ds(c, bt, stride=C) if C > 1 else pl.ds(0, bt)
        nxt = []
        for k in range(half):
            hr, hi = state[k], state[half + k]
            hp_scr[k, rows, :] = hr
            hp_scr[half + k, rows, :] = hi
            nxt.append(ar[k] * hr - ai[k] * hi + s_scr[k, rows, :])
        for k in range(half):
            hr, hi = state[k], state[half + k]
            nxt.append(ar[k] * hi + ai[k] * hr + s_scr[half + k, rows, :])
        state = nxt
    for k in range(nk):
        hl_ref[:, k * LANES:(k + 1) * LANES] = state[k]

    hp = jnp.concatenate([hp_scr[k] for k in range(nk)], axis=-1).astype(BF16)
    for n in range(L // per_tile):
        y = lax.dot_general(hp, wct_scr[n * MXU_N:(n + 1) * MXU_N, :], (((1,), (1,)), ((), ())),
                            preferred_element_type=F32)
        for h in range(per_tile):
            s = n * per_tile + h
            y_ref[s] += y[:, h * LANES:(h + 1) * LANES]


def _ssm(u4, taps, wbc, wcc, a_pow, d_skip, h0, *, L, C, bt):
    R = u4.shape[1]
    rows = bt * C
    nb = R // rows
    lw = L * LANES
    return pl.pallas_call(
        functools.partial(_ssm_kernel, L=L, C=C, bt=bt),
        grid=(N_LANE_BLOCKS, nb),
        in_specs=[
            pl.BlockSpec((None, rows, lw), lambda j, b: (j, b, 0)),
            pl.BlockSpec((None, L, LANES, LANES), lambda j, b: (j, 0, 0, 0)),
            pl.BlockSpec((None, L, 2, LANES, LANES), lambda j, b: (j, 0, 0, 0, 0)),
            pl.BlockSpec((None, L, 2, LANES, LANES), lambda j, b: (j, 0, 0, 0, 0)),
            pl.BlockSpec((None, 2, STATE_HALF), lambda j, b: (j, 0, 0)),
            pl.BlockSpec((None, 1, LANES), lambda j, b: (j, 0, 0)),
            pl.BlockSpec((None, None, bt, STATE_W), lambda j, b: (j, b, 0, 0)),
        ],
        out_specs=[
            pl.BlockSpec((L, rows, LANES), lambda j, b: (0, b, j)),
            pl.BlockSpec((None, None, bt, STATE_W), lambda j, b: (j, b, 0, 0)),
        ],
        out_shape=[
            jax.ShapeDtypeStruct((L, R, D_SSM), F32),
            jax.ShapeDtypeStruct((N_LANE_BLOCKS, nb, bt, STATE_W), F32),
        ],
        scratch_shapes=[
            pltpu.VMEM((lw, lw), BF16),
            pltpu.VMEM((lw, STATE_W), BF16),
            pltpu.VMEM((lw, STATE_W), BF16),
            pltpu.VMEM((STATE_W // LANES, rows, LANES), F32),
            pltpu.VMEM((STATE_W // LANES, rows, LANES), F32),
        ],
        compiler_params=_params(2),
        name="ssm",
    )(u4, taps, wbc, wcc, a_pow, d_skip, h0)


def _ssm_weights(lam_re, lam_im, log_dt, b_re, b_im, c_re, c_im, L):
    hp = lax.Precision.HIGHEST
    dt = jnp.exp(log_dt)[:, None]
    zr = lam_re * dt
    zi = lam_im * dt
    n_pow = jnp.arange(L + 1, dtype=F32)[:, None, None]
    mag = jnp.exp(zr[None] * n_pow)
    pr = mag * jnp.cos(zi[None] * n_pow)
    pi = mag * jnp.sin(zi[None] * n_pow)
    a1r, a1i = pr[1], pi[1]
    den = lam_re * lam_re + lam_im * lam_im
    qr = ((a1r - 1.0) * lam_re + a1i * lam_im) / den
    qi = (a1i * lam_re - (a1r - 1.0) * lam_im) / den
    bbr = qr[:, :, None] * b_re - qi[:, :, None] * b_im
    bbi = qr[:, :, None] * b_im + qi[:, :, None] * b_re

    nj, g8 = N_LANE_BLOCKS, GROUPS_PER_BLOCK
    eye = jnp.eye(g8, dtype=F32)

    car = c_re[None] * pr[:L, :, None, :] - c_im[None] * pi[:L, :, None, :]
    cai = c_re[None] * pi[:L, :, None, :] + c_im[None] * pr[:L, :, None, :]
    taps = (jnp.einsum("tgqn,gnp->gtqp", car, bbr, precision=hp)
            - jnp.einsum("tgqn,gnp->gtqp", cai, bbi, precision=hp))
    taps = taps.reshape(nj, g8, L, P_GROUP, P_GROUP)
    taps = jnp.einsum("jgtqp,gh->jtgphq", taps, eye).reshape(nj, L, LANES, LANES)

    n_rev = (L - 1.0) - jnp.arange(L, dtype=F32)[:, None, None]
    mag_rev = jnp.exp(zr[None] * n_rev)
    rev = mag_rev * jnp.cos(zi[None] * n_rev), mag_rev * jnp.sin(zi[None] * n_rev)
    er = rev[0][:, :, :, None] * bbr[None] - rev[1][:, :, :, None] * bbi[None]
    ei = rev[0][:, :, :, None] * bbi[None] + rev[1][:, :, :, None] * bbr[None]
    e = jnp.stack([er, ei], 1).reshape(L, 2, nj, g8, N_STATE, P_GROUP)
    e = jnp.transpose(e, (2, 0, 1, 3, 5, 4)).reshape(nj, L, 2, LANES, N_STATE)
    wbc = jnp.concatenate([e, e], axis=-1)

    mr = c_re[None] * pr[1:, :, None, :] - c_im[None] * pi[1:, :, None, :]
    mi = c_re[None] * pi[1:, :, None, :] + c_im[None] * pr[1:, :, None, :]
    m = jnp.stack([mr, -mi], 1).reshape(L, 2, nj, g8, P_GROUP, N_STATE)
    m = jnp.transpose(m, (2, 0, 1, 3, 4, 5)).reshape(nj, L, 2, LANES, N_STATE)
    wcc = jnp.concatenate([m, m], axis=-1)

    a_pow = jnp.stack([pr[L].reshape(nj, STATE_HALF), pi[L].reshape(nj, STATE_HALF)], 1)
    return taps.astype(BF16), wbc.astype(BF16), wcc.astype(BF16), a_pow


def _pack_state(re, im):
    b = re.shape[0]
    r = re.reshape(b, N_LANE_BLOCKS, STATE_HALF)
    i = im.reshape(b, N_LANE_BLOCKS, STATE_HALF)
    return jnp.transpose(jnp.concatenate([r, i], -1), (1, 0, 2))


def _unpack_state(h):
    b = h.shape[1]
    h = jnp.transpose(h, (1, 0, 2))
    re = h[:, :, :STATE_HALF].reshape(b, N_GROUPS, N_STATE)
    im = h[:, :, STATE_HALF:].reshape(b, N_GROUPS, N_STATE)
    return re, im


def _mixout_kernel(c_ref, y_ref, x_ref, wglu_ref, wout_ref, g_ref, o_ref, *, L, ct, sub):
    for c0, n in _sub_tiles(ct, sub):
        rows = L * n
        gy = jax.nn.gelu(y_ref[:, c0:c0 + n, :].reshape(rows, D_SSM))
        sg = gy * jax.nn.sigmoid(jnp.dot(gy.astype(BF16), wglu_ref[...], preferred_element_type=F32))
        m = jnp.dot(c_ref[:, c0:c0 + n, :].reshape(rows, D_CONV), wout_ref[0:D_CONV, :],
                    preferred_element_type=F32)
        m = m + jnp.dot(sg.astype(BF16), wout_ref[D_CONV:, :], preferred_element_type=F32)
        x = _gather_rows(x_ref, L, c0, n)
        o_ref[:, c0:c0 + n, :] = (x + _rms(m, g_ref[...])).reshape(L, n, D_MODEL)


def _mixout(c, y, x4, w_glu, w_out, g1, *, L, ct, sub):
    bq, cc = x4.shape[:2]
    nc = cc // ct
    R = bq * cc
    return pl.pallas_call(
        functools.partial(_mixout_kernel, L=L, ct=ct, sub=sub),
        grid=(bq, nc),
        in_specs=[
            _tp_spec(ct, L, D_CONV, nc),
            _tp_spec(ct, L, D_SSM, nc),
            _nat_spec(ct, L),
            _const((D_SSM, D_SSM)),
            _const((D_CONV + D_SSM, D_MODEL)),
            _const((1, D_MODEL)),
        ],
        out_specs=_tp_spec(ct, L, D_MODEL, nc),
        out_shape=jax.ShapeDtypeStruct((L, R, D_MODEL), F32),
        compiler_params=_params(2),
        name="mixout",
    )(c, y, x4, w_glu, w_out, g1)


def _memkv_kernel(m_ref, g_ref, wk_ref, wv_ref, k5_ref, v5_ref, kb_ref, vb_ref, *, nb):
    for b in range(nb):
        m = _rms(m_ref[b], g_ref[...]).astype(BF16)
        for w_ref, o5_ref, ob_ref in ((wk_ref, k5_ref, kb_ref), (wv_ref, v5_ref, vb_ref)):
            p = jnp.dot(m, w_ref[...], preferred_element_type=F32)
            ob_ref[b] = p.astype(BF16)
            for hd in range(N_HEADS):
                o5_ref[b, :, hd, :] = p[:, hd * HEAD_DIM:(hd + 1) * HEAD_DIM]


def _memkv(mem, g_mem, w_k, w_v):
    bq = mem.shape[0]
    nb = 2
    out5 = pl.BlockSpec((nb, N_MEM, N_HEADS, HEAD_DIM), lambda i: (i, 0, 0, 0))
    outb = pl.BlockSpec((nb, N_MEM, D_MODEL), lambda i: (i, 0, 0))
    return pl.pallas_call(
        functools.partial(_memkv_kernel, nb=nb),
        grid=(bq // nb,),
        in_specs=[
            pl.BlockSpec((nb, N_MEM, D_MODEL), lambda i: (i, 0, 0)),
            _const((1, D_MODEL)),
            _const((D_MODEL, D_MODEL)),
            _const((D_MODEL, D_MODEL)),
        ],
        out_specs=[out5, out5, outb, outb],
        out_shape=[jax.ShapeDtypeStruct((bq, N_MEM, N_HEADS, HEAD_DIM), F32)] * 2
        + [jax.ShapeDtypeStruct((bq, N_MEM, D_MODEL), BF16)] * 2,
        compiler_params=_params(1),
        name="memkv",
    )(mem, g_mem, w_k, w_v)


def _softmax_rows(s):
    s = s - jnp.max(s, axis=-1, keepdims=True)
    e = jnp.exp(s)
    return e / jnp.sum(e, axis=-1, keepdims=True)


def _attn_prompt_kernel(x_ref, k_ref, v_ref, wq_ref, wo_ref, gq_ref, go_ref, o_ref, *, L, ct, sub):
    kb = k_ref[...]
    vb = v_ref[...]
    for c0, n in _sub_tiles(ct, sub):
        rows = L * n
        x = x_ref[:, c0:c0 + n, :].reshape(rows, D_MODEL)
        h = _rms(x, gq_ref[...])
        q = jnp.dot(h.astype(BF16), wq_ref[...], preferred_element_type=F32).astype(BF16)
        heads = []
        for hd in range(N_HEADS):
            sl = slice(hd * HEAD_DIM, (hd + 1) * HEAD_DIM)
            sc = lax.dot_general(q[:, sl], kb[:, sl], (((1,), (1,)), ((), ())), preferred_element_type=F32)
            p = _softmax_rows(sc * (HEAD_DIM ** -0.5))
            heads.append(jnp.dot(p.astype(BF16), vb[:, sl], preferred_element_type=F32))
        o = jnp.concatenate(heads, axis=-1)
        a = jnp.dot(o.astype(BF16), wo_ref[...], preferred_element_type=F32)
        o_ref[:, c0:c0 + n, :] = (x + _rms(a, go_ref[...])).reshape(L, n, D_MODEL)


def _attn_prompt(x1, k, v, w_q, w_o, g2, g3, *, C, ct, sub):
    L, R, _ = x1.shape
    nc = C // ct
    return pl.pallas_call(
        functools.partial(_attn_prompt_kernel, L=L, ct=ct, sub=sub),
        grid=(R // C, nc),
        in_specs=[
            _tp_spec(ct, L, D_MODEL, nc),
            pl.BlockSpec((None, N_MEM, D_MODEL), lambda b, i: (b, 0, 0)),
            pl.BlockSpec((None, N_MEM, D_MODEL), lambda b, i: (b, 0, 0)),
            _const((D_MODEL, D_MODEL)),
            _const((D_MODEL, D_MODEL)),
            _const((1, D_MODEL)),
            _const((1, D_MODEL)),
        ],
        out_specs=_tp_spec(ct, L, D_MODEL, nc),
        out_shape=jax.ShapeDtypeStruct((L, R, D_MODEL), F32),
        compiler_params=_params(2),
        name="attn_prompt",
    )(x1, k, v, w_q, w_o, g2, g3)


def _qproj_kernel(x_ref, wq_ref, g_ref, q_ref):
    h = _rms(x_ref[...], g_ref[...])
    q_ref[...] = jnp.dot(h.astype(BF16), wq_ref[...], preferred_element_type=F32)


def _qproj(x1f, w_q, g2):
    rows = x1f.shape[0]
    return pl.pallas_call(
        _qproj_kernel,
        grid=(1,),
        in_specs=[_const((rows, D_MODEL)), _const((D_MODEL, D_MODEL)), _const((1, D_MODEL))],
        out_specs=pl.BlockSpec((rows, D_MODEL), lambda i: (0, 0)),
        out_shape=jax.ShapeDtypeStruct((rows, D_MODEL), F32),
        compiler_params=_params(1),
        name="qproj_sample",
    )(x1f, w_q, g2)


def _attn_sample_kernel(q_ref, k_ref, v_ref, o_ref, *, L, bb):
    rows = L * bb
    q2 = q_ref[...].reshape(rows, D_MODEL)
    qs = jnp.concatenate([q2[:, h * HEAD_DIM:(h + 1) * HEAD_DIM] for h in range(N_HEADS)], axis=0).astype(BF16)
    nr = N_HEADS * rows
    nc = N_MEM * N_HEADS
    row_head = lax.broadcasted_iota(jnp.int32, (nr, nc), 0) // rows
    col_head = lax.broadcasted_iota(jnp.int32, (nr, nc), 1) % N_HEADS
    same_head = row_head == col_head
    owner = lax.broadcasted_iota(jnp.int32, (nr, HEAD_DIM), 0) % bb
    acc = jnp.zeros((nr, HEAD_DIM), F32)
    for b in range(bb):
        ka = k_ref[b].reshape(nc, HEAD_DIM).astype(BF16)
        va = v_ref[b].reshape(nc, HEAD_DIM).astype(BF16)
        sc = lax.dot_general(qs, ka, (((1,), (1,)), ((), ())), preferred_element_type=F32)
        p = _softmax_rows(jnp.where(same_head, sc * (HEAD_DIM ** -0.5), -1e30))
        o = jnp.dot(p.astype(BF16), va, preferred_element_type=F32)
        acc = jnp.where(owner == b, o, acc)
    for h in range(N_HEADS):
        o_ref[:, :, h * HEAD_DIM:(h + 1) * HEAD_DIM] = acc[h * rows:(h + 1) * rows].reshape(L, bb, HEAD_DIM)


def _attn_sample(q3, k5, v5):
    L, R, _ = q3.shape
    bb = 8
    kv_spec = pl.BlockSpec((None, bb, N_MEM, N_HEADS, HEAD_DIM), lambda i: (0, i, 0, 0, 0))
    return pl.pallas_call(
        functools.partial(_attn_sample_kernel, L=L, bb=bb),
        grid=(R // bb,),
        in_specs=[pl.BlockSpec((L, bb, D_MODEL), lambda i: (0, i, 0)), kv_spec, kv_spec],
        out_specs=pl.BlockSpec((L, bb, D_MODEL), lambda i: (0, i, 0)),
        out_shape=jax.ShapeDtypeStruct((L, R, D_MODEL), F32),
        compiler_params=_params(1),
        name="attn_sample",
    )(q3, k5, v5)


def _oproj_kernel(o_ref, x_ref, wo_ref, g_ref, y_ref):
    a = jnp.dot(o_ref[...].astype(BF16), wo_ref[...], preferred_element_type=F32)
    y_ref[...] = x_ref[...] + _rms(a, g_ref[...])


def _oproj(of, x1f, w_o, g3):
    rows = of.shape[0]
    return pl.pallas_call(
        _oproj_kernel,
        grid=(1,),
        in_specs=[_const((rows, D_MODEL)), _const((rows, D_MODEL)), _const((D_MODEL, D_MODEL)),
                  _const((1, D_MODEL))],
        out_specs=pl.BlockSpec((rows, D_MODEL), lambda i: (0, 0)),
        out_shape=jax.ShapeDtypeStruct((rows, D_MODEL), F32),
        compiler_params=_params(1),
        name="oproj_sample",
    )(of, x1f, w_o, g3)


def _ffn_kernel(x_ref, wg_ref, wu_ref, wd_ref, gi_ref, go_ref, o_ref, *, L, ct):
    rows = L * ct
    x = x_ref[...].reshape(rows, D_MODEL)
    h = _rms(x, gi_ref[...]).astype(BF16)
    gate = jnp.dot(h, wg_ref[...], preferred_element_type=F32)
    up = jnp.dot(h, wu_ref[...], preferred_element_type=F32)
    act = (gate * jax.nn.sigmoid(gate) * up).astype(BF16)
    dn = jnp.dot(act, wd_ref[...], preferred_element_type=F32)
    y = x + _rms(dn, go_ref[...])
    for s in range(L):
        o_ref[:, s, :] = y[s * ct:(s + 1) * ct]


def _ffn(x2, w_gate, w_up, w_down, g4, g5, *, bq, C, ct):
    L = x2.shape[0]
    nc = C // ct
    return pl.pallas_call(
        functools.partial(_ffn_kernel, L=L, ct=ct),
        grid=(bq, nc),
        in_specs=[
            _tp_spec(ct, L, D_MODEL, nc),
            _const((D_MODEL, D_FF)),
            _const((D_MODEL, D_FF)),
            _const((D_FF, D_MODEL)),
            _const((1, D_MODEL)),
            _const((1, D_MODEL)),
        ],
        out_specs=_nat_spec(ct, L),
        out_shape=jax.ShapeDtypeStruct((bq, C, L, D_MODEL), F32),
        compiler_params=_params(2),
        name="ffn",
    )(x2, w_gate, w_up, w_down, g4, g5)


def _layer(x4, mem_k, mem_v, conv_state_t, h0_packed, wts, *, ct, sub, ffn_ct, sc, bt, sample):
    bq, C, L, _ = x4.shape
    R = bq * C
    ng = wts["norm_g"]
    g = [ng[i:i + 1] for i in range(6)]

    v, u4 = _inproj(x4, g[0], wts["w_in"], L=L, ct=ct, sub=sub)
    if sample:
        cact, conv_new = _conv_sample(v, conv_state_t, wts["w_dw"], wts["b_dw"], wts["ln_g"], wts["ln_b"])
    else:
        cact = _conv_prompt(v, wts["w_dw"], wts["b_dw"], wts["ln_g"], wts["ln_b"], L=L, C=C)
        tail = v.reshape(L, bq, C, D_CONV)[:, :, C - 2:, :]
        tail = jnp.transpose(tail, (1, 2, 0, 3)).reshape(bq, 2 * L, D_CONV)
        conv_new = tail[:, 2 * L - (CONV_WIDTH - 1):, :]

    sw = wts["ssm_L%d" % L]
    nb = R // (bt * sc)
    y, hl = _ssm(u4, sw[0], sw[1], sw[2], sw[3], wts["d_skip"], h0_packed.reshape(N_LANE_BLOCKS, nb, bt, STATE_W),
                 L=L, C=sc, bt=bt)
    hl = hl.reshape(N_LANE_BLOCKS, nb * bt, STATE_W)

    x1 = _mixout(cact, y, x4, wts["w_glu"], wts["w_out"], g[1], L=L, ct=ct, sub=sub)

    if sample:
        q = _qproj(x1.reshape(L * R, D_MODEL), wts["w_q"], g[2])
        o = _attn_sample(q.reshape(L, R, D_MODEL), mem_k, mem_v)
        x2 = _oproj(o.reshape(L * R, D_MODEL), x1.reshape(L * R, D_MODEL), wts["w_o"], g[3])
        x2 = x2.reshape(L, R, D_MODEL)
    else:
        x2 = _attn_prompt(x1, mem_k, mem_v, wts["w_q"], wts["w_o"], g[2], g[3], C=C, ct=ct, sub=sub)

    out = _ffn(x2, wts["w_gate"], wts["w_up"], wts["w_down"], g[4], g[5], bq=bq, C=C, ct=ffn_ct)
    return out, conv_new, hl


PROMPT_L = 16
PROMPT_SUB = 32
PROMPT_CT = 64


def kernel(x_prompt, x_sample, mem_prompt, cache_mem_k, cache_mem_v, state_conv, state_ssm_re, state_ssm_im,
           norm_g, mem_norm_g, w_in, w_dw, b_dw, ln_g, ln_b, lam_re, lam_im, log_dt, b_re, b_im, c_re, c_im,
           d_skip, w_glu, w_out, w_q, w_k, w_v, w_o, w_gate, w_up, w_down):
    depth = w_in.shape[0]
    bp, tp, _ = x_prompt.shape
    bs, ts, _ = x_sample.shape
    assert tp % (PROMPT_L * PROMPT_CT) == 0 and tp >= CONV_WIDTH - 1

    yp = x_prompt.reshape(bp, tp // PROMPT_L, PROMPT_L, D_MODEL)
    ys = x_sample.reshape(1, bs, ts, D_MODEL)
    outs = [[] for _ in range(8)]
    for l in range(depth):
        wts = {
            "norm_g": norm_g[l],
            "w_in": w_in[l].astype(BF16),
            "w_dw": w_dw[l],
            "b_dw": b_dw[l][None],
            "ln_g": ln_g[l][None],
            "ln_b": ln_b[l][None],
            "d_skip": d_skip[l].reshape(N_LANE_BLOCKS, 1, LANES),
            "w_glu": w_glu[l].astype(BF16),
            "w_out": w_out[l].astype(BF16),
            "w_q": w_q[l].astype(BF16),
            "w_o": w_o[l].astype(BF16),
            "w_gate": w_gate[l].astype(BF16),
            "w_up": w_up[l].astype(BF16),
            "w_down": w_down[l].astype(BF16),
        }
        ssm_args = (lam_re[l], lam_im[l], log_dt[l], b_re[l], b_im[l], c_re[l], c_im[l])
        wts["ssm_L%d" % PROMPT_L] = _ssm_weights(*ssm_args, PROMPT_L)
        wts["ssm_L%d" % ts] = _ssm_weights(*ssm_args, ts)

        kp, vp, kp16, vp16 = _memkv(mem_prompt, mem_norm_g[l][None], w_k[l].astype(BF16), w_v[l].astype(BF16))
        h0p = jnp.zeros((N_LANE_BLOCKS, bp, STATE_W), F32)
        yp, cp, hp = _layer(yp, kp16, vp16, None, h0p, wts, ct=PROMPT_CT, sub=PROMPT_SUB, ffn_ct=PROMPT_SUB,
                            sc=tp // PROMPT_L, bt=4, sample=False)
        h0s = _pack_state(state_ssm_re[l], state_ssm_im[l])
        ys, cs, hs = _layer(ys, cache_mem_k[l:l + 1], cache_mem_v[l:l + 1], jnp.transpose(state_conv[l], (1, 0, 2)),
                            h0s, wts, ct=bs, sub=bs, ffn_ct=bs, sc=1, bt=bs, sample=True)
        cs = jnp.transpose(cs, (1, 0, 2))

        hp_re, hp_im = _unpack_state(hp)
        hs_re, hs_im = _unpack_state(hs)
        for lst, val in zip(outs, (kp, vp, cp, hp_re, hp_im, cs, hs_re, hs_im)):
            lst.append(val)
    return (yp.reshape(bp, tp, D_MODEL), ys.reshape(bs, ts, D_MODEL)) + tuple(jnp.stack(o) for o in outs)
```

```python
import functools

import jax
import jax.numpy as jnp
from jax import lax
from jax.experimental import pallas as pl
from jax.experimental.pallas import tpu as pltpu

F32 = jnp.float32
BF16 = jnp.bfloat16

D_MODEL = 1024
D_CONV = 512
D_SSM = 512
CONV_WIDTH = 31
N_GROUPS = 32
P_GROUP = 16
N_STATE = 64
N_MEM = 256
N_HEADS = 4
HEAD_DIM = 256
D_FF = 2816
RMS_EPS = 1e-6
LN_EPS = 1e-5

LANES = 128
SUBLANES = 8
GROUPS_PER_BLOCK = LANES // P_GROUP
N_LANE_BLOCKS = D_SSM // LANES
STATE_HALF = GROUPS_PER_BLOCK * N_STATE
STATE_W = 2 * STATE_HALF
MXU_N = 256
VMEM_LIMIT = 56 * 1024 * 1024


def _params(n_axes, vmem=VMEM_LIMIT):
    return pltpu.CompilerParams(dimension_semantics=("arbitrary",) * n_axes, vmem_limit_bytes=vmem)


def _const(shape):
    nd = len(shape)
    return pl.BlockSpec(shape, lambda *_: (0,) * nd, pipeline_mode=pl.Buffered(1))


def _rms(x, g):
    return x * lax.rsqrt(jnp.mean(x * x, axis=-1, keepdims=True) + RMS_EPS) * g


def _cast_weights_once(n_axes, *pairs):
    first = pl.program_id(0) == 0
    for a in range(1, n_axes):
        first = jnp.logical_and(first, pl.program_id(a) == 0)

    @pl.when(first)
    def _():
        for src, dst in pairs:
            dst[...] = src[...].astype(BF16)


def _gather_rows(x_ref, L, c0, n):
    return jnp.concatenate([x_ref[c0:c0 + n, s, :] for s in range(L)], axis=0)


def _sub_tiles(ct, sub):
    return [(c0, sub) for c0 in range(0, ct, sub)]


def _nat_spec(ct, L):
    return pl.BlockSpec((None, ct, L, D_MODEL), lambda b, i: (b, i, 0, 0))


def _tp_spec(ct, L, width, nc):
    return pl.BlockSpec((L, ct, width), lambda b, i: (0, b * nc + i, 0))


def _inproj_kernel(x_ref, g_ref, w_ref, v_ref, u_ref, w16_scr, *, L, ct, sub):
    _cast_weights_once(2, (w_ref, w16_scr))
    for c0, n in _sub_tiles(ct, sub):
        x = _gather_rows(x_ref, L, c0, n)
        h = _rms(x, g_ref[...])
        z = jnp.dot(h.astype(BF16), w16_scr[...], preferred_element_type=F32)
        a = z[:, :D_CONV]
        g = z[:, D_CONV:2 * D_CONV]
        u = z[:, 2 * D_CONV:].astype(BF16)
        v_ref[:, c0:c0 + n, :] = (a * jax.nn.sigmoid(g)).reshape(L, n, D_CONV)
        for s in range(L):
            for j in range(N_LANE_BLOCKS):
                u_ref[j, c0:c0 + n, s * LANES:(s + 1) * LANES] = u[s * n:(s + 1) * n, j * LANES:(j + 1) * LANES]


def _inproj(x4, g0, w_in, *, L, ct, sub):
    bq, c = x4.shape[:2]
    nc = c // ct
    R = bq * c
    return pl.pallas_call(
        functools.partial(_inproj_kernel, L=L, ct=ct, sub=sub),
        grid=(bq, nc),
        in_specs=[_nat_spec(ct, L), _const((1, D_MODEL)), _const((D_MODEL, 2 * D_CONV + D_SSM))],
        out_specs=[
            _tp_spec(ct, L, D_CONV, nc),
            pl.BlockSpec((N_LANE_BLOCKS, ct, L * LANES), lambda b, i: (0, b * nc + i, 0)),
        ],
        out_shape=[
            jax.ShapeDtypeStruct((L, R, D_CONV), F32),
            jax.ShapeDtypeStruct((N_LANE_BLOCKS, R, L * LANES), BF16),
        ],
        scratch_shapes=[pltpu.VMEM(w_in.shape, BF16)],
        compiler_params=_params(2),
        name="inproj",
    )(x4, g0, w_in)


def _ln_silu(acc, g, b):
    mu = jnp.mean(acc, axis=-1, keepdims=True)
    xc = acc - mu
    var = jnp.mean(xc * xc, axis=-1, keepdims=True)
    y = xc * lax.rsqrt(var + LN_EPS) * g + b
    return y * jax.nn.sigmoid(y)


def _conv_shifted_copies(v_ref, v1_ref, v2_ref, *, L, C):
    row = lax.broadcasted_iota(jnp.int32, (C, D_CONV), 0)
    for s in range(L):
        x = v_ref[s]
        v1_ref[s] = jnp.where(row >= 1, pltpu.roll(x, 1, 0), 0.0)
        v2_ref[s] = jnp.where(row >= 2, pltpu.roll(x, 2, 0), 0.0)


def _conv_rows(srcs, w_ref, b_ref, g_ref, bb_ref, o_ref, r0, *, L, rc):
    sub8 = rc // SUBLANES
    for s in range(L):
        acc = jnp.broadcast_to(b_ref[...][None], (sub8, SUBLANES, D_CONV))
        for d in range(CONV_WIDTH):
            blk = (s - d) % L
            shift = (d - s + L - 1) // L if d > s else 0
            k = CONV_WIDTH - 1 - d
            src = srcs[shift][blk, pl.ds(r0, rc), :].reshape(sub8, SUBLANES, D_CONV)
            acc = acc + w_ref[k][None] * src
        y = _ln_silu(acc.reshape(rc, D_CONV), g_ref[...], bb_ref[...])
        o_ref[s, pl.ds(r0, rc), :] = y.astype(o_ref.dtype)


def _conv_prompt_kernel(v_ref, w_ref, b_ref, g_ref, bb_ref, o_ref, v1_ref, v2_ref, *, L, C, rc):
    _conv_shifted_copies(v_ref, v1_ref, v2_ref, L=L, C=C)
    srcs = (v_ref, v1_ref, v2_ref)

    def chunk(i, carry):
        _conv_rows(srcs, w_ref, b_ref, g_ref, bb_ref, o_ref, pl.multiple_of(i * rc, rc), L=L, rc=rc)
        return carry

    lax.fori_loop(0, C // rc, chunk, 0)


def _conv_prompt(v, w_dw, b_dw, ln_g, ln_b, *, L, C):
    R = v.shape[1]
    rc = 32
    return pl.pallas_call(
        functools.partial(_conv_prompt_kernel, L=L, C=C, rc=rc),
        grid=(R // C,),
        in_specs=[
            pl.BlockSpec((L, C, D_CONV), lambda b: (0, b, 0)),
            _const((CONV_WIDTH, SUBLANES, D_CONV)),
            _const((SUBLANES, D_CONV)),
            _const((1, D_CONV)),
            _const((1, D_CONV)),
        ],
        out_specs=pl.BlockSpec((L, C, D_CONV), lambda b: (0, b, 0)),
        out_shape=jax.ShapeDtypeStruct((L, R, D_CONV), BF16),
        scratch_shapes=[pltpu.VMEM((L, C, D_CONV), F32)] * 2,
        compiler_params=_params(1),
        name="conv_prompt",
    )(v, jnp.broadcast_to(w_dw[:, None, :], (CONV_WIDTH, SUBLANES, D_CONV)),
      jnp.broadcast_to(b_dw, (SUBLANES, D_CONV)), ln_g, ln_b)


def _conv_sample_kernel(v_ref, st_ref, w_ref, b_ref, g_ref, bb_ref, o_ref, new_ref, *, L, H):
    def ext(i):
        return st_ref[i] if i < H else v_ref[i - H]

    for t in range(L):
        acc = jnp.broadcast_to(b_ref[...], v_ref.shape[1:])
        for k in range(CONV_WIDTH):
            acc = acc + w_ref[k:k + 1, :] * ext(t + k)
        o_ref[t] = _ln_silu(acc, g_ref[...], bb_ref[...]).astype(o_ref.dtype)
    for i in range(H):
        new_ref[i] = ext(i + L)


def _conv_sample(v, state_t, w_dw, b_dw, ln_g, ln_b):
    L, R, _ = v.shape
    H = CONV_WIDTH - 1
    return pl.pallas_call(
        functools.partial(_conv_sample_kernel, L=L, H=H),
        grid=(1,),
        in_specs=[
            _const((L, R, D_CONV)),
            _const((H, R, D_CONV)),
            _const((CONV_WIDTH, D_CONV)),
            _const((1, D_CONV)),
            _const((1, D_CONV)),
            _const((1, D_CONV)),
        ],
        out_specs=[
            pl.BlockSpec((L, R, D_CONV), lambda i: (0, 0, 0)),
            pl.BlockSpec((H, R, D_CONV), lambda i: (0, 0, 0)),
        ],
        out_shape=[
            jax.ShapeDtypeStruct((L, R, D_CONV), BF16),
            jax.ShapeDtypeStruct((H, R, D_CONV), F32),
        ],
        compiler_params=_params(1),
        name="conv_sample",
    )(v, state_t, w_dw, b_dw, ln_g, ln_b)


def _expand_block_diag(d2):
    tiled = jnp.concatenate([d2] * (STATE_HALF // LANES), axis=-1)
    r = lax.broadcasted_iota(jnp.int32, tiled.shape, 0) // P_GROUP
    c = lax.broadcasted_iota(jnp.int32, tiled.shape, 1) // N_STATE
    return jnp.where(r == c, tiled, jnp.zeros_like(tiled))


def _ssm_kernel(u_ref, taps_ref, wbc_ref, wcc_ref, a_ref, d_ref, h0_ref, y_ref, hl_ref,
                w_scr, wb_scr, wct_scr, s_scr, hp_scr, *, L, C, bt):
    per_tile = MXU_N // LANES

    @pl.when(pl.program_id(1) == 0)
    def _():
        for sp in range(L):
            for s in range(L):
                rows = slice(s * LANES, (s + 1) * LANES)
                cols = slice(sp * LANES, (sp + 1) * LANES)
                if s <= sp:
                    w_scr[rows, cols] = taps_ref[sp - s]
                elif s // per_tile == sp // per_tile:
                    w_scr[rows, cols] = jnp.zeros((LANES, LANES), BF16)
        for s in range(L):
            rows = slice(s * LANES, (s + 1) * LANES)
            for a in range(2):
                cols = slice(a * STATE_HALF, (a + 1) * STATE_HALF)
                wb_scr[rows, cols] = _expand_block_diag(wbc_ref[s, a])
                wct_scr[rows, cols] = _expand_block_diag(wcc_ref[s, a])

    nk = STATE_W // LANES
    half = nk // 2
    inc_all = jnp.dot(u_ref[...], wb_scr[...], preferred_element_type=F32)
    for k in range(nk):
        s_scr[k] = inc_all[:, k * LANES:(k + 1) * LANES]

    d = d_ref[...]
    for n in range(L // per_tile):
        k = (n + 1) * MXU_N
        y = jnp.dot(u_ref[:, :k], w_scr[:k, n * MXU_N:(n + 1) * MXU_N], preferred_element_type=F32)
        for h in range(per_tile):
            s = n * per_tile + h
            us = u_ref[:, s * LANES:(s + 1) * LANES].astype(F32)
            y_ref[s] = y[:, h * LANES:(h + 1) * LANES] + d * us

    h0 = h0_ref[...]
    a = a_ref[...]

    def piece(x, k):
        return x[:, k * LANES:(k + 1) * LANES]

    ar = [piece(a[0:1], k) for k in range(half)]
    ai = [piece(a[1:2], k) for k in range(half)]
    state = [piece(h0, k) for k in range(nk)]
    for c in range(C):
        rows = pl.ds(c, bt, stride=C) if C > 1 else pl.ds(0, bt)
        nxt = []
        for k in range(half):
            hr, hi = state[k], state[half + k]
            hp_scr[k, rows, :] = hr
            hp_scr[half + k, rows, :] = hi
            nxt.append(ar[k] * hr - ai[k] * hi + s_scr[k, rows, :])
        for k in range(half):
            hr, hi = state[k], state[half + k]
            nxt.append(ar[k] * hi + ai[k] * hr + s_scr[half + k, rows, :])
        state = nxt
    for k in range(nk):
        hl_ref[:, k * LANES:(k + 1) * LANES] = state[k]

    hp = jnp.concatenate([hp_scr[k] for k in range(nk)], axis=-1).astype(BF16)
    for n in range(L // per_tile):
        y = lax.dot_general(hp, wct_scr[n * MXU_N:(n + 1) * MXU_N, :], (((1,), (1,)), ((), ())),
                            preferred_element_type=F32)
        for h in range(per_tile):
            s = n * per_tile + h
            y_ref[s] += y[:, h * LANES:(h + 1) * LANES]


def _ssm(u4, taps, wbc, wcc, a_pow, d_skip, h0, *, L, C, bt):
    R = u4.shape[1]
    rows = bt * C
    nb = R // rows
    lw = L * LANES
    in_specs = [
        pl.BlockSpec((None, rows, lw), lambda j, b: (j, b, 0)),
        pl.BlockSpec((None, L, LANES, LANES), lambda j, b: (j, 0, 0, 0)),
        pl.BlockSpec((None, L, 2, LANES, LANES), lambda j, b: (j, 0, 0, 0, 0)),
        pl.BlockSpec((None, L, 2, LANES, LANES), lambda j, b: (j, 0, 0, 0, 0)),
        pl.BlockSpec((None, 2, STATE_HALF), lambda j, b: (j, 0, 0)),
        pl.BlockSpec((None, 1, LANES), lambda j, b: (j, 0, 0)),
        pl.BlockSpec((None, None, bt, STATE_W), lambda j, b: (j, b, 0, 0)),
    ]
    out_specs = [
        pl.BlockSpec((L, rows, LANES), lambda j, b: (0, b, j)),
        pl.BlockSpec((None, None, bt, STATE_W), lambda j, b: (j, b, 0, 0)),
    ]
    out_shape = [
        jax.ShapeDtypeStruct((L, R, D_SSM), F32),
        jax.ShapeDtypeStruct((N_LANE_BLOCKS, nb, bt, STATE_W), F32),
    ]
    scratch = [
        pltpu.VMEM((lw, lw), BF16),
        pltpu.VMEM((lw, STATE_W), BF16),
        pltpu.VMEM((lw, STATE_W), BF16),
        pltpu.VMEM((STATE_W // LANES, rows, LANES), F32),
        pltpu.VMEM((STATE_W // LANES, rows, LANES), F32),
    ]
    return pl.pallas_call(
        functools.partial(_ssm_kernel, L=L, C=C, bt=bt),
        grid=(N_LANE_BLOCKS, nb),
        in_specs=in_specs,
        out_specs=out_specs,
        out_shape=out_shape,
        scratch_shapes=scratch,
        compiler_params=_params(2),
        name="ssm",
    )(u4, taps, wbc, wcc, a_pow, d_skip, h0)


def _ssm_weights(lam_re, lam_im, log_dt, b_re, b_im, c_re, c_im, L):
    hp = lax.Precision.HIGHEST
    dt = jnp.exp(log_dt)[:, None]
    zr = lam_re * dt
    zi = lam_im * dt
    n_pow = jnp.arange(L + 1, dtype=F32)[:, None, None]
    mag = jnp.exp(zr[None] * n_pow)
    pr = mag * jnp.cos(zi[None] * n_pow)
    pi = mag * jnp.sin(zi[None] * n_pow)
    a1r, a1i = pr[1], pi[1]
    den = lam_re * lam_re + lam_im * lam_im
    qr = ((a1r - 1.0) * lam_re + a1i * lam_im) / den
    qi = (a1i * lam_re - (a1r - 1.0) * lam_im) / den
    bbr = qr[:, :, None] * b_re - qi[:, :, None] * b_im
    bbi = qr[:, :, None] * b_im + qi[:, :, None] * b_re

    nj, g8 = N_LANE_BLOCKS, GROUPS_PER_BLOCK
    eye = jnp.eye(g8, dtype=F32)

    car = c_re[None] * pr[:L, :, None, :] - c_im[None] * pi[:L, :, None, :]
    cai = c_re[None] * pi[:L, :, None, :] + c_im[None] * pr[:L, :, None, :]
    taps = (jnp.einsum("tgqn,gnp->gtqp", car, bbr, precision=hp)
            - jnp.einsum("tgqn,gnp->gtqp", cai, bbi, precision=hp))
    taps = taps.reshape(nj, g8, L, P_GROUP, P_GROUP)
    taps = jnp.einsum("jgtqp,gh->jtgphq", taps, eye).reshape(nj, L, LANES, LANES)

    n_rev = (L - 1.0) - jnp.arange(L, dtype=F32)[:, None, None]
    mag_rev = jnp.exp(zr[None] * n_rev)
    rev = mag_rev * jnp.cos(zi[None] * n_rev), mag_rev * jnp.sin(zi[None] * n_rev)
    er = rev[0][:, :, :, None] * bbr[None] - rev[1][:, :, :, None] * bbi[None]
    ei = rev[0][:, :, :, None] * bbi[None] + rev[1][:, :, :, None] * bbr[None]
    e = jnp.stack([er, ei], 1).reshape(L, 2, nj, g8, N_STATE, P_GROUP)
    e = jnp.transpose(e, (2, 0, 1, 3, 5, 4)).reshape(nj, L, 2, LANES, N_STATE)
    wbc = jnp.concatenate([e, e], axis=-1)

    mr = c_re[None] * pr[1:, :, None, :] - c_im[None] * pi[1:, :, None, :]
    mi = c_re[None] * pi[1:, :, None, :] + c_im[None] * pr[1:, :, None, :]
    m = jnp.stack([mr, -mi], 1).reshape(L, 2, nj, g8, P_GROUP, N_STATE)
    m = jnp.transpose(m, (2, 0, 1, 3, 4, 5)).reshape(nj, L, 2, LANES, N_STATE)
    wcc = jnp.concatenate([m, m], axis=-1)

    def a_pow(n):
        return jnp.stack([pr[n].reshape(nj, STATE_HALF), pi[n].reshape(nj, STATE_HALF)], 1)

    return taps.astype(BF16), wbc.astype(BF16), wcc.astype(BF16), a_pow


def _ssm_weights_prefix(full, L_full, L):
    taps, wbc, wcc, a_pow = full
    return taps[:, :L], wbc[:, L_full - L:], wcc[:, :L], a_pow(L)


def _pack_state(re, im):
    b = re.shape[0]
    r = re.reshape(b, N_LANE_BLOCKS, STATE_HALF)
    i = im.reshape(b, N_LANE_BLOCKS, STATE_HALF)
    return jnp.transpose(jnp.concatenate([r, i], -1), (1, 0, 2))


def _unpack_state(h):
    b = h.shape[1]
    h = jnp.transpose(h, (1, 0, 2))
    re = h[:, :, :STATE_HALF].reshape(b, N_GROUPS, N_STATE)
    im = h[:, :, STATE_HALF:].reshape(b, N_GROUPS, N_STATE)
    return re, im


def _mixout_kernel(c_ref, y_ref, x_ref, wglu_ref, wout_ref, g_ref, o_ref, wglu_scr, wout_scr, *, L, ct, sub):
    _cast_weights_once(2, (wglu_ref, wglu_scr), (wout_ref, wout_scr))
    for c0, n in _sub_tiles(ct, sub):
        rows = L * n
        gy = jax.nn.gelu(y_ref[:, c0:c0 + n, :].reshape(rows, D_SSM))
        sg = gy * jax.nn.sigmoid(jnp.dot(gy.astype(BF16), wglu_scr[...], preferred_element_type=F32))
        m = jnp.dot(c_ref[:, c0:c0 + n, :].reshape(rows, D_CONV), wout_scr[0:D_CONV, :],
                    preferred_element_type=F32)
        m = m + jnp.dot(sg.astype(BF16), wout_scr[D_CONV:, :], preferred_element_type=F32)
        x = _gather_rows(x_ref, L, c0, n)
        o_ref[:, c0:c0 + n, :] = (x + _rms(m, g_ref[...])).reshape(L, n, D_MODEL)


def _mixout(c, y, x4, w_glu, w_out, g1, *, L, ct, sub):
    bq, cc = x4.shape[:2]
    nc = cc // ct
    R = bq * cc
    return pl.pallas_call(
        functools.partial(_mixout_kernel, L=L, ct=ct, sub=sub),
        grid=(bq, nc),
        in_specs=[
            _tp_spec(ct, L, D_CONV, nc),
            _tp_spec(ct, L, D_SSM, nc),
            _nat_spec(ct, L),
            _const((D_SSM, D_SSM)),
            _const((D_CONV + D_SSM, D_MODEL)),
            _const((1, D_MODEL)),
        ],
        out_specs=_tp_spec(ct, L, D_MODEL, nc),
        out_shape=jax.ShapeDtypeStruct((L, R, D_MODEL), F32),
        scratch_shapes=[pltpu.VMEM(w_glu.shape, BF16), pltpu.VMEM(w_out.shape, BF16)],
        compiler_params=_params(2),
        name="mixout",
    )(c, y, x4, w_glu, w_out, g1)


def _memkv_kernel(m_ref, g_ref, wk_ref, wv_ref, k5_ref, v5_ref, kb_ref, vb_ref, wk_scr, wv_scr, *, nb):
    _cast_weights_once(1, (wk_ref, wk_scr), (wv_ref, wv_scr))
    for b in range(nb):
        m = _rms(m_ref[b], g_ref[...]).astype(BF16)
        for w_ref, o5_ref, ob_ref in ((wk_scr, k5_ref, kb_ref), (wv_scr, v5_ref, vb_ref)):
            p = jnp.dot(m, w_ref[...], preferred_element_type=F32)
            ob_ref[b] = p.astype(BF16)
            for hd in range(N_HEADS):
                o5_ref[b, :, hd, :] = p[:, hd * HEAD_DIM:(hd + 1) * HEAD_DIM]


def _memkv(mem, g_mem, w_k, w_v):
    bq = mem.shape[0]
    nb = 2
    out5 = pl.BlockSpec((nb, N_MEM, N_HEADS, HEAD_DIM), lambda i: (i, 0, 0, 0))
    outb = pl.BlockSpec((nb, N_MEM, D_MODEL), lambda i: (i, 0, 0))
    return pl.pallas_call(
        functools.partial(_memkv_kernel, nb=nb),
        grid=(bq // nb,),
        in_specs=[
            pl.BlockSpec((nb, N_MEM, D_MODEL), lambda i: (i, 0, 0)),
            _const((1, D_MODEL)),
            _const((D_MODEL, D_MODEL)),
            _const((D_MODEL, D_MODEL)),
        ],
        out_specs=[out5, out5, outb, outb],
        out_shape=[jax.ShapeDtypeStruct((bq, N_MEM, N_HEADS, HEAD_DIM), F32)] * 2
        + [jax.ShapeDtypeStruct((bq, N_MEM, D_MODEL), BF16)] * 2,
        scratch_shapes=[pltpu.VMEM(w_k.shape, BF16), pltpu.VMEM(w_v.shape, BF16)],
        compiler_params=_params(1),
        name="memkv",
    )(mem, g_mem, w_k, w_v)


def _softmax_rows(s):
    s = s - jnp.max(s, axis=-1, keepdims=True)
    e = jnp.exp(s)
    return e / jnp.sum(e, axis=-1, keepdims=True)


def _attn_prompt_kernel(x_ref, k_ref, v_ref, wq32_ref, wo32_ref, gq_ref, go_ref, o_ref, wq_ref, wo_ref,
                        *, L, ct, sub):
    _cast_weights_once(2, (wq32_ref, wq_ref), (wo32_ref, wo_ref))
    kb = k_ref[...]
    vb = v_ref[...]
    for c0, n in _sub_tiles(ct, sub):
        rows = L * n
        x = x_ref[:, c0:c0 + n, :].reshape(rows, D_MODEL)
        h = _rms(x, gq_ref[...])
        q = jnp.dot(h.astype(BF16), wq_ref[...], preferred_element_type=F32).astype(BF16)
        heads = []
        for hd in range(N_HEADS):
            sl = slice(hd * HEAD_DIM, (hd + 1) * HEAD_DIM)
            sc = lax.dot_general(q[:, sl], kb[:, sl], (((1,), (1,)), ((), ())), preferred_element_type=F32)
            p = _softmax_rows(sc * (HEAD_DIM ** -0.5))
            heads.append(jnp.dot(p.astype(BF16), vb[:, sl], preferred_element_type=F32))
        o = jnp.concatenate(heads, axis=-1)
        a = jnp.dot(o.astype(BF16), wo_ref[...], preferred_element_type=F32)
        o_ref[:, c0:c0 + n, :] = (x + _rms(a, go_ref[...])).reshape(L, n, D_MODEL)


def _attn_prompt(x1, k, v, w_q, w_o, g2, g3, *, C, ct, sub):
    L, R, _ = x1.shape
    nc = C // ct
    return pl.pallas_call(
        functools.partial(_attn_prompt_kernel, L=L, ct=ct, sub=sub),
        grid=(R // C, nc),
        in_specs=[
            _tp_spec(ct, L, D_MODEL, nc),
            pl.BlockSpec((None, N_MEM, D_MODEL), lambda b, i: (b, 0, 0)),
            pl.BlockSpec((None, N_MEM, D_MODEL), lambda b, i: (b, 0, 0)),
            _const((D_MODEL, D_MODEL)),
            _const((D_MODEL, D_MODEL)),
            _const((1, D_MODEL)),
            _const((1, D_MODEL)),
        ],
        out_specs=_tp_spec(ct, L, D_MODEL, nc),
        out_shape=jax.ShapeDtypeStruct((L, R, D_MODEL), F32),
        scratch_shapes=[pltpu.VMEM(w_q.shape, BF16), pltpu.VMEM(w_o.shape, BF16)],
        compiler_params=_params(2),
        name="attn_prompt",
    )(x1, k, v, w_q, w_o, g2, g3)


def _qproj_kernel(x_ref, wq_ref, g_ref, q_ref):
    h = _rms(x_ref[...], g_ref[...])
    q_ref[...] = jnp.dot(h.astype(BF16), wq_ref[...].astype(BF16), preferred_element_type=F32)


def _qproj(x1f, w_q, g2):
    rows = x1f.shape[0]
    return pl.pallas_call(
        _qproj_kernel,
        grid=(1,),
        in_specs=[_const((rows, D_MODEL)), _const((D_MODEL, D_MODEL)), _const((1, D_MODEL))],
        out_specs=pl.BlockSpec((rows, D_MODEL), lambda i: (0, 0)),
        out_shape=jax.ShapeDtypeStruct((rows, D_MODEL), F32),
        compiler_params=_params(1),
        name="qproj_sample",
    )(x1f, w_q, g2)


def _attn_sample_kernel(q_ref, k_ref, v_ref, o_ref, *, L, bb):
    rows = L * bb
    q2 = q_ref[...].reshape(rows, D_MODEL)
    qs = jnp.concatenate([q2[:, h * HEAD_DIM:(h + 1) * HEAD_DIM] for h in range(N_HEADS)], axis=0).astype(BF16)
    nr = N_HEADS * rows
    nc = N_MEM * N_HEADS
    row_head = lax.broadcasted_iota(jnp.int32, (nr, nc), 0) // rows
    col_head = lax.broadcasted_iota(jnp.int32, (nr, nc), 1) % N_HEADS
    same_head = row_head == col_head
    owner = lax.broadcasted_iota(jnp.int32, (nr, HEAD_DIM), 0) % bb
    acc = jnp.zeros((nr, HEAD_DIM), F32)
    for b in range(bb):
        ka = k_ref[b].reshape(nc, HEAD_DIM).astype(BF16)
        va = v_ref[b].reshape(nc, HEAD_DIM).astype(BF16)
        sc = lax.dot_general(qs, ka, (((1,), (1,)), ((), ())), preferred_element_type=F32)
        p = _softmax_rows(jnp.where(same_head, sc * (HEAD_DIM ** -0.5), -1e30))
        o = jnp.dot(p.astype(BF16), va, preferred_element_type=F32)
        acc = jnp.where(owner == b, o, acc)
    for h in range(N_HEADS):
        o_ref[:, :, h * HEAD_DIM:(h + 1) * HEAD_DIM] = acc[h * rows:(h + 1) * rows].reshape(L, bb, HEAD_DIM)


def _attn_sample(q3, k5, v5):
    L, R, _ = q3.shape
    bb = 8
    kv_spec = pl.BlockSpec((None, bb, N_MEM, N_HEADS, HEAD_DIM), lambda i: (0, i, 0, 0, 0))
    return pl.pallas_call(
        functools.partial(_attn_sample_kernel, L=L, bb=bb),
        grid=(R // bb,),
        in_specs=[pl.BlockSpec((L, bb, D_MODEL), lambda i: (0, i, 0)), kv_spec, kv_spec],
        out_specs=pl.BlockSpec((L, bb, D_MODEL), lambda i: (0, i, 0)),
        out_shape=jax.ShapeDtypeStruct((L, R, D_MODEL), F32),
        compiler_params=_params(1),
        name="attn_sample",
    )(q3, k5, v5)


def _oproj_kernel(o_ref, x_ref, wo_ref, g_ref, y_ref):
    a = jnp.dot(o_ref[...].astype(BF16), wo_ref[...].astype(BF16), preferred_element_type=F32)
    y_ref[...] = x_ref[...] + _rms(a, g_ref[...])


def _oproj(of, x1f, w_o, g3):
    rows = of.shape[0]
    return pl.pallas_call(
        _oproj_kernel,
        grid=(1,),
        in_specs=[_const((rows, D_MODEL)), _const((rows, D_MODEL)), _const((D_MODEL, D_MODEL)),
                  _const((1, D_MODEL))],
        out_specs=pl.BlockSpec((rows, D_MODEL), lambda i: (0, 0)),
        out_shape=jax.ShapeDtypeStruct((rows, D_MODEL), F32),
        compiler_params=_params(1),
        name="oproj_sample",
    )(of, x1f, w_o, g3)


def _ffn_kernel(x_ref, wg_ref, wu_ref, wd_ref, gi_ref, go_ref, o_ref, *, L, ct):
    rows = L * ct
    x = x_ref[...].reshape(rows, D_MODEL)
    h = _rms(x, gi_ref[...]).astype(BF16)
    gate = jnp.dot(h, wg_ref[...], preferred_element_type=F32)
    up = jnp.dot(h, wu_ref[...], preferred_element_type=F32)
    act = (gate * jax.nn.sigmoid(gate) * up).astype(BF16)
    dn = jnp.dot(act, wd_ref[...], preferred_element_type=F32)
    y = x + _rms(dn, go_ref[...])
    for s in range(L):
        o_ref[:, s, :] = y[s * ct:(s + 1) * ct]


def _ffn(x2, w_gate, w_up, w_down, g4, g5, *, bq, C, ct):
    L = x2.shape[0]
    nc = C // ct
    return pl.pallas_call(
        functools.partial(_ffn_kernel, L=L, ct=ct),
        grid=(bq, nc),
        in_specs=[
            _tp_spec(ct, L, D_MODEL, nc),
            _const((D_MODEL, D_FF)),
            _const((D_MODEL, D_FF)),
            _const((D_FF, D_MODEL)),
            _const((1, D_MODEL)),
            _const((1, D_MODEL)),
        ],
        out_specs=_nat_spec(ct, L),
        out_shape=jax.ShapeDtypeStruct((bq, C, L, D_MODEL), F32),
        compiler_params=_params(2),
        name="ffn",
    )(x2, w_gate, w_up, w_down, g4, g5)


def _layer(x4, mem_k, mem_v, conv_state_t, h0_packed, wts, *, ct, sub, ffn_ct, sc, bt, sample):
    bq, C, L, _ = x4.shape
    R = bq * C
    ng = wts["norm_g"]
    g = [ng[i:i + 1] for i in range(6)]

    v, u4 = _inproj(x4, g[0], wts["w_in"], L=L, ct=ct, sub=sub)
    sw = wts["ssm_L%d" % L]
    nb = R // (bt * sc)
    ssm_args = (u4, sw[0], sw[1], sw[2], sw[3], wts["d_skip"], h0_packed.reshape(N_LANE_BLOCKS, nb, bt, STATE_W))
    conv_args = (wts["w_dw"], wts["b_dw"], wts["ln_g"], wts["ln_b"])
    y, hl = _ssm(*ssm_args, L=L, C=sc, bt=bt)
    if sample:
        cact, conv_new = _conv_sample(v, conv_state_t, *conv_args)
    else:
        cact = _conv_prompt(v, *conv_args, L=L, C=C)
        tail = v.reshape(L, bq, C, D_CONV)[:, :, C - 2:, :]
        tail = jnp.transpose(tail, (1, 2, 0, 3)).reshape(bq, 2 * L, D_CONV)
        conv_new = tail[:, 2 * L - (CONV_WIDTH - 1):, :]
    hl = hl.reshape(N_LANE_BLOCKS, nb * bt, STATE_W)

    x1 = _mixout(cact, y, x4, wts["w_glu"], wts["w_out"], g[1], L=L, ct=ct, sub=sub)

    if sample:
        q = _qproj(x1.reshape(L * R, D_MODEL), wts["w_q"], g[2])
        o = _attn_sample(q.reshape(L, R, D_MODEL), mem_k, mem_v)
        x2 = _oproj(o.reshape(L * R, D_MODEL), x1.reshape(L * R, D_MODEL), wts["w_o"], g[3])
        x2 = x2.reshape(L, R, D_MODEL)
    else:
        x2 = _attn_prompt(x1, mem_k, mem_v, wts["w_q"], wts["w_o"], g[2], g[3], C=C, ct=ct, sub=sub)

    out = _ffn(x2, wts["w_gate"], wts["w_up"], wts["w_down"], g[4], g[5], bq=bq, C=C, ct=ffn_ct)
    return out, conv_new, hl


PROMPT_L = 16
PROMPT_SUB = 32
PROMPT_CT = 64


def kernel(x_prompt, x_sample, mem_prompt, cache_mem_k, cache_mem_v, state_conv, state_ssm_re, state_ssm_im,
           norm_g, mem_norm_g, w_in, w_dw, b_dw, ln_g, ln_b, lam_re, lam_im, log_dt, b_re, b_im, c_re, c_im,
           d_skip, w_glu, w_out, w_q, w_k, w_v, w_o, w_gate, w_up, w_down):
    depth = w_in.shape[0]
    bp, tp, _ = x_prompt.shape
    bs, ts, _ = x_sample.shape
    assert tp % (PROMPT_L * PROMPT_CT) == 0 and tp >= CONV_WIDTH - 1

    yp = x_prompt.reshape(bp, tp // PROMPT_L, PROMPT_L, D_MODEL)
    ys = x_sample.reshape(1, bs, ts, D_MODEL)
    outs = [[] for _ in range(8)]
    for l in range(depth):
        wts = {
            "norm_g": norm_g[l],
            "w_in": w_in[l],
            "w_dw": w_dw[l],
            "b_dw": b_dw[l][None],
            "ln_g": ln_g[l][None],
            "ln_b": ln_b[l][None],
            "d_skip": d_skip[l].reshape(N_LANE_BLOCKS, 1, LANES),
            "w_glu": w_glu[l],
            "w_out": w_out[l],
            "w_q": w_q[l],
            "w_o": w_o[l],
            "w_gate": w_gate[l].astype(BF16),
            "w_up": w_up[l].astype(BF16),
            "w_down": w_down[l].astype(BF16),
        }
        ssm_args = (lam_re[l], lam_im[l], log_dt[l], b_re[l], b_im[l], c_re[l], c_im[l])
        assert ts <= PROMPT_L
        full = _ssm_weights(*ssm_args, PROMPT_L)
        wts["ssm_L%d" % PROMPT_L] = full[:3] + (full[3](PROMPT_L),)
        wts["ssm_L%d" % ts] = _ssm_weights_prefix(full, PROMPT_L, ts)

        kp, vp, kp16, vp16 = _memkv(mem_prompt, mem_norm_g[l][None], w_k[l], w_v[l])
        h0p = jnp.zeros((N_LANE_BLOCKS, bp, STATE_W), F32)
        yp, cp, hp = _layer(yp, kp16, vp16, None, h0p, wts, ct=PROMPT_CT, sub=PROMPT_SUB, ffn_ct=PROMPT_SUB,
                            sc=tp // PROMPT_L, bt=4, sample=False)
        h0s = _pack_state(state_ssm_re[l], state_ssm_im[l])
        ys, cs, hs = _layer(ys, cache_mem_k[l:l + 1], cache_mem_v[l:l + 1], jnp.transpose(state_conv[l], (1, 0, 2)),
                            h0s, wts, ct=bs, sub=bs, ffn_ct=bs, sc=1, bt=bs, sample=True)
        cs = jnp.transpose(cs, (1, 0, 2))

        hp_re, hp_im = _unpack_state(hp)
        hs_re, hs_im = _unpack_state(hs)
        for lst, val in zip(outs, (kp, vp, cp, hp_re, hp_im, cs, hs_re, hs_im)):
            lst.append(val)
    return (yp.reshape(bp, tp, D_MODEL), ys.reshape(bs, ts, D_MODEL)) + tuple(jnp.stack(o) for o in outs)
```

```python
import functools

import jax
import jax.numpy as jnp
from jax import lax
from jax.experimental import pallas as pl
from jax.experimental.pallas import tpu as pltpu

F32 = jnp.float32
BF16 = jnp.bfloat16

D_MODEL = 1024
D_CONV = 512
D_SSM = 512
CONV_WIDTH = 31
N_GROUPS = 32
P_GROUP = 16
N_STATE = 64
N_MEM = 256
N_HEADS = 4
HEAD_DIM = 256
D_FF = 2816
RMS_EPS = 1e-6
LN_EPS = 1e-5

LANES = 128
SUBLANES = 8
GROUPS_PER_BLOCK = LANES // P_GROUP
N_LANE_BLOCKS = D_SSM // LANES
STATE_HALF = GROUPS_PER_BLOCK * N_STATE
STATE_W = 2 * STATE_HALF
MXU_N = 256
VMEM_LIMIT = 56 * 1024 * 1024


def _params(n_axes, vmem=VMEM_LIMIT):
    return pltpu.CompilerParams(dimension_semantics=("arbitrary",) * n_axes, vmem_limit_bytes=vmem)


def _const(shape):
    nd = len(shape)
    return pl.BlockSpec(shape, lambda *_: (0,) * nd, pipeline_mode=pl.Buffered(1))


def _rms(x, g):
    return x * lax.rsqrt(jnp.mean(x * x, axis=-1, keepdims=True) + RMS_EPS) * g


def _cast_weights_once(n_axes, *pairs):
    first = pl.program_id(0) == 0
    for a in range(1, n_axes):
        first = jnp.logical_and(first, pl.program_id(a) == 0)

    @pl.when(first)
    def _():
        for src, dst in pairs:
            dst[...] = src[...].astype(BF16)


def _gather_rows(x_ref, L, c0, n):
    return jnp.concatenate([x_ref[c0:c0 + n, s, :] for s in range(L)], axis=0)


def _sub_tiles(ct, sub):
    return [(c0, sub) for c0 in range(0, ct, sub)]


def _nat_spec(ct, L):
    return pl.BlockSpec((None, ct, L, D_MODEL), lambda b, i: (b, i, 0, 0))


def _tp_spec(ct, L, width, nc):
    return pl.BlockSpec((L, ct, width), lambda b, i: (0, b * nc + i, 0))


def _inproj_kernel(x_ref, g_ref, w_ref, v_ref, u_ref, w16_scr, *, L, ct, sub):
    _cast_weights_once(2, (w_ref, w16_scr))
    for c0, n in _sub_tiles(ct, sub):
        x = _gather_rows(x_ref, L, c0, n)
        h = _rms(x, g_ref[...])
        z = jnp.dot(h.astype(BF16), w16_scr[...], preferred_element_type=F32)
        a = z[:, :D_CONV]
        g = z[:, D_CONV:2 * D_CONV]
        u = z[:, 2 * D_CONV:].astype(BF16)
        v_ref[:, c0:c0 + n, :] = (a * jax.nn.sigmoid(g)).reshape(L, n, D_CONV)
        for s in range(L):
            for j in range(N_LANE_BLOCKS):
                u_ref[j, c0:c0 + n, s * LANES:(s + 1) * LANES] = u[s * n:(s + 1) * n, j * LANES:(j + 1) * LANES]


def _inproj(x4, g0, w_in, *, L, ct, sub):
    bq, c = x4.shape[:2]
    nc = c // ct
    R = bq * c
    return pl.pallas_call(
        functools.partial(_inproj_kernel, L=L, ct=ct, sub=sub),
        grid=(bq, nc),
        in_specs=[_nat_spec(ct, L), _const((1, D_MODEL)), _const((D_MODEL, 2 * D_CONV + D_SSM))],
        out_specs=[
            _tp_spec(ct, L, D_CONV, nc),
            pl.BlockSpec((N_LANE_BLOCKS, ct, L * LANES), lambda b, i: (0, b * nc + i, 0)),
        ],
        out_shape=[
            jax.ShapeDtypeStruct((L, R, D_CONV), F32),
            jax.ShapeDtypeStruct((N_LANE_BLOCKS, R, L * LANES), BF16),
        ],
        scratch_shapes=[pltpu.VMEM(w_in.shape, BF16)],
        compiler_params=_params(2),
        name="inproj",
    )(x4, g0, w_in)


def _ln_silu(acc, g, b):
    mu = jnp.mean(acc, axis=-1, keepdims=True)
    xc = acc - mu
    var = jnp.mean(xc * xc, axis=-1, keepdims=True)
    y = xc * lax.rsqrt(var + LN_EPS) * g + b
    return y * jax.nn.sigmoid(y)


def _conv_shifted_copies(v_ref, v1_ref, v2_ref, *, L, C):
    row = lax.broadcasted_iota(jnp.int32, (C, D_CONV), 0)
    for s in range(L):
        x = v_ref[s]
        v1_ref[s] = jnp.where(row >= 1, pltpu.roll(x, 1, 0), 0.0)
        v2_ref[s] = jnp.where(row >= 2, pltpu.roll(x, 2, 0), 0.0)


def _conv_rows(srcs, w_ref, b_ref, g_ref, bb_ref, o_ref, r0, *, L, rc):
    sub8 = rc // SUBLANES
    for s in range(L):
        acc = jnp.broadcast_to(b_ref[...][None], (sub8, SUBLANES, D_CONV))
        for d in range(CONV_WIDTH):
            blk = (s - d) % L
            shift = (d - s + L - 1) // L if d > s else 0
            k = CONV_WIDTH - 1 - d
            src = srcs[shift][blk, pl.ds(r0, rc), :].reshape(sub8, SUBLANES, D_CONV)
            acc = acc + w_ref[k][None] * src
        y = _ln_silu(acc.reshape(rc, D_CONV), g_ref[...], bb_ref[...])
        o_ref[s, pl.ds(r0, rc), :] = y.astype(o_ref.dtype)


def _conv_prompt_kernel(v_ref, w_ref, b_ref, g_ref, bb_ref, o_ref, v1_ref, v2_ref, *, L, C, rc):
    _conv_shifted_copies(v_ref, v1_ref, v2_ref, L=L, C=C)
    srcs = (v_ref, v1_ref, v2_ref)

    def chunk(i, carry):
        _conv_rows(srcs, w_ref, b_ref, g_ref, bb_ref, o_ref, pl.multiple_of(i * rc, rc), L=L, rc=rc)
        return carry

    lax.fori_loop(0, C // rc, chunk, 0)


def _conv_prompt(v, w_dw, b_dw, ln_g, ln_b, *, L, C):
    R = v.shape[1]
    rc = 32
    return pl.pallas_call(
        functools.partial(_conv_prompt_kernel, L=L, C=C, rc=rc),
        grid=(R // C,),
        in_specs=[
            pl.BlockSpec((L, C, D_CONV), lambda b: (0, b, 0)),
            _const((CONV_WIDTH, SUBLANES, D_CONV)),
            _const((SUBLANES, D_CONV)),
            _const((1, D_CONV)),
            _const((1, D_CONV)),
        ],
        out_specs=pl.BlockSpec((L, C, D_CONV), lambda b: (0, b, 0)),
        out_shape=jax.ShapeDtypeStruct((L, R, D_CONV), BF16),
        scratch_shapes=[pltpu.VMEM((L, C, D_CONV), F32)] * 2,
        compiler_params=_params(1),
        name="conv_prompt",
    )(v, jnp.broadcast_to(w_dw[:, None, :], (CONV_WIDTH, SUBLANES, D_CONV)),
      jnp.broadcast_to(b_dw, (SUBLANES, D_CONV)), ln_g, ln_b)


def _conv_sample_kernel(v_ref, st_ref, w_ref, b_ref, g_ref, bb_ref, o_ref, new_ref, *, L, H):
    def ext(i):
        return st_ref[i] if i < H else v_ref[i - H]

    for t in range(L):
        acc = jnp.broadcast_to(b_ref[...], v_ref.shape[1:])
        for k in range(CONV_WIDTH):
            acc = acc + w_ref[k:k + 1, :] * ext(t + k)
        o_ref[t] = _ln_silu(acc, g_ref[...], bb_ref[...]).astype(o_ref.dtype)
    for i in range(H):
        new_ref[i] = ext(i + L)


def _conv_sample(v, state_t, w_dw, b_dw, ln_g, ln_b):
    L, R, _ = v.shape
    H = CONV_WIDTH - 1
    return pl.pallas_call(
        functools.partial(_conv_sample_kernel, L=L, H=H),
        grid=(1,),
        in_specs=[
            _const((L, R, D_CONV)),
            _const((H, R, D_CONV)),
            _const((CONV_WIDTH, D_CONV)),
            _const((1, D_CONV)),
            _const((1, D_CONV)),
            _const((1, D_CONV)),
        ],
        out_specs=[
            pl.BlockSpec((L, R, D_CONV), lambda i: (0, 0, 0)),
            pl.BlockSpec((H, R, D_CONV), lambda i: (0, 0, 0)),
        ],
        out_shape=[
            jax.ShapeDtypeStruct((L, R, D_CONV), BF16),
            jax.ShapeDtypeStruct((H, R, D_CONV), F32),
        ],
        compiler_params=_params(1),
        name="conv_sample",
    )(v, state_t, w_dw, b_dw, ln_g, ln_b)


def _expand_block_diag(d2):
    tiled = jnp.concatenate([d2] * (STATE_HALF // LANES), axis=-1)
    r = lax.broadcasted_iota(jnp.int32, tiled.shape, 0) // P_GROUP
    c = lax.broadcasted_iota(jnp.int32, tiled.shape, 1) // N_STATE
    return jnp.where(r == c, tiled, jnp.zeros_like(tiled))


def _ssm_kernel(u_ref, taps_ref, wbc_ref, wcc_ref, a_ref, d_ref, h0_ref, y_ref, hl_ref,
                w_scr, wb_scr, wct_scr, s_scr, hp_scr, tap_scr, *, L, C, bt):
    per_tile = MXU_N // LANES

    @pl.when(pl.program_id(1) == 0)
    def _():
        rep = (lax.broadcasted_iota(jnp.int32, (P_GROUP, LANES), 0)
               == lax.broadcasted_iota(jnp.int32, (P_GROUP, LANES), 1) % P_GROUP).astype(BF16)
        same_group = (lax.broadcasted_iota(jnp.int32, (LANES, LANES), 0) // P_GROUP
                      == lax.broadcasted_iota(jnp.int32, (LANES, LANES), 1) // P_GROUP)
        for t in range(L):
            full = jnp.dot(taps_ref[t].astype(BF16), rep, preferred_element_type=F32)
            tap_scr[t] = jnp.where(same_group, full, 0.0).astype(BF16)
        for sp in range(L):
            for s in range(L):
                rows = slice(s * LANES, (s + 1) * LANES)
                cols = slice(sp * LANES, (sp + 1) * LANES)
                if s <= sp:
                    w_scr[rows, cols] = tap_scr[sp - s]
                elif s // per_tile == sp // per_tile:
                    w_scr[rows, cols] = jnp.zeros((LANES, LANES), BF16)
        for s in range(L):
            rows = slice(s * LANES, (s + 1) * LANES)
            for a in range(2):
                cols = slice(a * STATE_HALF, (a + 1) * STATE_HALF)
                wb_scr[rows, cols] = _expand_block_diag(wbc_ref[s, a])
                wct_scr[rows, cols] = _expand_block_diag(wcc_ref[s, a])

    nk = STATE_W // LANES
    half = nk // 2
    inc_all = jnp.dot(u_ref[...], wb_scr[...], preferred_element_type=F32)
    for k in range(nk):
        s_scr[k] = inc_all[:, k * LANES:(k + 1) * LANES]

    d = d_ref[...]
    for n in range(L // per_tile):
        k = (n + 1) * MXU_N
        y = jnp.dot(u_ref[:, :k], w_scr[:k, n * MXU_N:(n + 1) * MXU_N], preferred_element_type=F32)
        for h in range(per_tile):
            s = n * per_tile + h
            us = u_ref[:, s * LANES:(s + 1) * LANES].astype(F32)
            y_ref[s] = y[:, h * LANES:(h + 1) * LANES] + d * us

    h0 = h0_ref[...]
    a = a_ref[...]

    def piece(x, k):
        return x[:, k * LANES:(k + 1) * LANES]

    ar = [piece(a[0:1], k) for k in range(half)]
    ai = [piece(a[1:2], k) for k in range(half)]
    state = [piece(h0, k) for k in range(nk)]
    for c in range(C):
        rows = pl.ds(c, bt, stride=C) if C > 1 else pl.ds(0, bt)
        nxt = []
        for k in range(half):
            hr, hi = state[k], state[half + k]
            hp_scr[k, rows, :] = hr
            hp_scr[half + k, rows, :] = hi
            nxt.append(ar[k] * hr - ai[k] * hi + s_scr[k, rows, :])
        for k in range(half):
            hr, hi = state[k], state[half + k]
            nxt.append(ar[k] * hi + ai[k] * hr + s_scr[half + k, rows, :])
        state = nxt
    for k in range(nk):
        hl_ref[:, k * LANES:(k + 1) * LANES] = state[k]

    hp = jnp.concatenate([hp_scr[k] for k in range(nk)], axis=-1).astype(BF16)
    for n in range(L // per_tile):
        y = lax.dot_general(hp, wct_scr[n * MXU_N:(n + 1) * MXU_N, :], (((1,), (1,)), ((), ())),
                            preferred_element_type=F32)
        for h in range(per_tile):
            s = n * per_tile + h
            y_ref[s] += y[:, h * LANES:(h + 1) * LANES]


def _ssm(u4, taps, wbc, wcc, a_pow, d_skip, h0, *, L, C, bt):
    R = u4.shape[1]
    rows = bt * C
    nb = R // rows
    lw = L * LANES
    in_specs = [
        pl.BlockSpec((None, rows, lw), lambda j, b: (j, b, 0)),
        pl.BlockSpec((None, L, LANES, P_GROUP), lambda j, b: (j, 0, 0, 0)),
        pl.BlockSpec((None, L, 2, LANES, LANES), lambda j, b: (j, 0, 0, 0, 0)),
        pl.BlockSpec((None, L, 2, LANES, LANES), lambda j, b: (j, 0, 0, 0, 0)),
        pl.BlockSpec((None, 2, STATE_HALF), lambda j, b: (j, 0, 0)),
        pl.BlockSpec((None, 1, LANES), lambda j, b: (j, 0, 0)),
        pl.BlockSpec((None, None, bt, STATE_W), lambda j, b: (j, b, 0, 0)),
    ]
    out_specs = [
        pl.BlockSpec((L, rows, LANES), lambda j, b: (0, b, j)),
        pl.BlockSpec((None, None, bt, STATE_W), lambda j, b: (j, b, 0, 0)),
    ]
    out_shape = [
        jax.ShapeDtypeStruct((L, R, D_SSM), F32),
        jax.ShapeDtypeStruct((N_LANE_BLOCKS, nb, bt, STATE_W), F32),
    ]
    scratch = [
        pltpu.VMEM((lw, lw), BF16),
        pltpu.VMEM((lw, STATE_W), BF16),
        pltpu.VMEM((lw, STATE_W), BF16),
        pltpu.VMEM((STATE_W // LANES, rows, LANES), F32),
        pltpu.VMEM((STATE_W // LANES, rows, LANES), F32),
        pltpu.VMEM((L, LANES, LANES), BF16),
    ]
    return pl.pallas_call(
        functools.partial(_ssm_kernel, L=L, C=C, bt=bt),
        grid=(N_LANE_BLOCKS, nb),
        in_specs=in_specs,
        out_specs=out_specs,
        out_shape=out_shape,
        scratch_shapes=scratch,
        compiler_params=_params(2),
        name="ssm",
    )(u4, taps, wbc, wcc, a_pow, d_skip, h0)


def _ssm_weights(lam_re, lam_im, log_dt, b_re, b_im, c_re, c_im, L):
    hp = lax.Precision.HIGHEST
    dt = jnp.exp(log_dt)[:, None]
    zr = lam_re * dt
    zi = lam_im * dt
    n_pow = jnp.arange(L + 1, dtype=F32)[:, None, None]
    mag = jnp.exp(zr[None] * n_pow)
    pr = mag * jnp.cos(zi[None] * n_pow)
    pi = mag * jnp.sin(zi[None] * n_pow)
    a1r, a1i = pr[1], pi[1]
    den = lam_re * lam_re + lam_im * lam_im
    qr = ((a1r - 1.0) * lam_re + a1i * lam_im) / den
    qi = (a1i * lam_re - (a1r - 1.0) * lam_im) / den
    bbr = qr[:, :, None] * b_re - qi[:, :, None] * b_im
    bbi = qr[:, :, None] * b_im + qi[:, :, None] * b_re

    nj, g8 = N_LANE_BLOCKS, GROUPS_PER_BLOCK

    car = c_re[None] * pr[:L, :, None, :] - c_im[None] * pi[:L, :, None, :]
    cai = c_re[None] * pi[:L, :, None, :] + c_im[None] * pr[:L, :, None, :]
    taps = (jnp.einsum("tgqn,gnp->gtpq", car, bbr, precision=hp)
            - jnp.einsum("tgqn,gnp->gtpq", cai, bbi, precision=hp))
    taps = jnp.transpose(taps.reshape(nj, g8, L, P_GROUP, P_GROUP), (0, 2, 1, 3, 4))
    taps = taps.reshape(nj, L, LANES, P_GROUP)

    n_rev = (L - 1.0) - jnp.arange(L, dtype=F32)[:, None, None]
    mag_rev = jnp.exp(zr[None] * n_rev)
    rev = mag_rev * jnp.cos(zi[None] * n_rev), mag_rev * jnp.sin(zi[None] * n_rev)
    er = rev[0][:, :, :, None] * bbr[None] - rev[1][:, :, :, None] * bbi[None]
    ei = rev[0][:, :, :, None] * bbi[None] + rev[1][:, :, :, None] * bbr[None]
    e = jnp.stack([er, ei], 1).reshape(L, 2, nj, g8, N_STATE, P_GROUP)
    e = jnp.transpose(e, (2, 0, 1, 3, 5, 4)).reshape(nj, L, 2, LANES, N_STATE)
    wbc = jnp.concatenate([e, e], axis=-1)

    mr = c_re[None] * pr[1:, :, None, :] - c_im[None] * pi[1:, :, None, :]
    mi = c_re[None] * pi[1:, :, None, :] + c_im[None] * pr[1:, :, None, :]
    m = jnp.stack([mr, -mi], 1).reshape(L, 2, nj, g8, P_GROUP, N_STATE)
    m = jnp.transpose(m, (2, 0, 1, 3, 4, 5)).reshape(nj, L, 2, LANES, N_STATE)
    wcc = jnp.concatenate([m, m], axis=-1)

    def a_pow(n):
        return jnp.stack([pr[n].reshape(nj, STATE_HALF), pi[n].reshape(nj, STATE_HALF)], 1)

    return taps, wbc.astype(BF16), wcc.astype(BF16), a_pow


def _ssm_weights_prefix(full, L_full, L):
    taps, wbc, wcc, a_pow = full
    return taps[:, :L], wbc[:, L_full - L:], wcc[:, :L], a_pow(L)


def _pack_state(re, im):
    b = re.shape[0]
    r = re.reshape(b, N_LANE_BLOCKS, STATE_HALF)
    i = im.reshape(b, N_LANE_BLOCKS, STATE_HALF)
    return jnp.transpose(jnp.concatenate([r, i], -1), (1, 0, 2))


def _unpack_state(h):
    b = h.shape[1]
    h = jnp.transpose(h, (1, 0, 2))
    re = h[:, :, :STATE_HALF].reshape(b, N_GROUPS, N_STATE)
    im = h[:, :, STATE_HALF:].reshape(b, N_GROUPS, N_STATE)
    return re, im


def _mixout_kernel(c_ref, y_ref, x_ref, wglu_ref, wout_ref, g_ref, o_ref, wglu_scr, wout_scr, *, L, ct, sub):
    _cast_weights_once(2, (wglu_ref, wglu_scr), (wout_ref, wout_scr))
    for c0, n in _sub_tiles(ct, sub):
        rows = L * n
        gy = jax.nn.gelu(y_ref[:, c0:c0 + n, :].reshape(rows, D_SSM))
        sg = gy * jax.nn.sigmoid(jnp.dot(gy.astype(BF16), wglu_scr[...], preferred_element_type=F32))
        m = jnp.dot(c_ref[:, c0:c0 + n, :].reshape(rows, D_CONV), wout_scr[0:D_CONV, :],
                    preferred_element_type=F32)
        m = m + jnp.dot(sg.astype(BF16), wout_scr[D_CONV:, :], preferred_element_type=F32)
        x = _gather_rows(x_ref, L, c0, n)
        o_ref[:, c0:c0 + n, :] = (x + _rms(m, g_ref[...])).reshape(L, n, D_MODEL)


def _mixout(c, y, x4, w_glu, w_out, g1, *, L, ct, sub):
    bq, cc = x4.shape[:2]
    nc = cc // ct
    R = bq * cc
    return pl.pallas_call(
        functools.partial(_mixout_kernel, L=L, ct=ct, sub=sub),
        grid=(bq, nc),
        in_specs=[
            _tp_spec(ct, L, D_CONV, nc),
            _tp_spec(ct, L, D_SSM, nc),
            _nat_spec(ct, L),
            _const((D_SSM, D_SSM)),
            _const((D_CONV + D_SSM, D_MODEL)),
            _const((1, D_MODEL)),
        ],
        out_specs=_tp_spec(ct, L, D_MODEL, nc),
        out_shape=jax.ShapeDtypeStruct((L, R, D_MODEL), F32),
        scratch_shapes=[pltpu.VMEM(w_glu.shape, BF16), pltpu.VMEM(w_out.shape, BF16)],
        compiler_params=_params(2),
        name="mixout",
    )(c, y, x4, w_glu, w_out, g1)


def _memkv_kernel(m_ref, g_ref, wk_ref, wv_ref, k5_ref, v5_ref, kb_ref, vb_ref, wk_scr, wv_scr, *, nb):
    _cast_weights_once(1, (wk_ref, wk_scr), (wv_ref, wv_scr))
    for b in range(nb):
        m = _rms(m_ref[b], g_ref[...]).astype(BF16)
        for w_ref, o5_ref, ob_ref in ((wk_scr, k5_ref, kb_ref), (wv_scr, v5_ref, vb_ref)):
            p = jnp.dot(m, w_ref[...], preferred_element_type=F32)
            ob_ref[b] = p.astype(BF16)
            for hd in range(N_HEADS):
                o5_ref[b, :, hd, :] = p[:, hd * HEAD_DIM:(hd + 1) * HEAD_DIM]


def _memkv(mem, g_mem, w_k, w_v):
    bq = mem.shape[0]
    nb = 2
    out5 = pl.BlockSpec((nb, N_MEM, N_HEADS, HEAD_DIM), lambda i: (i, 0, 0, 0))
    outb = pl.BlockSpec((nb, N_MEM, D_MODEL), lambda i: (i, 0, 0))
    return pl.pallas_call(
        functools.partial(_memkv_kernel, nb=nb),
        grid=(bq // nb,),
        in_specs=[
            pl.BlockSpec((nb, N_MEM, D_MODEL), lambda i: (i, 0, 0)),
            _const((1, D_MODEL)),
            _const((D_MODEL, D_MODEL)),
            _const((D_MODEL, D_MODEL)),
        ],
        out_specs=[out5, out5, outb, outb],
        out_shape=[jax.ShapeDtypeStruct((bq, N_MEM, N_HEADS, HEAD_DIM), F32)] * 2
        + [jax.ShapeDtypeStruct((bq, N_MEM, D_MODEL), BF16)] * 2,
        scratch_shapes=[pltpu.VMEM(w_k.shape, BF16), pltpu.VMEM(w_v.shape, BF16)],
        compiler_params=_params(1),
        name="memkv",
    )(mem, g_mem, w_k, w_v)


def _softmax_rows(s):
    s = s - jnp.max(s, axis=-1, keepdims=True)
    e = jnp.exp(s)
    return e / jnp.sum(e, axis=-1, keepdims=True)


def _attn_prompt_kernel(x_ref, k_ref, v_ref, wq32_ref, wo32_ref, gq_ref, go_ref, o_ref, wq_ref, wo_ref,
                        *, L, ct, sub):
    _cast_weights_once(2, (wq32_ref, wq_ref), (wo32_ref, wo_ref))
    kb = k_ref[...]
    vb = v_ref[...]
    for c0, n in _sub_tiles(ct, sub):
        rows = L * n
        x = x_ref[:, c0:c0 + n, :].reshape(rows, D_MODEL)
        h = _rms(x, gq_ref[...])
        q = jnp.dot(h.astype(BF16), wq_ref[...], preferred_element_type=F32).astype(BF16)
        heads = []
        for hd in range(N_HEADS):
            sl = slice(hd * HEAD_DIM, (hd + 1) * HEAD_DIM)
            sc = lax.dot_general(q[:, sl], kb[:, sl], (((1,), (1,)), ((), ())), preferred_element_type=F32)
            p = _softmax_rows(sc * (HEAD_DIM ** -0.5))
            heads.append(jnp.dot(p.astype(BF16), vb[:, sl], preferred_element_type=F32))
        o = jnp.concatenate(heads, axis=-1)
        a = jnp.dot(o.astype(BF16), wo_ref[...], preferred_element_type=F32)
        o_ref[:, c0:c0 + n, :] = (x + _rms(a, go_ref[...])).reshape(L, n, D_MODEL)


def _attn_prompt(x1, k, v, w_q, w_o, g2, g3, *, C, ct, sub):
    L, R, _ = x1.shape
    nc = C // ct
    return pl.pallas_call(
        functools.partial(_attn_prompt_kernel, L=L, ct=ct, sub=sub),
        grid=(R // C, nc),
        in_specs=[
            _tp_spec(ct, L, D_MODEL, nc),
            pl.BlockSpec((None, N_MEM, D_MODEL), lambda b, i: (b, 0, 0)),
            pl.BlockSpec((None, N_MEM, D_MODEL), lambda b, i: (b, 0, 0)),
            _const((D_MODEL, D_MODEL)),
            _const((D_MODEL, D_MODEL)),
            _const((1, D_MODEL)),
            _const((1, D_MODEL)),
        ],
        out_specs=_tp_spec(ct, L, D_MODEL, nc),
        out_shape=jax.ShapeDtypeStruct((L, R, D_MODEL), F32),
        scratch_shapes=[pltpu.VMEM(w_q.shape, BF16), pltpu.VMEM(w_o.shape, BF16)],
        compiler_params=_params(2),
        name="attn_prompt",
    )(x1, k, v, w_q, w_o, g2, g3)


def _qproj_kernel(x_ref, wq_ref, g_ref, q_ref):
    h = _rms(x_ref[...], g_ref[...])
    q_ref[...] = jnp.dot(h.astype(BF16), wq_ref[...].astype(BF16), preferred_element_type=F32)


def _qproj(x1f, w_q, g2):
    rows = x1f.shape[0]
    return pl.pallas_call(
        _qproj_kernel,
        grid=(1,),
        in_specs=[_const((rows, D_MODEL)), _const((D_MODEL, D_MODEL)), _const((1, D_MODEL))],
        out_specs=pl.BlockSpec((rows, D_MODEL), lambda i: (0, 0)),
        out_shape=jax.ShapeDtypeStruct((rows, D_MODEL), F32),
        compiler_params=_params(1),
        name="qproj_sample",
    )(x1f, w_q, g2)


def _attn_sample_kernel(q_ref, k_ref, v_ref, o_ref, *, L, bb):
    rows = L * bb
    q2 = q_ref[...].reshape(rows, D_MODEL)
    qs = jnp.concatenate([q2[:, h * HEAD_DIM:(h + 1) * HEAD_DIM] for h in range(N_HEADS)], axis=0).astype(BF16)
    nr = N_HEADS * rows
    nc = N_MEM * N_HEADS
    row_head = lax.broadcasted_iota(jnp.int32, (nr, nc), 0) // rows
    col_head = lax.broadcasted_iota(jnp.int32, (nr, nc), 1) % N_HEADS
    same_head = row_head == col_head
    owner = lax.broadcasted_iota(jnp.int32, (nr, HEAD_DIM), 0) % bb
    acc = jnp.zeros((nr, HEAD_DIM), F32)
    for b in range(bb):
        ka = k_ref[b].reshape(nc, HEAD_DIM).astype(BF16)
        va = v_ref[b].reshape(nc, HEAD_DIM).astype(BF16)
        sc = lax.dot_general(qs, ka, (((1,), (1,)), ((), ())), preferred_element_type=F32)
        p = _softmax_rows(jnp.where(same_head, sc * (HEAD_DIM ** -0.5), -1e30))
        o = jnp.dot(p.astype(BF16), va, preferred_element_type=F32)
        acc = jnp.where(owner == b, o, acc)
    for h in range(N_HEADS):
        o_ref[:, :, h * HEAD_DIM:(h + 1) * HEAD_DIM] = acc[h * rows:(h + 1) * rows].reshape(L, bb, HEAD_DIM)


def _attn_sample(q3, k5, v5):
    L, R, _ = q3.shape
    bb = 8
    kv_spec = pl.BlockSpec((None, bb, N_MEM, N_HEADS, HEAD_DIM), lambda i: (0, i, 0, 0, 0))
    return pl.pallas_call(
        functools.partial(_attn_sample_kernel, L=L, bb=bb),
        grid=(R // bb,),
        in_specs=[pl.BlockSpec((L, bb, D_MODEL), lambda i: (0, i, 0)), kv_spec, kv_spec],
        out_specs=pl.BlockSpec((L, bb, D_MODEL), lambda i: (0, i, 0)),
        out_shape=jax.ShapeDtypeStruct((L, R, D_MODEL), F32),
        compiler_params=_params(1),
        name="attn_sample",
    )(q3, k5, v5)


def _oproj_kernel(o_ref, x_ref, wo_ref, g_ref, y_ref):
    a = jnp.dot(o_ref[...].astype(BF16), wo_ref[...].astype(BF16), preferred_element_type=F32)
    y_ref[...] = x_ref[...] + _rms(a, g_ref[...])


def _oproj(of, x1f, w_o, g3):
    rows = of.shape[0]
    return pl.pallas_call(
        _oproj_kernel,
        grid=(1,),
        in_specs=[_const((rows, D_MODEL)), _const((rows, D_MODEL)), _const((D_MODEL, D_MODEL)),
                  _const((1, D_MODEL))],
        out_specs=pl.BlockSpec((rows, D_MODEL), lambda i: (0, 0)),
        out_shape=jax.ShapeDtypeStruct((rows, D_MODEL), F32),
        compiler_params=_params(1),
        name="oproj_sample",
    )(of, x1f, w_o, g3)


def _ffn_kernel(x_ref, wg_ref, wu_ref, wd_ref, gi_ref, go_ref, o_ref, *, L, ct):
    rows = L * ct
    x = x_ref[...].reshape(rows, D_MODEL)
    h = _rms(x, gi_ref[...]).astype(BF16)
    gate = jnp.dot(h, wg_ref[...], preferred_element_type=F32)
    up = jnp.dot(h, wu_ref[...], preferred_element_type=F32)
    act = (gate * jax.nn.sigmoid(gate) * up).astype(BF16)
    dn = jnp.dot(act, wd_ref[...], preferred_element_type=F32)
    y = x + _rms(dn, go_ref[...])
    for s in range(L):
        o_ref[:, s, :] = y[s * ct:(s + 1) * ct]


def _ffn(x2, w_gate, w_up, w_down, g4, g5, *, bq, C, ct):
    L = x2.shape[0]
    nc = C // ct
    return pl.pallas_call(
        functools.partial(_ffn_kernel, L=L, ct=ct),
        grid=(bq, nc),
        in_specs=[
            _tp_spec(ct, L, D_MODEL, nc),
            _const((D_MODEL, D_FF)),
            _const((D_MODEL, D_FF)),
            _const((D_FF, D_MODEL)),
            _const((1, D_MODEL)),
            _const((1, D_MODEL)),
        ],
        out_specs=_nat_spec(ct, L),
        out_shape=jax.ShapeDtypeStruct((bq, C, L, D_MODEL), F32),
        compiler_params=_params(2),
        name="ffn",
    )(x2, w_gate, w_up, w_down, g4, g5)


def _layer(x4, mem_k, mem_v, conv_state_t, h0_packed, wts, *, ct, sub, ffn_ct, sc, bt, sample):
    bq, C, L, _ = x4.shape
    R = bq * C
    ng = wts["norm_g"]
    g = [ng[i:i + 1] for i in range(6)]

    v, u4 = _inproj(x4, g[0], wts["w_in"], L=L, ct=ct, sub=sub)
    sw = wts["ssm_L%d" % L]
    nb = R // (bt * sc)
    ssm_args = (u4, sw[0], sw[1], sw[2], sw[3], wts["d_skip"], h0_packed.reshape(N_LANE_BLOCKS, nb, bt, STATE_W))
    conv_args = (wts["w_dw"], wts["b_dw"], wts["ln_g"], wts["ln_b"])
    y, hl = _ssm(*ssm_args, L=L, C=sc, bt=bt)
    if sample:
        cact, conv_new = _conv_sample(v, conv_state_t, *conv_args)
    else:
        cact = _conv_prompt(v, *conv_args, L=L, C=C)
        tail = v.reshape(L, bq, C, D_CONV)[:, :, C - 2:, :]
        tail = jnp.transpose(tail, (1, 2, 0, 3)).reshape(bq, 2 * L, D_CONV)
        conv_new = tail[:, 2 * L - (CONV_WIDTH - 1):, :]
    hl = hl.reshape(N_LANE_BLOCKS, nb * bt, STATE_W)

    x1 = _mixout(cact, y, x4, wts["w_glu"], wts["w_out"], g[1], L=L, ct=ct, sub=sub)

    if sample:
        q = _qproj(x1.reshape(L * R, D_MODEL), wts["w_q"], g[2])
        o = _attn_sample(q.reshape(L, R, D_MODEL), mem_k, mem_v)
        x2 = _oproj(o.reshape(L * R, D_MODEL), x1.reshape(L * R, D_MODEL), wts["w_o"], g[3])
        x2 = x2.reshape(L, R, D_MODEL)
    else:
        x2 = _attn_prompt(x1, mem_k, mem_v, wts["w_q"], wts["w_o"], g[2], g[3], C=C, ct=ct, sub=sub)

    out = _ffn(x2, wts["w_gate"], wts["w_up"], wts["w_down"], g[4], g[5], bq=bq, C=C, ct=ffn_ct)
    return out, conv_new, hl


PROMPT_L = 16
PROMPT_SUB = 32
PROMPT_CT = 64


def kernel(x_prompt, x_sample, mem_prompt, cache_mem_k, cache_mem_v, state_conv, state_ssm_re, state_ssm_im,
           norm_g, mem_norm_g, w_in, w_dw, b_dw, ln_g, ln_b, lam_re, lam_im, log_dt, b_re, b_im, c_re, c_im,
           d_skip, w_glu, w_out, w_q, w_k, w_v, w_o, w_gate, w_up, w_down):
    depth = w_in.shape[0]
    bp, tp, _ = x_prompt.shape
    bs, ts, _ = x_sample.shape
    assert tp % (PROMPT_L * PROMPT_CT) == 0 and tp >= CONV_WIDTH - 1

    yp = x_prompt.reshape(bp, tp // PROMPT_L, PROMPT_L, D_MODEL)
    ys = x_sample.reshape(1, bs, ts, D_MODEL)
    outs = [[] for _ in range(8)]
    for l in range(depth):
        wts = {
            "norm_g": norm_g[l],
            "w_in": w_in[l],
            "w_dw": w_dw[l],
            "b_dw": b_dw[l][None],
            "ln_g": ln_g[l][None],
            "ln_b": ln_b[l][None],
            "d_skip": d_skip[l].reshape(N_LANE_BLOCKS, 1, LANES),
            "w_glu": w_glu[l],
            "w_out": w_out[l],
            "w_q": w_q[l],
            "w_o": w_o[l],
            "w_gate": w_gate[l].astype(BF16),
            "w_up": w_up[l].astype(BF16),
            "w_down": w_down[l].astype(BF16),
        }
        ssm_args = (lam_re[l], lam_im[l], log_dt[l], b_re[l], b_im[l], c_re[l], c_im[l])
        assert ts <= PROMPT_L
        full = _ssm_weights(*ssm_args, PROMPT_L)
        wts["ssm_L%d" % PROMPT_L] = full[:3] + (full[3](PROMPT_L),)
        wts["ssm_L%d" % ts] = _ssm_weights_prefix(full, PROMPT_L, ts)

        kp, vp, kp16, vp16 = _memkv(mem_prompt, mem_norm_g[l][None], w_k[l], w_v[l])
        h0p = jnp.zeros((N_LANE_BLOCKS, bp, STATE_W), F32)
        yp, cp, hp = _layer(yp, kp16, vp16, None, h0p, wts, ct=PROMPT_CT, sub=PROMPT_SUB, ffn_ct=PROMPT_SUB,
                            sc=tp // PROMPT_L, bt=4, sample=False)
        h0s = _pack_state(state_ssm_re[l], state_ssm_im[l])
        ys, cs, hs = _layer(ys, cache_mem_k[l:l + 1], cache_mem_v[l:l + 1], jnp.transpose(state_conv[l], (1, 0, 2)),
                            h0s, wts, ct=bs, sub=bs, ffn_ct=bs, sc=1, bt=bs, sample=True)
        cs = jnp.transpose(cs, (1, 0, 2))

        hp_re, hp_im = _unpack_state(hp)
        hs_re, hs_im = _unpack_state(hs)
        for lst, val in zip(outs, (kp, vp, cp, hp_re, hp_im, cs, hs_re, hs_im)):
            lst.append(val)
    return (yp.reshape(bp, tp, D_MODEL), ys.reshape(bs, ts, D_MODEL)) + tuple(jnp.stack(o) for o in outs)
```

```python
import functools

import jax
import jax.numpy as jnp
from jax import lax
from jax.experimental import pallas as pl
from jax.experimental.pallas import tpu as pltpu

F32 = jnp.float32
BF16 = jnp.bfloat16

D_MODEL = 1024
D_CONV = 512
D_SSM = 512
CONV_WIDTH = 31
N_GROUPS = 32
P_GROUP = 16
N_STATE = 64
N_MEM = 256
N_HEADS = 4
HEAD_DIM = 256
D_FF = 2816
RMS_EPS = 1e-6
LN_EPS = 1e-5

LANES = 128
SUBLANES = 8
GROUPS_PER_BLOCK = LANES // P_GROUP
N_LANE_BLOCKS = D_SSM // LANES
STATE_HALF = GROUPS_PER_BLOCK * N_STATE
STATE_W = 2 * STATE_HALF
MXU_N = 256
VMEM_LIMIT = 56 * 1024 * 1024


def _params(n_axes, vmem=VMEM_LIMIT):
    return pltpu.CompilerParams(dimension_semantics=("arbitrary",) * n_axes, vmem_limit_bytes=vmem)


def _const(shape):
    nd = len(shape)
    return pl.BlockSpec(shape, lambda *_: (0,) * nd, pipeline_mode=pl.Buffered(1))


def _rms(x, g):
    return x * lax.rsqrt(jnp.mean(x * x, axis=-1, keepdims=True) + RMS_EPS) * g


def _cast_weights_once(n_axes, *pairs):
    first = pl.program_id(0) == 0
    for a in range(1, n_axes):
        first = jnp.logical_and(first, pl.program_id(a) == 0)

    @pl.when(first)
    def _():
        for src, dst in pairs:
            dst[...] = src[...].astype(BF16)


def _gather_rows(x_ref, L, c0, n):
    return jnp.concatenate([x_ref[c0:c0 + n, s, :] for s in range(L)], axis=0)


def _sub_tiles(ct, sub):
    return [(c0, sub) for c0 in range(0, ct, sub)]


def _nat_spec(ct, L):
    return pl.BlockSpec((None, ct, L, D_MODEL), lambda b, i: (b, i, 0, 0))


def _tp_spec(ct, L, width, nc):
    return pl.BlockSpec((L, ct, width), lambda b, i: (0, b * nc + i, 0))


def _inproj_kernel(x_ref, g_ref, w_ref, v_ref, u_ref, w16_scr, *, L, ct, sub):
    _cast_weights_once(2, (w_ref, w16_scr))
    for c0, n in _sub_tiles(ct, sub):
        x = _gather_rows(x_ref, L, c0, n)
        h = _rms(x, g_ref[...])
        z = jnp.dot(h.astype(BF16), w16_scr[...], preferred_element_type=F32)
        a = z[:, :D_CONV]
        g = z[:, D_CONV:2 * D_CONV]
        u = z[:, 2 * D_CONV:].astype(BF16)
        v_ref[:, c0:c0 + n, :] = (a * jax.nn.sigmoid(g)).reshape(L, n, D_CONV)
        for s in range(L):
            for j in range(N_LANE_BLOCKS):
                u_ref[j, c0:c0 + n, s * LANES:(s + 1) * LANES] = u[s * n:(s + 1) * n, j * LANES:(j + 1) * LANES]


def _inproj(x4, g0, w_in, *, L, ct, sub):
    bq, c = x4.shape[:2]
    nc = c // ct
    R = bq * c
    return pl.pallas_call(
        functools.partial(_inproj_kernel, L=L, ct=ct, sub=sub),
        grid=(bq, nc),
        in_specs=[_nat_spec(ct, L), _const((1, D_MODEL)), _const((D_MODEL, 2 * D_CONV + D_SSM))],
        out_specs=[
            _tp_spec(ct, L, D_CONV, nc),
            pl.BlockSpec((N_LANE_BLOCKS, ct, L * LANES), lambda b, i: (0, b * nc + i, 0)),
        ],
        out_shape=[
            jax.ShapeDtypeStruct((L, R, D_CONV), F32),
            jax.ShapeDtypeStruct((N_LANE_BLOCKS, R, L * LANES), BF16),
        ],
        scratch_shapes=[pltpu.VMEM(w_in.shape, BF16)],
        compiler_params=_params(2),
        name="inproj",
    )(x4, g0, w_in)


def _ln_silu(acc, g, b):
    mu = jnp.mean(acc, axis=-1, keepdims=True)
    xc = acc - mu
    var = jnp.mean(xc * xc, axis=-1, keepdims=True)
    y = xc * lax.rsqrt(var + LN_EPS) * g + b
    return y * jax.nn.sigmoid(y)


def _conv_shifted_copies(v_ref, v1_ref, v2_ref, *, L, C):
    row = lax.broadcasted_iota(jnp.int32, (C, D_CONV), 0)
    for s in range(L):
        x = v_ref[s]
        v1_ref[s] = jnp.where(row >= 1, pltpu.roll(x, 1, 0), 0.0)
        v2_ref[s] = jnp.where(row >= 2, pltpu.roll(x, 2, 0), 0.0)


def _conv_rows(srcs, w_ref, b_ref, g_ref, bb_ref, o_ref, r0, *, L, rc):
    sub8 = rc // SUBLANES
    for s in range(L):
        acc = jnp.broadcast_to(b_ref[...][None], (sub8, SUBLANES, D_CONV))
        for d in range(CONV_WIDTH):
            blk = (s - d) % L
            shift = (d - s + L - 1) // L if d > s else 0
            k = CONV_WIDTH - 1 - d
            src = srcs[shift][blk, pl.ds(r0, rc), :].reshape(sub8, SUBLANES, D_CONV)
            acc = acc + w_ref[k][None] * src
        y = _ln_silu(acc.reshape(rc, D_CONV), g_ref[...], bb_ref[...])
        o_ref[s, pl.ds(r0, rc), :] = y.astype(o_ref.dtype)


def _conv_prompt_kernel(v_ref, w_ref, b_ref, g_ref, bb_ref, o_ref, v1_ref, v2_ref, *, L, C, rc):
    _conv_shifted_copies(v_ref, v1_ref, v2_ref, L=L, C=C)
    srcs = (v_ref, v1_ref, v2_ref)

    def chunk(i, carry):
        _conv_rows(srcs, w_ref, b_ref, g_ref, bb_ref, o_ref, pl.multiple_of(i * rc, rc), L=L, rc=rc)
        return carry

    lax.fori_loop(0, C // rc, chunk, 0)


def _conv_prompt(v, w_dw, b_dw, ln_g, ln_b, *, L, C):
    R = v.shape[1]
    rc = 32
    return pl.pallas_call(
        functools.partial(_conv_prompt_kernel, L=L, C=C, rc=rc),
        grid=(R // C,),
        in_specs=[
            pl.BlockSpec((L, C, D_CONV), lambda b: (0, b, 0)),
            _const((CONV_WIDTH, SUBLANES, D_CONV)),
            _const((SUBLANES, D_CONV)),
            _const((1, D_CONV)),
            _const((1, D_CONV)),
        ],
        out_specs=pl.BlockSpec((L, C, D_CONV), lambda b: (0, b, 0)),
        out_shape=jax.ShapeDtypeStruct((L, R, D_CONV), BF16),
        scratch_shapes=[pltpu.VMEM((L, C, D_CONV), F32)] * 2,
        compiler_params=_params(1),
        name="conv_prompt",
    )(v, jnp.broadcast_to(w_dw[:, None, :], (CONV_WIDTH, SUBLANES, D_CONV)),
      jnp.broadcast_to(b_dw, (SUBLANES, D_CONV)), ln_g, ln_b)


def _conv_sample_kernel(v_ref, st_ref, w_ref, b_ref, g_ref, bb_ref, o_ref, new_ref, *, L, H):
    def ext(i):
        return st_ref[i] if i < H else v_ref[i - H]

    for t in range(L):
        acc = jnp.broadcast_to(b_ref[...], v_ref.shape[1:])
        for k in range(CONV_WIDTH):
            acc = acc + w_ref[k:k + 1, :] * ext(t + k)
        o_ref[t] = _ln_silu(acc, g_ref[...], bb_ref[...]).astype(o_ref.dtype)
    for i in range(H):
        new_ref[i] = ext(i + L)


def _conv_sample(v, state_t, w_dw, b_dw, ln_g, ln_b):
    L, R, _ = v.shape
    H = CONV_WIDTH - 1
    return pl.pallas_call(
        functools.partial(_conv_sample_kernel, L=L, H=H),
        grid=(1,),
        in_specs=[
            _const((L, R, D_CONV)),
            _const((H, R, D_CONV)),
            _const((CONV_WIDTH, D_CONV)),
            _const((1, D_CONV)),
            _const((1, D_CONV)),
            _const((1, D_CONV)),
        ],
        out_specs=[
            pl.BlockSpec((L, R, D_CONV), lambda i: (0, 0, 0)),
            pl.BlockSpec((H, R, D_CONV), lambda i: (0, 0, 0)),
        ],
        out_shape=[
            jax.ShapeDtypeStruct((L, R, D_CONV), BF16),
            jax.ShapeDtypeStruct((H, R, D_CONV), F32),
        ],
        compiler_params=_params(1),
        name="conv_sample",
    )(v, state_t, w_dw, b_dw, ln_g, ln_b)


def _expand_block_diag(d2):
    tiled = jnp.concatenate([d2] * (STATE_HALF // LANES), axis=-1)
    r = lax.broadcasted_iota(jnp.int32, tiled.shape, 0) // P_GROUP
    c = lax.broadcasted_iota(jnp.int32, tiled.shape, 1) // N_STATE
    return jnp.where(r == c, tiled, jnp.zeros_like(tiled))


def _ssm_kernel(u_ref, wbc_ref, wcc_ref, a_ref, d_ref, h0_ref, y_ref, hl_ref,
                w_scr, wb_scr, wct_scr, s_scr, hp_scr, tap_scr, *, L, C, bt):
    per_tile = MXU_N // LANES

    @pl.when(pl.program_id(1) == 0)
    def _():
        for s in range(L):
            rows = slice(s * LANES, (s + 1) * LANES)
            for a in range(2):
                cols = slice(a * STATE_HALF, (a + 1) * STATE_HALF)
                wb_scr[rows, cols] = _expand_block_diag(wbc_ref[s, a])
                wct_scr[rows, cols] = _expand_block_diag(wcc_ref[s + 1, a])
        bbar = wb_scr[(L - 1) * LANES:L * LANES, :]
        nt = (((1,), (1,)), ((), ()))
        c0 = jnp.concatenate([_expand_block_diag(wcc_ref[0, a]) for a in range(2)], axis=-1)
        tap_scr[0] = lax.dot_general(bbar, c0, nt, preferred_element_type=F32).astype(BF16)
        for t in range(1, L):
            tap_scr[t] = lax.dot_general(bbar, wct_scr[(t - 1) * LANES:t * LANES, :], nt,
                                         preferred_element_type=F32).astype(BF16)
        for sp in range(L):
            for s in range(L):
                rows = slice(s * LANES, (s + 1) * LANES)
                cols = slice(sp * LANES, (sp + 1) * LANES)
                if s <= sp:
                    w_scr[rows, cols] = tap_scr[sp - s]
                elif s // per_tile == sp // per_tile:
                    w_scr[rows, cols] = jnp.zeros((LANES, LANES), BF16)

    nk = STATE_W // LANES
    half = nk // 2
    inc_all = jnp.dot(u_ref[...], wb_scr[...], preferred_element_type=F32)
    for k in range(nk):
        s_scr[k] = inc_all[:, k * LANES:(k + 1) * LANES]

    d = d_ref[...]
    for n in range(L // per_tile):
        k = (n + 1) * MXU_N
        y = jnp.dot(u_ref[:, :k], w_scr[:k, n * MXU_N:(n + 1) * MXU_N], preferred_element_type=F32)
        for h in range(per_tile):
            s = n * per_tile + h
            us = u_ref[:, s * LANES:(s + 1) * LANES].astype(F32)
            y_ref[s] = y[:, h * LANES:(h + 1) * LANES] + d * us

    h0 = h0_ref[...]
    a = a_ref[...]

    def piece(x, k):
        return x[:, k * LANES:(k + 1) * LANES]

    ar = [piece(a[0:1], k) for k in range(half)]
    ai = [piece(a[1:2], k) for k in range(half)]
    state = [piece(h0, k) for k in range(nk)]
    for c in range(C):
        rows = pl.ds(c, bt, stride=C) if C > 1 else pl.ds(0, bt)
        nxt = []
        for k in range(half):
            hr, hi = state[k], state[half + k]
            hp_scr[k, rows, :] = hr
            hp_scr[half + k, rows, :] = hi
            nxt.append(ar[k] * hr - ai[k] * hi + s_scr[k, rows, :])
        for k in range(half):
            hr, hi = state[k], state[half + k]
            nxt.append(ar[k] * hi + ai[k] * hr + s_scr[half + k, rows, :])
        state = nxt
    for k in range(nk):
        hl_ref[:, k * LANES:(k + 1) * LANES] = state[k]

    hp = jnp.concatenate([hp_scr[k] for k in range(nk)], axis=-1).astype(BF16)
    for n in range(L // per_tile):
        y = lax.dot_general(hp, wct_scr[n * MXU_N:(n + 1) * MXU_N, :], (((1,), (1,)), ((), ())),
                            preferred_element_type=F32)
        for h in range(per_tile):
            s = n * per_tile + h
            y_ref[s] += y[:, h * LANES:(h + 1) * LANES]


def _ssm(u4, wbc, wcc, a_pow, d_skip, h0, *, L, C, bt):
    R = u4.shape[1]
    rows = bt * C
    nb = R // rows
    lw = L * LANES
    in_specs = [
        pl.BlockSpec((None, rows, lw), lambda j, b: (j, b, 0)),
        pl.BlockSpec((None, L, 2, LANES, LANES), lambda j, b: (j, 0, 0, 0, 0)),
        pl.BlockSpec((None, L + 1, 2, LANES, LANES), lambda j, b: (j, 0, 0, 0, 0)),
        pl.BlockSpec((None, 2, STATE_HALF), lambda j, b: (j, 0, 0)),
        pl.BlockSpec((None, 1, LANES), lambda j, b: (j, 0, 0)),
        pl.BlockSpec((None, None, bt, STATE_W), lambda j, b: (j, b, 0, 0)),
    ]
    out_specs = [
        pl.BlockSpec((L, rows, LANES), lambda j, b: (0, b, j)),
        pl.BlockSpec((None, None, bt, STATE_W), lambda j, b: (j, b, 0, 0)),
    ]
    out_shape = [
        jax.ShapeDtypeStruct((L, R, D_SSM), F32),
        jax.ShapeDtypeStruct((N_LANE_BLOCKS, nb, bt, STATE_W), F32),
    ]
    scratch = [
        pltpu.VMEM((lw, lw), BF16),
        pltpu.VMEM((lw, STATE_W), BF16),
        pltpu.VMEM((lw, STATE_W), BF16),
        pltpu.VMEM((STATE_W // LANES, rows, LANES), F32),
        pltpu.VMEM((STATE_W // LANES, rows, LANES), F32),
        pltpu.VMEM((L, LANES, LANES), BF16),
    ]
    return pl.pallas_call(
        functools.partial(_ssm_kernel, L=L, C=C, bt=bt),
        grid=(N_LANE_BLOCKS, nb),
        in_specs=in_specs,
        out_specs=out_specs,
        out_shape=out_shape,
        scratch_shapes=scratch,
        compiler_params=_params(2),
        name="ssm",
    )(u4, wbc, wcc, a_pow, d_skip, h0)


def _ssm_weights(lam_re, lam_im, log_dt, b_re, b_im, c_re, c_im, L):
    dt = jnp.exp(log_dt)[:, None]
    zr = lam_re * dt
    zi = lam_im * dt
    n_pow = jnp.arange(L + 1, dtype=F32)[:, None, None]
    mag = jnp.exp(zr[None] * n_pow)
    pr = mag * jnp.cos(zi[None] * n_pow)
    pi = mag * jnp.sin(zi[None] * n_pow)
    a1r, a1i = pr[1], pi[1]
    den = lam_re * lam_re + lam_im * lam_im
    qr = ((a1r - 1.0) * lam_re + a1i * lam_im) / den
    qi = (a1i * lam_re - (a1r - 1.0) * lam_im) / den
    bbr = qr[:, :, None] * b_re - qi[:, :, None] * b_im
    bbi = qr[:, :, None] * b_im + qi[:, :, None] * b_re

    nj, g8 = N_LANE_BLOCKS, GROUPS_PER_BLOCK

    n_rev = (L - 1.0) - jnp.arange(L, dtype=F32)[:, None, None]
    mag_rev = jnp.exp(zr[None] * n_rev)
    rev = mag_rev * jnp.cos(zi[None] * n_rev), mag_rev * jnp.sin(zi[None] * n_rev)
    er = rev[0][:, :, :, None] * bbr[None] - rev[1][:, :, :, None] * bbi[None]
    ei = rev[0][:, :, :, None] * bbi[None] + rev[1][:, :, :, None] * bbr[None]
    e = jnp.stack([er, ei], 1).reshape(L, 2, nj, g8, N_STATE, P_GROUP)
    e = jnp.transpose(e, (2, 0, 1, 3, 5, 4)).reshape(nj, L, 2, LANES, N_STATE)
    wbc = jnp.concatenate([e, e], axis=-1)

    mr = c_re[None] * pr[:, :, None, :] - c_im[None] * pi[:, :, None, :]
    mi = c_re[None] * pi[:, :, None, :] + c_im[None] * pr[:, :, None, :]
    m = jnp.stack([mr, -mi], 1).reshape(L + 1, 2, nj, g8, P_GROUP, N_STATE)
    m = jnp.transpose(m, (2, 0, 1, 3, 4, 5)).reshape(nj, L + 1, 2, LANES, N_STATE)
    wcc = jnp.concatenate([m, m], axis=-1)

    def a_pow(n):
        return jnp.stack([pr[n].reshape(nj, STATE_HALF), pi[n].reshape(nj, STATE_HALF)], 1)

    return wbc.astype(BF16), wcc.astype(BF16), a_pow


def _ssm_weights_prefix(full, L_full, L):
    wbc, wcc, a_pow = full
    return wbc[:, L_full - L:], wcc[:, :L + 1], a_pow(L)


def _pack_state(re, im):
    b = re.shape[0]
    r = re.reshape(b, N_LANE_BLOCKS, STATE_HALF)
    i = im.reshape(b, N_LANE_BLOCKS, STATE_HALF)
    return jnp.transpose(jnp.concatenate([r, i], -1), (1, 0, 2))


def _unpack_state(h):
    b = h.shape[1]
    h = jnp.transpose(h, (1, 0, 2))
    re = h[:, :, :STATE_HALF].reshape(b, N_GROUPS, N_STATE)
    im = h[:, :, STATE_HALF:].reshape(b, N_GROUPS, N_STATE)
    return re, im


def _mixout_kernel(c_ref, y_ref, x_ref, wglu_ref, wout_ref, g_ref, o_ref, wglu_scr, wout_scr, *, L, ct, sub):
    _cast_weights_once(2, (wglu_ref, wglu_scr), (wout_ref, wout_scr))
    for c0, n in _sub_tiles(ct, sub):
        rows = L * n
        gy = jax.nn.gelu(y_ref[:, c0:c0 + n, :].reshape(rows, D_SSM))
        sg = gy * jax.nn.sigmoid(jnp.dot(gy.astype(BF16), wglu_scr[...], preferred_element_type=F32))
        m = jnp.dot(c_ref[:, c0:c0 + n, :].reshape(rows, D_CONV), wout_scr[0:D_CONV, :],
                    preferred_element_type=F32)
        m = m + jnp.dot(sg.astype(BF16), wout_scr[D_CONV:, :], preferred_element_type=F32)
        x = _gather_rows(x_ref, L, c0, n)
        o_ref[:, c0:c0 + n, :] = (x + _rms(m, g_ref[...])).reshape(L, n, D_MODEL)


def _mixout(c, y, x4, w_glu, w_out, g1, *, L, ct, sub):
    bq, cc = x4.shape[:2]
    nc = cc // ct
    R = bq * cc
    return pl.pallas_call(
        functools.partial(_mixout_kernel, L=L, ct=ct, sub=sub),
        grid=(bq, nc),
        in_specs=[
            _tp_spec(ct, L, D_CONV, nc),
            _tp_spec(ct, L, D_SSM, nc),
            _nat_spec(ct, L),
            _const((D_SSM, D_SSM)),
            _const((D_CONV + D_SSM, D_MODEL)),
            _const((1, D_MODEL)),
        ],
        out_specs=_tp_spec(ct, L, D_MODEL, nc),
        out_shape=jax.ShapeDtypeStruct((L, R, D_MODEL), F32),
        scratch_shapes=[pltpu.VMEM(w_glu.shape, BF16), pltpu.VMEM(w_out.shape, BF16)],
        compiler_params=_params(2),
        name="mixout",
    )(c, y, x4, w_glu, w_out, g1)


def _memkv_kernel(m_ref, g_ref, wk_ref, wv_ref, k5_ref, v5_ref, kb_ref, vb_ref, wk_scr, wv_scr, *, nb):
    _cast_weights_once(1, (wk_ref, wk_scr), (wv_ref, wv_scr))
    for b in range(nb):
        m = _rms(m_ref[b], g_ref[...]).astype(BF16)
        for w_ref, o5_ref, ob_ref in ((wk_scr, k5_ref, kb_ref), (wv_scr, v5_ref, vb_ref)):
            p = jnp.dot(m, w_ref[...], preferred_element_type=F32)
            ob_ref[b] = p.astype(BF16)
            for hd in range(N_HEADS):
                o5_ref[b, :, hd, :] = p[:, hd * HEAD_DIM:(hd + 1) * HEAD_DIM]


def _memkv(mem, g_mem, w_k, w_v):
    bq = mem.shape[0]
    nb = 2
    out5 = pl.BlockSpec((nb, N_MEM, N_HEADS, HEAD_DIM), lambda i: (i, 0, 0, 0))
    outb = pl.BlockSpec((nb, N_MEM, D_MODEL), lambda i: (i, 0, 0))
    return pl.pallas_call(
        functools.partial(_memkv_kernel, nb=nb),
        grid=(bq // nb,),
        in_specs=[
            pl.BlockSpec((nb, N_MEM, D_MODEL), lambda i: (i, 0, 0)),
            _const((1, D_MODEL)),
            _const((D_MODEL, D_MODEL)),
            _const((D_MODEL, D_MODEL)),
        ],
        out_specs=[out5, out5, outb, outb],
        out_shape=[jax.ShapeDtypeStruct((bq, N_MEM, N_HEADS, HEAD_DIM), F32)] * 2
        + [jax.ShapeDtypeStruct((bq, N_MEM, D_MODEL), BF16)] * 2,
        scratch_shapes=[pltpu.VMEM(w_k.shape, BF16), pltpu.VMEM(w_v.shape, BF16)],
        compiler_params=_params(1),
        name="memkv",
    )(mem, g_mem, w_k, w_v)


def _softmax_rows(s):
    s = s - jnp.max(s, axis=-1, keepdims=True)
    e = jnp.exp(s)
    return e / jnp.sum(e, axis=-1, keepdims=True)


def _attn_prompt_kernel(x_ref, k_ref, v_ref, wq32_ref, wo32_ref, gq_ref, go_ref, o_ref, wq_ref, wo_ref,
                        *, L, ct, sub):
    _cast_weights_once(2, (wq32_ref, wq_ref), (wo32_ref, wo_ref))
    kb = k_ref[...]
    vb = v_ref[...]
    for c0, n in _sub_tiles(ct, sub):
        rows = L * n
        x = x_ref[:, c0:c0 + n, :].reshape(rows, D_MODEL)
        h = _rms(x, gq_ref[...])
        q = jnp.dot(h.astype(BF16), wq_ref[...], preferred_element_type=F32).astype(BF16)
        heads = []
        for hd in range(N_HEADS):
            sl = slice(hd * HEAD_DIM, (hd + 1) * HEAD_DIM)
            sc = lax.dot_general(q[:, sl], kb[:, sl], (((1,), (1,)), ((), ())), preferred_element_type=F32)
            p = _softmax_rows(sc * (HEAD_DIM ** -0.5))
            heads.append(jnp.dot(p.astype(BF16), vb[:, sl], preferred_element_type=F32))
        o = jnp.concatenate(heads, axis=-1)
        a = jnp.dot(o.astype(BF16), wo_ref[...], preferred_element_type=F32)
        o_ref[:, c0:c0 + n, :] = (x + _rms(a, go_ref[...])).reshape(L, n, D_MODEL)


def _attn_prompt(x1, k, v, w_q, w_o, g2, g3, *, C, ct, sub):
    L, R, _ = x1.shape
    nc = C // ct
    return pl.pallas_call(
        functools.partial(_attn_prompt_kernel, L=L, ct=ct, sub=sub),
        grid=(R // C, nc),
        in_specs=[
            _tp_spec(ct, L, D_MODEL, nc),
            pl.BlockSpec((None, N_MEM, D_MODEL), lambda b, i: (b, 0, 0)),
            pl.BlockSpec((None, N_MEM, D_MODEL), lambda b, i: (b, 0, 0)),
            _const((D_MODEL, D_MODEL)),
            _const((D_MODEL, D_MODEL)),
            _const((1, D_MODEL)),
            _const((1, D_MODEL)),
        ],
        out_specs=_tp_spec(ct, L, D_MODEL, nc),
        out_shape=jax.ShapeDtypeStruct((L, R, D_MODEL), F32),
        scratch_shapes=[pltpu.VMEM(w_q.shape, BF16), pltpu.VMEM(w_o.shape, BF16)],
        compiler_params=_params(2),
        name="attn_prompt",
    )(x1, k, v, w_q, w_o, g2, g3)


def _qproj_kernel(x_ref, wq_ref, g_ref, q_ref):
    h = _rms(x_ref[...], g_ref[...])
    q_ref[...] = jnp.dot(h.astype(BF16), wq_ref[...].astype(BF16), preferred_element_type=F32)


def _qproj(x1f, w_q, g2):
    rows = x1f.shape[0]
    return pl.pallas_call(
        _qproj_kernel,
        grid=(1,),
        in_specs=[_const((rows, D_MODEL)), _const((D_MODEL, D_MODEL)), _const((1, D_MODEL))],
        out_specs=pl.BlockSpec((rows, D_MODEL), lambda i: (0, 0)),
        out_shape=jax.ShapeDtypeStruct((rows, D_MODEL), F32),
        compiler_params=_params(1),
        name="qproj_sample",
    )(x1f, w_q, g2)


def _attn_sample_kernel(q_ref, k_ref, v_ref, o_ref, *, L, bb):
    rows = L * bb
    q2 = q_ref[...].reshape(rows, D_MODEL)
    qs = jnp.concatenate([q2[:, h * HEAD_DIM:(h + 1) * HEAD_DIM] for h in range(N_HEADS)], axis=0).astype(BF16)
    nr = N_HEADS * rows
    nc = N_MEM * N_HEADS
    row_head = lax.broadcasted_iota(jnp.int32, (nr, nc), 0) // rows
    col_head = lax.broadcasted_iota(jnp.int32, (nr, nc), 1) % N_HEADS
    same_head = row_head == col_head
    owner = lax.broadcasted_iota(jnp.int32, (nr, HEAD_DIM), 0) % bb
    acc = jnp.zeros((nr, HEAD_DIM), F32)
    for b in range(bb):
        ka = k_ref[b].reshape(nc, HEAD_DIM).astype(BF16)
        va = v_ref[b].reshape(nc, HEAD_DIM).astype(BF16)
        sc = lax.dot_general(qs, ka, (((1,), (1,)), ((), ())), preferred_element_type=F32)
        p = _softmax_rows(jnp.where(same_head, sc * (HEAD_DIM ** -0.5), -1e30))
        o = jnp.dot(p.astype(BF16), va, preferred_element_type=F32)
        acc = jnp.where(owner == b, o, acc)
    for h in range(N_HEADS):
        o_ref[:, :, h * HEAD_DIM:(h + 1) * HEAD_DIM] = acc[h * rows:(h + 1) * rows].reshape(L, bb, HEAD_DIM)


def _attn_sample(q3, k5, v5):
    L, R, _ = q3.shape
    bb = 8
    kv_spec = pl.BlockSpec((None, bb, N_MEM, N_HEADS, HEAD_DIM), lambda i: (0, i, 0, 0, 0))
    return pl.pallas_call(
        functools.partial(_attn_sample_kernel, L=L, bb=bb),
        grid=(R // bb,),
        in_specs=[pl.BlockSpec((L, bb, D_MODEL), lambda i: (0, i, 0)), kv_spec, kv_spec],
        out_specs=pl.BlockSpec((L, bb, D_MODEL), lambda i: (0, i, 0)),
        out_shape=jax.ShapeDtypeStruct((L, R, D_MODEL), F32),
        compiler_params=_params(1),
        name="attn_sample",
    )(q3, k5, v5)


def _oproj_kernel(o_ref, x_ref, wo_ref, g_ref, y_ref):
    a = jnp.dot(o_ref[...].astype(BF16), wo_ref[...].astype(BF16), preferred_element_type=F32)
    y_ref[...] = x_ref[...] + _rms(a, g_ref[...])


def _oproj(of, x1f, w_o, g3):
    rows = of.shape[0]
    return pl.pallas_call(
        _oproj_kernel,
        grid=(1,),
        in_specs=[_const((rows, D_MODEL)), _const((rows, D_MODEL)), _const((D_MODEL, D_MODEL)),
                  _const((1, D_MODEL))],
        out_specs=pl.BlockSpec((rows, D_MODEL), lambda i: (0, 0)),
        out_shape=jax.ShapeDtypeStruct((rows, D_MODEL), F32),
        compiler_params=_params(1),
        name="oproj_sample",
    )(of, x1f, w_o, g3)


def _ffn_kernel(x_ref, wg_ref, wu_ref, wd_ref, gi_ref, go_ref, o_ref, *, L, ct):
    rows = L * ct
    x = x_ref[...].reshape(rows, D_MODEL)
    h = _rms(x, gi_ref[...]).astype(BF16)
    gate = jnp.dot(h, wg_ref[...], preferred_element_type=F32)
    up = jnp.dot(h, wu_ref[...], preferred_element_type=F32)
    act = (gate * jax.nn.sigmoid(gate) * up).astype(BF16)
    dn = jnp.dot(act, wd_ref[...], preferred_element_type=F32)
    y = x + _rms(dn, go_ref[...])
    for s in range(L):
        o_ref[:, s, :] = y[s * ct:(s + 1) * ct]


def _ffn(x2, w_gate, w_up, w_down, g4, g5, *, bq, C, ct):
    L = x2.shape[0]
    nc = C // ct
    return pl.pallas_call(
        functools.partial(_ffn_kernel, L=L, ct=ct),
        grid=(bq, nc),
        in_specs=[
            _tp_spec(ct, L, D_MODEL, nc),
            _const((D_MODEL, D_FF)),
            _const((D_MODEL, D_FF)),
            _const((D_FF, D_MODEL)),
            _const((1, D_MODEL)),
            _const((1, D_MODEL)),
        ],
        out_specs=_nat_spec(ct, L),
        out_shape=jax.ShapeDtypeStruct((bq, C, L, D_MODEL), F32),
        compiler_params=_params(2),
        name="ffn",
    )(x2, w_gate, w_up, w_down, g4, g5)


def _layer(x4, mem_k, mem_v, conv_state_t, h0_packed, wts, *, ct, sub, ffn_ct, sc, bt, sample):
    bq, C, L, _ = x4.shape
    R = bq * C
    ng = wts["norm_g"]
    g = [ng[i:i + 1] for i in range(6)]

    v, u4 = _inproj(x4, g[0], wts["w_in"], L=L, ct=ct, sub=sub)
    sw = wts["ssm_L%d" % L]
    nb = R // (bt * sc)
    ssm_args = (u4, sw[0], sw[1], sw[2], wts["d_skip"], h0_packed.reshape(N_LANE_BLOCKS, nb, bt, STATE_W))
    conv_args = (wts["w_dw"], wts["b_dw"], wts["ln_g"], wts["ln_b"])
    y, hl = _ssm(*ssm_args, L=L, C=sc, bt=bt)
    if sample:
        cact, conv_new = _conv_sample(v, conv_state_t, *conv_args)
    else:
        cact = _conv_prompt(v, *conv_args, L=L, C=C)
        tail = v.reshape(L, bq, C, D_CONV)[:, :, C - 2:, :]
        tail = jnp.transpose(tail, (1, 2, 0, 3)).reshape(bq, 2 * L, D_CONV)
        conv_new = tail[:, 2 * L - (CONV_WIDTH - 1):, :]
    hl = hl.reshape(N_LANE_BLOCKS, nb * bt, STATE_W)

    x1 = _mixout(cact, y, x4, wts["w_glu"], wts["w_out"], g[1], L=L, ct=ct, sub=sub)

    if sample:
        q = _qproj(x1.reshape(L * R, D_MODEL), wts["w_q"], g[2])
        o = _attn_sample(q.reshape(L, R, D_MODEL), mem_k, mem_v)
        x2 = _oproj(o.reshape(L * R, D_MODEL), x1.reshape(L * R, D_MODEL), wts["w_o"], g[3])
        x2 = x2.reshape(L, R, D_MODEL)
    else:
        x2 = _attn_prompt(x1, mem_k, mem_v, wts["w_q"], wts["w_o"], g[2], g[3], C=C, ct=ct, sub=sub)

    out = _ffn(x2, wts["w_gate"], wts["w_up"], wts["w_down"], g[4], g[5], bq=bq, C=C, ct=ffn_ct)
    return out, conv_new, hl


PROMPT_L = 16
PROMPT_SUB = 32
PROMPT_CT = 64


def kernel(x_prompt, x_sample, mem_prompt, cache_mem_k, cache_mem_v, state_conv, state_ssm_re, state_ssm_im,
           norm_g, mem_norm_g, w_in, w_dw, b_dw, ln_g, ln_b, lam_re, lam_im, log_dt, b_re, b_im, c_re, c_im,
           d_skip, w_glu, w_out, w_q, w_k, w_v, w_o, w_gate, w_up, w_down):
    depth = w_in.shape[0]
    bp, tp, _ = x_prompt.shape
    bs, ts, _ = x_sample.shape
    assert tp % (PROMPT_L * PROMPT_CT) == 0 and tp >= CONV_WIDTH - 1

    yp = x_prompt.reshape(bp, tp // PROMPT_L, PROMPT_L, D_MODEL)
    ys = x_sample.reshape(1, bs, ts, D_MODEL)
    outs = [[] for _ in range(8)]
    for l in range(depth):
        wts = {
            "norm_g": norm_g[l],
            "w_in": w_in[l],
            "w_dw": w_dw[l],
            "b_dw": b_dw[l][None],
            "ln_g": ln_g[l][None],
            "ln_b": ln_b[l][None],
            "d_skip": d_skip[l].reshape(N_LANE_BLOCKS, 1, LANES),
            "w_glu": w_glu[l],
            "w_out": w_out[l],
            "w_q": w_q[l],
            "w_o": w_o[l],
            "w_gate": w_gate[l].astype(BF16),
            "w_up": w_up[l].astype(BF16),
            "w_down": w_down[l].astype(BF16),
        }
        ssm_args = (lam_re[l], lam_im[l], log_dt[l], b_re[l], b_im[l], c_re[l], c_im[l])
        assert ts <= PROMPT_L
        full = _ssm_weights(*ssm_args, PROMPT_L)
        wts["ssm_L%d" % PROMPT_L] = full[:2] + (full[2](PROMPT_L),)
        wts["ssm_L%d" % ts] = _ssm_weights_prefix(full, PROMPT_L, ts)

        kp, vp, kp16, vp16 = _memkv(mem_prompt, mem_norm_g[l][None], w_k[l], w_v[l])
        h0p = jnp.zeros((N_LANE_BLOCKS, bp, STATE_W), F32)
        yp, cp, hp = _layer(yp, kp16, vp16, None, h0p, wts, ct=PROMPT_CT, sub=PROMPT_SUB, ffn_ct=PROMPT_SUB,
                            sc=tp // PROMPT_L, bt=4, sample=False)
        h0s = _pack_state(state_ssm_re[l], state_ssm_im[l])
        ys, cs, hs = _layer(ys, cache_mem_k[l:l + 1], cache_mem_v[l:l + 1], jnp.transpose(state_conv[l], (1, 0, 2)),
                            h0s, wts, ct=bs, sub=bs, ffn_ct=bs, sc=1, bt=bs, sample=True)
        cs = jnp.transpose(cs, (1, 0, 2))

        hp_re, hp_im = _unpack_state(hp)
        hs_re, hs_im = _unpack_state(hs)
        for lst, val in zip(outs, (kp, vp, cp, hp_re, hp_im, cs, hs_re, hs_im)):
            lst.append(val)
    return (yp.reshape(bp, tp, D_MODEL), ys.reshape(bs, ts, D_MODEL)) + tuple(jnp.stack(o) for o in outs)
```

```python
import functools

import jax
import jax.numpy as jnp
from jax import lax
from jax.experimental import pallas as pl
from jax.experimental.pallas import tpu as pltpu

F32 = jnp.float32
BF16 = jnp.bfloat16

D_MODEL = 1024
D_CONV = 512
D_SSM = 512
CONV_WIDTH = 31
N_GROUPS = 32
P_GROUP = 16
N_STATE = 64
N_MEM = 256
N_HEADS = 4
HEAD_DIM = 256
D_FF = 2816
RMS_EPS = 1e-6
LN_EPS = 1e-5

LANES = 128
SUBLANES = 8
GROUPS_PER_BLOCK = LANES // P_GROUP
N_LANE_BLOCKS = D_SSM // LANES
STATE_HALF = GROUPS_PER_BLOCK * N_STATE
STATE_W = 2 * STATE_HALF
MXU_N = 256
VMEM_LIMIT = 56 * 1024 * 1024


def _params(n_axes, vmem=VMEM_LIMIT):
    return pltpu.CompilerParams(dimension_semantics=("arbitrary",) * n_axes, vmem_limit_bytes=vmem)


def _const(shape):
    nd = len(shape)
    return pl.BlockSpec(shape, lambda *_: (0,) * nd, pipeline_mode=pl.Buffered(1))


def _rms(x, g):
    return x * lax.rsqrt(jnp.mean(x * x, axis=-1, keepdims=True) + RMS_EPS) * g


def _cast_weights_once(n_axes, *pairs):
    first = pl.program_id(0) == 0
    for a in range(1, n_axes):
        first = jnp.logical_and(first, pl.program_id(a) == 0)

    @pl.when(first)
    def _():
        for src, dst in pairs:
            dst[...] = src[...].astype(BF16)


def _gather_rows(x_ref, L, c0, n):
    return jnp.concatenate([x_ref[c0:c0 + n, s, :] for s in range(L)], axis=0)


def _sub_tiles(ct, sub):
    return [(c0, sub) for c0 in range(0, ct, sub)]


def _nat_spec(ct, L):
    return pl.BlockSpec((None, ct, L, D_MODEL), lambda b, i: (b, i, 0, 0))


def _tp_spec(ct, L, width, nc):
    return pl.BlockSpec((L, ct, width), lambda b, i: (0, b * nc + i, 0))


def _inproj_kernel(x_ref, g_ref, w_ref, v_ref, u_ref, w16_scr, *, L, ct, sub):
    _cast_weights_once(2, (w_ref, w16_scr))
    for c0, n in _sub_tiles(ct, sub):
        x = _gather_rows(x_ref, L, c0, n)
        h = _rms(x, g_ref[...])
        z = jnp.dot(h.astype(BF16), w16_scr[...], preferred_element_type=F32)
        a = z[:, :D_CONV]
        g = z[:, D_CONV:2 * D_CONV]
        u = z[:, 2 * D_CONV:].astype(BF16)
        v_ref[:, c0:c0 + n, :] = (a * jax.nn.sigmoid(g)).reshape(L, n, D_CONV)
        for s in range(L):
            for j in range(N_LANE_BLOCKS):
                u_ref[j, c0:c0 + n, s * LANES:(s + 1) * LANES] = u[s * n:(s + 1) * n, j * LANES:(j + 1) * LANES]


def _inproj(x4, g0, w_in, *, L, ct, sub):
    bq, c = x4.shape[:2]
    nc = c // ct
    R = bq * c
    return pl.pallas_call(
        functools.partial(_inproj_kernel, L=L, ct=ct, sub=sub),
        grid=(bq, nc),
        in_specs=[_nat_spec(ct, L), _const((1, D_MODEL)), _const((D_MODEL, 2 * D_CONV + D_SSM))],
        out_specs=[
            _tp_spec(ct, L, D_CONV, nc),
            pl.BlockSpec((N_LANE_BLOCKS, ct, L * LANES), lambda b, i: (0, b * nc + i, 0)),
        ],
        out_shape=[
            jax.ShapeDtypeStruct((L, R, D_CONV), F32),
            jax.ShapeDtypeStruct((N_LANE_BLOCKS, R, L * LANES), BF16),
        ],
        scratch_shapes=[pltpu.VMEM(w_in.shape, BF16)],
        compiler_params=_params(2),
        name="inproj",
    )(x4, g0, w_in)


def _ln_silu(acc, g, b):
    mu = jnp.mean(acc, axis=-1, keepdims=True)
    xc = acc - mu
    var = jnp.mean(xc * xc, axis=-1, keepdims=True)
    y = xc * lax.rsqrt(var + LN_EPS) * g + b
    return y * jax.nn.sigmoid(y)


def _conv_shifted_copies(v_ref, v1_ref, v2_ref, *, L, C):
    row = lax.broadcasted_iota(jnp.int32, (C, D_CONV), 0)
    for s in range(L):
        x = v_ref[s]
        v1_ref[s] = jnp.where(row >= 1, pltpu.roll(x, 1, 0), 0.0)
        v2_ref[s] = jnp.where(row >= 2, pltpu.roll(x, 2, 0), 0.0)


def _conv_rows(srcs, w_ref, b_ref, g_ref, bb_ref, o_ref, r0, *, L, rc):
    sub8 = rc // SUBLANES
    for s in range(L):
        acc = jnp.broadcast_to(b_ref[...][None], (sub8, SUBLANES, D_CONV))
        for d in range(CONV_WIDTH):
            blk = (s - d) % L
            shift = (d - s + L - 1) // L if d > s else 0
            k = CONV_WIDTH - 1 - d
            src = srcs[shift][blk, pl.ds(r0, rc), :].reshape(sub8, SUBLANES, D_CONV)
            acc = acc + w_ref[k][None] * src
        y = _ln_silu(acc.reshape(rc, D_CONV), g_ref[...], bb_ref[...])
        o_ref[s, pl.ds(r0, rc), :] = y.astype(o_ref.dtype)


def _conv_prompt_kernel(v_ref, w_ref, b_ref, g_ref, bb_ref, o_ref, v1_ref, v2_ref, *, L, C, rc):
    _conv_shifted_copies(v_ref, v1_ref, v2_ref, L=L, C=C)
    srcs = (v_ref, v1_ref, v2_ref)

    def chunk(i, carry):
        _conv_rows(srcs, w_ref, b_ref, g_ref, bb_ref, o_ref, pl.multiple_of(i * rc, rc), L=L, rc=rc)
        return carry

    lax.fori_loop(0, C // rc, chunk, 0)


def _conv_prompt(v, w_dw, b_dw, ln_g, ln_b, *, L, C):
    R = v.shape[1]
    rc = 32
    return pl.pallas_call(
        functools.partial(_conv_prompt_kernel, L=L, C=C, rc=rc),
        grid=(R // C,),
        in_specs=[
            pl.BlockSpec((L, C, D_CONV), lambda b: (0, b, 0)),
            _const((CONV_WIDTH, SUBLANES, D_CONV)),
            _const((SUBLANES, D_CONV)),
            _const((1, D_CONV)),
            _const((1, D_CONV)),
        ],
        out_specs=pl.BlockSpec((L, C, D_CONV), lambda b: (0, b, 0)),
        out_shape=jax.ShapeDtypeStruct((L, R, D_CONV), BF16),
        scratch_shapes=[pltpu.VMEM((L, C, D_CONV), F32)] * 2,
        compiler_params=_params(1),
        name="conv_prompt",
    )(v, jnp.broadcast_to(w_dw[:, None, :], (CONV_WIDTH, SUBLANES, D_CONV)),
      jnp.broadcast_to(b_dw, (SUBLANES, D_CONV)), ln_g, ln_b)


def _conv_sample_kernel(v_ref, st_ref, w_ref, b_ref, g_ref, bb_ref, o_ref, new_ref, *, L, H):
    def ext(i):
        return st_ref[i] if i < H else v_ref[i - H]

    for t in range(L):
        acc = jnp.broadcast_to(b_ref[...], v_ref.shape[1:])
        for k in range(CONV_WIDTH):
            acc = acc + w_ref[k:k + 1, :] * ext(t + k)
        o_ref[t] = _ln_silu(acc, g_ref[...], bb_ref[...]).astype(o_ref.dtype)
    for i in range(H):
        new_ref[i] = ext(i + L)


def _conv_sample(v, state_t, w_dw, b_dw, ln_g, ln_b):
    L, R, _ = v.shape
    H = CONV_WIDTH - 1
    return pl.pallas_call(
        functools.partial(_conv_sample_kernel, L=L, H=H),
        grid=(1,),
        in_specs=[
            _const((L, R, D_CONV)),
            _const((H, R, D_CONV)),
            _const((CONV_WIDTH, D_CONV)),
            _const((1, D_CONV)),
            _const((1, D_CONV)),
            _const((1, D_CONV)),
        ],
        out_specs=[
            pl.BlockSpec((L, R, D_CONV), lambda i: (0, 0, 0)),
            pl.BlockSpec((H, R, D_CONV), lambda i: (0, 0, 0)),
        ],
        out_shape=[
            jax.ShapeDtypeStruct((L, R, D_CONV), BF16),
            jax.ShapeDtypeStruct((H, R, D_CONV), F32),
        ],
        compiler_params=_params(1),
        name="conv_sample",
    )(v, state_t, w_dw, b_dw, ln_g, ln_b)


def _expand_block_diag(d):
    tiled = jnp.concatenate([d] * GROUPS_PER_BLOCK, axis=-1)
    r = lax.broadcasted_iota(jnp.int32, tiled.shape, 0) // P_GROUP
    c = lax.broadcasted_iota(jnp.int32, tiled.shape, 1) // N_STATE
    return jnp.where(r == c, tiled, jnp.zeros_like(tiled))


def _ssm_kernel(u_ref, wbc_ref, wcc_ref, a_ref, d_ref, h0_ref, y_ref, hl_ref,
                w_scr, wb_scr, wct_scr, s_scr, hp_scr, tap_scr, *, L, C, bt):
    per_tile = MXU_N // LANES

    @pl.when(pl.program_id(1) == 0)
    def _():
        for s in range(L):
            rows = slice(s * LANES, (s + 1) * LANES)
            for a in range(2):
                cols = slice(a * STATE_HALF, (a + 1) * STATE_HALF)
                wb_scr[rows, cols] = _expand_block_diag(wbc_ref[s, a])
                wct_scr[rows, cols] = _expand_block_diag(wcc_ref[s + 1, a])
        bbar = wb_scr[(L - 1) * LANES:L * LANES, :]
        nt = (((1,), (1,)), ((), ()))
        c0 = jnp.concatenate([_expand_block_diag(wcc_ref[0, a]) for a in range(2)], axis=-1)
        tap_scr[0] = lax.dot_general(bbar, c0, nt, preferred_element_type=F32).astype(BF16)
        for t in range(1, L):
            tap_scr[t] = lax.dot_general(bbar, wct_scr[(t - 1) * LANES:t * LANES, :], nt,
                                         preferred_element_type=F32).astype(BF16)
        for sp in range(L):
            for s in range(L):
                rows = slice(s * LANES, (s + 1) * LANES)
                cols = slice(sp * LANES, (sp + 1) * LANES)
                if s <= sp:
                    w_scr[rows, cols] = tap_scr[sp - s]
                elif s // per_tile == sp // per_tile:
                    w_scr[rows, cols] = jnp.zeros((LANES, LANES), BF16)

    nk = STATE_W // LANES
    half = nk // 2
    rows = bt * C
    inc_all = jnp.dot(u_ref[...], wb_scr[...], preferred_element_type=F32)
    for k in range(half):
        s_scr[k, 0:rows, :] = inc_all[:, k * LANES:(k + 1) * LANES]
        s_scr[k, rows:2 * rows, :] = inc_all[:, (half + k) * LANES:(half + k + 1) * LANES]

    d = d_ref[...]
    for n in range(L // per_tile):
        k = (n + 1) * MXU_N
        y = jnp.dot(u_ref[:, :k], w_scr[:k, n * MXU_N:(n + 1) * MXU_N], preferred_element_type=F32)
        for h in range(per_tile):
            s = n * per_tile + h
            us = u_ref[:, s * LANES:(s + 1) * LANES].astype(F32)
            y_ref[s] = y[:, h * LANES:(h + 1) * LANES] + d * us

    h0 = h0_ref[...]
    a = a_ref[...]

    def piece(x, k):
        return x[:, k * LANES:(k + 1) * LANES]

    def swap_parts(x):
        if 2 * bt == SUBLANES:
            return pltpu.roll(x, bt, 0)
        return jnp.concatenate([x[bt:], x[:bt]], axis=0)

    im_rows = lax.broadcasted_iota(jnp.int32, (2 * bt, LANES), 0) >= bt
    a_same = [jnp.broadcast_to(piece(a[0:1], k), (2 * bt, LANES)) for k in range(half)]
    a_cross = [jnp.where(im_rows, piece(a[1:2], k), -piece(a[1:2], k)) for k in range(half)]
    state = [jnp.concatenate([piece(h0, k), piece(h0, half + k)], axis=0) for k in range(half)]
    for c in range(C):
        sel = pl.ds(c, 2 * bt, stride=C) if C > 1 else pl.ds(0, 2 * bt)
        for k in range(half):
            hp_scr[k, sel, :] = state[k]
            state[k] = a_same[k] * state[k] + a_cross[k] * swap_parts(state[k]) + s_scr[k, sel, :]
    for k in range(half):
        hl_ref[:, k * LANES:(k + 1) * LANES] = state[k][:bt]
        hl_ref[:, (half + k) * LANES:(half + k + 1) * LANES] = state[k][bt:]

    hp = jnp.concatenate([hp_scr[k, 0:rows, :] for k in range(half)]
                         + [hp_scr[k, rows:2 * rows, :] for k in range(half)], axis=-1).astype(BF16)
    for n in range(L // per_tile):
        y = lax.dot_general(hp, wct_scr[n * MXU_N:(n + 1) * MXU_N, :], (((1,), (1,)), ((), ())),
                            preferred_element_type=F32)
        for h in range(per_tile):
            s = n * per_tile + h
            y_ref[s] += y[:, h * LANES:(h + 1) * LANES]


def _ssm(u4, wbc, wcc, a_pow, d_skip, h0, *, L, C, bt):
    R = u4.shape[1]
    rows = bt * C
    nb = R // rows
    lw = L * LANES
    in_specs = [
        pl.BlockSpec((None, rows, lw), lambda j, b: (j, b, 0)),
        pl.BlockSpec((None, L, 2, LANES, N_STATE), lambda j, b: (j, 0, 0, 0, 0)),
        pl.BlockSpec((None, L + 1, 2, LANES, N_STATE), lambda j, b: (j, 0, 0, 0, 0)),
        pl.BlockSpec((None, 2, STATE_HALF), lambda j, b: (j, 0, 0)),
        pl.BlockSpec((None, 1, LANES), lambda j, b: (j, 0, 0)),
        pl.BlockSpec((None, None, bt, STATE_W), lambda j, b: (j, b, 0, 0)),
    ]
    out_specs = [
        pl.BlockSpec((L, rows, LANES), lambda j, b: (0, b, j)),
        pl.BlockSpec((None, None, bt, STATE_W), lambda j, b: (j, b, 0, 0)),
    ]
    out_shape = [
        jax.ShapeDtypeStruct((L, R, D_SSM), F32),
        jax.ShapeDtypeStruct((N_LANE_BLOCKS, nb, bt, STATE_W), F32),
    ]
    scratch = [
        pltpu.VMEM((lw, lw), BF16),
        pltpu.VMEM((lw, STATE_W), BF16),
        pltpu.VMEM((lw, STATE_W), BF16),
        pltpu.VMEM((STATE_HALF // LANES, 2 * rows, LANES), F32),
        pltpu.VMEM((STATE_HALF // LANES, 2 * rows, LANES), F32),
        pltpu.VMEM((L, LANES, LANES), BF16),
    ]
    return pl.pallas_call(
        functools.partial(_ssm_kernel, L=L, C=C, bt=bt),
        grid=(N_LANE_BLOCKS, nb),
        in_specs=in_specs,
        out_specs=out_specs,
        out_shape=out_shape,
        scratch_shapes=scratch,
        compiler_params=_params(2),
        name="ssm",
    )(u4, wbc, wcc, a_pow, d_skip, h0)


def _ssm_weights(lam_re, lam_im, log_dt, b_re, b_im, c_re, c_im, L):
    dt = jnp.exp(log_dt)[:, None]
    zr = lam_re * dt
    zi = lam_im * dt
    n_pow = jnp.arange(L + 1, dtype=F32)[:, None, None]
    mag = jnp.exp(zr[None] * n_pow)
    pr = mag * jnp.cos(zi[None] * n_pow)
    pi = mag * jnp.sin(zi[None] * n_pow)
    a1r, a1i = pr[1], pi[1]
    den = lam_re * lam_re + lam_im * lam_im
    qr = ((a1r - 1.0) * lam_re + a1i * lam_im) / den
    qi = (a1i * lam_re - (a1r - 1.0) * lam_im) / den
    bbr = qr[:, :, None] * b_re - qi[:, :, None] * b_im
    bbi = qr[:, :, None] * b_im + qi[:, :, None] * b_re

    nj, g8 = N_LANE_BLOCKS, GROUPS_PER_BLOCK

    n_rev = (L - 1.0) - jnp.arange(L, dtype=F32)[:, None, None]
    mag_rev = jnp.exp(zr[None] * n_rev)
    rev = mag_rev * jnp.cos(zi[None] * n_rev), mag_rev * jnp.sin(zi[None] * n_rev)
    er = rev[0][:, :, :, None] * bbr[None] - rev[1][:, :, :, None] * bbi[None]
    ei = rev[0][:, :, :, None] * bbi[None] + rev[1][:, :, :, None] * bbr[None]
    e = jnp.stack([er, ei], 1).reshape(L, 2, nj, g8, N_STATE, P_GROUP)
    wbc = jnp.transpose(e, (2, 0, 1, 3, 5, 4)).reshape(nj, L, 2, LANES, N_STATE)

    mr = c_re[None] * pr[:, :, None, :] - c_im[None] * pi[:, :, None, :]
    mi = c_re[None] * pi[:, :, None, :] + c_im[None] * pr[:, :, None, :]
    m = jnp.stack([mr, -mi], 1).reshape(L + 1, 2, nj, g8, P_GROUP, N_STATE)
    wcc = jnp.transpose(m, (2, 0, 1, 3, 4, 5)).reshape(nj, L + 1, 2, LANES, N_STATE)

    def a_pow(n):
        return jnp.stack([pr[n].reshape(nj, STATE_HALF), pi[n].reshape(nj, STATE_HALF)], 1)

    return wbc.astype(BF16), wcc.astype(BF16), a_pow


def _ssm_weights_prefix(full, L_full, L):
    wbc, wcc, a_pow = full
    return wbc[:, L_full - L:], wcc[:, :L + 1], a_pow(L)


def _pack_state(re, im):
    b = re.shape[0]
    r = re.reshape(b, N_LANE_BLOCKS, STATE_HALF)
    i = im.reshape(b, N_LANE_BLOCKS, STATE_HALF)
    return jnp.transpose(jnp.concatenate([r, i], -1), (1, 0, 2))


def _unpack_state(h):
    b = h.shape[1]
    h = jnp.transpose(h, (1, 0, 2))
    re = h[:, :, :STATE_HALF].reshape(b, N_GROUPS, N_STATE)
    im = h[:, :, STATE_HALF:].reshape(b, N_GROUPS, N_STATE)
    return re, im


def _mixout_kernel(c_ref, y_ref, x_ref, wglu_ref, wout_ref, g_ref, o_ref, wglu_scr, wout_scr, *, L, ct, sub):
    _cast_weights_once(2, (wglu_ref, wglu_scr), (wout_ref, wout_scr))
    for c0, n in _sub_tiles(ct, sub):
        rows = L * n
        gy = jax.nn.gelu(y_ref[:, c0:c0 + n, :].reshape(rows, D_SSM))
        sg = gy * jax.nn.sigmoid(jnp.dot(gy.astype(BF16), wglu_scr[...], preferred_element_type=F32))
        m = jnp.dot(c_ref[:, c0:c0 + n, :].reshape(rows, D_CONV), wout_scr[0:D_CONV, :],
                    preferred_element_type=F32)
        m = m + jnp.dot(sg.astype(BF16), wout_scr[D_CONV:, :], preferred_element_type=F32)
        x = _gather_rows(x_ref, L, c0, n)
        o_ref[:, c0:c0 + n, :] = (x + _rms(m, g_ref[...])).reshape(L, n, D_MODEL)


def _mixout(c, y, x4, w_glu, w_out, g1, *, L, ct, sub):
    bq, cc = x4.shape[:2]
    nc = cc // ct
    R = bq * cc
    return pl.pallas_call(
        functools.partial(_mixout_kernel, L=L, ct=ct, sub=sub),
        grid=(bq, nc),
        in_specs=[
            _tp_spec(ct, L, D_CONV, nc),
            _tp_spec(ct, L, D_SSM, nc),
            _nat_spec(ct, L),
            _const((D_SSM, D_SSM)),
            _const((D_CONV + D_SSM, D_MODEL)),
            _const((1, D_MODEL)),
        ],
        out_specs=_tp_spec(ct, L, D_MODEL, nc),
        out_shape=jax.ShapeDtypeStruct((L, R, D_MODEL), F32),
        scratch_shapes=[pltpu.VMEM(w_glu.shape, BF16), pltpu.VMEM(w_out.shape, BF16)],
        compiler_params=_params(2),
        name="mixout",
    )(c, y, x4, w_glu, w_out, g1)


def _memkv_kernel(m_ref, g_ref, wk_ref, wv_ref, k5_ref, v5_ref, kb_ref, vb_ref, wk_scr, wv_scr, *, nb):
    _cast_weights_once(1, (wk_ref, wk_scr), (wv_ref, wv_scr))
    for b in range(nb):
        m = _rms(m_ref[b], g_ref[...]).astype(BF16)
        for w_ref, o5_ref, ob_ref in ((wk_scr, k5_ref, kb_ref), (wv_scr, v5_ref, vb_ref)):
            p = jnp.dot(m, w_ref[...], preferred_element_type=F32)
            ob_ref[b] = p.astype(BF16)
            for hd in range(N_HEADS):
                o5_ref[b, :, hd, :] = p[:, hd * HEAD_DIM:(hd + 1) * HEAD_DIM]


def _memkv(mem, g_mem, w_k, w_v):
    bq = mem.shape[0]
    nb = 2
    out5 = pl.BlockSpec((nb, N_MEM, N_HEADS, HEAD_DIM), lambda i: (i, 0, 0, 0))
    outb = pl.BlockSpec((nb, N_MEM, D_MODEL), lambda i: (i, 0, 0))
    return pl.pallas_call(
        functools.partial(_memkv_kernel, nb=nb),
        grid=(bq // nb,),
        in_specs=[
            pl.BlockSpec((nb, N_MEM, D_MODEL), lambda i: (i, 0, 0)),
            _const((1, D_MODEL)),
            _const((D_MODEL, D_MODEL)),
            _const((D_MODEL, D_MODEL)),
        ],
        out_specs=[out5, out5, outb, outb],
        out_shape=[jax.ShapeDtypeStruct((bq, N_MEM, N_HEADS, HEAD_DIM), F32)] * 2
        + [jax.ShapeDtypeStruct((bq, N_MEM, D_MODEL), BF16)] * 2,
        scratch_shapes=[pltpu.VMEM(w_k.shape, BF16), pltpu.VMEM(w_v.shape, BF16)],
        compiler_params=_params(1),
        name="memkv",
    )(mem, g_mem, w_k, w_v)


def _softmax_rows(s):
    s = s - jnp.max(s, axis=-1, keepdims=True)
    e = jnp.exp(s)
    return e / jnp.sum(e, axis=-1, keepdims=True)


def _attn_prompt_kernel(x_ref, k_ref, v_ref, wq32_ref, wo32_ref, gq_ref, go_ref, o_ref, wq_ref, wo_ref,
                        *, L, ct, sub):
    _cast_weights_once(2, (wq32_ref, wq_ref), (wo32_ref, wo_ref))
    kb = k_ref[...]
    vb = v_ref[...]
    for c0, n in _sub_tiles(ct, sub):
        rows = L * n
        x = x_ref[:, c0:c0 + n, :].reshape(rows, D_MODEL)
        h = _rms(x, gq_ref[...])
        q = jnp.dot(h.astype(BF16), wq_ref[...], preferred_element_type=F32).astype(BF16)
        heads = []
        for hd in range(N_HEADS):
            sl = slice(hd * HEAD_DIM, (hd + 1) * HEAD_DIM)
            sc = lax.dot_general(q[:, sl], kb[:, sl], (((1,), (1,)), ((), ())), preferred_element_type=F32)
            p = _softmax_rows(sc * (HEAD_DIM ** -0.5))
            heads.append(jnp.dot(p.astype(BF16), vb[:, sl], preferred_element_type=F32))
        o = jnp.concatenate(heads, axis=-1)
        a = jnp.dot(o.astype(BF16), wo_ref[...], preferred_element_type=F32)
        o_ref[:, c0:c0 + n, :] = (x + _rms(a, go_ref[...])).reshape(L, n, D_MODEL)


def _attn_prompt(x1, k, v, w_q, w_o, g2, g3, *, C, ct, sub):
    L, R, _ = x1.shape
    nc = C // ct
    return pl.pallas_call(
        functools.partial(_attn_prompt_kernel, L=L, ct=ct, sub=sub),
        grid=(R // C, nc),
        in_specs=[
            _tp_spec(ct, L, D_MODEL, nc),
            pl.BlockSpec((None, N_MEM, D_MODEL), lambda b, i: (b, 0, 0)),
            pl.BlockSpec((None, N_MEM, D_MODEL), lambda b, i: (b, 0, 0)),
            _const((D_MODEL, D_MODEL)),
            _const((D_MODEL, D_MODEL)),
            _const((1, D_MODEL)),
            _const((1, D_MODEL)),
        ],
        out_specs=_tp_spec(ct, L, D_MODEL, nc),
        out_shape=jax.ShapeDtypeStruct((L, R, D_MODEL), F32),
        scratch_shapes=[pltpu.VMEM(w_q.shape, BF16), pltpu.VMEM(w_o.shape, BF16)],
        compiler_params=_params(2),
        name="attn_prompt",
    )(x1, k, v, w_q, w_o, g2, g3)


def _qproj_kernel(x_ref, wq_ref, g_ref, q_ref):
    h = _rms(x_ref[...], g_ref[...])
    q_ref[...] = jnp.dot(h.astype(BF16), wq_ref[...].astype(BF16), preferred_element_type=F32)


def _qproj(x1f, w_q, g2):
    rows = x1f.shape[0]
    return pl.pallas_call(
        _qproj_kernel,
        grid=(1,),
        in_specs=[_const((rows, D_MODEL)), _const((D_MODEL, D_MODEL)), _const((1, D_MODEL))],
        out_specs=pl.BlockSpec((rows, D_MODEL), lambda i: (0, 0)),
        out_shape=jax.ShapeDtypeStruct((rows, D_MODEL), F32),
        compiler_params=_params(1),
        name="qproj_sample",
    )(x1f, w_q, g2)


def _attn_sample_kernel(q_ref, k_ref, v_ref, o_ref, *, L, bb):
    rows = L * bb
    q2 = q_ref[...].reshape(rows, D_MODEL)
    qs = jnp.concatenate([q2[:, h * HEAD_DIM:(h + 1) * HEAD_DIM] for h in range(N_HEADS)], axis=0).astype(BF16)
    nr = N_HEADS * rows
    nc = N_MEM * N_HEADS
    row_head = lax.broadcasted_iota(jnp.int32, (nr, nc), 0) // rows
    col_head = lax.broadcasted_iota(jnp.int32, (nr, nc), 1) % N_HEADS
    same_head = row_head == col_head
    owner = lax.broadcasted_iota(jnp.int32, (nr, HEAD_DIM), 0) % bb
    acc = jnp.zeros((nr, HEAD_DIM), F32)
    for b in range(bb):
        ka = k_ref[b].reshape(nc, HEAD_DIM).astype(BF16)
        va = v_ref[b].reshape(nc, HEAD_DIM).astype(BF16)
        sc = lax.dot_general(qs, ka, (((1,), (1,)), ((), ())), preferred_element_type=F32)
        p = _softmax_rows(jnp.where(same_head, sc * (HEAD_DIM ** -0.5), -1e30))
        o = jnp.dot(p.astype(BF16), va, preferred_element_type=F32)
        acc = jnp.where(owner == b, o, acc)
    for h in range(N_HEADS):
        o_ref[:, :, h * HEAD_DIM:(h + 1) * HEAD_DIM] = acc[h * rows:(h + 1) * rows].reshape(L, bb, HEAD_DIM)


def _attn_sample(q3, k5, v5):
    L, R, _ = q3.shape
    bb = 8
    kv_spec = pl.BlockSpec((None, bb, N_MEM, N_HEADS, HEAD_DIM), lambda i: (0, i, 0, 0, 0))
    return pl.pallas_call(
        functools.partial(_attn_sample_kernel, L=L, bb=bb),
        grid=(R // bb,),
        in_specs=[pl.BlockSpec((L, bb, D_MODEL), lambda i: (0, i, 0)), kv_spec, kv_spec],
        out_specs=pl.BlockSpec((L, bb, D_MODEL), lambda i: (0, i, 0)),
        out_shape=jax.ShapeDtypeStruct((L, R, D_MODEL), F32),
        compiler_params=_params(1),
        name="attn_sample",
    )(q3, k5, v5)


def _oproj_kernel(o_ref, x_ref, wo_ref, g_ref, y_ref):
    a = jnp.dot(o_ref[...].astype(BF16), wo_ref[...].astype(BF16), preferred_element_type=F32)
    y_ref[...] = x_ref[...] + _rms(a, g_ref[...])


def _oproj(of, x1f, w_o, g3):
    rows = of.shape[0]
    return pl.pallas_call(
        _oproj_kernel,
        grid=(1,),
        in_specs=[_const((rows, D_MODEL)), _const((rows, D_MODEL)), _const((D_MODEL, D_MODEL)),
                  _const((1, D_MODEL))],
        out_specs=pl.BlockSpec((rows, D_MODEL), lambda i: (0, 0)),
        out_shape=jax.ShapeDtypeStruct((rows, D_MODEL), F32),
        compiler_params=_params(1),
        name="oproj_sample",
    )(of, x1f, w_o, g3)


def _ffn_kernel(x_ref, wg_ref, wu_ref, wd_ref, gi_ref, go_ref, o_ref, *, L, ct):
    rows = L * ct
    x = x_ref[...].reshape(rows, D_MODEL)
    h = _rms(x, gi_ref[...]).astype(BF16)
    gate = jnp.dot(h, wg_ref[...], preferred_element_type=F32)
    up = jnp.dot(h, wu_ref[...], preferred_element_type=F32)
    act = (gate * jax.nn.sigmoid(gate) * up).astype(BF16)
    dn = jnp.dot(act, wd_ref[...], preferred_element_type=F32)
    y = x + _rms(dn, go_ref[...])
    for s in range(L):
        o_ref[:, s, :] = y[s * ct:(s + 1) * ct]


def _ffn(x2, w_gate, w_up, w_down, g4, g5, *, bq, C, ct):
    L = x2.shape[0]
    nc = C // ct
    return pl.pallas_call(
        functools.partial(_ffn_kernel, L=L, ct=ct),
        grid=(bq, nc),
        in_specs=[
            _tp_spec(ct, L, D_MODEL, nc),
            _const((D_MODEL, D_FF)),
            _const((D_MODEL, D_FF)),
            _const((D_FF, D_MODEL)),
            _const((1, D_MODEL)),
            _const((1, D_MODEL)),
        ],
        out_specs=_nat_spec(ct, L),
        out_shape=jax.ShapeDtypeStruct((bq, C, L, D_MODEL), F32),
        compiler_params=_params(2),
        name="ffn",
    )(x2, w_gate, w_up, w_down, g4, g5)


def _layer(x4, mem_k, mem_v, conv_state_t, h0_packed, wts, *, ct, sub, ffn_ct, sc, bt, sample):
    bq, C, L, _ = x4.shape
    R = bq * C
    ng = wts["norm_g"]
    g = [ng[i:i + 1] for i in range(6)]

    v, u4 = _inproj(x4, g[0], wts["w_in"], L=L, ct=ct, sub=sub)
    sw = wts["ssm_L%d" % L]
    nb = R // (bt * sc)
    ssm_args = (u4, sw[0], sw[1], sw[2], wts["d_skip"], h0_packed.reshape(N_LANE_BLOCKS, nb, bt, STATE_W))
    conv_args = (wts["w_dw"], wts["b_dw"], wts["ln_g"], wts["ln_b"])
    y, hl = _ssm(*ssm_args, L=L, C=sc, bt=bt)
    if sample:
        cact, conv_new = _conv_sample(v, conv_state_t, *conv_args)
    else:
        cact = _conv_prompt(v, *conv_args, L=L, C=C)
        tail = v.reshape(L, bq, C, D_CONV)[:, :, C - 2:, :]
        tail = jnp.transpose(tail, (1, 2, 0, 3)).reshape(bq, 2 * L, D_CONV)
        conv_new = tail[:, 2 * L - (CONV_WIDTH - 1):, :]
    hl = hl.reshape(N_LANE_BLOCKS, nb * bt, STATE_W)

    x1 = _mixout(cact, y, x4, wts["w_glu"], wts["w_out"], g[1], L=L, ct=ct, sub=sub)

    if sample:
        q = _qproj(x1.reshape(L * R, D_MODEL), wts["w_q"], g[2])
        o = _attn_sample(q.reshape(L, R, D_MODEL), mem_k, mem_v)
        x2 = _oproj(o.reshape(L * R, D_MODEL), x1.reshape(L * R, D_MODEL), wts["w_o"], g[3])
        x2 = x2.reshape(L, R, D_MODEL)
    else:
        x2 = _attn_prompt(x1, mem_k, mem_v, wts["w_q"], wts["w_o"], g[2], g[3], C=C, ct=ct, sub=sub)

    out = _ffn(x2, wts["w_gate"], wts["w_up"], wts["w_down"], g[4], g[5], bq=bq, C=C, ct=ffn_ct)
    return out, conv_new, hl


PROMPT_L = 16
PROMPT_SUB = 32
PROMPT_CT = 64


def kernel(x_prompt, x_sample, mem_prompt, cache_mem_k, cache_mem_v, state_conv, state_ssm_re, state_ssm_im,
           norm_g, mem_norm_g, w_in, w_dw, b_dw, ln_g, ln_b, lam_re, lam_im, log_dt, b_re, b_im, c_re, c_im,
           d_skip, w_glu, w_out, w_q, w_k, w_v, w_o, w_gate, w_up, w_down):
    depth = w_in.shape[0]
    bp, tp, _ = x_prompt.shape
    bs, ts, _ = x_sample.shape
    assert tp % (PROMPT_L * PROMPT_CT) == 0 and tp >= CONV_WIDTH - 1

    yp = x_prompt.reshape(bp, tp // PROMPT_L, PROMPT_L, D_MODEL)
    ys = x_sample.reshape(1, bs, ts, D_MODEL)
    outs = [[] for _ in range(8)]
    for l in range(depth):
        wts = {
            "norm_g": norm_g[l],
            "w_in": w_in[l],
            "w_dw": w_dw[l],
            "b_dw": b_dw[l][None],
            "ln_g": ln_g[l][None],
            "ln_b": ln_b[l][None],
            "d_skip": d_skip[l].reshape(N_LANE_BLOCKS, 1, LANES),
            "w_glu": w_glu[l],
            "w_out": w_out[l],
            "w_q": w_q[l],
            "w_o": w_o[l],
            "w_gate": w_gate[l].astype(BF16),
            "w_up": w_up[l].astype(BF16),
            "w_down": w_down[l].astype(BF16),
        }
        ssm_args = (lam_re[l], lam_im[l], log_dt[l], b_re[l], b_im[l], c_re[l], c_im[l])
        assert ts <= PROMPT_L
        full = _ssm_weights(*ssm_args, PROMPT_L)
        wts["ssm_L%d" % PROMPT_L] = full[:2] + (full[2](PROMPT_L),)
        wts["ssm_L%d" % ts] = _ssm_weights_prefix(full, PROMPT_L, ts)

        kp, vp, kp16, vp16 = _memkv(mem_prompt, mem_norm_g[l][None], w_k[l], w_v[l])
        h0p = jnp.zeros((N_LANE_BLOCKS, bp, STATE_W), F32)
        yp, cp, hp = _layer(yp, kp16, vp16, None, h0p, wts, ct=PROMPT_CT, sub=PROMPT_SUB, ffn_ct=PROMPT_SUB,
                            sc=tp // PROMPT_L, bt=4, sample=False)
        h0s = _pack_state(state_ssm_re[l], state_ssm_im[l])
        ys, cs, hs = _layer(ys, cache_mem_k[l:l + 1], cache_mem_v[l:l + 1], jnp.transpose(state_conv[l], (1, 0, 2)),
                            h0s, wts, ct=bs, sub=bs, ffn_ct=bs, sc=1, bt=bs, sample=True)
        cs = jnp.transpose(cs, (1, 0, 2))

        hp_re, hp_im = _unpack_state(hp)
        hs_re, hs_im = _unpack_state(hs)
        for lst, val in zip(outs, (kp, vp, cp, hp_re, hp_im, cs, hs_re, hs_im)):
            lst.append(val)
    return (yp.reshape(bp, tp, D_MODEL), ys.reshape(bs, ts, D_MODEL)) + tuple(jnp.stack(o) for o in outs)
```

```python
import functools

import jax
import jax.numpy as jnp
from jax import lax
from jax.experimental import pallas as pl
from jax.experimental.pallas import tpu as pltpu

F32 = jnp.float32
BF16 = jnp.bfloat16

D_MODEL = 1024
D_CONV = 512
D_SSM = 512
CONV_WIDTH = 31
N_GROUPS = 32
P_GROUP = 16
N_STATE = 64
N_MEM = 256
N_HEADS = 4
HEAD_DIM = 256
D_FF = 2816
RMS_EPS = 1e-6
LN_EPS = 1e-5

LANES = 128
SUBLANES = 8
GROUPS_PER_BLOCK = LANES // P_GROUP
N_LANE_BLOCKS = D_SSM // LANES
STATE_HALF = GROUPS_PER_BLOCK * N_STATE
STATE_W = 2 * STATE_HALF
MXU_N = 256
VMEM_LIMIT = 56 * 1024 * 1024


def _params(n_axes, vmem=VMEM_LIMIT):
    return pltpu.CompilerParams(dimension_semantics=("arbitrary",) * n_axes, vmem_limit_bytes=vmem)


def _const(shape):
    nd = len(shape)
    return pl.BlockSpec(shape, lambda *_: (0,) * nd, pipeline_mode=pl.Buffered(1))


def _rms(x, g):
    return x * lax.rsqrt(jnp.mean(x * x, axis=-1, keepdims=True) + RMS_EPS) * g


def _cast_weights_once(n_axes, *pairs):
    first = pl.program_id(0) == 0
    for a in range(1, n_axes):
        first = jnp.logical_and(first, pl.program_id(a) == 0)

    @pl.when(first)
    def _():
        for src, dst in pairs:
            dst[...] = src[...].astype(BF16)


def _gather_rows(x_ref, L, c0, n):
    return jnp.concatenate([x_ref[c0:c0 + n, s, :] for s in range(L)], axis=0)


def _sub_tiles(ct, sub):
    return [(c0, sub) for c0 in range(0, ct, sub)]


def _nat_spec(ct, L):
    return pl.BlockSpec((None, ct, L, D_MODEL), lambda b, i: (b, i, 0, 0))


def _tp_spec(ct, L, width, nc):
    return pl.BlockSpec((L, ct, width), lambda b, i: (0, b * nc + i, 0))


def _inproj_kernel(x_ref, g_ref, w_ref, v_ref, u_ref, w16_scr, *, L, ct, sub):
    _cast_weights_once(2, (w_ref, w16_scr))
    for c0, n in _sub_tiles(ct, sub):
        x = _gather_rows(x_ref, L, c0, n)
        h = _rms(x, g_ref[...])
        z = jnp.dot(h.astype(BF16), w16_scr[...], preferred_element_type=F32)
        a = z[:, :D_CONV]
        g = z[:, D_CONV:2 * D_CONV]
        u = z[:, 2 * D_CONV:].astype(BF16)
        v_ref[:, c0:c0 + n, :] = (a * jax.nn.sigmoid(g)).reshape(L, n, D_CONV)
        for s in range(L):
            for j in range(N_LANE_BLOCKS):
                u_ref[j, c0:c0 + n, s * LANES:(s + 1) * LANES] = u[s * n:(s + 1) * n, j * LANES:(j + 1) * LANES]


def _inproj(x4, g0, w_in, *, L, ct, sub):
    bq, c = x4.shape[:2]
    nc = c // ct
    R = bq * c
    return pl.pallas_call(
        functools.partial(_inproj_kernel, L=L, ct=ct, sub=sub),
        grid=(bq, nc),
        in_specs=[_nat_spec(ct, L), _const((1, D_MODEL)), _const((D_MODEL, 2 * D_CONV + D_SSM))],
        out_specs=[
            _tp_spec(ct, L, D_CONV, nc),
            pl.BlockSpec((N_LANE_BLOCKS, ct, L * LANES), lambda b, i: (0, b * nc + i, 0)),
        ],
        out_shape=[
            jax.ShapeDtypeStruct((L, R, D_CONV), F32),
            jax.ShapeDtypeStruct((N_LANE_BLOCKS, R, L * LANES), BF16),
        ],
        scratch_shapes=[pltpu.VMEM(w_in.shape, BF16)],
        compiler_params=_params(2),
        name="inproj",
    )(x4, g0, w_in)


def _ln_silu(acc, g, b):
    mu = jnp.mean(acc, axis=-1, keepdims=True)
    xc = acc - mu
    var = jnp.mean(xc * xc, axis=-1, keepdims=True)
    y = xc * lax.rsqrt(var + LN_EPS) * g + b
    return y * jax.nn.sigmoid(y)


def _conv_shifted_copies(v_ref, v1_ref, v2_ref, *, L, C):
    row = lax.broadcasted_iota(jnp.int32, (C, D_CONV), 0)
    for s in range(L):
        x = v_ref[s]
        v1_ref[s] = jnp.where(row >= 1, pltpu.roll(x, 1, 0), 0.0)
        v2_ref[s] = jnp.where(row >= 2, pltpu.roll(x, 2, 0), 0.0)


def _conv_rows(srcs, w_ref, b_ref, g_ref, bb_ref, o_ref, r0, *, L, rc):
    sub8 = rc // SUBLANES
    for s in range(L):
        acc = jnp.broadcast_to(b_ref[...][None], (sub8, SUBLANES, D_CONV))
        for d in range(CONV_WIDTH):
            blk = (s - d) % L
            shift = (d - s + L - 1) // L if d > s else 0
            k = CONV_WIDTH - 1 - d
            src = srcs[shift][blk, pl.ds(r0, rc), :].reshape(sub8, SUBLANES, D_CONV)
            acc = acc + w_ref[k][None] * src
        y = _ln_silu(acc.reshape(rc, D_CONV), g_ref[...], bb_ref[...])
        o_ref[s, pl.ds(r0, rc), :] = y.astype(o_ref.dtype)


def _conv_prompt_kernel(v_ref, w_ref, b_ref, g_ref, bb_ref, o_ref, v1_ref, v2_ref, *, L, C, rc):
    _conv_shifted_copies(v_ref, v1_ref, v2_ref, L=L, C=C)
    srcs = (v_ref, v1_ref, v2_ref)

    def chunk(i, carry):
        _conv_rows(srcs, w_ref, b_ref, g_ref, bb_ref, o_ref, pl.multiple_of(i * rc, rc), L=L, rc=rc)
        return carry

    lax.fori_loop(0, C // rc, chunk, 0)


def _conv_prompt(v, w_dw, b_dw, ln_g, ln_b, *, L, C):
    R = v.shape[1]
    rc = 32
    return pl.pallas_call(
        functools.partial(_conv_prompt_kernel, L=L, C=C, rc=rc),
        grid=(R // C,),
        in_specs=[
            pl.BlockSpec((L, C, D_CONV), lambda b: (0, b, 0)),
            _const((CONV_WIDTH, SUBLANES, D_CONV)),
            _const((SUBLANES, D_CONV)),
            _const((1, D_CONV)),
            _const((1, D_CONV)),
        ],
        out_specs=pl.BlockSpec((L, C, D_CONV), lambda b: (0, b, 0)),
        out_shape=jax.ShapeDtypeStruct((L, R, D_CONV), BF16),
        scratch_shapes=[pltpu.VMEM((L, C, D_CONV), F32)] * 2,
        compiler_params=_params(1),
        name="conv_prompt",
    )(v, jnp.broadcast_to(w_dw[:, None, :], (CONV_WIDTH, SUBLANES, D_CONV)),
      jnp.broadcast_to(b_dw, (SUBLANES, D_CONV)), ln_g, ln_b)


def _conv_sample_kernel(v_ref, st_ref, w_ref, b_ref, g_ref, bb_ref, o_ref, new_ref, *, L, H):
    def ext(i):
        return st_ref[i] if i < H else v_ref[i - H]

    for t in range(L):
        acc = jnp.broadcast_to(b_ref[...], v_ref.shape[1:])
        for k in range(CONV_WIDTH):
            acc = acc + w_ref[k:k + 1, :] * ext(t + k)
        o_ref[t] = _ln_silu(acc, g_ref[...], bb_ref[...]).astype(o_ref.dtype)
    for i in range(H):
        new_ref[i] = ext(i + L)


def _conv_sample(v, state_t, w_dw, b_dw, ln_g, ln_b):
    L, R, _ = v.shape
    H = CONV_WIDTH - 1
    return pl.pallas_call(
        functools.partial(_conv_sample_kernel, L=L, H=H),
        grid=(1,),
        in_specs=[
            _const((L, R, D_CONV)),
            _const((H, R, D_CONV)),
            _const((CONV_WIDTH, D_CONV)),
            _const((1, D_CONV)),
            _const((1, D_CONV)),
            _const((1, D_CONV)),
        ],
        out_specs=[
            pl.BlockSpec((L, R, D_CONV), lambda i: (0, 0, 0)),
            pl.BlockSpec((H, R, D_CONV), lambda i: (0, 0, 0)),
        ],
        out_shape=[
            jax.ShapeDtypeStruct((L, R, D_CONV), BF16),
            jax.ShapeDtypeStruct((H, R, D_CONV), F32),
        ],
        compiler_params=_params(1),
        name="conv_sample",
    )(v, state_t, w_dw, b_dw, ln_g, ln_b)


def _expand_block_diag(d):
    tiled = jnp.concatenate([d] * GROUPS_PER_BLOCK, axis=-1)
    r = lax.broadcasted_iota(jnp.int32, tiled.shape, 0) // P_GROUP
    c = lax.broadcasted_iota(jnp.int32, tiled.shape, 1) // N_STATE
    return jnp.where(r == c, tiled, jnp.zeros_like(tiled))


def _ssm_kernel(u_ref, wbc_ref, wcc_ref, a_ref, d_ref, h0_ref, y_ref, hl_ref,
                w_scr, wb_scr, wct_scr, s_scr, hp_scr, tap_scr, *, L, C, bt):
    per_tile = MXU_N // LANES

    @pl.when(pl.program_id(1) == 0)
    def _():
        for s in range(L):
            rows = slice(s * LANES, (s + 1) * LANES)
            for a in range(2):
                cols = slice(a * STATE_HALF, (a + 1) * STATE_HALF)
                wb_scr[rows, cols] = _expand_block_diag(wbc_ref[s, a])
                wct_scr[rows, cols] = _expand_block_diag(wcc_ref[s + 1, a])
        bbar = wb_scr[(L - 1) * LANES:L * LANES, :]
        nt = (((1,), (1,)), ((), ()))
        c0 = jnp.concatenate([_expand_block_diag(wcc_ref[0, a]) for a in range(2)], axis=-1)
        tap_scr[0] = lax.dot_general(bbar, c0, nt, preferred_element_type=F32).astype(BF16)
        for t in range(1, L):
            tap_scr[t] = lax.dot_general(bbar, wct_scr[(t - 1) * LANES:t * LANES, :], nt,
                                         preferred_element_type=F32).astype(BF16)
        for sp in range(L):
            for s in range(L):
                rows = slice(s * LANES, (s + 1) * LANES)
                cols = slice(sp * LANES, (sp + 1) * LANES)
                if s <= sp:
                    w_scr[rows, cols] = tap_scr[sp - s]
                elif s // per_tile == sp // per_tile:
                    w_scr[rows, cols] = jnp.zeros((LANES, LANES), BF16)

    nk = STATE_W // LANES
    half = nk // 2
    rows = bt * C
    inc_all = jnp.dot(u_ref[...], wb_scr[...], preferred_element_type=F32)
    for k in range(half):
        s_scr[k, 0:rows, :] = inc_all[:, k * LANES:(k + 1) * LANES]
        s_scr[k, rows:2 * rows, :] = inc_all[:, (half + k) * LANES:(half + k + 1) * LANES]

    d = d_ref[...]
    for n in range(L // per_tile):
        k = (n + 1) * MXU_N
        y = jnp.dot(u_ref[:, :k], w_scr[:k, n * MXU_N:(n + 1) * MXU_N], preferred_element_type=F32)
        for h in range(per_tile):
            s = n * per_tile + h
            us = u_ref[:, s * LANES:(s + 1) * LANES].astype(F32)
            y_ref[s] = y[:, h * LANES:(h + 1) * LANES] + d * us

    h0 = h0_ref[...]
    a = a_ref[...]

    def piece(x, k):
        return x[:, k * LANES:(k + 1) * LANES]

    def swap_parts(x):
        if 2 * bt == SUBLANES:
            return pltpu.roll(x, bt, 0)
        return jnp.concatenate([x[bt:], x[:bt]], axis=0)

    im_rows = lax.broadcasted_iota(jnp.int32, (2 * bt, LANES), 0) >= bt
    a_same = [jnp.broadcast_to(piece(a[0:1], k), (2 * bt, LANES)) for k in range(half)]
    a_cross = [jnp.where(im_rows, piece(a[1:2], k), -piece(a[1:2], k)) for k in range(half)]
    state = [jnp.concatenate([piece(h0, k), piece(h0, half + k)], axis=0) for k in range(half)]
    for c in range(C):
        sel = pl.ds(c, 2 * bt, stride=C) if C > 1 else pl.ds(0, 2 * bt)
        for k in range(half):
            hp_scr[k, sel, :] = state[k]
            state[k] = a_same[k] * state[k] + a_cross[k] * swap_parts(state[k]) + s_scr[k, sel, :]
    for k in range(half):
        hl_ref[:, k * LANES:(k + 1) * LANES] = state[k][:bt]
        hl_ref[:, (half + k) * LANES:(half + k + 1) * LANES] = state[k][bt:]

    hp = jnp.concatenate([hp_scr[k, 0:rows, :] for k in range(half)]
                         + [hp_scr[k, rows:2 * rows, :] for k in range(half)], axis=-1).astype(BF16)
    for n in range(L // per_tile):
        y = lax.dot_general(hp, wct_scr[n * MXU_N:(n + 1) * MXU_N, :], (((1,), (1,)), ((), ())),
                            preferred_element_type=F32)
        for h in range(per_tile):
            s = n * per_tile + h
            y_ref[s] += y[:, h * LANES:(h + 1) * LANES]


def _ssm(u4, wbc, wcc, a_pow, d_skip, h0, *, L, C, bt):
    R = u4.shape[1]
    rows = bt * C
    nb = R // rows
    lw = L * LANES
    in_specs = [
        pl.BlockSpec((None, rows, lw), lambda j, b: (j, b, 0)),
        pl.BlockSpec((None, L, 2, LANES, N_STATE), lambda j, b: (j, 0, 0, 0, 0)),
        pl.BlockSpec((None, L + 1, 2, LANES, N_STATE), lambda j, b: (j, 0, 0, 0, 0)),
        pl.BlockSpec((None, 2, STATE_HALF), lambda j, b: (j, 0, 0)),
        pl.BlockSpec((None, 1, LANES), lambda j, b: (j, 0, 0)),
        pl.BlockSpec((None, None, bt, STATE_W), lambda j, b: (j, b, 0, 0)),
    ]
    out_specs = [
        pl.BlockSpec((L, rows, LANES), lambda j, b: (0, b, j)),
        pl.BlockSpec((None, None, bt, STATE_W), lambda j, b: (j, b, 0, 0)),
    ]
    out_shape = [
        jax.ShapeDtypeStruct((L, R, D_SSM), F32),
        jax.ShapeDtypeStruct((N_LANE_BLOCKS, nb, bt, STATE_W), F32),
    ]
    scratch = [
        pltpu.VMEM((lw, lw), BF16),
        pltpu.VMEM((lw, STATE_W), BF16),
        pltpu.VMEM((lw, STATE_W), BF16),
        pltpu.VMEM((STATE_HALF // LANES, 2 * rows, LANES), F32),
        pltpu.VMEM((STATE_HALF // LANES, 2 * rows, LANES), F32),
        pltpu.VMEM((L, LANES, LANES), BF16),
    ]
    return pl.pallas_call(
        functools.partial(_ssm_kernel, L=L, C=C, bt=bt),
        grid=(N_LANE_BLOCKS, nb),
        in_specs=in_specs,
        out_specs=out_specs,
        out_shape=out_shape,
        scratch_shapes=scratch,
        compiler_params=_params(2),
        name="ssm",
    )(u4, wbc, wcc, a_pow, d_skip, h0)


def _ssm_weights(lam_re, lam_im, log_dt, b_re, b_im, c_re, c_im, L):
    dt = jnp.exp(log_dt)[:, None]
    zr = lam_re * dt
    zi = lam_im * dt
    n_pow = jnp.arange(L + 1, dtype=F32)[:, None, None]
    mag = jnp.exp(zr[None] * n_pow)
    pr = mag * jnp.cos(zi[None] * n_pow)
    pi = mag * jnp.sin(zi[None] * n_pow)
    a1r, a1i = pr[1], pi[1]
    den = lam_re * lam_re + lam_im * lam_im
    qr = ((a1r - 1.0) * lam_re + a1i * lam_im) / den
    qi = (a1i * lam_re - (a1r - 1.0) * lam_im) / den
    bbr = qr[:, :, None] * b_re - qi[:, :, None] * b_im
    bbi = qr[:, :, None] * b_im + qi[:, :, None] * b_re

    nj, g8 = N_LANE_BLOCKS, GROUPS_PER_BLOCK

    n_rev = (L - 1.0) - jnp.arange(L, dtype=F32)[:, None, None]
    mag_rev = jnp.exp(zr[None] * n_rev)
    rev = mag_rev * jnp.cos(zi[None] * n_rev), mag_rev * jnp.sin(zi[None] * n_rev)
    er = rev[0][:, :, :, None] * bbr[None] - rev[1][:, :, :, None] * bbi[None]
    ei = rev[0][:, :, :, None] * bbi[None] + rev[1][:, :, :, None] * bbr[None]
    e = jnp.stack([er, ei], 1).reshape(L, 2, nj, g8, N_STATE, P_GROUP)
    wbc = jnp.transpose(e, (2, 0, 1, 3, 5, 4)).reshape(nj, L, 2, LANES, N_STATE)

    mr = c_re[None] * pr[:, :, None, :] - c_im[None] * pi[:, :, None, :]
    mi = c_re[None] * pi[:, :, None, :] + c_im[None] * pr[:, :, None, :]
    m = jnp.stack([mr, -mi], 1).reshape(L + 1, 2, nj, g8, P_GROUP, N_STATE)
    wcc = jnp.transpose(m, (2, 0, 1, 3, 4, 5)).reshape(nj, L + 1, 2, LANES, N_STATE)

    def a_pow(n):
        return jnp.stack([pr[n].reshape(nj, STATE_HALF), pi[n].reshape(nj, STATE_HALF)], 1)

    return wbc.astype(BF16), wcc.astype(BF16), a_pow


def _ssm_weights_prefix(full, L_full, L):
    wbc, wcc, a_pow = full
    return wbc[:, L_full - L:], wcc[:, :L + 1], a_pow(L)


def _pack_state(re, im):
    b = re.shape[0]
    r = re.reshape(b, N_LANE_BLOCKS, STATE_HALF)
    i = im.reshape(b, N_LANE_BLOCKS, STATE_HALF)
    return jnp.transpose(jnp.concatenate([r, i], -1), (1, 0, 2))


def _unpack_state(h):
    b = h.shape[1]
    h = jnp.transpose(h, (1, 0, 2))
    re = h[:, :, :STATE_HALF].reshape(b, N_GROUPS, N_STATE)
    im = h[:, :, STATE_HALF:].reshape(b, N_GROUPS, N_STATE)
    return re, im


def _mixout_kernel(c_ref, y_ref, x_ref, wglu_ref, wout_ref, g_ref, o_ref, wglu_scr, wout_scr, *, L, ct, sub):
    _cast_weights_once(2, (wglu_ref, wglu_scr), (wout_ref, wout_scr))
    for c0, n in _sub_tiles(ct, sub):
        rows = L * n
        gy = jax.nn.gelu(y_ref[:, c0:c0 + n, :].reshape(rows, D_SSM))
        sg = gy * jax.nn.sigmoid(jnp.dot(gy.astype(BF16), wglu_scr[...], preferred_element_type=F32))
        m = jnp.dot(c_ref[:, c0:c0 + n, :].reshape(rows, D_CONV), wout_scr[0:D_CONV, :],
                    preferred_element_type=F32)
        m = m + jnp.dot(sg.astype(BF16), wout_scr[D_CONV:, :], preferred_element_type=F32)
        x = _gather_rows(x_ref, L, c0, n)
        o_ref[:, c0:c0 + n, :] = (x + _rms(m, g_ref[...])).reshape(L, n, D_MODEL)


def _mixout(c, y, x4, w_glu, w_out, g1, *, L, ct, sub):
    bq, cc = x4.shape[:2]
    nc = cc // ct
    R = bq * cc
    return pl.pallas_call(
        functools.partial(_mixout_kernel, L=L, ct=ct, sub=sub),
        grid=(bq, nc),
        in_specs=[
            _tp_spec(ct, L, D_CONV, nc),
            _tp_spec(ct, L, D_SSM, nc),
            _nat_spec(ct, L),
            _const((D_SSM, D_SSM)),
            _const((D_CONV + D_SSM, D_MODEL)),
            _const((1, D_MODEL)),
        ],
        out_specs=_tp_spec(ct, L, D_MODEL, nc),
        out_shape=jax.ShapeDtypeStruct((L, R, D_MODEL), F32),
        scratch_shapes=[pltpu.VMEM(w_glu.shape, BF16), pltpu.VMEM(w_out.shape, BF16)],
        compiler_params=_params(2),
        name="mixout",
    )(c, y, x4, w_glu, w_out, g1)


def _memkv_kernel(m_ref, g_ref, wk_ref, wv_ref, k5_ref, v5_ref, kb_ref, vb_ref, wk_scr, wv_scr, *, nb):
    _cast_weights_once(1, (wk_ref, wk_scr), (wv_ref, wv_scr))
    for b in range(nb):
        m = _rms(m_ref[b], g_ref[...]).astype(BF16)
        for w_ref, o5_ref, ob_ref in ((wk_scr, k5_ref, kb_ref), (wv_scr, v5_ref, vb_ref)):
            p = jnp.dot(m, w_ref[...], preferred_element_type=F32)
            ob_ref[b] = p.astype(BF16)
            for hd in range(N_HEADS):
                o5_ref[b, :, hd, :] = p[:, hd * HEAD_DIM:(hd + 1) * HEAD_DIM]


def _memkv(mem, g_mem, w_k, w_v):
    bq = mem.shape[0]
    nb = 2
    out5 = pl.BlockSpec((nb, N_MEM, N_HEADS, HEAD_DIM), lambda i: (i, 0, 0, 0))
    outb = pl.BlockSpec((nb, N_MEM, D_MODEL), lambda i: (i, 0, 0))
    return pl.pallas_call(
        functools.partial(_memkv_kernel, nb=nb),
        grid=(bq // nb,),
        in_specs=[
            pl.BlockSpec((nb, N_MEM, D_MODEL), lambda i: (i, 0, 0)),
            _const((1, D_MODEL)),
            _const((D_MODEL, D_MODEL)),
            _const((D_MODEL, D_MODEL)),
        ],
        out_specs=[out5, out5, outb, outb],
        out_shape=[jax.ShapeDtypeStruct((bq, N_MEM, N_HEADS, HEAD_DIM), F32)] * 2
        + [jax.ShapeDtypeStruct((bq, N_MEM, D_MODEL), BF16)] * 2,
        scratch_shapes=[pltpu.VMEM(w_k.shape, BF16), pltpu.VMEM(w_v.shape, BF16)],
        compiler_params=_params(1),
        name="memkv",
    )(mem, g_mem, w_k, w_v)


def _softmax_rows(s):
    s = s - jnp.max(s, axis=-1, keepdims=True)
    e = jnp.exp(s)
    return e / jnp.sum(e, axis=-1, keepdims=True)


def _attn_prompt_kernel(x_ref, k_ref, v_ref, wq32_ref, wo32_ref, gq_ref, go_ref, o_ref, wq_ref, wo_ref,
                        *, L, ct, sub):
    _cast_weights_once(2, (wq32_ref, wq_ref), (wo32_ref, wo_ref))
    kb = k_ref[...]
    vb = v_ref[...]
    for c0, n in _sub_tiles(ct, sub):
        rows = L * n
        x = x_ref[:, c0:c0 + n, :].reshape(rows, D_MODEL)
        h = _rms(x, gq_ref[...])
        q = jnp.dot(h.astype(BF16), wq_ref[...], preferred_element_type=F32).astype(BF16)
        heads = []
        for hd in range(N_HEADS):
            sl = slice(hd * HEAD_DIM, (hd + 1) * HEAD_DIM)
            sc = lax.dot_general(q[:, sl], kb[:, sl], (((1,), (1,)), ((), ())), preferred_element_type=F32)
            p = _softmax_rows(sc * (HEAD_DIM ** -0.5))
            heads.append(jnp.dot(p.astype(BF16), vb[:, sl], preferred_element_type=F32))
        o = jnp.concatenate(heads, axis=-1)
        a = jnp.dot(o.astype(BF16), wo_ref[...], preferred_element_type=F32)
        o_ref[:, c0:c0 + n, :] = (x + _rms(a, go_ref[...])).reshape(L, n, D_MODEL)


def _attn_prompt(x1, k, v, w_q, w_o, g2, g3, *, C, ct, sub):
    L, R, _ = x1.shape
    nc = C // ct
    return pl.pallas_call(
        functools.partial(_attn_prompt_kernel, L=L, ct=ct, sub=sub),
        grid=(R // C, nc),
        in_specs=[
            _tp_spec(ct, L, D_MODEL, nc),
            pl.BlockSpec((None, N_MEM, D_MODEL), lambda b, i: (b, 0, 0)),
            pl.BlockSpec((None, N_MEM, D_MODEL), lambda b, i: (b, 0, 0)),
            _const((D_MODEL, D_MODEL)),
            _const((D_MODEL, D_MODEL)),
            _const((1, D_MODEL)),
            _const((1, D_MODEL)),
        ],
        out_specs=_tp_spec(ct, L, D_MODEL, nc),
        out_shape=jax.ShapeDtypeStruct((L, R, D_MODEL), F32),
        scratch_shapes=[pltpu.VMEM(w_q.shape, BF16), pltpu.VMEM(w_o.shape, BF16)],
        compiler_params=_params(2),
        name="attn_prompt",
    )(x1, k, v, w_q, w_o, g2, g3)


def _qproj_kernel(x_ref, wq_ref, g_ref, q_ref):
    h = _rms(x_ref[...], g_ref[...])
    q_ref[...] = jnp.dot(h.astype(BF16), wq_ref[...].astype(BF16), preferred_element_type=F32)


def _qproj(x1f, w_q, g2):
    rows = x1f.shape[0]
    return pl.pallas_call(
        _qproj_kernel,
        grid=(1,),
        in_specs=[_const((rows, D_MODEL)), _const((D_MODEL, D_MODEL)), _const((1, D_MODEL))],
        out_specs=pl.BlockSpec((rows, D_MODEL), lambda i: (0, 0)),
        out_shape=jax.ShapeDtypeStruct((rows, D_MODEL), F32),
        compiler_params=_params(1),
        name="qproj_sample",
    )(x1f, w_q, g2)


def _attn_sample_kernel(q_ref, k_ref, v_ref, o_ref, *, L, bb):
    rows = L * bb
    q2 = q_ref[...].reshape(rows, D_MODEL)
    qs = jnp.concatenate([q2[:, h * HEAD_DIM:(h + 1) * HEAD_DIM] for h in range(N_HEADS)], axis=0).astype(BF16)
    nr = N_HEADS * rows
    nc = N_MEM * N_HEADS
    row_head = lax.broadcasted_iota(jnp.int32, (nr, nc), 0) // rows
    col_head = lax.broadcasted_iota(jnp.int32, (nr, nc), 1) % N_HEADS
    same_head = row_head == col_head
    owner = lax.broadcasted_iota(jnp.int32, (nr, HEAD_DIM), 0) % bb
    acc = jnp.zeros((nr, HEAD_DIM), F32)
    for b in range(bb):
        ka = k_ref[b].reshape(nc, HEAD_DIM).astype(BF16)
        va = v_ref[b].reshape(nc, HEAD_DIM).astype(BF16)
        sc = lax.dot_general(qs, ka, (((1,), (1,)), ((), ())), preferred_element_type=F32)
        p = _softmax_rows(jnp.where(same_head, sc * (HEAD_DIM ** -0.5), -1e30))
        o = jnp.dot(p.astype(BF16), va, preferred_element_type=F32)
        acc = jnp.where(owner == b, o, acc)
    for h in range(N_HEADS):
        o_ref[:, :, h * HEAD_DIM:(h + 1) * HEAD_DIM] = acc[h * rows:(h + 1) * rows].reshape(L, bb, HEAD_DIM)


def _attn_sample(q3, k5, v5):
    L, R, _ = q3.shape
    bb = 8
    kv_spec = pl.BlockSpec((None, bb, N_MEM, N_HEADS, HEAD_DIM), lambda i: (0, i, 0, 0, 0))
    return pl.pallas_call(
        functools.partial(_attn_sample_kernel, L=L, bb=bb),
        grid=(R // bb,),
        in_specs=[pl.BlockSpec((L, bb, D_MODEL), lambda i: (0, i, 0)), kv_spec, kv_spec],
        out_specs=pl.BlockSpec((L, bb, D_MODEL), lambda i: (0, i, 0)),
        out_shape=jax.ShapeDtypeStruct((L, R, D_MODEL), F32),
        compiler_params=_params(1),
        name="attn_sample",
    )(q3, k5, v5)


def _oproj_kernel(o_ref, x_ref, wo_ref, g_ref, y_ref):
    a = jnp.dot(o_ref[...].astype(BF16), wo_ref[...].astype(BF16), preferred_element_type=F32)
    y_ref[...] = x_ref[...] + _rms(a, g_ref[...])


def _oproj(of, x1f, w_o, g3):
    rows = of.shape[0]
    return pl.pallas_call(
        _oproj_kernel,
        grid=(1,),
        in_specs=[_const((rows, D_MODEL)), _const((rows, D_MODEL)), _const((D_MODEL, D_MODEL)),
                  _const((1, D_MODEL))],
        out_specs=pl.BlockSpec((rows, D_MODEL), lambda i: (0, 0)),
        out_shape=jax.ShapeDtypeStruct((rows, D_MODEL), F32),
        compiler_params=_params(1),
        name="oproj_sample",
    )(of, x1f, w_o, g3)


FFN_WEIGHT_CHUNKS = 8


def _stream_cast(w_hbm, w16_scr, stage, sem):
    rpc = stage.shape[1]
    n = w_hbm.shape[0] // rpc

    def chunk(k):
        return pltpu.make_async_copy(w_hbm.at[pl.ds(k * rpc, rpc)], stage.at[k % 2], sem.at[k % 2])

    chunk(0).start()
    for k in range(n):
        if k + 1 < n:
            chunk(k + 1).start()
        chunk(k).wait()
        w16_scr[pl.ds(k * rpc, rpc), :] = stage[k % 2].astype(BF16)


def _ffn_tile(x_ref, o_ref, wg_scr, wu_scr, wd_scr, gi_ref, go_ref, *, L, ct):
    rows = L * ct
    x = x_ref[...].reshape(rows, D_MODEL)
    h = _rms(x, gi_ref[...]).astype(BF16)
    gate = jnp.dot(h, wg_scr[...], preferred_element_type=F32)
    up = jnp.dot(h, wu_scr[...], preferred_element_type=F32)
    act = (gate * jax.nn.sigmoid(gate) * up).astype(BF16)
    dn = jnp.dot(act, wd_scr[...], preferred_element_type=F32)
    y = x + _rms(dn, go_ref[...])
    for s in range(L):
        o_ref[:, s, :] = y[s * ct:(s + 1) * ct]


def _ffn_kernel(xp_ref, xs_ref, wg_hbm, wu_hbm, wd_hbm, gi_ref, go_ref, op_ref, os_ref,
                wg_scr, wu_scr, wd_scr, stage_in, stage_out, sem, *, n_prompt, Lp, ctp, Ls, cts):
    i = pl.program_id(0)

    @pl.when(i == 0)
    def _():
        _stream_cast(wg_hbm, wg_scr, stage_in, sem)
        _stream_cast(wu_hbm, wu_scr, stage_in, sem)
        _stream_cast(wd_hbm, wd_scr, stage_out, sem)

    @pl.when(i < n_prompt)
    def _():
        _ffn_tile(xp_ref, op_ref, wg_scr, wu_scr, wd_scr, gi_ref, go_ref, L=Lp, ct=ctp)

    @pl.when(i == n_prompt)
    def _():
        _ffn_tile(xs_ref, os_ref, wg_scr, wu_scr, wd_scr, gi_ref, go_ref, L=Ls, ct=cts)


def _ffn(xp, xs, w_gate, w_up, w_down, g4, g5, *, bq, C, ct):
    Lp = xp.shape[0]
    Ls, rs, _ = xs.shape
    nc = C // ct
    n = bq * nc
    tile = lambda i: jnp.minimum(i, n - 1)
    hbm = pl.BlockSpec(memory_space=pl.ANY)
    return pl.pallas_call(
        functools.partial(_ffn_kernel, n_prompt=n, Lp=Lp, ctp=ct, Ls=Ls, cts=rs),
        grid=(n + 1,),
        in_specs=[
            pl.BlockSpec((Lp, ct, D_MODEL), lambda i: (0, tile(i), 0)),
            _const((Ls, rs, D_MODEL)),
            hbm, hbm, hbm,
            _const((1, D_MODEL)),
            _const((1, D_MODEL)),
        ],
        out_specs=[
            pl.BlockSpec((None, ct, Lp, D_MODEL), lambda i: (tile(i) // nc, tile(i) % nc, 0, 0)),
            pl.BlockSpec((None, rs, Ls, D_MODEL), lambda i: (0, 0, 0, 0)),
        ],
        out_shape=[
            jax.ShapeDtypeStruct((bq, C, Lp, D_MODEL), F32),
            jax.ShapeDtypeStruct((1, rs, Ls, D_MODEL), F32),
        ],
        scratch_shapes=[
            pltpu.VMEM((D_MODEL, D_FF), BF16),
            pltpu.VMEM((D_MODEL, D_FF), BF16),
            pltpu.VMEM((D_FF, D_MODEL), BF16),
            pltpu.VMEM((2, D_MODEL // FFN_WEIGHT_CHUNKS, D_FF), F32),
            pltpu.VMEM((2, D_FF // FFN_WEIGHT_CHUNKS, D_MODEL), F32),
            pltpu.SemaphoreType.DMA((2,)),
        ],
        compiler_params=_params(1),
        name="ffn",
    )(xp, xs, w_gate, w_up, w_down, g4, g5)


def _layer(x4, mem_k, mem_v, conv_state_t, h0_packed, wts, *, ct, sub, sc, bt, sample):
    bq, C, L, _ = x4.shape
    R = bq * C
    ng = wts["norm_g"]
    g = [ng[i:i + 1] for i in range(6)]

    v, u4 = _inproj(x4, g[0], wts["w_in"], L=L, ct=ct, sub=sub)
    sw = wts["ssm_L%d" % L]
    nb = R // (bt * sc)
    ssm_args = (u4, sw[0], sw[1], sw[2], wts["d_skip"], h0_packed.reshape(N_LANE_BLOCKS, nb, bt, STATE_W))
    conv_args = (wts["w_dw"], wts["b_dw"], wts["ln_g"], wts["ln_b"])
    y, hl = _ssm(*ssm_args, L=L, C=sc, bt=bt)
    if sample:
        cact, conv_new = _conv_sample(v, conv_state_t, *conv_args)
    else:
        cact = _conv_prompt(v, *conv_args, L=L, C=C)
        tail = v.reshape(L, bq, C, D_CONV)[:, :, C - 2:, :]
        tail = jnp.transpose(tail, (1, 2, 0, 3)).reshape(bq, 2 * L, D_CONV)
        conv_new = tail[:, 2 * L - (CONV_WIDTH - 1):, :]
    hl = hl.reshape(N_LANE_BLOCKS, nb * bt, STATE_W)

    x1 = _mixout(cact, y, x4, wts["w_glu"], wts["w_out"], g[1], L=L, ct=ct, sub=sub)

    if sample:
        q = _qproj(x1.reshape(L * R, D_MODEL), wts["w_q"], g[2])
        o = _attn_sample(q.reshape(L, R, D_MODEL), mem_k, mem_v)
        x2 = _oproj(o.reshape(L * R, D_MODEL), x1.reshape(L * R, D_MODEL), wts["w_o"], g[3])
        x2 = x2.reshape(L, R, D_MODEL)
    else:
        x2 = _attn_prompt(x1, mem_k, mem_v, wts["w_q"], wts["w_o"], g[2], g[3], C=C, ct=ct, sub=sub)

    return x2, conv_new, hl


PROMPT_L = 16
PROMPT_SUB = 32
PROMPT_CT = 64


def kernel(x_prompt, x_sample, mem_prompt, cache_mem_k, cache_mem_v, state_conv, state_ssm_re, state_ssm_im,
           norm_g, mem_norm_g, w_in, w_dw, b_dw, ln_g, ln_b, lam_re, lam_im, log_dt, b_re, b_im, c_re, c_im,
           d_skip, w_glu, w_out, w_q, w_k, w_v, w_o, w_gate, w_up, w_down):
    depth = w_in.shape[0]
    bp, tp, _ = x_prompt.shape
    bs, ts, _ = x_sample.shape
    assert tp % (PROMPT_L * PROMPT_CT) == 0 and tp >= CONV_WIDTH - 1

    yp = x_prompt.reshape(bp, tp // PROMPT_L, PROMPT_L, D_MODEL)
    ys = x_sample.reshape(1, bs, ts, D_MODEL)
    outs = [[] for _ in range(8)]
    for l in range(depth):
        wts = {
            "norm_g": norm_g[l],
            "w_in": w_in[l],
            "w_dw": w_dw[l],
            "b_dw": b_dw[l][None],
            "ln_g": ln_g[l][None],
            "ln_b": ln_b[l][None],
            "d_skip": d_skip[l].reshape(N_LANE_BLOCKS, 1, LANES),
            "w_glu": w_glu[l],
            "w_out": w_out[l],
            "w_q": w_q[l],
            "w_o": w_o[l],
        }
        ssm_args = (lam_re[l], lam_im[l], log_dt[l], b_re[l], b_im[l], c_re[l], c_im[l])
        assert ts <= PROMPT_L
        full = _ssm_weights(*ssm_args, PROMPT_L)
        wts["ssm_L%d" % PROMPT_L] = full[:2] + (full[2](PROMPT_L),)
        wts["ssm_L%d" % ts] = _ssm_weights_prefix(full, PROMPT_L, ts)

        kp, vp, kp16, vp16 = _memkv(mem_prompt, mem_norm_g[l][None], w_k[l], w_v[l])
        h0p = jnp.zeros((N_LANE_BLOCKS, bp, STATE_W), F32)
        xp, cp, hp = _layer(yp, kp16, vp16, None, h0p, wts, ct=PROMPT_CT, sub=PROMPT_SUB,
                            sc=tp // PROMPT_L, bt=4, sample=False)
        h0s = _pack_state(state_ssm_re[l], state_ssm_im[l])
        xs, cs, hs = _layer(ys, cache_mem_k[l:l + 1], cache_mem_v[l:l + 1], jnp.transpose(state_conv[l], (1, 0, 2)),
                            h0s, wts, ct=bs, sub=bs, sc=1, bt=bs, sample=True)
        cs = jnp.transpose(cs, (1, 0, 2))
        yp, ys = _ffn(xp, xs, w_gate[l], w_up[l], w_down[l], norm_g[l][4:5], norm_g[l][5:6],
                      bq=bp, C=tp // PROMPT_L, ct=PROMPT_SUB)

        hp_re, hp_im = _unpack_state(hp)
        hs_re, hs_im = _unpack_state(hs)
        for lst, val in zip(outs, (kp, vp, cp, hp_re, hp_im, cs, hs_re, hs_im)):
            lst.append(val)
    return (yp.reshape(bp, tp, D_MODEL), ys.reshape(bs, ts, D_MODEL)) + tuple(jnp.stack(o) for o in outs)
```

```python
import functools

import jax
import jax.numpy as jnp
from jax import lax
from jax.experimental import pallas as pl
from jax.experimental.pallas import tpu as pltpu

F32 = jnp.float32
BF16 = jnp.bfloat16

D_MODEL = 1024
D_CONV = 512
D_SSM = 512
CONV_WIDTH = 31
N_GROUPS = 32
P_GROUP = 16
N_STATE = 64
N_MEM = 256
N_HEADS = 4
HEAD_DIM = 256
D_FF = 2816
RMS_EPS = 1e-6
LN_EPS = 1e-5

LANES = 128
SUBLANES = 8
GROUPS_PER_BLOCK = LANES // P_GROUP
N_LANE_BLOCKS = D_SSM // LANES
STATE_HALF = GROUPS_PER_BLOCK * N_STATE
STATE_W = 2 * STATE_HALF
MXU_N = 256
VMEM_LIMIT = 56 * 1024 * 1024


def _params(n_axes, vmem=VMEM_LIMIT):
    return pltpu.CompilerParams(dimension_semantics=("arbitrary",) * n_axes, vmem_limit_bytes=vmem)


def _const(shape):
    nd = len(shape)
    return pl.BlockSpec(shape, lambda *_: (0,) * nd, pipeline_mode=pl.Buffered(1))


def _rms(x, g):
    return x * lax.rsqrt(jnp.mean(x * x, axis=-1, keepdims=True) + RMS_EPS) * g


def _cast_weights_once(n_axes, *pairs):
    first = pl.program_id(0) == 0
    for a in range(1, n_axes):
        first = jnp.logical_and(first, pl.program_id(a) == 0)

    @pl.when(first)
    def _():
        for src, dst in pairs:
            dst[...] = src[...].astype(BF16)


def _gather_rows(x_ref, L, c0, n):
    return jnp.concatenate([x_ref[c0:c0 + n, s, :] for s in range(L)], axis=0)


def _sub_tiles(ct, sub):
    return [(c0, sub) for c0 in range(0, ct, sub)]


def _nat_spec(ct, L):
    return pl.BlockSpec((None, ct, L, D_MODEL), lambda b, i: (b, i, 0, 0))


def _tp_spec(ct, L, width, nc):
    return pl.BlockSpec((L, ct, width), lambda b, i: (0, b * nc + i, 0))


def _inproj_kernel(x_ref, g_ref, w_ref, v_ref, u_ref, w16_scr, *, L, ct, sub):
    _cast_weights_once(2, (w_ref, w16_scr))
    for c0, n in _sub_tiles(ct, sub):
        x = _gather_rows(x_ref, L, c0, n)
        h = _rms(x, g_ref[...])
        z = jnp.dot(h.astype(BF16), w16_scr[...], preferred_element_type=F32)
        a = z[:, :D_CONV]
        g = z[:, D_CONV:2 * D_CONV]
        u = z[:, 2 * D_CONV:].astype(BF16)
        v_ref[:, c0:c0 + n, :] = (a * jax.nn.sigmoid(g)).reshape(L, n, D_CONV)
        for s in range(L):
            for j in range(N_LANE_BLOCKS):
                u_ref[j, c0:c0 + n, s * LANES:(s + 1) * LANES] = u[s * n:(s + 1) * n, j * LANES:(j + 1) * LANES]


def _inproj(x4, g0, w_in, *, L, ct, sub):
    bq, c = x4.shape[:2]
    nc = c // ct
    R = bq * c
    return pl.pallas_call(
        functools.partial(_inproj_kernel, L=L, ct=ct, sub=sub),
        grid=(bq, nc),
        in_specs=[_nat_spec(ct, L), _const((1, D_MODEL)), _const((D_MODEL, 2 * D_CONV + D_SSM))],
        out_specs=[
            _tp_spec(ct, L, D_CONV, nc),
            pl.BlockSpec((N_LANE_BLOCKS, ct, L * LANES), lambda b, i: (0, b * nc + i, 0)),
        ],
        out_shape=[
            jax.ShapeDtypeStruct((L, R, D_CONV), F32),
            jax.ShapeDtypeStruct((N_LANE_BLOCKS, R, L * LANES), BF16),
        ],
        scratch_shapes=[pltpu.VMEM(w_in.shape, BF16)],
        compiler_params=_params(2),
        name="inproj",
    )(x4, g0, w_in)


def _ln_silu(acc, g, b):
    mu = jnp.mean(acc, axis=-1, keepdims=True)
    xc = acc - mu
    var = jnp.mean(xc * xc, axis=-1, keepdims=True)
    y = xc * lax.rsqrt(var + LN_EPS) * g + b
    return y * jax.nn.sigmoid(y)


def _conv_shifted_copies(v_ref, v1_ref, v2_ref, *, L, C):
    row = lax.broadcasted_iota(jnp.int32, (C, D_CONV), 0)
    for s in range(L):
        x = v_ref[s]
        v1_ref[s] = jnp.where(row >= 1, pltpu.roll(x, 1, 0), 0.0)
        v2_ref[s] = jnp.where(row >= 2, pltpu.roll(x, 2, 0), 0.0)


def _conv_rows(srcs, w_ref, b_ref, g_ref, bb_ref, o_ref, r0, *, L, rc):
    sub8 = rc // SUBLANES
    for s in range(L):
        acc = jnp.broadcast_to(b_ref[...][None], (sub8, SUBLANES, D_CONV))
        for d in range(CONV_WIDTH):
            blk = (s - d) % L
            shift = (d - s + L - 1) // L if d > s else 0
            k = CONV_WIDTH - 1 - d
            src = srcs[shift][blk, pl.ds(r0, rc), :].reshape(sub8, SUBLANES, D_CONV)
            acc = acc + w_ref[k][None] * src
        y = _ln_silu(acc.reshape(rc, D_CONV), g_ref[...], bb_ref[...])
        o_ref[s, pl.ds(r0, rc), :] = y.astype(o_ref.dtype)


def _conv_prompt_kernel(v_ref, w_ref, b_ref, g_ref, bb_ref, o_ref, v1_ref, v2_ref, *, L, C, rc):
    _conv_shifted_copies(v_ref, v1_ref, v2_ref, L=L, C=C)
    srcs = (v_ref, v1_ref, v2_ref)

    def chunk(i, carry):
        _conv_rows(srcs, w_ref, b_ref, g_ref, bb_ref, o_ref, pl.multiple_of(i * rc, rc), L=L, rc=rc)
        return carry

    lax.fori_loop(0, C // rc, chunk, 0)


def _conv_prompt(v, w_dw, b_dw, ln_g, ln_b, *, L, C):
    R = v.shape[1]
    rc = 32
    return pl.pallas_call(
        functools.partial(_conv_prompt_kernel, L=L, C=C, rc=rc),
        grid=(R // C,),
        in_specs=[
            pl.BlockSpec((L, C, D_CONV), lambda b: (0, b, 0)),
            _const((CONV_WIDTH, SUBLANES, D_CONV)),
            _const((SUBLANES, D_CONV)),
            _const((1, D_CONV)),
            _const((1, D_CONV)),
        ],
        out_specs=pl.BlockSpec((L, C, D_CONV), lambda b: (0, b, 0)),
        out_shape=jax.ShapeDtypeStruct((L, R, D_CONV), BF16),
        scratch_shapes=[pltpu.VMEM((L, C, D_CONV), F32)] * 2,
        compiler_params=_params(1),
        name="conv_prompt",
    )(v, jnp.broadcast_to(w_dw[:, None, :], (CONV_WIDTH, SUBLANES, D_CONV)),
      jnp.broadcast_to(b_dw, (SUBLANES, D_CONV)), ln_g, ln_b)


def _conv_sample_kernel(v_ref, st_ref, w_ref, b_ref, g_ref, bb_ref, o_ref, new_ref, *, L, H):
    def ext(i):
        return st_ref[i] if i < H else v_ref[i - H]

    for t in range(L):
        acc = jnp.broadcast_to(b_ref[...], v_ref.shape[1:])
        for k in range(CONV_WIDTH):
            acc = acc + w_ref[k:k + 1, :] * ext(t + k)
        o_ref[t] = _ln_silu(acc, g_ref[...], bb_ref[...]).astype(o_ref.dtype)
    for i in range(H):
        new_ref[i] = ext(i + L)


def _conv_sample(v, state_t, w_dw, b_dw, ln_g, ln_b):
    L, R, _ = v.shape
    H = CONV_WIDTH - 1
    return pl.pallas_call(
        functools.partial(_conv_sample_kernel, L=L, H=H),
        grid=(1,),
        in_specs=[
            _const((L, R, D_CONV)),
            _const((H, R, D_CONV)),
            _const((CONV_WIDTH, D_CONV)),
            _const((1, D_CONV)),
            _const((1, D_CONV)),
            _const((1, D_CONV)),
        ],
        out_specs=[
            pl.BlockSpec((L, R, D_CONV), lambda i: (0, 0, 0)),
            pl.BlockSpec((H, R, D_CONV), lambda i: (0, 0, 0)),
        ],
        out_shape=[
            jax.ShapeDtypeStruct((L, R, D_CONV), BF16),
            jax.ShapeDtypeStruct((H, R, D_CONV), F32),
        ],
        compiler_params=_params(1),
        name="conv_sample",
    )(v, state_t, w_dw, b_dw, ln_g, ln_b)


def _expand_block_diag(d):
    tiled = jnp.concatenate([d] * GROUPS_PER_BLOCK, axis=-1)
    r = lax.broadcasted_iota(jnp.int32, tiled.shape, 0) // P_GROUP
    c = lax.broadcasted_iota(jnp.int32, tiled.shape, 1) // N_STATE
    return jnp.where(r == c, tiled, jnp.zeros_like(tiled))


def _ssm_kernel(u_ref, wbc_ref, wcc_ref, a_ref, d_ref, h0_ref, y_ref, hl_ref,
                w_scr, wb_scr, wct_scr, s_scr, hp_scr, tap_scr, *, L, C, bt):
    per_tile = MXU_N // LANES

    @pl.when(pl.program_id(1) == 0)
    def _():
        for s in range(L):
            rows = slice(s * LANES, (s + 1) * LANES)
            for a in range(2):
                cols = slice(a * STATE_HALF, (a + 1) * STATE_HALF)
                wb_scr[rows, cols] = _expand_block_diag(wbc_ref[s, a])
                wct_scr[rows, cols] = _expand_block_diag(wcc_ref[s + 1, a])
        bbar = wb_scr[(L - 1) * LANES:L * LANES, :]
        nt = (((1,), (1,)), ((), ()))
        c0 = jnp.concatenate([_expand_block_diag(wcc_ref[0, a]) for a in range(2)], axis=-1)
        tap_scr[0] = lax.dot_general(bbar, c0, nt, preferred_element_type=F32).astype(BF16)
        for t in range(1, L):
            tap_scr[t] = lax.dot_general(bbar, wct_scr[(t - 1) * LANES:t * LANES, :], nt,
                                         preferred_element_type=F32).astype(BF16)
        for sp in range(L):
            for s in range(L):
                rows = slice(s * LANES, (s + 1) * LANES)
                cols = slice(sp * LANES, (sp + 1) * LANES)
                if s <= sp:
                    w_scr[rows, cols] = tap_scr[sp - s]
                elif s // per_tile == sp // per_tile:
                    w_scr[rows, cols] = jnp.zeros((LANES, LANES), BF16)

    nk = STATE_W // LANES
    half = nk // 2
    rows = bt * C
    inc_all = jnp.dot(u_ref[...], wb_scr[...], preferred_element_type=F32)
    for k in range(half):
        s_scr[k, 0:rows, :] = inc_all[:, k * LANES:(k + 1) * LANES]
        s_scr[k, rows:2 * rows, :] = inc_all[:, (half + k) * LANES:(half + k + 1) * LANES]

    d = d_ref[...]
    for n in range(L // per_tile):
        k = (n + 1) * MXU_N
        y = jnp.dot(u_ref[:, :k], w_scr[:k, n * MXU_N:(n + 1) * MXU_N], preferred_element_type=F32)
        for h in range(per_tile):
            s = n * per_tile + h
            us = u_ref[:, s * LANES:(s + 1) * LANES].astype(F32)
            y_ref[s] = y[:, h * LANES:(h + 1) * LANES] + d * us

    h0 = h0_ref[...]
    a = a_ref[...]

    def piece(x, k):
        return x[:, k * LANES:(k + 1) * LANES]

    def swap_parts(x):
        if 2 * bt == SUBLANES:
            return pltpu.roll(x, bt, 0)
        return jnp.concatenate([x[bt:], x[:bt]], axis=0)

    im_rows = lax.broadcasted_iota(jnp.int32, (2 * bt, LANES), 0) >= bt
    a_same = [jnp.broadcast_to(piece(a[0:1], k), (2 * bt, LANES)) for k in range(half)]
    a_cross = [jnp.where(im_rows, piece(a[1:2], k), -piece(a[1:2], k)) for k in range(half)]
    state = [jnp.concatenate([piece(h0, k), piece(h0, half + k)], axis=0) for k in range(half)]
    for c in range(C):
        sel = pl.ds(c, 2 * bt, stride=C) if C > 1 else pl.ds(0, 2 * bt)
        for k in range(half):
            hp_scr[k, sel, :] = state[k]
            state[k] = a_same[k] * state[k] + a_cross[k] * swap_parts(state[k]) + s_scr[k, sel, :]
    for k in range(half):
        hl_ref[:, k * LANES:(k + 1) * LANES] = state[k][:bt]
        hl_ref[:, (half + k) * LANES:(half + k + 1) * LANES] = state[k][bt:]

    hp = jnp.concatenate([hp_scr[k, 0:rows, :] for k in range(half)]
                         + [hp_scr[k, rows:2 * rows, :] for k in range(half)], axis=-1).astype(BF16)
    for n in range(L // per_tile):
        y = lax.dot_general(hp, wct_scr[n * MXU_N:(n + 1) * MXU_N, :], (((1,), (1,)), ((), ())),
                            preferred_element_type=F32)
        for h in range(per_tile):
            s = n * per_tile + h
            y_ref[s] += y[:, h * LANES:(h + 1) * LANES]


def _ssm(u4, wbc, wcc, a_pow, d_skip, h0, *, L, C, bt):
    R = u4.shape[1]
    rows = bt * C
    nb = R // rows
    lw = L * LANES
    in_specs = [
        pl.BlockSpec((None, rows, lw), lambda j, b: (j, b, 0)),
        pl.BlockSpec((None, L, 2, LANES, N_STATE), lambda j, b: (j, 0, 0, 0, 0)),
        pl.BlockSpec((None, L + 1, 2, LANES, N_STATE), lambda j, b: (j, 0, 0, 0, 0)),
        pl.BlockSpec((None, 2, STATE_HALF), lambda j, b: (j, 0, 0)),
        pl.BlockSpec((None, 1, LANES), lambda j, b: (j, 0, 0)),
        pl.BlockSpec((None, None, bt, STATE_W), lambda j, b: (j, b, 0, 0)),
    ]
    out_specs = [
        pl.BlockSpec((L, rows, LANES), lambda j, b: (0, b, j)),
        pl.BlockSpec((None, None, bt, STATE_W), lambda j, b: (j, b, 0, 0)),
    ]
    out_shape = [
        jax.ShapeDtypeStruct((L, R, D_SSM), F32),
        jax.ShapeDtypeStruct((N_LANE_BLOCKS, nb, bt, STATE_W), F32),
    ]
    scratch = [
        pltpu.VMEM((lw, lw), BF16),
        pltpu.VMEM((lw, STATE_W), BF16),
        pltpu.VMEM((lw, STATE_W), BF16),
        pltpu.VMEM((STATE_HALF // LANES, 2 * rows, LANES), F32),
        pltpu.VMEM((STATE_HALF // LANES, 2 * rows, LANES), F32),
        pltpu.VMEM((L, LANES, LANES), BF16),
    ]
    return pl.pallas_call(
        functools.partial(_ssm_kernel, L=L, C=C, bt=bt),
        grid=(N_LANE_BLOCKS, nb),
        in_specs=in_specs,
        out_specs=out_specs,
        out_shape=out_shape,
        scratch_shapes=scratch,
        compiler_params=_params(2),
        name="ssm",
    )(u4, wbc, wcc, a_pow, d_skip, h0)


def _ssm_weights(lam_re, lam_im, log_dt, b_re, b_im, c_re, c_im, L):
    dt = jnp.exp(log_dt)[:, None]
    zr = lam_re * dt
    zi = lam_im * dt
    n_pow = jnp.arange(L + 1, dtype=F32)[:, None, None]
    mag = jnp.exp(zr[None] * n_pow)
    pr = mag * jnp.cos(zi[None] * n_pow)
    pi = mag * jnp.sin(zi[None] * n_pow)
    a1r, a1i = pr[1], pi[1]
    den = lam_re * lam_re + lam_im * lam_im
    qr = ((a1r - 1.0) * lam_re + a1i * lam_im) / den
    qi = (a1i * lam_re - (a1r - 1.0) * lam_im) / den
    bbr = qr[:, :, None] * b_re - qi[:, :, None] * b_im
    bbi = qr[:, :, None] * b_im + qi[:, :, None] * b_re

    nj, g8 = N_LANE_BLOCKS, GROUPS_PER_BLOCK

    n_rev = (L - 1.0) - jnp.arange(L, dtype=F32)[:, None, None]
    mag_rev = jnp.exp(zr[None] * n_rev)
    rev = mag_rev * jnp.cos(zi[None] * n_rev), mag_rev * jnp.sin(zi[None] * n_rev)
    er = rev[0][:, :, :, None] * bbr[None] - rev[1][:, :, :, None] * bbi[None]
    ei = rev[0][:, :, :, None] * bbi[None] + rev[1][:, :, :, None] * bbr[None]
    e = jnp.stack([er, ei], 1).reshape(L, 2, nj, g8, N_STATE, P_GROUP)
    wbc = jnp.transpose(e, (2, 0, 1, 3, 5, 4)).reshape(nj, L, 2, LANES, N_STATE)

    mr = c_re[None] * pr[:, :, None, :] - c_im[None] * pi[:, :, None, :]
    mi = c_re[None] * pi[:, :, None, :] + c_im[None] * pr[:, :, None, :]
    m = jnp.stack([mr, -mi], 1).reshape(L + 1, 2, nj, g8, P_GROUP, N_STATE)
    wcc = jnp.transpose(m, (2, 0, 1, 3, 4, 5)).reshape(nj, L + 1, 2, LANES, N_STATE)

    def a_pow(n):
        return jnp.stack([pr[n].reshape(nj, STATE_HALF), pi[n].reshape(nj, STATE_HALF)], 1)

    return wbc.astype(BF16), wcc.astype(BF16), a_pow


def _ssm_weights_prefix(full, L_full, L):
    wbc, wcc, a_pow = full
    return wbc[:, L_full - L:], wcc[:, :L + 1], a_pow(L)


def _pack_state(re, im):
    b = re.shape[0]
    r = re.reshape(b, N_LANE_BLOCKS, STATE_HALF)
    i = im.reshape(b, N_LANE_BLOCKS, STATE_HALF)
    return jnp.transpose(jnp.concatenate([r, i], -1), (1, 0, 2))


def _unpack_state(h):
    b = h.shape[1]
    h = jnp.transpose(h, (1, 0, 2))
    re = h[:, :, :STATE_HALF].reshape(b, N_GROUPS, N_STATE)
    im = h[:, :, STATE_HALF:].reshape(b, N_GROUPS, N_STATE)
    return re, im


def _mixout_kernel(c_ref, y_ref, x_ref, wglu_ref, wout_ref, g_ref, o_ref, wglu_scr, wout_scr, *, L, ct, sub):
    _cast_weights_once(2, (wglu_ref, wglu_scr), (wout_ref, wout_scr))
    for c0, n in _sub_tiles(ct, sub):
        rows = L * n
        gy = jax.nn.gelu(y_ref[:, c0:c0 + n, :].reshape(rows, D_SSM))
        sg = gy * jax.nn.sigmoid(jnp.dot(gy.astype(BF16), wglu_scr[...], preferred_element_type=F32))
        m = jnp.dot(c_ref[:, c0:c0 + n, :].reshape(rows, D_CONV), wout_scr[0:D_CONV, :],
                    preferred_element_type=F32)
        m = m + jnp.dot(sg.astype(BF16), wout_scr[D_CONV:, :], preferred_element_type=F32)
        x = _gather_rows(x_ref, L, c0, n)
        o_ref[:, c0:c0 + n, :] = (x + _rms(m, g_ref[...])).reshape(L, n, D_MODEL)


def _mixout(c, y, x4, w_glu, w_out, g1, *, L, ct, sub):
    bq, cc = x4.shape[:2]
    nc = cc // ct
    R = bq * cc
    return pl.pallas_call(
        functools.partial(_mixout_kernel, L=L, ct=ct, sub=sub),
        grid=(bq, nc),
        in_specs=[
            _tp_spec(ct, L, D_CONV, nc),
            _tp_spec(ct, L, D_SSM, nc),
            _nat_spec(ct, L),
            _const((D_SSM, D_SSM)),
            _const((D_CONV + D_SSM, D_MODEL)),
            _const((1, D_MODEL)),
        ],
        out_specs=_tp_spec(ct, L, D_MODEL, nc),
        out_shape=jax.ShapeDtypeStruct((L, R, D_MODEL), F32),
        scratch_shapes=[pltpu.VMEM(w_glu.shape, BF16), pltpu.VMEM(w_out.shape, BF16)],
        compiler_params=_params(2),
        name="mixout",
    )(c, y, x4, w_glu, w_out, g1)


def _memkv_kernel(m_ref, g_ref, wk_ref, wv_ref, k5_ref, v5_ref, kb_ref, vb_ref, wk_scr, wv_scr, *, nb):
    _cast_weights_once(1, (wk_ref, wk_scr), (wv_ref, wv_scr))
    for b in range(nb):
        m = _rms(m_ref[b], g_ref[...]).astype(BF16)
        for w_ref, o5_ref, ob_ref in ((wk_scr, k5_ref, kb_ref), (wv_scr, v5_ref, vb_ref)):
            p = jnp.dot(m, w_ref[...], preferred_element_type=F32)
            ob_ref[b] = p.astype(BF16)
            for hd in range(N_HEADS):
                o5_ref[b, :, hd, :] = p[:, hd * HEAD_DIM:(hd + 1) * HEAD_DIM]


def _memkv(mem, g_mem, w_k, w_v):
    bq = mem.shape[0]
    nb = 2
    out5 = pl.BlockSpec((nb, N_MEM, N_HEADS, HEAD_DIM), lambda i: (i, 0, 0, 0))
    outb = pl.BlockSpec((nb, N_MEM, D_MODEL), lambda i: (i, 0, 0))
    return pl.pallas_call(
        functools.partial(_memkv_kernel, nb=nb),
        grid=(bq // nb,),
        in_specs=[
            pl.BlockSpec((nb, N_MEM, D_MODEL), lambda i: (i, 0, 0)),
            _const((1, D_MODEL)),
            _const((D_MODEL, D_MODEL)),
            _const((D_MODEL, D_MODEL)),
        ],
        out_specs=[out5, out5, outb, outb],
        out_shape=[jax.ShapeDtypeStruct((bq, N_MEM, N_HEADS, HEAD_DIM), F32)] * 2
        + [jax.ShapeDtypeStruct((bq, N_MEM, D_MODEL), BF16)] * 2,
        scratch_shapes=[pltpu.VMEM(w_k.shape, BF16), pltpu.VMEM(w_v.shape, BF16)],
        compiler_params=_params(1),
        name="memkv",
    )(mem, g_mem, w_k, w_v)


def _softmax_rows(s):
    s = s - jnp.max(s, axis=-1, keepdims=True)
    e = jnp.exp(s)
    return e / jnp.sum(e, axis=-1, keepdims=True)


def _attn_prompt_kernel(x_ref, k_ref, v_ref, wq32_ref, wo32_ref, gq_ref, go_ref, o_ref, wq_ref, wo_ref,
                        *, L, ct, sub):
    _cast_weights_once(2, (wq32_ref, wq_ref), (wo32_ref, wo_ref))
    kb = k_ref[...]
    vb = v_ref[...]
    for c0, n in _sub_tiles(ct, sub):
        rows = L * n
        x = x_ref[:, c0:c0 + n, :].reshape(rows, D_MODEL)
        h = _rms(x, gq_ref[...])
        q = jnp.dot(h.astype(BF16), wq_ref[...], preferred_element_type=F32).astype(BF16)
        heads = []
        for hd in range(N_HEADS):
            sl = slice(hd * HEAD_DIM, (hd + 1) * HEAD_DIM)
            sc = lax.dot_general(q[:, sl], kb[:, sl], (((1,), (1,)), ((), ())), preferred_element_type=F32)
            p = _softmax_rows(sc * (HEAD_DIM ** -0.5))
            heads.append(jnp.dot(p.astype(BF16), vb[:, sl], preferred_element_type=F32))
        o = jnp.concatenate(heads, axis=-1)
        a = jnp.dot(o.astype(BF16), wo_ref[...], preferred_element_type=F32)
        o_ref[:, c0:c0 + n, :] = (x + _rms(a, go_ref[...])).reshape(L, n, D_MODEL)


def _attn_prompt(x1, k, v, w_q, w_o, g2, g3, *, C, ct, sub):
    L, R, _ = x1.shape
    nc = C // ct
    return pl.pallas_call(
        functools.partial(_attn_prompt_kernel, L=L, ct=ct, sub=sub),
        grid=(R // C, nc),
        in_specs=[
            _tp_spec(ct, L, D_MODEL, nc),
            pl.BlockSpec((None, N_MEM, D_MODEL), lambda b, i: (b, 0, 0)),
            pl.BlockSpec((None, N_MEM, D_MODEL), lambda b, i: (b, 0, 0)),
            _const((D_MODEL, D_MODEL)),
            _const((D_MODEL, D_MODEL)),
            _const((1, D_MODEL)),
            _const((1, D_MODEL)),
        ],
        out_specs=_tp_spec(ct, L, D_MODEL, nc),
        out_shape=jax.ShapeDtypeStruct((L, R, D_MODEL), F32),
        scratch_shapes=[pltpu.VMEM(w_q.shape, BF16), pltpu.VMEM(w_o.shape, BF16)],
        compiler_params=_params(2),
        name="attn_prompt",
    )(x1, k, v, w_q, w_o, g2, g3)


def _qproj_kernel(x_ref, wq_ref, g_ref, q_ref):
    h = _rms(x_ref[...], g_ref[...])
    q_ref[...] = jnp.dot(h.astype(BF16), wq_ref[...].astype(BF16), preferred_element_type=F32)


def _qproj(x1f, w_q, g2):
    rows = x1f.shape[0]
    return pl.pallas_call(
        _qproj_kernel,
        grid=(1,),
        in_specs=[_const((rows, D_MODEL)), _const((D_MODEL, D_MODEL)), _const((1, D_MODEL))],
        out_specs=pl.BlockSpec((rows, D_MODEL), lambda i: (0, 0)),
        out_shape=jax.ShapeDtypeStruct((rows, D_MODEL), F32),
        compiler_params=_params(1),
        name="qproj_sample",
    )(x1f, w_q, g2)


def _attn_sample_kernel(q_ref, k_ref, v_ref, o_ref, *, L, bb):
    rows = L * bb
    q2 = q_ref[...].reshape(rows, D_MODEL)
    qs = jnp.concatenate([q2[:, h * HEAD_DIM:(h + 1) * HEAD_DIM] for h in range(N_HEADS)], axis=0).astype(BF16)
    nr = N_HEADS * rows
    nc = N_MEM * N_HEADS
    row_head = lax.broadcasted_iota(jnp.int32, (nr, nc), 0) // rows
    col_head = lax.broadcasted_iota(jnp.int32, (nr, nc), 1) % N_HEADS
    same_head = row_head == col_head
    owner = lax.broadcasted_iota(jnp.int32, (nr, HEAD_DIM), 0) % bb
    acc = jnp.zeros((nr, HEAD_DIM), F32)
    for b in range(bb):
        ka = k_ref[b].reshape(nc, HEAD_DIM).astype(BF16)
        va = v_ref[b].reshape(nc, HEAD_DIM).astype(BF16)
        sc = lax.dot_general(qs, ka, (((1,), (1,)), ((), ())), preferred_element_type=F32)
        p = _softmax_rows(jnp.where(same_head, sc * (HEAD_DIM ** -0.5), -1e30))
        o = jnp.dot(p.astype(BF16), va, preferred_element_type=F32)
        acc = jnp.where(owner == b, o, acc)
    for h in range(N_HEADS):
        o_ref[:, :, h * HEAD_DIM:(h + 1) * HEAD_DIM] = acc[h * rows:(h + 1) * rows].reshape(L, bb, HEAD_DIM)


def _attn_sample(q3, k5, v5):
    L, R, _ = q3.shape
    bb = 8
    kv_spec = pl.BlockSpec((None, bb, N_MEM, N_HEADS, HEAD_DIM), lambda i: (0, i, 0, 0, 0))
    return pl.pallas_call(
        functools.partial(_attn_sample_kernel, L=L, bb=bb),
        grid=(R // bb,),
        in_specs=[pl.BlockSpec((L, bb, D_MODEL), lambda i: (0, i, 0)), kv_spec, kv_spec],
        out_specs=pl.BlockSpec((L, bb, D_MODEL), lambda i: (0, i, 0)),
        out_shape=jax.ShapeDtypeStruct((L, R, D_MODEL), F32),
        compiler_params=_params(1),
        name="attn_sample",
    )(q3, k5, v5)


def _oproj_kernel(o_ref, x_ref, wo_ref, g_ref, y_ref):
    a = jnp.dot(o_ref[...].astype(BF16), wo_ref[...].astype(BF16), preferred_element_type=F32)
    y_ref[...] = x_ref[...] + _rms(a, g_ref[...])


def _oproj(of, x1f, w_o, g3):
    rows = of.shape[0]
    return pl.pallas_call(
        _oproj_kernel,
        grid=(1,),
        in_specs=[_const((rows, D_MODEL)), _const((rows, D_MODEL)), _const((D_MODEL, D_MODEL)),
                  _const((1, D_MODEL))],
        out_specs=pl.BlockSpec((rows, D_MODEL), lambda i: (0, 0)),
        out_shape=jax.ShapeDtypeStruct((rows, D_MODEL), F32),
        compiler_params=_params(1),
        name="oproj_sample",
    )(of, x1f, w_o, g3)


FFN_WEIGHT_CHUNKS = 16
FFN_STAGE_SLOTS = 4


def _stream_cast(w_hbm, w16_scr, stage, sem):
    slots, rpc = stage.shape[0], stage.shape[1]
    n = w_hbm.shape[0] // rpc

    def chunk(k):
        return pltpu.make_async_copy(w_hbm.at[pl.ds(k * rpc, rpc)], stage.at[k % slots], sem.at[k % slots])

    for k in range(min(slots - 1, n)):
        chunk(k).start()
    for k in range(n):
        if k + slots - 1 < n:
            chunk(k + slots - 1).start()
        chunk(k).wait()
        w16_scr[pl.ds(k * rpc, rpc), :] = stage[k % slots].astype(BF16)


def _ffn_tile(x_ref, o_ref, wg_scr, wu_scr, wd_scr, gi_ref, go_ref, *, L, ct):
    rows = L * ct
    x = x_ref[...].reshape(rows, D_MODEL)
    h = _rms(x, gi_ref[...]).astype(BF16)
    gate = jnp.dot(h, wg_scr[...], preferred_element_type=F32)
    up = jnp.dot(h, wu_scr[...], preferred_element_type=F32)
    act = (gate * jax.nn.sigmoid(gate) * up).astype(BF16)
    dn = jnp.dot(act, wd_scr[...], preferred_element_type=F32)
    y = x + _rms(dn, go_ref[...])
    for s in range(L):
        o_ref[:, s, :] = y[s * ct:(s + 1) * ct]


def _ffn_kernel(xp_ref, xs_ref, wg_hbm, wu_hbm, wd_hbm, gi_ref, go_ref, op_ref, os_ref,
                wg_scr, wu_scr, wd_scr, stage_in, stage_out, sem, *, n_prompt, Lp, ctp, Ls, cts):
    i = pl.program_id(0)

    @pl.when(i == 0)
    def _():
        _stream_cast(wg_hbm, wg_scr, stage_in, sem)
        _stream_cast(wu_hbm, wu_scr, stage_in, sem)
        _stream_cast(wd_hbm, wd_scr, stage_out, sem)

    @pl.when(i < n_prompt)
    def _():
        _ffn_tile(xp_ref, op_ref, wg_scr, wu_scr, wd_scr, gi_ref, go_ref, L=Lp, ct=ctp)

    @pl.when(i == n_prompt)
    def _():
        _ffn_tile(xs_ref, os_ref, wg_scr, wu_scr, wd_scr, gi_ref, go_ref, L=Ls, ct=cts)


def _ffn(xp, xs, w_gate, w_up, w_down, g4, g5, *, bq, C, ct):
    Lp = xp.shape[0]
    Ls, rs, _ = xs.shape
    nc = C // ct
    n = bq * nc
    tile = lambda i: jnp.minimum(i, n - 1)
    hbm = pl.BlockSpec(memory_space=pl.ANY)
    return pl.pallas_call(
        functools.partial(_ffn_kernel, n_prompt=n, Lp=Lp, ctp=ct, Ls=Ls, cts=rs),
        grid=(n + 1,),
        in_specs=[
            pl.BlockSpec((Lp, ct, D_MODEL), lambda i: (0, tile(i), 0)),
            _const((Ls, rs, D_MODEL)),
            hbm, hbm, hbm,
            _const((1, D_MODEL)),
            _const((1, D_MODEL)),
        ],
        out_specs=[
            pl.BlockSpec((None, ct, Lp, D_MODEL), lambda i: (tile(i) // nc, tile(i) % nc, 0, 0)),
            pl.BlockSpec((None, rs, Ls, D_MODEL), lambda i: (0, 0, 0, 0)),
        ],
        out_shape=[
            jax.ShapeDtypeStruct((bq, C, Lp, D_MODEL), F32),
            jax.ShapeDtypeStruct((1, rs, Ls, D_MODEL), F32),
        ],
        scratch_shapes=[
            pltpu.VMEM((D_MODEL, D_FF), BF16),
            pltpu.VMEM((D_MODEL, D_FF), BF16),
            pltpu.VMEM((D_FF, D_MODEL), BF16),
            pltpu.VMEM((FFN_STAGE_SLOTS, D_MODEL // FFN_WEIGHT_CHUNKS, D_FF), F32),
            pltpu.VMEM((FFN_STAGE_SLOTS, D_FF // FFN_WEIGHT_CHUNKS, D_MODEL), F32),
            pltpu.SemaphoreType.DMA((FFN_STAGE_SLOTS,)),
        ],
        compiler_params=_params(1),
        name="ffn",
    )(xp, xs, w_gate, w_up, w_down, g4, g5)


def _layer(x4, mem_k, mem_v, conv_state_t, h0_packed, wts, *, ct, sub, sc, bt, sample):
    bq, C, L, _ = x4.shape
    R = bq * C
    ng = wts["norm_g"]
    g = [ng[i:i + 1] for i in range(6)]

    v, u4 = _inproj(x4, g[0], wts["w_in"], L=L, ct=ct, sub=sub)
    sw = wts["ssm_L%d" % L]
    nb = R // (bt * sc)
    ssm_args = (u4, sw[0], sw[1], sw[2], wts["d_skip"], h0_packed.reshape(N_LANE_BLOCKS, nb, bt, STATE_W))
    conv_args = (wts["w_dw"], wts["b_dw"], wts["ln_g"], wts["ln_b"])
    y, hl = _ssm(*ssm_args, L=L, C=sc, bt=bt)
    if sample:
        cact, conv_new = _conv_sample(v, conv_state_t, *conv_args)
    else:
        cact = _conv_prompt(v, *conv_args, L=L, C=C)
        tail = v.reshape(L, bq, C, D_CONV)[:, :, C - 2:, :]
        tail = jnp.transpose(tail, (1, 2, 0, 3)).reshape(bq, 2 * L, D_CONV)
        conv_new = tail[:, 2 * L - (CONV_WIDTH - 1):, :]
    hl = hl.reshape(N_LANE_BLOCKS, nb * bt, STATE_W)

    x1 = _mixout(cact, y, x4, wts["w_glu"], wts["w_out"], g[1], L=L, ct=ct, sub=sub)

    if sample:
        q = _qproj(x1.reshape(L * R, D_MODEL), wts["w_q"], g[2])
        o = _attn_sample(q.reshape(L, R, D_MODEL), mem_k, mem_v)
        x2 = _oproj(o.reshape(L * R, D_MODEL), x1.reshape(L * R, D_MODEL), wts["w_o"], g[3])
        x2 = x2.reshape(L, R, D_MODEL)
    else:
        x2 = _attn_prompt(x1, mem_k, mem_v, wts["w_q"], wts["w_o"], g[2], g[3], C=C, ct=ct, sub=sub)

    return x2, conv_new, hl


PROMPT_L = 16
PROMPT_SUB = 32
PROMPT_CT = 64


def kernel(x_prompt, x_sample, mem_prompt, cache_mem_k, cache_mem_v, state_conv, state_ssm_re, state_ssm_im,
           norm_g, mem_norm_g, w_in, w_dw, b_dw, ln_g, ln_b, lam_re, lam_im, log_dt, b_re, b_im, c_re, c_im,
           d_skip, w_glu, w_out, w_q, w_k, w_v, w_o, w_gate, w_up, w_down):
    depth = w_in.shape[0]
    bp, tp, _ = x_prompt.shape
    bs, ts, _ = x_sample.shape
    assert tp % (PROMPT_L * PROMPT_CT) == 0 and tp >= CONV_WIDTH - 1

    yp = x_prompt.reshape(bp, tp // PROMPT_L, PROMPT_L, D_MODEL)
    ys = x_sample.reshape(1, bs, ts, D_MODEL)
    outs = [[] for _ in range(8)]
    for l in range(depth):
        wts = {
            "norm_g": norm_g[l],
            "w_in": w_in[l],
            "w_dw": w_dw[l],
            "b_dw": b_dw[l][None],
            "ln_g": ln_g[l][None],
            "ln_b": ln_b[l][None],
            "d_skip": d_skip[l].reshape(N_LANE_BLOCKS, 1, LANES),
            "w_glu": w_glu[l],
            "w_out": w_out[l],
            "w_q": w_q[l],
            "w_o": w_o[l],
        }
        ssm_args = (lam_re[l], lam_im[l], log_dt[l], b_re[l], b_im[l], c_re[l], c_im[l])
        assert ts <= PROMPT_L
        full = _ssm_weights(*ssm_args, PROMPT_L)
        wts["ssm_L%d" % PROMPT_L] = full[:2] + (full[2](PROMPT_L),)
        wts["ssm_L%d" % ts] = _ssm_weights_prefix(full, PROMPT_L, ts)

        kp, vp, kp16, vp16 = _memkv(mem_prompt, mem_norm_g[l][None], w_k[l], w_v[l])
        h0p = jnp.zeros((N_LANE_BLOCKS, bp, STATE_W), F32)
        xp, cp, hp = _layer(yp, kp16, vp16, None, h0p, wts, ct=PROMPT_CT, sub=PROMPT_SUB,
                            sc=tp // PROMPT_L, bt=4, sample=False)
        h0s = _pack_state(state_ssm_re[l], state_ssm_im[l])
        xs, cs, hs = _layer(ys, cache_mem_k[l:l + 1], cache_mem_v[l:l + 1], jnp.transpose(state_conv[l], (1, 0, 2)),
                            h0s, wts, ct=bs, sub=bs, sc=1, bt=bs, sample=True)
        cs = jnp.transpose(cs, (1, 0, 2))
        yp, ys = _ffn(xp, xs, w_gate[l], w_up[l], w_down[l], norm_g[l][4:5], norm_g[l][5:6],
                      bq=bp, C=tp // PROMPT_L, ct=PROMPT_SUB)

        hp_re, hp_im = _unpack_state(hp)
        hs_re, hs_im = _unpack_state(hs)
        for lst, val in zip(outs, (kp, vp, cp, hp_re, hp_im, cs, hs_re, hs_im)):
            lst.append(val)
    return (yp.reshape(bp, tp, D_MODEL), ys.reshape(bs, ts, D_MODEL)) + tuple(jnp.stack(o) for o in outs)
```

```python
import functools

import jax
import jax.numpy as jnp
from jax import lax
from jax.experimental import pallas as pl
from jax.experimental.pallas import tpu as pltpu

F32 = jnp.float32
BF16 = jnp.bfloat16

D_MODEL = 1024
D_CONV = 512
D_SSM = 512
CONV_WIDTH = 31
N_GROUPS = 32
P_GROUP = 16
N_STATE = 64
N_MEM = 256
N_HEADS = 4
HEAD_DIM = 256
D_FF = 2816
RMS_EPS = 1e-6
LN_EPS = 1e-5

LANES = 128
SUBLANES = 8
GROUPS_PER_BLOCK = LANES // P_GROUP
N_LANE_BLOCKS = D_SSM // LANES
STATE_HALF = GROUPS_PER_BLOCK * N_STATE
STATE_W = 2 * STATE_HALF
MXU_N = 256
VMEM_LIMIT = 56 * 1024 * 1024


def _params(n_axes, vmem=VMEM_LIMIT):
    return pltpu.CompilerParams(dimension_semantics=("arbitrary",) * n_axes, vmem_limit_bytes=vmem)


def _const(shape):
    nd = len(shape)
    return pl.BlockSpec(shape, lambda *_: (0,) * nd, pipeline_mode=pl.Buffered(1))


def _rms(x, g):
    return x * lax.rsqrt(jnp.mean(x * x, axis=-1, keepdims=True) + RMS_EPS) * g


def _cast_weights_once(n_axes, *pairs):
    first = pl.program_id(0) == 0
    for a in range(1, n_axes):
        first = jnp.logical_and(first, pl.program_id(a) == 0)

    @pl.when(first)
    def _():
        for src, dst in pairs:
            dst[...] = src[...].astype(BF16)


def _gather_rows(x_ref, L, c0, n):
    return jnp.concatenate([x_ref[c0:c0 + n, s, :] for s in range(L)], axis=0)


def _sub_tiles(ct, sub):
    return [(c0, sub) for c0 in range(0, ct, sub)]


def _nat_spec(ct, L):
    return pl.BlockSpec((None, ct, L, D_MODEL), lambda b, i: (b, i, 0, 0))


def _tp_spec(ct, L, width, nc):
    return pl.BlockSpec((L, ct, width), lambda b, i: (0, b * nc + i, 0))


def _inproj_kernel(x_ref, g_ref, w_ref, v_ref, u_ref, w16_scr, *, L, ct, sub):
    _cast_weights_once(2, (w_ref, w16_scr))
    for c0, n in _sub_tiles(ct, sub):
        x = _gather_rows(x_ref, L, c0, n)
        h = _rms(x, g_ref[...])
        z = jnp.dot(h.astype(BF16), w16_scr[...], preferred_element_type=F32)
        a = z[:, :D_CONV]
        g = z[:, D_CONV:2 * D_CONV]
        u = z[:, 2 * D_CONV:].astype(BF16)
        v_ref[:, c0:c0 + n, :] = (a * jax.nn.sigmoid(g)).reshape(L, n, D_CONV)
        for s in range(L):
            for j in range(N_LANE_BLOCKS):
                u_ref[j, c0:c0 + n, s * LANES:(s + 1) * LANES] = u[s * n:(s + 1) * n, j * LANES:(j + 1) * LANES]


def _inproj(x4, g0, w_in, *, L, ct, sub):
    bq, c = x4.shape[:2]
    nc = c // ct
    R = bq * c
    return pl.pallas_call(
        functools.partial(_inproj_kernel, L=L, ct=ct, sub=sub),
        grid=(bq, nc),
        in_specs=[_nat_spec(ct, L), _const((1, D_MODEL)), _const((D_MODEL, 2 * D_CONV + D_SSM))],
        out_specs=[
            _tp_spec(ct, L, D_CONV, nc),
            pl.BlockSpec((N_LANE_BLOCKS, ct, L * LANES), lambda b, i: (0, b * nc + i, 0)),
        ],
        out_shape=[
            jax.ShapeDtypeStruct((L, R, D_CONV), F32),
            jax.ShapeDtypeStruct((N_LANE_BLOCKS, R, L * LANES), BF16),
        ],
        scratch_shapes=[pltpu.VMEM(w_in.shape, BF16)],
        compiler_params=_params(2),
        name="inproj",
    )(x4, g0, w_in)


def _ln_silu(acc, g, b):
    mu = jnp.mean(acc, axis=-1, keepdims=True)
    xc = acc - mu
    var = jnp.mean(xc * xc, axis=-1, keepdims=True)
    y = xc * lax.rsqrt(var + LN_EPS) * g + b
    return y * jax.nn.sigmoid(y)


def _conv_shifted_copies(v_ref, v1_ref, v2_ref, *, L, C):
    row = lax.broadcasted_iota(jnp.int32, (C, D_CONV), 0)
    for s in range(L):
        x = v_ref[s]
        v1_ref[s] = jnp.where(row >= 1, pltpu.roll(x, 1, 0), 0.0)
        v2_ref[s] = jnp.where(row >= 2, pltpu.roll(x, 2, 0), 0.0)


def _conv_rows(srcs, w_ref, b_ref, g_ref, bb_ref, o_ref, r0, *, L, rc):
    sub8 = rc // SUBLANES
    for s in range(L):
        acc = jnp.broadcast_to(b_ref[...][None], (sub8, SUBLANES, D_CONV))
        for d in range(CONV_WIDTH):
            blk = (s - d) % L
            shift = (d - s + L - 1) // L if d > s else 0
            k = CONV_WIDTH - 1 - d
            src = srcs[shift][blk, pl.ds(r0, rc), :].reshape(sub8, SUBLANES, D_CONV)
            acc = acc + w_ref[k][None] * src
        y = _ln_silu(acc.reshape(rc, D_CONV), g_ref[...], bb_ref[...])
        o_ref[s, pl.ds(r0, rc), :] = y.astype(o_ref.dtype)


def _conv_prompt_kernel(v_ref, w_ref, b_ref, g_ref, bb_ref, o_ref, v1_ref, v2_ref, *, L, C, rc):
    _conv_shifted_copies(v_ref, v1_ref, v2_ref, L=L, C=C)
    srcs = (v_ref, v1_ref, v2_ref)

    def chunk(i, carry):
        _conv_rows(srcs, w_ref, b_ref, g_ref, bb_ref, o_ref, pl.multiple_of(i * rc, rc), L=L, rc=rc)
        return carry

    lax.fori_loop(0, C // rc, chunk, 0)


def _conv_prompt(v, w_dw, b_dw, ln_g, ln_b, *, L, C):
    R = v.shape[1]
    rc = 32
    return pl.pallas_call(
        functools.partial(_conv_prompt_kernel, L=L, C=C, rc=rc),
        grid=(R // C,),
        in_specs=[
            pl.BlockSpec((L, C, D_CONV), lambda b: (0, b, 0)),
            _const((CONV_WIDTH, SUBLANES, D_CONV)),
            _const((SUBLANES, D_CONV)),
            _const((1, D_CONV)),
            _const((1, D_CONV)),
        ],
        out_specs=pl.BlockSpec((L, C, D_CONV), lambda b: (0, b, 0)),
        out_shape=jax.ShapeDtypeStruct((L, R, D_CONV), BF16),
        scratch_shapes=[pltpu.VMEM((L, C, D_CONV), F32)] * 2,
        compiler_params=_params(1),
        name="conv_prompt",
    )(v, jnp.broadcast_to(w_dw[:, None, :], (CONV_WIDTH, SUBLANES, D_CONV)),
      jnp.broadcast_to(b_dw, (SUBLANES, D_CONV)), ln_g, ln_b)


def _conv_sample_kernel(v_ref, st_ref, w_ref, b_ref, g_ref, bb_ref, o_ref, new_ref, *, L, H):
    def ext(i):
        return st_ref[i] if i < H else v_ref[i - H]

    for t in range(L):
        acc = jnp.broadcast_to(b_ref[...], v_ref.shape[1:])
        for k in range(CONV_WIDTH):
            acc = acc + w_ref[k:k + 1, :] * ext(t + k)
        o_ref[t] = _ln_silu(acc, g_ref[...], bb_ref[...]).astype(o_ref.dtype)
    for i in range(H):
        new_ref[i] = ext(i + L)


def _conv_sample(v, state_t, w_dw, b_dw, ln_g, ln_b):
    L, R, _ = v.shape
    H = CONV_WIDTH - 1
    return pl.pallas_call(
        functools.partial(_conv_sample_kernel, L=L, H=H),
        grid=(1,),
        in_specs=[
            _const((L, R, D_CONV)),
            _const((H, R, D_CONV)),
            _const((CONV_WIDTH, D_CONV)),
            _const((1, D_CONV)),
            _const((1, D_CONV)),
            _const((1, D_CONV)),
        ],
        out_specs=[
            pl.BlockSpec((L, R, D_CONV), lambda i: (0, 0, 0)),
            pl.BlockSpec((H, R, D_CONV), lambda i: (0, 0, 0)),
        ],
        out_shape=[
            jax.ShapeDtypeStruct((L, R, D_CONV), BF16),
            jax.ShapeDtypeStruct((H, R, D_CONV), F32),
        ],
        compiler_params=_params(1),
        name="conv_sample",
    )(v, state_t, w_dw, b_dw, ln_g, ln_b)


def _expand_block_diag(d):
    tiled = jnp.concatenate([d] * GROUPS_PER_BLOCK, axis=-1)
    r = lax.broadcasted_iota(jnp.int32, tiled.shape, 0) // P_GROUP
    c = lax.broadcasted_iota(jnp.int32, tiled.shape, 1) // N_STATE
    return jnp.where(r == c, tiled, jnp.zeros_like(tiled))


def _ssm_kernel(u_ref, wbc_ref, wcc_ref, a_ref, d_ref, h0_ref, y_ref, hl_ref,
                w_scr, wb_scr, wct_scr, s_scr, hp_scr, tap_scr, *, L, C, bt):
    per_tile = MXU_N // LANES

    @pl.when(pl.program_id(1) == 0)
    def _():
        for s in range(L):
            rows = slice(s * LANES, (s + 1) * LANES)
            for a in range(2):
                cols = slice(a * STATE_HALF, (a + 1) * STATE_HALF)
                wb_scr[rows, cols] = _expand_block_diag(wbc_ref[s, a])
                wct_scr[rows, cols] = _expand_block_diag(wcc_ref[s + 1, a])
        bbar = wb_scr[(L - 1) * LANES:L * LANES, :]
        nt = (((1,), (1,)), ((), ()))
        c0 = jnp.concatenate([_expand_block_diag(wcc_ref[0, a]) for a in range(2)], axis=-1)
        tap_scr[0] = lax.dot_general(bbar, c0, nt, preferred_element_type=F32).astype(BF16)
        for t in range(1, L):
            tap_scr[t] = lax.dot_general(bbar, wct_scr[(t - 1) * LANES:t * LANES, :], nt,
                                         preferred_element_type=F32).astype(BF16)
        for sp in range(L):
            for s in range(L):
                rows = slice(s * LANES, (s + 1) * LANES)
                cols = slice(sp * LANES, (sp + 1) * LANES)
                if s <= sp:
                    w_scr[rows, cols] = tap_scr[sp - s]
                elif s // per_tile == sp // per_tile:
                    w_scr[rows, cols] = jnp.zeros((LANES, LANES), BF16)

    nk = STATE_W // LANES
    half = nk // 2
    rows = bt * C
    inc_all = jnp.dot(u_ref[...], wb_scr[...], preferred_element_type=F32)
    for k in range(half):
        s_scr[k, 0:rows, :] = inc_all[:, k * LANES:(k + 1) * LANES]
        s_scr[k, rows:2 * rows, :] = inc_all[:, (half + k) * LANES:(half + k + 1) * LANES]

    d = d_ref[...]
    for n in range(L // per_tile):
        k = (n + 1) * MXU_N
        y = jnp.dot(u_ref[:, :k], w_scr[:k, n * MXU_N:(n + 1) * MXU_N], preferred_element_type=F32)
        for h in range(per_tile):
            s = n * per_tile + h
            us = u_ref[:, s * LANES:(s + 1) * LANES].astype(F32)
            y_ref[s] = y[:, h * LANES:(h + 1) * LANES] + d * us

    h0 = h0_ref[...]
    a = a_ref[...]

    def piece(x, k):
        return x[:, k * LANES:(k + 1) * LANES]

    def swap_parts(x):
        if 2 * bt == SUBLANES:
            return pltpu.roll(x, bt, 0)
        return jnp.concatenate([x[bt:], x[:bt]], axis=0)

    im_rows = lax.broadcasted_iota(jnp.int32, (2 * bt, LANES), 0) >= bt
    a_same = [jnp.broadcast_to(piece(a[0:1], k), (2 * bt, LANES)) for k in range(half)]
    a_cross = [jnp.where(im_rows, piece(a[1:2], k), -piece(a[1:2], k)) for k in range(half)]
    state = [jnp.concatenate([piece(h0, k), piece(h0, half + k)], axis=0) for k in range(half)]
    for c in range(C):
        sel = pl.ds(c, 2 * bt, stride=C) if C > 1 else pl.ds(0, 2 * bt)
        for k in range(half):
            hp_scr[k, sel, :] = state[k]
            state[k] = a_same[k] * state[k] + a_cross[k] * swap_parts(state[k]) + s_scr[k, sel, :]
    for k in range(half):
        hl_ref[:, k * LANES:(k + 1) * LANES] = state[k][:bt]
        hl_ref[:, (half + k) * LANES:(half + k + 1) * LANES] = state[k][bt:]

    hp = jnp.concatenate([hp_scr[k, 0:rows, :] for k in range(half)]
                         + [hp_scr[k, rows:2 * rows, :] for k in range(half)], axis=-1).astype(BF16)
    for n in range(L // per_tile):
        y = lax.dot_general(hp, wct_scr[n * MXU_N:(n + 1) * MXU_N, :], (((1,), (1,)), ((), ())),
                            preferred_element_type=F32)
        for h in range(per_tile):
            s = n * per_tile + h
            y_ref[s] += y[:, h * LANES:(h + 1) * LANES]


def _ssm(u4, wbc, wcc, a_pow, d_skip, h0, *, L, C, bt):
    R = u4.shape[1]
    rows = bt * C
    nb = R // rows
    lw = L * LANES
    in_specs = [
        pl.BlockSpec((None, rows, lw), lambda j, b: (j, b, 0)),
        pl.BlockSpec((None, L, 2, LANES, N_STATE), lambda j, b: (j, 0, 0, 0, 0)),
        pl.BlockSpec((None, L + 1, 2, LANES, N_STATE), lambda j, b: (j, 0, 0, 0, 0)),
        pl.BlockSpec((None, 2, STATE_HALF), lambda j, b: (j, 0, 0)),
        pl.BlockSpec((None, 1, LANES), lambda j, b: (j, 0, 0)),
        pl.BlockSpec((None, None, bt, STATE_W), lambda j, b: (j, b, 0, 0)),
    ]
    out_specs = [
        pl.BlockSpec((L, rows, LANES), lambda j, b: (0, b, j)),
        pl.BlockSpec((None, None, bt, STATE_W), lambda j, b: (j, b, 0, 0)),
    ]
    out_shape = [
        jax.ShapeDtypeStruct((L, R, D_SSM), F32),
        jax.ShapeDtypeStruct((N_LANE_BLOCKS, nb, bt, STATE_W), F32),
    ]
    scratch = [
        pltpu.VMEM((lw, lw), BF16),
        pltpu.VMEM((lw, STATE_W), BF16),
        pltpu.VMEM((lw, STATE_W), BF16),
        pltpu.VMEM((STATE_HALF // LANES, 2 * rows, LANES), F32),
        pltpu.VMEM((STATE_HALF // LANES, 2 * rows, LANES), F32),
        pltpu.VMEM((L, LANES, LANES), BF16),
    ]
    return pl.pallas_call(
        functools.partial(_ssm_kernel, L=L, C=C, bt=bt),
        grid=(N_LANE_BLOCKS, nb),
        in_specs=in_specs,
        out_specs=out_specs,
        out_shape=out_shape,
        scratch_shapes=scratch,
        compiler_params=_params(2),
        name="ssm",
    )(u4, wbc, wcc, a_pow, d_skip, h0)


def _ssm_weights(lam_re, lam_im, log_dt, b_re, b_im, c_re, c_im, L):
    dt = jnp.exp(log_dt)[:, None]
    zr = lam_re * dt
    zi = lam_im * dt
    n_pow = jnp.arange(L + 1, dtype=F32)[:, None, None]
    mag = jnp.exp(zr[None] * n_pow)
    pr = mag * jnp.cos(zi[None] * n_pow)
    pi = mag * jnp.sin(zi[None] * n_pow)
    a1r, a1i = pr[1], pi[1]
    den = lam_re * lam_re + lam_im * lam_im
    qr = ((a1r - 1.0) * lam_re + a1i * lam_im) / den
    qi = (a1i * lam_re - (a1r - 1.0) * lam_im) / den
    bbr = qr[:, :, None] * b_re - qi[:, :, None] * b_im
    bbi = qr[:, :, None] * b_im + qi[:, :, None] * b_re

    nj, g8 = N_LANE_BLOCKS, GROUPS_PER_BLOCK

    n_rev = (L - 1.0) - jnp.arange(L, dtype=F32)[:, None, None]
    mag_rev = jnp.exp(zr[None] * n_rev)
    rev = mag_rev * jnp.cos(zi[None] * n_rev), mag_rev * jnp.sin(zi[None] * n_rev)
    er = rev[0][:, :, :, None] * bbr[None] - rev[1][:, :, :, None] * bbi[None]
    ei = rev[0][:, :, :, None] * bbi[None] + rev[1][:, :, :, None] * bbr[None]
    e = jnp.stack([er, ei], 1).reshape(L, 2, nj, g8, N_STATE, P_GROUP)
    wbc = jnp.transpose(e, (2, 0, 1, 3, 5, 4)).reshape(nj, L, 2, LANES, N_STATE)

    mr = c_re[None] * pr[:, :, None, :] - c_im[None] * pi[:, :, None, :]
    mi = c_re[None] * pi[:, :, None, :] + c_im[None] * pr[:, :, None, :]
    m = jnp.stack([mr, -mi], 1).reshape(L + 1, 2, nj, g8, P_GROUP, N_STATE)
    wcc = jnp.transpose(m, (2, 0, 1, 3, 4, 5)).reshape(nj, L + 1, 2, LANES, N_STATE)

    def a_pow(n):
        return jnp.stack([pr[n].reshape(nj, STATE_HALF), pi[n].reshape(nj, STATE_HALF)], 1)

    return wbc.astype(BF16), wcc.astype(BF16), a_pow


def _ssm_weights_prefix(full, L_full, L):
    wbc, wcc, a_pow = full
    return wbc[:, L_full - L:], wcc[:, :L + 1], a_pow(L)


def _pack_state(re, im):
    b = re.shape[0]
    r = re.reshape(b, N_LANE_BLOCKS, STATE_HALF)
    i = im.reshape(b, N_LANE_BLOCKS, STATE_HALF)
    return jnp.transpose(jnp.concatenate([r, i], -1), (1, 0, 2))


def _unpack_state(h):
    b = h.shape[1]
    h = jnp.transpose(h, (1, 0, 2))
    re = h[:, :, :STATE_HALF].reshape(b, N_GROUPS, N_STATE)
    im = h[:, :, STATE_HALF:].reshape(b, N_GROUPS, N_STATE)
    return re, im


def _mixout_kernel(c_ref, y_ref, x_ref, wglu_ref, wout_ref, g_ref, o_ref, wglu_scr, wout_scr, *, L, ct, sub):
    _cast_weights_once(2, (wglu_ref, wglu_scr), (wout_ref, wout_scr))
    for c0, n in _sub_tiles(ct, sub):
        rows = L * n
        gy = jax.nn.gelu(y_ref[:, c0:c0 + n, :].reshape(rows, D_SSM))
        sg = gy * jax.nn.sigmoid(jnp.dot(gy.astype(BF16), wglu_scr[...], preferred_element_type=F32))
        m = jnp.dot(c_ref[:, c0:c0 + n, :].reshape(rows, D_CONV), wout_scr[0:D_CONV, :],
                    preferred_element_type=F32)
        m = m + jnp.dot(sg.astype(BF16), wout_scr[D_CONV:, :], preferred_element_type=F32)
        x = _gather_rows(x_ref, L, c0, n)
        o_ref[:, c0:c0 + n, :] = (x + _rms(m, g_ref[...])).reshape(L, n, D_MODEL)


def _mixout(c, y, x4, w_glu, w_out, g1, *, L, ct, sub):
    bq, cc = x4.shape[:2]
    nc = cc // ct
    R = bq * cc
    return pl.pallas_call(
        functools.partial(_mixout_kernel, L=L, ct=ct, sub=sub),
        grid=(bq, nc),
        in_specs=[
            _tp_spec(ct, L, D_CONV, nc),
            _tp_spec(ct, L, D_SSM, nc),
            _nat_spec(ct, L),
            _const((D_SSM, D_SSM)),
            _const((D_CONV + D_SSM, D_MODEL)),
            _const((1, D_MODEL)),
        ],
        out_specs=_tp_spec(ct, L, D_MODEL, nc),
        out_shape=jax.ShapeDtypeStruct((L, R, D_MODEL), F32),
        scratch_shapes=[pltpu.VMEM(w_glu.shape, BF16), pltpu.VMEM(w_out.shape, BF16)],
        compiler_params=_params(2),
        name="mixout",
    )(c, y, x4, w_glu, w_out, g1)


def _memkv_kernel(m_ref, g_ref, wk_ref, wv_ref, k5_ref, v5_ref, kb_ref, vb_ref, wk_scr, wv_scr, *, nb):
    _cast_weights_once(1, (wk_ref, wk_scr), (wv_ref, wv_scr))
    for b in range(nb):
        m = _rms(m_ref[b], g_ref[...]).astype(BF16)
        for w_ref, o5_ref, ob_ref in ((wk_scr, k5_ref, kb_ref), (wv_scr, v5_ref, vb_ref)):
            p = jnp.dot(m, w_ref[...], preferred_element_type=F32)
            ob_ref[b] = p.astype(BF16)
            for hd in range(N_HEADS):
                o5_ref[b, :, hd, :] = p[:, hd * HEAD_DIM:(hd + 1) * HEAD_DIM]


def _memkv(mem, g_mem, w_k, w_v):
    bq = mem.shape[0]
    nb = 2
    out5 = pl.BlockSpec((nb, N_MEM, N_HEADS, HEAD_DIM), lambda i: (i, 0, 0, 0))
    outb = pl.BlockSpec((nb, N_MEM, D_MODEL), lambda i: (i, 0, 0))
    return pl.pallas_call(
        functools.partial(_memkv_kernel, nb=nb),
        grid=(bq // nb,),
        in_specs=[
            pl.BlockSpec((nb, N_MEM, D_MODEL), lambda i: (i, 0, 0)),
            _const((1, D_MODEL)),
            _const((D_MODEL, D_MODEL)),
            _const((D_MODEL, D_MODEL)),
        ],
        out_specs=[out5, out5, outb, outb],
        out_shape=[jax.ShapeDtypeStruct((bq, N_MEM, N_HEADS, HEAD_DIM), F32)] * 2
        + [jax.ShapeDtypeStruct((bq, N_MEM, D_MODEL), BF16)] * 2,
        scratch_shapes=[pltpu.VMEM(w_k.shape, BF16), pltpu.VMEM(w_v.shape, BF16)],
        compiler_params=_params(1),
        name="memkv",
    )(mem, g_mem, w_k, w_v)


def _softmax_rows(s):
    s = s - jnp.max(s, axis=-1, keepdims=True)
    e = jnp.exp(s)
    return e / jnp.sum(e, axis=-1, keepdims=True)


def _attn_prompt_tile(x, kb, vb, wq_ref, wo_ref, gq_ref, go_ref):
    h = _rms(x, gq_ref[...])
    q = jnp.dot(h.astype(BF16), wq_ref[...], preferred_element_type=F32).astype(BF16)
    heads = []
    for hd in range(N_HEADS):
        sl = slice(hd * HEAD_DIM, (hd + 1) * HEAD_DIM)
        sc = lax.dot_general(q[:, sl], kb[:, sl], (((1,), (1,)), ((), ())), preferred_element_type=F32)
        p = _softmax_rows(sc * (HEAD_DIM ** -0.5))
        heads.append(jnp.dot(p.astype(BF16), vb[:, sl], preferred_element_type=F32))
    o = jnp.concatenate(heads, axis=-1)
    a = jnp.dot(o.astype(BF16), wo_ref[...], preferred_element_type=F32)
    return x + _rms(a, go_ref[...])


def _attn_sample_block(q2, k_ref, v_ref, *, bb):
    rows = q2.shape[0]
    qs = jnp.concatenate([q2[:, h * HEAD_DIM:(h + 1) * HEAD_DIM] for h in range(N_HEADS)], axis=0).astype(BF16)
    nr = N_HEADS * rows
    nc = N_MEM * N_HEADS
    row_head = lax.broadcasted_iota(jnp.int32, (nr, nc), 0) // rows
    col_head = lax.broadcasted_iota(jnp.int32, (nr, nc), 1) % N_HEADS
    same_head = row_head == col_head
    owner = lax.broadcasted_iota(jnp.int32, (nr, HEAD_DIM), 0) % bb
    acc = jnp.zeros((nr, HEAD_DIM), F32)
    for b in range(bb):
        ka = k_ref[b].reshape(nc, HEAD_DIM).astype(BF16)
        va = v_ref[b].reshape(nc, HEAD_DIM).astype(BF16)
        sc = lax.dot_general(qs, ka, (((1,), (1,)), ((), ())), preferred_element_type=F32)
        p = _softmax_rows(jnp.where(same_head, sc * (HEAD_DIM ** -0.5), -1e30))
        o = jnp.dot(p.astype(BF16), va, preferred_element_type=F32)
        acc = jnp.where(owner == b, o, acc)
    return jnp.concatenate([acc[h * rows:(h + 1) * rows] for h in range(N_HEADS)], axis=-1)


def _attn_kernel(x_ref, k_ref, v_ref, wq32_ref, wo32_ref, gq_ref, go_ref, qs_ref, ck_ref, cv_ref, o_ref, os_ref,
                 wq_ref, wo_ref, *, L, ct, bb):
    _cast_weights_once(1, (wq32_ref, wq_ref), (wo32_ref, wo_ref))
    x = x_ref[...].reshape(L * ct, D_MODEL)
    y = _attn_prompt_tile(x, k_ref[...], v_ref[...], wq_ref, wo_ref, gq_ref, go_ref)
    o_ref[...] = y.reshape(L, ct, D_MODEL)
    os_ref[...] = _attn_sample_block(qs_ref[...], ck_ref, cv_ref, bb=bb)


def _attn(x1, k, v, w_q, w_o, g2, g3, q_grp, cache_k, cache_v, *, C, ct):
    L, R, _ = x1.shape
    nc = C // ct
    n, rows_s, _ = q_grp.shape
    bb = cache_k.shape[1] // n
    assert n == R // ct and n * bb == cache_k.shape[1]
    kv_spec = pl.BlockSpec((None, bb, N_MEM, N_HEADS, HEAD_DIM), lambda i: (0, i, 0, 0, 0))
    grp_spec = pl.BlockSpec((None, rows_s, D_MODEL), lambda i: (i, 0, 0))
    return pl.pallas_call(
        functools.partial(_attn_kernel, L=L, ct=ct, bb=bb),
        grid=(n,),
        in_specs=[
            pl.BlockSpec((L, ct, D_MODEL), lambda i: (0, i, 0)),
            pl.BlockSpec((None, N_MEM, D_MODEL), lambda i: (i // nc, 0, 0)),
            pl.BlockSpec((None, N_MEM, D_MODEL), lambda i: (i // nc, 0, 0)),
            _const((D_MODEL, D_MODEL)),
            _const((D_MODEL, D_MODEL)),
            _const((1, D_MODEL)),
            _const((1, D_MODEL)),
            grp_spec, kv_spec, kv_spec,
        ],
        out_specs=[pl.BlockSpec((L, ct, D_MODEL), lambda i: (0, i, 0)), grp_spec],
        out_shape=[jax.ShapeDtypeStruct((L, R, D_MODEL), F32), jax.ShapeDtypeStruct(q_grp.shape, F32)],
        scratch_shapes=[pltpu.VMEM(w_q.shape, BF16), pltpu.VMEM(w_o.shape, BF16)],
        compiler_params=_params(1),
        name="attn",
    )(x1, k, v, w_q, w_o, g2, g3, q_grp, cache_k, cache_v)


def _qproj_kernel(x_ref, wq_ref, g_ref, q_ref):
    h = _rms(x_ref[...], g_ref[...])
    q_ref[...] = jnp.dot(h.astype(BF16), wq_ref[...].astype(BF16), preferred_element_type=F32)


def _qproj(x1f, w_q, g2):
    rows = x1f.shape[0]
    return pl.pallas_call(
        _qproj_kernel,
        grid=(1,),
        in_specs=[_const((rows, D_MODEL)), _const((D_MODEL, D_MODEL)), _const((1, D_MODEL))],
        out_specs=pl.BlockSpec((rows, D_MODEL), lambda i: (0, 0)),
        out_shape=jax.ShapeDtypeStruct((rows, D_MODEL), F32),
        compiler_params=_params(1),
        name="qproj_sample",
    )(x1f, w_q, g2)


def _oproj_kernel(o_ref, x_ref, wo_ref, g_ref, y_ref):
    a = jnp.dot(o_ref[...].astype(BF16), wo_ref[...].astype(BF16), preferred_element_type=F32)
    y_ref[...] = x_ref[...] + _rms(a, g_ref[...])


def _oproj(of, x1f, w_o, g3):
    rows = of.shape[0]
    return pl.pallas_call(
        _oproj_kernel,
        grid=(1,),
        in_specs=[_const((rows, D_MODEL)), _const((rows, D_MODEL)), _const((D_MODEL, D_MODEL)),
                  _const((1, D_MODEL))],
        out_specs=pl.BlockSpec((rows, D_MODEL), lambda i: (0, 0)),
        out_shape=jax.ShapeDtypeStruct((rows, D_MODEL), F32),
        compiler_params=_params(1),
        name="oproj_sample",
    )(of, x1f, w_o, g3)


FFN_WEIGHT_CHUNKS = 16
FFN_STAGE_SLOTS = 4


def _stream_cast(w_hbm, w16_scr, stage, sem):
    slots, rpc = stage.shape[0], stage.shape[1]
    n = w_hbm.shape[0] // rpc

    def chunk(k):
        return pltpu.make_async_copy(w_hbm.at[pl.ds(k * rpc, rpc)], stage.at[k % slots], sem.at[k % slots])

    for k in range(min(slots - 1, n)):
        chunk(k).start()
    for k in range(n):
        if k + slots - 1 < n:
            chunk(k + slots - 1).start()
        chunk(k).wait()
        w16_scr[pl.ds(k * rpc, rpc), :] = stage[k % slots].astype(BF16)


def _ffn_tile(x_ref, o_ref, wg_scr, wu_scr, wd_scr, gi_ref, go_ref, *, L, ct):
    rows = L * ct
    x = x_ref[...].reshape(rows, D_MODEL)
    h = _rms(x, gi_ref[...]).astype(BF16)
    gate = jnp.dot(h, wg_scr[...], preferred_element_type=F32)
    up = jnp.dot(h, wu_scr[...], preferred_element_type=F32)
    act = (gate * jax.nn.sigmoid(gate) * up).astype(BF16)
    dn = jnp.dot(act, wd_scr[...], preferred_element_type=F32)
    y = x + _rms(dn, go_ref[...])
    for s in range(L):
        o_ref[:, s, :] = y[s * ct:(s + 1) * ct]


def _ffn_kernel(xp_ref, xs_ref, wg_hbm, wu_hbm, wd_hbm, gi_ref, go_ref, op_ref, os_ref,
                wg_scr, wu_scr, wd_scr, stage_in, stage_out, sem, *, n_prompt, Lp, ctp, Ls, cts):
    i = pl.program_id(0)

    @pl.when(i == 0)
    def _():
        _stream_cast(wg_hbm, wg_scr, stage_in, sem)
        _stream_cast(wu_hbm, wu_scr, stage_in, sem)
        _stream_cast(wd_hbm, wd_scr, stage_out, sem)

    @pl.when(i < n_prompt)
    def _():
        _ffn_tile(xp_ref, op_ref, wg_scr, wu_scr, wd_scr, gi_ref, go_ref, L=Lp, ct=ctp)

    @pl.when(i == n_prompt)
    def _():
        _ffn_tile(xs_ref, os_ref, wg_scr, wu_scr, wd_scr, gi_ref, go_ref, L=Ls, ct=cts)


def _ffn(xp, xs, w_gate, w_up, w_down, g4, g5, *, bq, C, ct):
    Lp = xp.shape[0]
    Ls, rs, _ = xs.shape
    nc = C // ct
    n = bq * nc
    tile = lambda i: jnp.minimum(i, n - 1)
    hbm = pl.BlockSpec(memory_space=pl.ANY)
    return pl.pallas_call(
        functools.partial(_ffn_kernel, n_prompt=n, Lp=Lp, ctp=ct, Ls=Ls, cts=rs),
        grid=(n + 1,),
        in_specs=[
            pl.BlockSpec((Lp, ct, D_MODEL), lambda i: (0, tile(i), 0)),
            _const((Ls, rs, D_MODEL)),
            hbm, hbm, hbm,
            _const((1, D_MODEL)),
            _const((1, D_MODEL)),
        ],
        out_specs=[
            pl.BlockSpec((None, ct, Lp, D_MODEL), lambda i: (tile(i) // nc, tile(i) % nc, 0, 0)),
            pl.BlockSpec((None, rs, Ls, D_MODEL), lambda i: (0, 0, 0, 0)),
        ],
        out_shape=[
            jax.ShapeDtypeStruct((bq, C, Lp, D_MODEL), F32),
            jax.ShapeDtypeStruct((1, rs, Ls, D_MODEL), F32),
        ],
        scratch_shapes=[
            pltpu.VMEM((D_MODEL, D_FF), BF16),
            pltpu.VMEM((D_MODEL, D_FF), BF16),
            pltpu.VMEM((D_FF, D_MODEL), BF16),
            pltpu.VMEM((FFN_STAGE_SLOTS, D_MODEL // FFN_WEIGHT_CHUNKS, D_FF), F32),
            pltpu.VMEM((FFN_STAGE_SLOTS, D_FF // FFN_WEIGHT_CHUNKS, D_MODEL), F32),
            pltpu.SemaphoreType.DMA((FFN_STAGE_SLOTS,)),
        ],
        compiler_params=_params(1),
        name="ffn",
    )(xp, xs, w_gate, w_up, w_down, g4, g5)


def _mixer(x4, conv_state_t, h0_packed, wts, *, ct, sub, sc, bt, sample):
    bq, C, L, _ = x4.shape
    R = bq * C
    g = [wts["norm_g"][i:i + 1] for i in range(2)]

    v, u4 = _inproj(x4, g[0], wts["w_in"], L=L, ct=ct, sub=sub)
    sw = wts["ssm_L%d" % L]
    nb = R // (bt * sc)
    ssm_args = (u4, sw[0], sw[1], sw[2], wts["d_skip"], h0_packed.reshape(N_LANE_BLOCKS, nb, bt, STATE_W))
    conv_args = (wts["w_dw"], wts["b_dw"], wts["ln_g"], wts["ln_b"])
    y, hl = _ssm(*ssm_args, L=L, C=sc, bt=bt)
    if sample:
        cact, conv_new = _conv_sample(v, conv_state_t, *conv_args)
    else:
        cact = _conv_prompt(v, *conv_args, L=L, C=C)
        tail = v.reshape(L, bq, C, D_CONV)[:, :, C - 2:, :]
        tail = jnp.transpose(tail, (1, 2, 0, 3)).reshape(bq, 2 * L, D_CONV)
        conv_new = tail[:, 2 * L - (CONV_WIDTH - 1):, :]
    hl = hl.reshape(N_LANE_BLOCKS, nb * bt, STATE_W)

    x1 = _mixout(cact, y, x4, wts["w_glu"], wts["w_out"], g[1], L=L, ct=ct, sub=sub)
    return x1, conv_new, hl


PROMPT_L = 16
PROMPT_SUB = 32
PROMPT_CT = 64


def kernel(x_prompt, x_sample, mem_prompt, cache_mem_k, cache_mem_v, state_conv, state_ssm_re, state_ssm_im,
           norm_g, mem_norm_g, w_in, w_dw, b_dw, ln_g, ln_b, lam_re, lam_im, log_dt, b_re, b_im, c_re, c_im,
           d_skip, w_glu, w_out, w_q, w_k, w_v, w_o, w_gate, w_up, w_down):
    depth = w_in.shape[0]
    bp, tp, _ = x_prompt.shape
    bs, ts, _ = x_sample.shape
    assert tp % (PROMPT_L * PROMPT_CT) == 0 and tp >= CONV_WIDTH - 1

    yp = x_prompt.reshape(bp, tp // PROMPT_L, PROMPT_L, D_MODEL)
    ys = x_sample.reshape(1, bs, ts, D_MODEL)
    outs = [[] for _ in range(8)]
    for l in range(depth):
        wts = {
            "norm_g": norm_g[l],
            "w_in": w_in[l],
            "w_dw": w_dw[l],
            "b_dw": b_dw[l][None],
            "ln_g": ln_g[l][None],
            "ln_b": ln_b[l][None],
            "d_skip": d_skip[l].reshape(N_LANE_BLOCKS, 1, LANES),
            "w_glu": w_glu[l],
            "w_out": w_out[l],
        }
        ssm_args = (lam_re[l], lam_im[l], log_dt[l], b_re[l], b_im[l], c_re[l], c_im[l])
        assert ts <= PROMPT_L
        full = _ssm_weights(*ssm_args, PROMPT_L)
        wts["ssm_L%d" % PROMPT_L] = full[:2] + (full[2](PROMPT_L),)
        wts["ssm_L%d" % ts] = _ssm_weights_prefix(full, PROMPT_L, ts)

        kp, vp, kp16, vp16 = _memkv(mem_prompt, mem_norm_g[l][None], w_k[l], w_v[l])
        h0p = jnp.zeros((N_LANE_BLOCKS, bp, STATE_W), F32)
        cp_chunks = tp // PROMPT_L
        xp, cp, hp = _mixer(yp, None, h0p, wts, ct=PROMPT_CT, sub=PROMPT_SUB, sc=cp_chunks, bt=4, sample=False)
        h0s = _pack_state(state_ssm_re[l], state_ssm_im[l])
        xs, cs, hs = _mixer(ys, jnp.transpose(state_conv[l], (1, 0, 2)), h0s, wts, ct=bs, sub=bs, sc=1, bt=bs,
                            sample=True)
        cs = jnp.transpose(cs, (1, 0, 2))

        g2, g3 = norm_g[l][2:3], norm_g[l][3:4]
        n_tiles = bp * cp_chunks // PROMPT_SUB
        bb = bs // n_tiles
        xs_flat = xs.reshape(ts * bs, D_MODEL)
        q = _qproj(xs_flat, w_q[l], g2).reshape(ts, n_tiles, bb, D_MODEL)
        q = jnp.transpose(q, (1, 0, 2, 3)).reshape(n_tiles, ts * bb, D_MODEL)
        xp, o = _attn(xp, kp16, vp16, w_q[l], w_o[l], g2, g3, q, cache_mem_k[l:l + 1], cache_mem_v[l:l + 1],
                      C=cp_chunks, ct=PROMPT_SUB)
        o = jnp.transpose(o.reshape(n_tiles, ts, bb, D_MODEL), (1, 0, 2, 3)).reshape(ts * bs, D_MODEL)
        xs = _oproj(o, xs_flat, w_o[l], g3).reshape(ts, bs, D_MODEL)

        yp, ys = _ffn(xp, xs, w_gate[l], w_up[l], w_down[l], norm_g[l][4:5], norm_g[l][5:6],
                      bq=bp, C=cp_chunks, ct=PROMPT_SUB)

        hp_re, hp_im = _unpack_state(hp)
        hs_re, hs_im = _unpack_state(hs)
        for lst, val in zip(outs, (kp, vp, cp, hp_re, hp_im, cs, hs_re, hs_im)):
            lst.append(val)
    return (yp.reshape(bp, tp, D_MODEL), ys.reshape(bs, ts, D_MODEL)) + tuple(jnp.stack(o) for o in outs)
```

```python
import functools

import jax
import jax.numpy as jnp
from jax import lax
from jax.experimental import pallas as pl
from jax.experimental.pallas import tpu as pltpu

F32 = jnp.float32
BF16 = jnp.bfloat16

D_MODEL = 1024
D_CONV = 512
D_SSM = 512
CONV_WIDTH = 31
N_GROUPS = 32
P_GROUP = 16
N_STATE = 64
N_MEM = 256
N_HEADS = 4
HEAD_DIM = 256
D_FF = 2816
RMS_EPS = 1e-6
LN_EPS = 1e-5

LANES = 128
SUBLANES = 8
GROUPS_PER_BLOCK = LANES // P_GROUP
N_LANE_BLOCKS = D_SSM // LANES
STATE_HALF = GROUPS_PER_BLOCK * N_STATE
STATE_W = 2 * STATE_HALF
MXU_N = 256
VMEM_LIMIT = 56 * 1024 * 1024


def _params(n_axes, vmem=VMEM_LIMIT):
    return pltpu.CompilerParams(dimension_semantics=("arbitrary",) * n_axes, vmem_limit_bytes=vmem)


def _const(shape):
    nd = len(shape)
    return pl.BlockSpec(shape, lambda *_: (0,) * nd, pipeline_mode=pl.Buffered(1))


def _rms(x, g):
    return x * lax.rsqrt(jnp.mean(x * x, axis=-1, keepdims=True) + RMS_EPS) * g


def _cast_weights_once(n_axes, *pairs):
    first = pl.program_id(0) == 0
    for a in range(1, n_axes):
        first = jnp.logical_and(first, pl.program_id(a) == 0)

    @pl.when(first)
    def _():
        for src, dst in pairs:
            dst[...] = src[...].astype(BF16)


def _gather_rows(x_ref, L, c0, n):
    return jnp.concatenate([x_ref[c0:c0 + n, s, :] for s in range(L)], axis=0)


def _sub_tiles(ct, sub):
    return [(c0, sub) for c0 in range(0, ct, sub)]


def _nat_spec(ct, L):
    return pl.BlockSpec((None, ct, L, D_MODEL), lambda b, i: (b, i, 0, 0))


def _tp_spec(ct, L, width, nc):
    return pl.BlockSpec((L, ct, width), lambda b, i: (0, b * nc + i, 0))


def _inproj_kernel(x_ref, g_ref, w_ref, v_ref, u_ref, w16_scr, *, L, ct, sub):
    _cast_weights_once(2, (w_ref, w16_scr))
    for c0, n in _sub_tiles(ct, sub):
        x = _gather_rows(x_ref, L, c0, n)
        h = _rms(x, g_ref[...])
        z = jnp.dot(h.astype(BF16), w16_scr[...], preferred_element_type=F32)
        a = z[:, :D_CONV]
        g = z[:, D_CONV:2 * D_CONV]
        u = z[:, 2 * D_CONV:].astype(BF16)
        v_ref[:, c0:c0 + n, :] = (a * jax.nn.sigmoid(g)).reshape(L, n, D_CONV)
        for s in range(L):
            for j in range(N_LANE_BLOCKS):
                u_ref[j, c0:c0 + n, s * LANES:(s + 1) * LANES] = u[s * n:(s + 1) * n, j * LANES:(j + 1) * LANES]


def _inproj(x4, g0, w_in, *, L, ct, sub):
    bq, c = x4.shape[:2]
    nc = c // ct
    R = bq * c
    return pl.pallas_call(
        functools.partial(_inproj_kernel, L=L, ct=ct, sub=sub),
        grid=(bq, nc),
        in_specs=[_nat_spec(ct, L), _const((1, D_MODEL)), _const((D_MODEL, 2 * D_CONV + D_SSM))],
        out_specs=[
            _tp_spec(ct, L, D_CONV, nc),
            pl.BlockSpec((N_LANE_BLOCKS, ct, L * LANES), lambda b, i: (0, b * nc + i, 0)),
        ],
        out_shape=[
            jax.ShapeDtypeStruct((L, R, D_CONV), F32),
            jax.ShapeDtypeStruct((N_LANE_BLOCKS, R, L * LANES), BF16),
        ],
        scratch_shapes=[pltpu.VMEM(w_in.shape, BF16)],
        compiler_params=_params(2),
        name="inproj",
    )(x4, g0, w_in)


def _ln_silu(acc, g, b):
    mu = jnp.mean(acc, axis=-1, keepdims=True)
    xc = acc - mu
    var = jnp.mean(xc * xc, axis=-1, keepdims=True)
    y = xc * lax.rsqrt(var + LN_EPS) * g + b
    return y * jax.nn.sigmoid(y)


def _conv_shifted_copies(v_ref, v1_ref, v2_ref, *, L, C):
    row = lax.broadcasted_iota(jnp.int32, (C, D_CONV), 0)
    for s in range(L):
        x = v_ref[s]
        v1_ref[s] = jnp.where(row >= 1, pltpu.roll(x, 1, 0), 0.0)
        v2_ref[s] = jnp.where(row >= 2, pltpu.roll(x, 2, 0), 0.0)


def _conv_rows(srcs, w_ref, b_ref, g_ref, bb_ref, o_ref, r0, *, L, rc):
    sub8 = rc // SUBLANES
    for s in range(L):
        acc = jnp.broadcast_to(b_ref[...][None], (sub8, SUBLANES, D_CONV))
        for d in range(CONV_WIDTH):
            blk = (s - d) % L
            shift = (d - s + L - 1) // L if d > s else 0
            k = CONV_WIDTH - 1 - d
            src = srcs[shift][blk, pl.ds(r0, rc), :].reshape(sub8, SUBLANES, D_CONV)
            acc = acc + w_ref[k][None] * src
        y = _ln_silu(acc.reshape(rc, D_CONV), g_ref[...], bb_ref[...])
        o_ref[s, pl.ds(r0, rc), :] = y.astype(o_ref.dtype)


def _conv_prompt_kernel(v_ref, w_ref, b_ref, g_ref, bb_ref, o_ref, v1_ref, v2_ref, *, L, C, rc):
    _conv_shifted_copies(v_ref, v1_ref, v2_ref, L=L, C=C)
    srcs = (v_ref, v1_ref, v2_ref)

    def chunk(i, carry):
        _conv_rows(srcs, w_ref, b_ref, g_ref, bb_ref, o_ref, pl.multiple_of(i * rc, rc), L=L, rc=rc)
        return carry

    lax.fori_loop(0, C // rc, chunk, 0)


def _conv_prompt(v, w_dw, b_dw, ln_g, ln_b, *, L, C):
    R = v.shape[1]
    rc = 32
    return pl.pallas_call(
        functools.partial(_conv_prompt_kernel, L=L, C=C, rc=rc),
        grid=(R // C,),
        in_specs=[
            pl.BlockSpec((L, C, D_CONV), lambda b: (0, b, 0)),
            _const((CONV_WIDTH, SUBLANES, D_CONV)),
            _const((SUBLANES, D_CONV)),
            _const((1, D_CONV)),
            _const((1, D_CONV)),
        ],
        out_specs=pl.BlockSpec((L, C, D_CONV), lambda b: (0, b, 0)),
        out_shape=jax.ShapeDtypeStruct((L, R, D_CONV), BF16),
        scratch_shapes=[pltpu.VMEM((L, C, D_CONV), F32)] * 2,
        compiler_params=_params(1),
        name="conv_prompt",
    )(v, jnp.broadcast_to(w_dw[:, None, :], (CONV_WIDTH, SUBLANES, D_CONV)),
      jnp.broadcast_to(b_dw, (SUBLANES, D_CONV)), ln_g, ln_b)


def _conv_sample_kernel(v_ref, st_ref, w_ref, b_ref, g_ref, bb_ref, o_ref, new_ref, *, L, H):
    def ext(i):
        return st_ref[i] if i < H else v_ref[i - H]

    for t in range(L):
        acc = jnp.broadcast_to(b_ref[...], v_ref.shape[1:])
        for k in range(CONV_WIDTH):
            acc = acc + w_ref[k:k + 1, :] * ext(t + k)
        o_ref[t] = _ln_silu(acc, g_ref[...], bb_ref[...]).astype(o_ref.dtype)
    for i in range(H):
        new_ref[i] = ext(i + L)


def _conv_sample(v, state_t, w_dw, b_dw, ln_g, ln_b):
    L, R, _ = v.shape
    H = CONV_WIDTH - 1
    rt = 32
    rows = lambda n: pl.BlockSpec((n, rt, D_CONV), lambda i: (0, i, 0))
    return pl.pallas_call(
        functools.partial(_conv_sample_kernel, L=L, H=H),
        grid=(R // rt,),
        in_specs=[
            rows(L),
            rows(H),
            _const((CONV_WIDTH, D_CONV)),
            _const((1, D_CONV)),
            _const((1, D_CONV)),
            _const((1, D_CONV)),
        ],
        out_specs=[rows(L), rows(H)],
        out_shape=[
            jax.ShapeDtypeStruct((L, R, D_CONV), BF16),
            jax.ShapeDtypeStruct((H, R, D_CONV), F32),
        ],
        compiler_params=_params(1),
        name="conv_sample",
    )(v, state_t, w_dw, b_dw, ln_g, ln_b)


def _expand_block_diag(d):
    tiled = jnp.concatenate([d] * GROUPS_PER_BLOCK, axis=-1)
    r = lax.broadcasted_iota(jnp.int32, tiled.shape, 0) // P_GROUP
    c = lax.broadcasted_iota(jnp.int32, tiled.shape, 1) // N_STATE
    return jnp.where(r == c, tiled, jnp.zeros_like(tiled))


def _ssm_kernel(u_ref, wbc_ref, wcc_ref, a_ref, d_ref, h0_ref, y_ref, hl_ref,
                w_scr, wb_scr, wct_scr, s_scr, hp_scr, tap_scr, *, L, C, bt):
    per_tile = MXU_N // LANES

    @pl.when(pl.program_id(1) == 0)
    def _():
        for s in range(L):
            rows = slice(s * LANES, (s + 1) * LANES)
            for a in range(2):
                cols = slice(a * STATE_HALF, (a + 1) * STATE_HALF)
                wb_scr[rows, cols] = _expand_block_diag(wbc_ref[s, a])
                wct_scr[rows, cols] = _expand_block_diag(wcc_ref[s + 1, a])
        bbar = wb_scr[(L - 1) * LANES:L * LANES, :]
        nt = (((1,), (1,)), ((), ()))
        c0 = jnp.concatenate([_expand_block_diag(wcc_ref[0, a]) for a in range(2)], axis=-1)
        tap_scr[0] = lax.dot_general(bbar, c0, nt, preferred_element_type=F32).astype(BF16)
        for t in range(1, L):
            tap_scr[t] = lax.dot_general(bbar, wct_scr[(t - 1) * LANES:t * LANES, :], nt,
                                         preferred_element_type=F32).astype(BF16)
        for sp in range(L):
            for s in range(L):
                rows = slice(s * LANES, (s + 1) * LANES)
                cols = slice(sp * LANES, (sp + 1) * LANES)
                if s <= sp:
                    w_scr[rows, cols] = tap_scr[sp - s]
                elif s // per_tile == sp // per_tile:
                    w_scr[rows, cols] = jnp.zeros((LANES, LANES), BF16)

    nk = STATE_W // LANES
    half = nk // 2
    rows = bt * C
    inc_all = jnp.dot(u_ref[...], wb_scr[...], preferred_element_type=F32)
    for k in range(half):
        s_scr[k, 0:rows, :] = inc_all[:, k * LANES:(k + 1) * LANES]
        s_scr[k, rows:2 * rows, :] = inc_all[:, (half + k) * LANES:(half + k + 1) * LANES]

    d = d_ref[...]
    for n in range(L // per_tile):
        k = (n + 1) * MXU_N
        y = jnp.dot(u_ref[:, :k], w_scr[:k, n * MXU_N:(n + 1) * MXU_N], preferred_element_type=F32)
        for h in range(per_tile):
            s = n * per_tile + h
            us = u_ref[:, s * LANES:(s + 1) * LANES].astype(F32)
            y_ref[s] = y[:, h * LANES:(h + 1) * LANES] + d * us

    h0 = h0_ref[...]
    a = a_ref[...]

    def piece(x, k):
        return x[:, k * LANES:(k + 1) * LANES]

    def swap_parts(x):
        if 2 * bt == SUBLANES:
            return pltpu.roll(x, bt, 0)
        return jnp.concatenate([x[bt:], x[:bt]], axis=0)

    im_rows = lax.broadcasted_iota(jnp.int32, (2 * bt, LANES), 0) >= bt
    a_same = [jnp.broadcast_to(piece(a[0:1], k), (2 * bt, LANES)) for k in range(half)]
    a_cross = [jnp.where(im_rows, piece(a[1:2], k), -piece(a[1:2], k)) for k in range(half)]
    state = [jnp.concatenate([piece(h0, k), piece(h0, half + k)], axis=0) for k in range(half)]
    for c in range(C):
        sel = pl.ds(c, 2 * bt, stride=C) if C > 1 else pl.ds(0, 2 * bt)
        for k in range(half):
            hp_scr[k, sel, :] = state[k]
            state[k] = a_same[k] * state[k] + a_cross[k] * swap_parts(state[k]) + s_scr[k, sel, :]
    for k in range(half):
        hl_ref[:, k * LANES:(k + 1) * LANES] = state[k][:bt]
        hl_ref[:, (half + k) * LANES:(half + k + 1) * LANES] = state[k][bt:]

    hp = jnp.concatenate([hp_scr[k, 0:rows, :] for k in range(half)]
                         + [hp_scr[k, rows:2 * rows, :] for k in range(half)], axis=-1).astype(BF16)
    for n in range(L // per_tile):
        y = lax.dot_general(hp, wct_scr[n * MXU_N:(n + 1) * MXU_N, :], (((1,), (1,)), ((), ())),
                            preferred_element_type=F32)
        for h in range(per_tile):
            s = n * per_tile + h
            y_ref[s] += y[:, h * LANES:(h + 1) * LANES]


def _ssm(u4, wbc, wcc, a_pow, d_skip, h0, *, L, C, bt):
    R = u4.shape[1]
    rows = bt * C
    nb = R // rows
    lw = L * LANES
    in_specs = [
        pl.BlockSpec((None, rows, lw), lambda j, b: (j, b, 0)),
        pl.BlockSpec((None, L, 2, LANES, N_STATE), lambda j, b: (j, 0, 0, 0, 0)),
        pl.BlockSpec((None, L + 1, 2, LANES, N_STATE), lambda j, b: (j, 0, 0, 0, 0)),
        pl.BlockSpec((None, 2, STATE_HALF), lambda j, b: (j, 0, 0)),
        pl.BlockSpec((None, 1, LANES), lambda j, b: (j, 0, 0)),
        pl.BlockSpec((None, None, bt, STATE_W), lambda j, b: (j, b, 0, 0)),
    ]
    out_specs = [
        pl.BlockSpec((L, rows, LANES), lambda j, b: (0, b, j)),
        pl.BlockSpec((None, None, bt, STATE_W), lambda j, b: (j, b, 0, 0)),
    ]
    out_shape = [
        jax.ShapeDtypeStruct((L, R, D_SSM), F32),
        jax.ShapeDtypeStruct((N_LANE_BLOCKS, nb, bt, STATE_W), F32),
    ]
    scratch = [
        pltpu.VMEM((lw, lw), BF16),
        pltpu.VMEM((lw, STATE_W), BF16),
        pltpu.VMEM((lw, STATE_W), BF16),
        pltpu.VMEM((STATE_HALF // LANES, 2 * rows, LANES), F32),
        pltpu.VMEM((STATE_HALF // LANES, 2 * rows, LANES), F32),
        pltpu.VMEM((L, LANES, LANES), BF16),
    ]
    return pl.pallas_call(
        functools.partial(_ssm_kernel, L=L, C=C, bt=bt),
        grid=(N_LANE_BLOCKS, nb),
        in_specs=in_specs,
        out_specs=out_specs,
        out_shape=out_shape,
        scratch_shapes=scratch,
        compiler_params=_params(2),
        name="ssm",
    )(u4, wbc, wcc, a_pow, d_skip, h0)


def _ssm_weights(lam_re, lam_im, log_dt, b_re, b_im, c_re, c_im, L):
    dt = jnp.exp(log_dt)[:, None]
    zr = lam_re * dt
    zi = lam_im * dt
    n_pow = jnp.arange(L + 1, dtype=F32)[:, None, None]
    mag = jnp.exp(zr[None] * n_pow)
    pr = mag * jnp.cos(zi[None] * n_pow)
    pi = mag * jnp.sin(zi[None] * n_pow)
    a1r, a1i = pr[1], pi[1]
    den = lam_re * lam_re + lam_im * lam_im
    qr = ((a1r - 1.0) * lam_re + a1i * lam_im) / den
    qi = (a1i * lam_re - (a1r - 1.0) * lam_im) / den
    bbr = qr[:, :, None] * b_re - qi[:, :, None] * b_im
    bbi = qr[:, :, None] * b_im + qi[:, :, None] * b_re

    nj, g8 = N_LANE_BLOCKS, GROUPS_PER_BLOCK

    n_rev = (L - 1.0) - jnp.arange(L, dtype=F32)[:, None, None]
    mag_rev = jnp.exp(zr[None] * n_rev)
    rev = mag_rev * jnp.cos(zi[None] * n_rev), mag_rev * jnp.sin(zi[None] * n_rev)
    er = rev[0][:, :, :, None] * bbr[None] - rev[1][:, :, :, None] * bbi[None]
    ei = rev[0][:, :, :, None] * bbi[None] + rev[1][:, :, :, None] * bbr[None]
    e = jnp.stack([er, ei], 1).reshape(L, 2, nj, g8, N_STATE, P_GROUP)
    wbc = jnp.transpose(e, (2, 0, 1, 3, 5, 4)).reshape(nj, L, 2, LANES, N_STATE)

    mr = c_re[None] * pr[:, :, None, :] - c_im[None] * pi[:, :, None, :]
    mi = c_re[None] * pi[:, :, None, :] + c_im[None] * pr[:, :, None, :]
    m = jnp.stack([mr, -mi], 1).reshape(L + 1, 2, nj, g8, P_GROUP, N_STATE)
    wcc = jnp.transpose(m, (2, 0, 1, 3, 4, 5)).reshape(nj, L + 1, 2, LANES, N_STATE)

    def a_pow(n):
        return jnp.stack([pr[n].reshape(nj, STATE_HALF), pi[n].reshape(nj, STATE_HALF)], 1)

    return wbc.astype(BF16), wcc.astype(BF16), a_pow


def _ssm_weights_prefix(full, L_full, L):
    wbc, wcc, a_pow = full
    return wbc[:, L_full - L:], wcc[:, :L + 1], a_pow(L)


def _pack_state(re, im):
    b = re.shape[0]
    r = re.reshape(b, N_LANE_BLOCKS, STATE_HALF)
    i = im.reshape(b, N_LANE_BLOCKS, STATE_HALF)
    return jnp.transpose(jnp.concatenate([r, i], -1), (1, 0, 2))


def _unpack_state(h):
    b = h.shape[1]
    h = jnp.transpose(h, (1, 0, 2))
    re = h[:, :, :STATE_HALF].reshape(b, N_GROUPS, N_STATE)
    im = h[:, :, STATE_HALF:].reshape(b, N_GROUPS, N_STATE)
    return re, im


def _mixout_kernel(c_ref, y_ref, x_ref, wglu_ref, wout_ref, g_ref, o_ref, wglu_scr, wout_scr, *, L, ct, sub):
    _cast_weights_once(2, (wglu_ref, wglu_scr), (wout_ref, wout_scr))
    for c0, n in _sub_tiles(ct, sub):
        rows = L * n
        gy = jax.nn.gelu(y_ref[:, c0:c0 + n, :].reshape(rows, D_SSM))
        sg = gy * jax.nn.sigmoid(jnp.dot(gy.astype(BF16), wglu_scr[...], preferred_element_type=F32))
        m = jnp.dot(c_ref[:, c0:c0 + n, :].reshape(rows, D_CONV), wout_scr[0:D_CONV, :],
                    preferred_element_type=F32)
        m = m + jnp.dot(sg.astype(BF16), wout_scr[D_CONV:, :], preferred_element_type=F32)
        x = _gather_rows(x_ref, L, c0, n)
        o_ref[:, c0:c0 + n, :] = (x + _rms(m, g_ref[...])).reshape(L, n, D_MODEL)


def _mixout(c, y, x4, w_glu, w_out, g1, *, L, ct, sub):
    bq, cc = x4.shape[:2]
    nc = cc // ct
    R = bq * cc
    return pl.pallas_call(
        functools.partial(_mixout_kernel, L=L, ct=ct, sub=sub),
        grid=(bq, nc),
        in_specs=[
            _tp_spec(ct, L, D_CONV, nc),
            _tp_spec(ct, L, D_SSM, nc),
            _nat_spec(ct, L),
            _const((D_SSM, D_SSM)),
            _const((D_CONV + D_SSM, D_MODEL)),
            _const((1, D_MODEL)),
        ],
        out_specs=_tp_spec(ct, L, D_MODEL, nc),
        out_shape=jax.ShapeDtypeStruct((L, R, D_MODEL), F32),
        scratch_shapes=[pltpu.VMEM(w_glu.shape, BF16), pltpu.VMEM(w_out.shape, BF16)],
        compiler_params=_params(2),
        name="mixout",
    )(c, y, x4, w_glu, w_out, g1)


def _memkv_kernel(m_ref, g_ref, wk_ref, wv_ref, k5_ref, v5_ref, kb_ref, vb_ref, wk_scr, wv_scr, *, nb):
    _cast_weights_once(1, (wk_ref, wk_scr), (wv_ref, wv_scr))
    for b in range(nb):
        m = _rms(m_ref[b], g_ref[...]).astype(BF16)
        for w_ref, o5_ref, ob_ref in ((wk_scr, k5_ref, kb_ref), (wv_scr, v5_ref, vb_ref)):
            p = jnp.dot(m, w_ref[...], preferred_element_type=F32)
            ob_ref[b] = p.astype(BF16)
            for hd in range(N_HEADS):
                o5_ref[b, :, hd, :] = p[:, hd * HEAD_DIM:(hd + 1) * HEAD_DIM]


def _memkv(mem, g_mem, w_k, w_v):
    bq = mem.shape[0]
    nb = 2
    out5 = pl.BlockSpec((nb, N_MEM, N_HEADS, HEAD_DIM), lambda i: (i, 0, 0, 0))
    outb = pl.BlockSpec((nb, N_MEM, D_MODEL), lambda i: (i, 0, 0))
    return pl.pallas_call(
        functools.partial(_memkv_kernel, nb=nb),
        grid=(bq // nb,),
        in_specs=[
            pl.BlockSpec((nb, N_MEM, D_MODEL), lambda i: (i, 0, 0)),
            _const((1, D_MODEL)),
            _const((D_MODEL, D_MODEL)),
            _const((D_MODEL, D_MODEL)),
        ],
        out_specs=[out5, out5, outb, outb],
        out_shape=[jax.ShapeDtypeStruct((bq, N_MEM, N_HEADS, HEAD_DIM), F32)] * 2
        + [jax.ShapeDtypeStruct((bq, N_MEM, D_MODEL), BF16)] * 2,
        scratch_shapes=[pltpu.VMEM(w_k.shape, BF16), pltpu.VMEM(w_v.shape, BF16)],
        compiler_params=_params(1),
        name="memkv",
    )(mem, g_mem, w_k, w_v)


def _softmax_rows(s):
    s = s - jnp.max(s, axis=-1, keepdims=True)
    e = jnp.exp(s)
    return e / jnp.sum(e, axis=-1, keepdims=True)


def _attn_prompt_tile(x, kb, vb, wq_ref, wo_ref, gq_ref, go_ref):
    h = _rms(x, gq_ref[...])
    q = jnp.dot(h.astype(BF16), wq_ref[...], preferred_element_type=F32).astype(BF16)
    heads = []
    for hd in range(N_HEADS):
        sl = slice(hd * HEAD_DIM, (hd + 1) * HEAD_DIM)
        sc = lax.dot_general(q[:, sl], kb[:, sl], (((1,), (1,)), ((), ())), preferred_element_type=F32)
        p = _softmax_rows(sc * (HEAD_DIM ** -0.5))
        heads.append(jnp.dot(p.astype(BF16), vb[:, sl], preferred_element_type=F32))
    o = jnp.concatenate(heads, axis=-1)
    a = jnp.dot(o.astype(BF16), wo_ref[...], preferred_element_type=F32)
    return x + _rms(a, go_ref[...])


def _attn_sample_block(q2, k_ref, v_ref, *, bb):
    rows = q2.shape[0]
    qs = jnp.concatenate([q2[:, h * HEAD_DIM:(h + 1) * HEAD_DIM] for h in range(N_HEADS)], axis=0).astype(BF16)
    nr = N_HEADS * rows
    nc = N_MEM * N_HEADS
    row_head = lax.broadcasted_iota(jnp.int32, (nr, nc), 0) // rows
    col_head = lax.broadcasted_iota(jnp.int32, (nr, nc), 1) % N_HEADS
    same_head = row_head == col_head
    owner = lax.broadcasted_iota(jnp.int32, (nr, HEAD_DIM), 0) % bb
    acc = jnp.zeros((nr, HEAD_DIM), F32)
    for b in range(bb):
        ka = k_ref[b].reshape(nc, HEAD_DIM).astype(BF16)
        va = v_ref[b].reshape(nc, HEAD_DIM).astype(BF16)
        sc = lax.dot_general(qs, ka, (((1,), (1,)), ((), ())), preferred_element_type=F32)
        p = _softmax_rows(jnp.where(same_head, sc * (HEAD_DIM ** -0.5), -1e30))
        o = jnp.dot(p.astype(BF16), va, preferred_element_type=F32)
        acc = jnp.where(owner == b, o, acc)
    return jnp.concatenate([acc[h * rows:(h + 1) * rows] for h in range(N_HEADS)], axis=-1)


def _attn_kernel(x_ref, k_ref, v_ref, wq32_ref, wo32_ref, gq_ref, go_ref, qs_ref, ck_ref, cv_ref, o_ref, os_ref,
                 wq_ref, wo_ref, *, L, ct, bb):
    _cast_weights_once(1, (wq32_ref, wq_ref), (wo32_ref, wo_ref))
    x = x_ref[...].reshape(L * ct, D_MODEL)
    y = _attn_prompt_tile(x, k_ref[...], v_ref[...], wq_ref, wo_ref, gq_ref, go_ref)
    o_ref[...] = y.reshape(L, ct, D_MODEL)
    os_ref[...] = _attn_sample_block(qs_ref[...], ck_ref, cv_ref, bb=bb)


def _attn(x1, k, v, w_q, w_o, g2, g3, q_grp, cache_k, cache_v, *, C, ct):
    L, R, _ = x1.shape
    nc = C // ct
    n, rows_s, _ = q_grp.shape
    bb = cache_k.shape[1] // n
    assert n == R // ct and n * bb == cache_k.shape[1]
    kv_spec = pl.BlockSpec((None, bb, N_MEM, N_HEADS, HEAD_DIM), lambda i: (0, i, 0, 0, 0))
    grp_spec = pl.BlockSpec((None, rows_s, D_MODEL), lambda i: (i, 0, 0))
    return pl.pallas_call(
        functools.partial(_attn_kernel, L=L, ct=ct, bb=bb),
        grid=(n,),
        in_specs=[
            pl.BlockSpec((L, ct, D_MODEL), lambda i: (0, i, 0)),
            pl.BlockSpec((None, N_MEM, D_MODEL), lambda i: (i // nc, 0, 0)),
            pl.BlockSpec((None, N_MEM, D_MODEL), lambda i: (i // nc, 0, 0)),
            _const((D_MODEL, D_MODEL)),
            _const((D_MODEL, D_MODEL)),
            _const((1, D_MODEL)),
            _const((1, D_MODEL)),
            grp_spec, kv_spec, kv_spec,
        ],
        out_specs=[pl.BlockSpec((L, ct, D_MODEL), lambda i: (0, i, 0)), grp_spec],
        out_shape=[jax.ShapeDtypeStruct((L, R, D_MODEL), F32), jax.ShapeDtypeStruct(q_grp.shape, F32)],
        scratch_shapes=[pltpu.VMEM(w_q.shape, BF16), pltpu.VMEM(w_o.shape, BF16)],
        compiler_params=_params(1),
        name="attn",
    )(x1, k, v, w_q, w_o, g2, g3, q_grp, cache_k, cache_v)


def _qproj_kernel(x_ref, wq_ref, g_ref, q_ref):
    h = _rms(x_ref[...], g_ref[...])
    q_ref[...] = jnp.dot(h.astype(BF16), wq_ref[...].astype(BF16), preferred_element_type=F32)


def _qproj(x1f, w_q, g2):
    rows = x1f.shape[0]
    return pl.pallas_call(
        _qproj_kernel,
        grid=(1,),
        in_specs=[_const((rows, D_MODEL)), _const((D_MODEL, D_MODEL)), _const((1, D_MODEL))],
        out_specs=pl.BlockSpec((rows, D_MODEL), lambda i: (0, 0)),
        out_shape=jax.ShapeDtypeStruct((rows, D_MODEL), F32),
        compiler_params=_params(1),
        name="qproj_sample",
    )(x1f, w_q, g2)


def _oproj_kernel(o_ref, x_ref, wo_ref, g_ref, y_ref):
    a = jnp.dot(o_ref[...].astype(BF16), wo_ref[...].astype(BF16), preferred_element_type=F32)
    y_ref[...] = x_ref[...] + _rms(a, g_ref[...])


def _oproj(of, x1f, w_o, g3):
    rows = of.shape[0]
    return pl.pallas_call(
        _oproj_kernel,
        grid=(1,),
        in_specs=[_const((rows, D_MODEL)), _const((rows, D_MODEL)), _const((D_MODEL, D_MODEL)),
                  _const((1, D_MODEL))],
        out_specs=pl.BlockSpec((rows, D_MODEL), lambda i: (0, 0)),
        out_shape=jax.ShapeDtypeStruct((rows, D_MODEL), F32),
        compiler_params=_params(1),
        name="oproj_sample",
    )(of, x1f, w_o, g3)


FFN_WEIGHT_CHUNKS = 16
FFN_STAGE_SLOTS = 4


def _stream_cast(w_hbm, w16_scr, stage, sem):
    slots, rpc = stage.shape[0], stage.shape[1]
    n = w_hbm.shape[0] // rpc

    def chunk(k):
        return pltpu.make_async_copy(w_hbm.at[pl.ds(k * rpc, rpc)], stage.at[k % slots], sem.at[k % slots])

    for k in range(min(slots - 1, n)):
        chunk(k).start()
    for k in range(n):
        if k + slots - 1 < n:
            chunk(k + slots - 1).start()
        chunk(k).wait()
        w16_scr[pl.ds(k * rpc, rpc), :] = stage[k % slots].astype(BF16)


def _ffn_tile(x_ref, o_ref, wg_scr, wu_scr, wd_scr, gi_ref, go_ref, *, L, ct):
    rows = L * ct
    x = x_ref[...].reshape(rows, D_MODEL)
    h = _rms(x, gi_ref[...]).astype(BF16)
    gate = jnp.dot(h, wg_scr[...], preferred_element_type=F32)
    up = jnp.dot(h, wu_scr[...], preferred_element_type=F32)
    act = (gate * jax.nn.sigmoid(gate) * up).astype(BF16)
    dn = jnp.dot(act, wd_scr[...], preferred_element_type=F32)
    y = x + _rms(dn, go_ref[...])
    for s in range(L):
        o_ref[:, s, :] = y[s * ct:(s + 1) * ct]


def _ffn_kernel(xp_ref, xs_ref, wg_hbm, wu_hbm, wd_hbm, gi_ref, go_ref, op_ref, os_ref,
                wg_scr, wu_scr, wd_scr, stage_in, stage_out, sem, *, n_prompt, Lp, ctp, Ls, cts):
    i = pl.program_id(0)

    @pl.when(i == 0)
    def _():
        _stream_cast(wg_hbm, wg_scr, stage_in, sem)
        _stream_cast(wu_hbm, wu_scr, stage_in, sem)
        _stream_cast(wd_hbm, wd_scr, stage_out, sem)

    @pl.when(i < n_prompt)
    def _():
        _ffn_tile(xp_ref, op_ref, wg_scr, wu_scr, wd_scr, gi_ref, go_ref, L=Lp, ct=ctp)

    @pl.when(i == n_prompt)
    def _():
        _ffn_tile(xs_ref, os_ref, wg_scr, wu_scr, wd_scr, gi_ref, go_ref, L=Ls, ct=cts)


def _ffn(xp, xs, w_gate, w_up, w_down, g4, g5, *, bq, C, ct):
    Lp = xp.shape[0]
    Ls, rs, _ = xs.shape
    nc = C // ct
    n = bq * nc
    tile = lambda i: jnp.minimum(i, n - 1)
    hbm = pl.BlockSpec(memory_space=pl.ANY)
    return pl.pallas_call(
        functools.partial(_ffn_kernel, n_prompt=n, Lp=Lp, ctp=ct, Ls=Ls, cts=rs),
        grid=(n + 1,),
        in_specs=[
            pl.BlockSpec((Lp, ct, D_MODEL), lambda i: (0, tile(i), 0)),
            _const((Ls, rs, D_MODEL)),
            hbm, hbm, hbm,
            _const((1, D_MODEL)),
            _const((1, D_MODEL)),
        ],
        out_specs=[
            pl.BlockSpec((None, ct, Lp, D_MODEL), lambda i: (tile(i) // nc, tile(i) % nc, 0, 0)),
            pl.BlockSpec((None, rs, Ls, D_MODEL), lambda i: (0, 0, 0, 0)),
        ],
        out_shape=[
            jax.ShapeDtypeStruct((bq, C, Lp, D_MODEL), F32),
            jax.ShapeDtypeStruct((1, rs, Ls, D_MODEL), F32),
        ],
        scratch_shapes=[
            pltpu.VMEM((D_MODEL, D_FF), BF16),
            pltpu.VMEM((D_MODEL, D_FF), BF16),
            pltpu.VMEM((D_FF, D_MODEL), BF16),
            pltpu.VMEM((FFN_STAGE_SLOTS, D_MODEL // FFN_WEIGHT_CHUNKS, D_FF), F32),
            pltpu.VMEM((FFN_STAGE_SLOTS, D_FF // FFN_WEIGHT_CHUNKS, D_MODEL), F32),
            pltpu.SemaphoreType.DMA((FFN_STAGE_SLOTS,)),
        ],
        compiler_params=_params(1),
        name="ffn",
    )(xp, xs, w_gate, w_up, w_down, g4, g5)


def _mixer(x4, conv_state_t, h0_packed, wts, *, in_ct, ct, sub, sc, bt, sample):
    bq, C, L, _ = x4.shape
    R = bq * C
    g = [wts["norm_g"][i:i + 1] for i in range(2)]

    v, u4 = _inproj(x4, g[0], wts["w_in"], L=L, ct=in_ct, sub=sub)
    sw = wts["ssm_L%d" % L]
    nb = R // (bt * sc)
    ssm_args = (u4, sw[0], sw[1], sw[2], wts["d_skip"], h0_packed.reshape(N_LANE_BLOCKS, nb, bt, STATE_W))
    conv_args = (wts["w_dw"], wts["b_dw"], wts["ln_g"], wts["ln_b"])
    y, hl = _ssm(*ssm_args, L=L, C=sc, bt=bt)
    if sample:
        cact, conv_new = _conv_sample(v, conv_state_t, *conv_args)
    else:
        cact = _conv_prompt(v, *conv_args, L=L, C=C)
        tail = v.reshape(L, bq, C, D_CONV)[:, :, C - 2:, :]
        tail = jnp.transpose(tail, (1, 2, 0, 3)).reshape(bq, 2 * L, D_CONV)
        conv_new = tail[:, 2 * L - (CONV_WIDTH - 1):, :]
    hl = hl.reshape(N_LANE_BLOCKS, nb * bt, STATE_W)

    x1 = _mixout(cact, y, x4, wts["w_glu"], wts["w_out"], g[1], L=L, ct=ct, sub=sub)
    return x1, conv_new, hl


PROMPT_L = 16
PROMPT_SUB = 32
PROMPT_CT = 64
PROMPT_IN_CT = 128


def kernel(x_prompt, x_sample, mem_prompt, cache_mem_k, cache_mem_v, state_conv, state_ssm_re, state_ssm_im,
           norm_g, mem_norm_g, w_in, w_dw, b_dw, ln_g, ln_b, lam_re, lam_im, log_dt, b_re, b_im, c_re, c_im,
           d_skip, w_glu, w_out, w_q, w_k, w_v, w_o, w_gate, w_up, w_down):
    depth = w_in.shape[0]
    bp, tp, _ = x_prompt.shape
    bs, ts, _ = x_sample.shape
    assert tp % (PROMPT_L * PROMPT_IN_CT) == 0 and PROMPT_IN_CT % PROMPT_CT == 0 and tp >= CONV_WIDTH - 1

    yp = x_prompt.reshape(bp, tp // PROMPT_L, PROMPT_L, D_MODEL)
    ys = x_sample.reshape(1, bs, ts, D_MODEL)
    outs = [[] for _ in range(8)]
    for l in range(depth):
        wts = {
            "norm_g": norm_g[l],
            "w_in": w_in[l],
            "w_dw": w_dw[l],
            "b_dw": b_dw[l][None],
            "ln_g": ln_g[l][None],
            "ln_b": ln_b[l][None],
            "d_skip": d_skip[l].reshape(N_LANE_BLOCKS, 1, LANES),
            "w_glu": w_glu[l],
            "w_out": w_out[l],
        }
        ssm_args = (lam_re[l], lam_im[l], log_dt[l], b_re[l], b_im[l], c_re[l], c_im[l])
        assert ts <= PROMPT_L
        full = _ssm_weights(*ssm_args, PROMPT_L)
        wts["ssm_L%d" % PROMPT_L] = full[:2] + (full[2](PROMPT_L),)
        wts["ssm_L%d" % ts] = _ssm_weights_prefix(full, PROMPT_L, ts)

        kp, vp, kp16, vp16 = _memkv(mem_prompt, mem_norm_g[l][None], w_k[l], w_v[l])
        h0p = jnp.zeros((N_LANE_BLOCKS, bp, STATE_W), F32)
        cp_chunks = tp // PROMPT_L
        xp, cp, hp = _mixer(yp, None, h0p, wts, in_ct=PROMPT_IN_CT, ct=PROMPT_CT, sub=PROMPT_SUB, sc=cp_chunks,
                            bt=4, sample=False)
        h0s = _pack_state(state_ssm_re[l], state_ssm_im[l])
        xs, cs, hs = _mixer(ys, jnp.transpose(state_conv[l], (1, 0, 2)), h0s, wts, in_ct=bs, ct=bs, sub=bs, sc=1,
                            bt=bs, sample=True)
        cs = jnp.transpose(cs, (1, 0, 2))

        g2, g3 = norm_g[l][2:3], norm_g[l][3:4]
        n_tiles = bp * cp_chunks // PROMPT_SUB
        bb = bs // n_tiles
        xs_flat = xs.reshape(ts * bs, D_MODEL)
        q = _qproj(xs_flat, w_q[l], g2).reshape(ts, n_tiles, bb, D_MODEL)
        q = jnp.transpose(q, (1, 0, 2, 3)).reshape(n_tiles, ts * bb, D_MODEL)
        xp, o = _attn(xp, kp16, vp16, w_q[l], w_o[l], g2, g3, q, cache_mem_k[l:l + 1], cache_mem_v[l:l + 1],
                      C=cp_chunks, ct=PROMPT_SUB)
        o = jnp.transpose(o.reshape(n_tiles, ts, bb, D_MODEL), (1, 0, 2, 3)).reshape(ts * bs, D_MODEL)
        xs = _oproj(o, xs_flat, w_o[l], g3).reshape(ts, bs, D_MODEL)

        yp, ys = _ffn(xp, xs, w_gate[l], w_up[l], w_down[l], norm_g[l][4:5], norm_g[l][5:6],
                      bq=bp, C=cp_chunks, ct=PROMPT_SUB)

        hp_re, hp_im = _unpack_state(hp)
        hs_re, hs_im = _unpack_state(hs)
        for lst, val in zip(outs, (kp, vp, cp, hp_re, hp_im, cs, hs_re, hs_im)):
            lst.append(val)
    return (yp.reshape(bp, tp, D_MODEL), ys.reshape(bs, ts, D_MODEL)) + tuple(jnp.stack(o) for o in outs)
```

```python
import functools

import jax
import jax.numpy as jnp
from jax import lax
from jax.experimental import pallas as pl
from jax.experimental.pallas import tpu as pltpu

F32 = jnp.float32
BF16 = jnp.bfloat16

D_MODEL = 1024
D_CONV = 512
D_SSM = 512
CONV_WIDTH = 31
N_GROUPS = 32
P_GROUP = 16
N_STATE = 64
N_MEM = 256
N_HEADS = 4
HEAD_DIM = 256
D_FF = 2816
RMS_EPS = 1e-6
LN_EPS = 1e-5

LANES = 128
SUBLANES = 8
GROUPS_PER_BLOCK = LANES // P_GROUP
N_LANE_BLOCKS = D_SSM // LANES
STATE_HALF = GROUPS_PER_BLOCK * N_STATE
STATE_W = 2 * STATE_HALF
MXU_N = 256
VMEM_LIMIT = 56 * 1024 * 1024


def _params(n_axes, vmem=VMEM_LIMIT):
    return pltpu.CompilerParams(dimension_semantics=("arbitrary",) * n_axes, vmem_limit_bytes=vmem)


def _const(shape):
    nd = len(shape)
    return pl.BlockSpec(shape, lambda *_: (0,) * nd, pipeline_mode=pl.Buffered(1))


def _rms(x, g):
    return x * lax.rsqrt(jnp.mean(x * x, axis=-1, keepdims=True) + RMS_EPS) * g


def _cast_weights_once(n_axes, *pairs):
    first = pl.program_id(0) == 0
    for a in range(1, n_axes):
        first = jnp.logical_and(first, pl.program_id(a) == 0)

    @pl.when(first)
    def _():
        for src, dst in pairs:
            dst[...] = src[...].astype(BF16)


def _gather_rows(x_ref, L, c0, n):
    return jnp.concatenate([x_ref[c0:c0 + n, s, :] for s in range(L)], axis=0)


def _sub_tiles(ct, sub):
    return [(c0, sub) for c0 in range(0, ct, sub)]


def _nat_spec(ct, L):
    return pl.BlockSpec((None, ct, L, D_MODEL), lambda b, i: (b, i, 0, 0))


def _tp_spec(ct, L, width, nc):
    return pl.BlockSpec((L, ct, width), lambda b, i: (0, b * nc + i, 0))


def _inproj_kernel(x_ref, g_ref, w_ref, v_ref, u_ref, w16_scr, *, L, ct, sub):
    _cast_weights_once(2, (w_ref, w16_scr))
    for c0, n in _sub_tiles(ct, sub):
        x = _gather_rows(x_ref, L, c0, n)
        h = _rms(x, g_ref[...])
        z = jnp.dot(h.astype(BF16), w16_scr[...], preferred_element_type=F32)
        a = z[:, :D_CONV]
        g = z[:, D_CONV:2 * D_CONV]
        u = z[:, 2 * D_CONV:].astype(BF16)
        v_ref[:, c0:c0 + n, :] = (a * jax.nn.sigmoid(g)).reshape(L, n, D_CONV)
        for s in range(L):
            for j in range(N_LANE_BLOCKS):
                u_ref[j, c0:c0 + n, s * LANES:(s + 1) * LANES] = u[s * n:(s + 1) * n, j * LANES:(j + 1) * LANES]


def _inproj(x4, g0, w_in, *, L, ct, sub):
    bq, c = x4.shape[:2]
    nc = c // ct
    R = bq * c
    return pl.pallas_call(
        functools.partial(_inproj_kernel, L=L, ct=ct, sub=sub),
        grid=(bq, nc),
        in_specs=[_nat_spec(ct, L), _const((1, D_MODEL)), _const((D_MODEL, 2 * D_CONV + D_SSM))],
        out_specs=[
            _tp_spec(ct, L, D_CONV, nc),
            pl.BlockSpec((N_LANE_BLOCKS, ct, L * LANES), lambda b, i: (0, b * nc + i, 0)),
        ],
        out_shape=[
            jax.ShapeDtypeStruct((L, R, D_CONV), F32),
            jax.ShapeDtypeStruct((N_LANE_BLOCKS, R, L * LANES), BF16),
        ],
        scratch_shapes=[pltpu.VMEM(w_in.shape, BF16)],
        compiler_params=_params(2),
        name="inproj",
    )(x4, g0, w_in)


def _ln_silu(acc, g, b):
    mu = jnp.mean(acc, axis=-1, keepdims=True)
    xc = acc - mu
    var = jnp.mean(xc * xc, axis=-1, keepdims=True)
    y = xc * lax.rsqrt(var + LN_EPS) * g + b
    return y * jax.nn.sigmoid(y)


def _conv_shifted_copies(v_ref, v1_ref, v2_ref, *, L, C):
    row = lax.broadcasted_iota(jnp.int32, (C, D_CONV), 0)
    for s in range(L):
        x = v_ref[s]
        v1_ref[s] = jnp.where(row >= 1, pltpu.roll(x, 1, 0), 0.0)
        v2_ref[s] = jnp.where(row >= 2, pltpu.roll(x, 2, 0), 0.0)


def _conv_rows(srcs, w_ref, b_ref, g_ref, bb_ref, o_ref, r0, *, L, rc):
    sub8 = rc // SUBLANES
    for s in range(L):
        acc = jnp.broadcast_to(b_ref[...][None], (sub8, SUBLANES, D_CONV))
        for d in range(CONV_WIDTH):
            blk = (s - d) % L
            shift = (d - s + L - 1) // L if d > s else 0
            k = CONV_WIDTH - 1 - d
            src = srcs[shift][blk, pl.ds(r0, rc), :].reshape(sub8, SUBLANES, D_CONV)
            acc = acc + w_ref[k][None] * src
        y = _ln_silu(acc.reshape(rc, D_CONV), g_ref[...], bb_ref[...])
        o_ref[s, pl.ds(r0, rc), :] = y.astype(o_ref.dtype)


def _conv_prompt_kernel(v_ref, w_ref, b_ref, g_ref, bb_ref, o_ref, v1_ref, v2_ref, *, L, C, rc):
    _conv_shifted_copies(v_ref, v1_ref, v2_ref, L=L, C=C)
    srcs = (v_ref, v1_ref, v2_ref)

    def chunk(i, carry):
        _conv_rows(srcs, w_ref, b_ref, g_ref, bb_ref, o_ref, pl.multiple_of(i * rc, rc), L=L, rc=rc)
        return carry

    lax.fori_loop(0, C // rc, chunk, 0)


def _conv_prompt(v, w_dw, b_dw, ln_g, ln_b, *, L, C):
    R = v.shape[1]
    rc = 32
    return pl.pallas_call(
        functools.partial(_conv_prompt_kernel, L=L, C=C, rc=rc),
        grid=(R // C,),
        in_specs=[
            pl.BlockSpec((L, C, D_CONV), lambda b: (0, b, 0)),
            _const((CONV_WIDTH, SUBLANES, D_CONV)),
            _const((SUBLANES, D_CONV)),
            _const((1, D_CONV)),
            _const((1, D_CONV)),
        ],
        out_specs=pl.BlockSpec((L, C, D_CONV), lambda b: (0, b, 0)),
        out_shape=jax.ShapeDtypeStruct((L, R, D_CONV), BF16),
        scratch_shapes=[pltpu.VMEM((L, C, D_CONV), F32)] * 2,
        compiler_params=_params(1),
        name="conv_prompt",
    )(v, jnp.broadcast_to(w_dw[:, None, :], (CONV_WIDTH, SUBLANES, D_CONV)),
      jnp.broadcast_to(b_dw, (SUBLANES, D_CONV)), ln_g, ln_b)


def _conv_sample_kernel(v_ref, st_ref, w_ref, b_ref, g_ref, bb_ref, o_ref, new_ref, *, L, H):
    def ext(i):
        return st_ref[i] if i < H else v_ref[i - H]

    for t in range(L):
        acc = jnp.broadcast_to(b_ref[...], v_ref.shape[1:])
        for k in range(CONV_WIDTH):
            acc = acc + w_ref[k:k + 1, :] * ext(t + k)
        o_ref[t] = _ln_silu(acc, g_ref[...], bb_ref[...]).astype(o_ref.dtype)
    for i in range(H):
        new_ref[i] = ext(i + L)


def _conv_sample(v, state_t, w_dw, b_dw, ln_g, ln_b):
    L, R, _ = v.shape
    H = CONV_WIDTH - 1
    rt = 32
    rows = lambda n: pl.BlockSpec((n, rt, D_CONV), lambda i: (0, i, 0))
    return pl.pallas_call(
        functools.partial(_conv_sample_kernel, L=L, H=H),
        grid=(R // rt,),
        in_specs=[
            rows(L),
            rows(H),
            _const((CONV_WIDTH, D_CONV)),
            _const((1, D_CONV)),
            _const((1, D_CONV)),
            _const((1, D_CONV)),
        ],
        out_specs=[rows(L), rows(H)],
        out_shape=[
            jax.ShapeDtypeStruct((L, R, D_CONV), BF16),
            jax.ShapeDtypeStruct((H, R, D_CONV), F32),
        ],
        compiler_params=_params(1),
        name="conv_sample",
    )(v, state_t, w_dw, b_dw, ln_g, ln_b)


def _expand_block_diag(d):
    tiled = jnp.concatenate([d] * GROUPS_PER_BLOCK, axis=-1)
    r = lax.broadcasted_iota(jnp.int32, tiled.shape, 0) // P_GROUP
    c = lax.broadcasted_iota(jnp.int32, tiled.shape, 1) // N_STATE
    return jnp.where(r == c, tiled, jnp.zeros_like(tiled))


def _ssm_kernel(u_ref, wbc_ref, wcc_ref, a_ref, d_ref, h0_ref, y_ref, hl_ref,
                w_scr, wb_scr, wct_scr, s_scr, hp_scr, tap_scr, *, L, C, bt):
    per_tile = MXU_N // LANES

    @pl.when(pl.program_id(1) == 0)
    def _():
        for s in range(L):
            rows = slice(s * LANES, (s + 1) * LANES)
            for a in range(2):
                cols = slice(a * STATE_HALF, (a + 1) * STATE_HALF)
                wb_scr[rows, cols] = _expand_block_diag(wbc_ref[s, a])
                wct_scr[rows, cols] = _expand_block_diag(wcc_ref[s + 1, a])
        bbar = wb_scr[(L - 1) * LANES:L * LANES, :]
        nt = (((1,), (1,)), ((), ()))
        c0 = jnp.concatenate([_expand_block_diag(wcc_ref[0, a]) for a in range(2)], axis=-1)
        tap_scr[0] = lax.dot_general(bbar, c0, nt, preferred_element_type=F32).astype(BF16)
        for t in range(1, L):
            tap_scr[t] = lax.dot_general(bbar, wct_scr[(t - 1) * LANES:t * LANES, :], nt,
                                         preferred_element_type=F32).astype(BF16)
        for sp in range(L):
            for s in range(L):
                rows = slice(s * LANES, (s + 1) * LANES)
                cols = slice(sp * LANES, (sp + 1) * LANES)
                if s <= sp:
                    w_scr[rows, cols] = tap_scr[sp - s]
                elif s // per_tile == sp // per_tile:
                    w_scr[rows, cols] = jnp.zeros((LANES, LANES), BF16)

    nk = STATE_W // LANES
    half = nk // 2
    rows = bt * C
    inc_all = jnp.dot(u_ref[...], wb_scr[...], preferred_element_type=F32)
    for k in range(half):
        s_scr[k, 0:rows, :] = inc_all[:, k * LANES:(k + 1) * LANES]
        s_scr[k, rows:2 * rows, :] = inc_all[:, (half + k) * LANES:(half + k + 1) * LANES]

    d = d_ref[...]
    for n in range(L // per_tile):
        k = (n + 1) * MXU_N
        y = jnp.dot(u_ref[:, :k], w_scr[:k, n * MXU_N:(n + 1) * MXU_N], preferred_element_type=F32)
        for h in range(per_tile):
            s = n * per_tile + h
            us = u_ref[:, s * LANES:(s + 1) * LANES].astype(F32)
            y_ref[s] = y[:, h * LANES:(h + 1) * LANES] + d * us

    h0 = h0_ref[...]
    a = a_ref[...]

    def piece(x, k):
        return x[:, k * LANES:(k + 1) * LANES]

    def swap_parts(x):
        if 2 * bt == SUBLANES:
            return pltpu.roll(x, bt, 0)
        return jnp.concatenate([x[bt:], x[:bt]], axis=0)

    im_rows = lax.broadcasted_iota(jnp.int32, (2 * bt, LANES), 0) >= bt
    a_same = [jnp.broadcast_to(piece(a[0:1], k), (2 * bt, LANES)) for k in range(half)]
    a_cross = [jnp.where(im_rows, piece(a[1:2], k), -piece(a[1:2], k)) for k in range(half)]
    state = [jnp.concatenate([piece(h0, k), piece(h0, half + k)], axis=0) for k in range(half)]
    for c in range(C):
        sel = pl.ds(c, 2 * bt, stride=C) if C > 1 else pl.ds(0, 2 * bt)
        for k in range(half):
            hp_scr[k, sel, :] = state[k]
            state[k] = a_same[k] * state[k] + a_cross[k] * swap_parts(state[k]) + s_scr[k, sel, :]
    for k in range(half):
        hl_ref[:, k * LANES:(k + 1) * LANES] = state[k][:bt]
        hl_ref[:, (half + k) * LANES:(half + k + 1) * LANES] = state[k][bt:]

    hp = jnp.concatenate([hp_scr[k, 0:rows, :] for k in range(half)]
                         + [hp_scr[k, rows:2 * rows, :] for k in range(half)], axis=-1).astype(BF16)
    for n in range(L // per_tile):
        y = lax.dot_general(hp, wct_scr[n * MXU_N:(n + 1) * MXU_N, :], (((1,), (1,)), ((), ())),
                            preferred_element_type=F32)
        for h in range(per_tile):
            s = n * per_tile + h
            y_ref[s] += y[:, h * LANES:(h + 1) * LANES]


def _ssm(u4, wbc, wcc, a_pow, d_skip, h0, *, L, C, bt):
    R = u4.shape[1]
    rows = bt * C
    nb = R // rows
    lw = L * LANES
    in_specs = [
        pl.BlockSpec((None, rows, lw), lambda j, b: (j, b, 0)),
        pl.BlockSpec((None, L, 2, LANES, N_STATE), lambda j, b: (j, 0, 0, 0, 0)),
        pl.BlockSpec((None, L + 1, 2, LANES, N_STATE), lambda j, b: (j, 0, 0, 0, 0)),
        pl.BlockSpec((None, 2, STATE_HALF), lambda j, b: (j, 0, 0)),
        pl.BlockSpec((None, 1, LANES), lambda j, b: (j, 0, 0)),
        pl.BlockSpec((None, None, bt, STATE_W), lambda j, b: (j, b, 0, 0)),
    ]
    out_specs = [
        pl.BlockSpec((L, rows, LANES), lambda j, b: (0, b, j)),
        pl.BlockSpec((None, None, bt, STATE_W), lambda j, b: (j, b, 0, 0)),
    ]
    out_shape = [
        jax.ShapeDtypeStruct((L, R, D_SSM), F32),
        jax.ShapeDtypeStruct((N_LANE_BLOCKS, nb, bt, STATE_W), F32),
    ]
    scratch = [
        pltpu.VMEM((lw, lw), BF16),
        pltpu.VMEM((lw, STATE_W), BF16),
        pltpu.VMEM((lw, STATE_W), BF16),
        pltpu.VMEM((STATE_HALF // LANES, 2 * rows, LANES), F32),
        pltpu.VMEM((STATE_HALF // LANES, 2 * rows, LANES), F32),
        pltpu.VMEM((L, LANES, LANES), BF16),
    ]
    return pl.pallas_call(
        functools.partial(_ssm_kernel, L=L, C=C, bt=bt),
        grid=(N_LANE_BLOCKS, nb),
        in_specs=in_specs,
        out_specs=out_specs,
        out_shape=out_shape,
        scratch_shapes=scratch,
        compiler_params=_params(2),
        name="ssm",
    )(u4, wbc, wcc, a_pow, d_skip, h0)


def _ssm_weights(lam_re, lam_im, log_dt, b_re, b_im, c_re, c_im, L):
    dt = jnp.exp(log_dt)[:, None]
    zr = lam_re * dt
    zi = lam_im * dt
    n_pow = jnp.arange(L + 1, dtype=F32)[:, None, None]
    mag = jnp.exp(zr[None] * n_pow)
    pr = mag * jnp.cos(zi[None] * n_pow)
    pi = mag * jnp.sin(zi[None] * n_pow)
    a1r, a1i = pr[1], pi[1]
    den = lam_re * lam_re + lam_im * lam_im
    qr = ((a1r - 1.0) * lam_re + a1i * lam_im) / den
    qi = (a1i * lam_re - (a1r - 1.0) * lam_im) / den
    bbr = qr[:, :, None] * b_re - qi[:, :, None] * b_im
    bbi = qr[:, :, None] * b_im + qi[:, :, None] * b_re

    nj, g8 = N_LANE_BLOCKS, GROUPS_PER_BLOCK

    n_rev = (L - 1.0) - jnp.arange(L, dtype=F32)[:, None, None]
    mag_rev = jnp.exp(zr[None] * n_rev)
    rev = mag_rev * jnp.cos(zi[None] * n_rev), mag_rev * jnp.sin(zi[None] * n_rev)
    er = rev[0][:, :, :, None] * bbr[None] - rev[1][:, :, :, None] * bbi[None]
    ei = rev[0][:, :, :, None] * bbi[None] + rev[1][:, :, :, None] * bbr[None]
    e = jnp.stack([er, ei], 1).reshape(L, 2, nj, g8, N_STATE, P_GROUP)
    wbc = jnp.transpose(e, (2, 0, 1, 3, 5, 4)).reshape(nj, L, 2, LANES, N_STATE)

    mr = c_re[None] * pr[:, :, None, :] - c_im[None] * pi[:, :, None, :]
    mi = c_re[None] * pi[:, :, None, :] + c_im[None] * pr[:, :, None, :]
    m = jnp.stack([mr, -mi], 1).reshape(L + 1, 2, nj, g8, P_GROUP, N_STATE)
    wcc = jnp.transpose(m, (2, 0, 1, 3, 4, 5)).reshape(nj, L + 1, 2, LANES, N_STATE)

    def a_pow(n):
        return jnp.stack([pr[n].reshape(nj, STATE_HALF), pi[n].reshape(nj, STATE_HALF)], 1)

    return wbc.astype(BF16), wcc.astype(BF16), a_pow


def _ssm_weights_prefix(full, L_full, L):
    wbc, wcc, a_pow = full
    return wbc[:, L_full - L:], wcc[:, :L + 1], a_pow(L)


def _pack_state(re, im):
    b = re.shape[0]
    r = re.reshape(b, N_LANE_BLOCKS, STATE_HALF)
    i = im.reshape(b, N_LANE_BLOCKS, STATE_HALF)
    return jnp.transpose(jnp.concatenate([r, i], -1), (1, 0, 2))


def _unpack_state(h):
    b = h.shape[1]
    h = jnp.transpose(h, (1, 0, 2))
    re = h[:, :, :STATE_HALF].reshape(b, N_GROUPS, N_STATE)
    im = h[:, :, STATE_HALF:].reshape(b, N_GROUPS, N_STATE)
    return re, im


def _mixout_kernel(c_ref, y_ref, x_ref, wglu_ref, wout_ref, g_ref, o_ref, wglu_scr, wout_scr, *, L, ct, sub):
    _cast_weights_once(2, (wglu_ref, wglu_scr), (wout_ref, wout_scr))
    tiles = _sub_tiles(ct, sub)
    gy = [jax.nn.gelu(y_ref[:, c0:c0 + n, :].reshape(L * n, D_SSM)) for c0, n in tiles]
    gate = [jnp.dot(g.astype(BF16), wglu_scr[...], preferred_element_type=F32) for g in gy]
    m = [jnp.dot(c_ref[:, c0:c0 + n, :].reshape(L * n, D_CONV), wout_scr[0:D_CONV, :],
                 preferred_element_type=F32) for c0, n in tiles]
    sg = [g * jax.nn.sigmoid(t) for g, t in zip(gy, gate)]
    m = [a + jnp.dot(s.astype(BF16), wout_scr[D_CONV:, :], preferred_element_type=F32) for a, s in zip(m, sg)]
    for (c0, n), mm in zip(tiles, m):
        x = _gather_rows(x_ref, L, c0, n)
        o_ref[:, c0:c0 + n, :] = (x + _rms(mm, g_ref[...])).reshape(L, n, D_MODEL)


def _mixout(c, y, x4, w_glu, w_out, g1, *, L, ct, sub):
    bq, cc = x4.shape[:2]
    nc = cc // ct
    R = bq * cc
    return pl.pallas_call(
        functools.partial(_mixout_kernel, L=L, ct=ct, sub=sub),
        grid=(bq, nc),
        in_specs=[
            _tp_spec(ct, L, D_CONV, nc),
            _tp_spec(ct, L, D_SSM, nc),
            _nat_spec(ct, L),
            _const((D_SSM, D_SSM)),
            _const((D_CONV + D_SSM, D_MODEL)),
            _const((1, D_MODEL)),
        ],
        out_specs=_tp_spec(ct, L, D_MODEL, nc),
        out_shape=jax.ShapeDtypeStruct((L, R, D_MODEL), F32),
        scratch_shapes=[pltpu.VMEM(w_glu.shape, BF16), pltpu.VMEM(w_out.shape, BF16)],
        compiler_params=_params(2),
        name="mixout",
    )(c, y, x4, w_glu, w_out, g1)


def _memkv_kernel(m_ref, g_ref, wk_ref, wv_ref, k5_ref, v5_ref, kb_ref, vb_ref, wk_scr, wv_scr, *, nb):
    _cast_weights_once(1, (wk_ref, wk_scr), (wv_ref, wv_scr))
    for b in range(nb):
        m = _rms(m_ref[b], g_ref[...]).astype(BF16)
        for w_ref, o5_ref, ob_ref in ((wk_scr, k5_ref, kb_ref), (wv_scr, v5_ref, vb_ref)):
            p = jnp.dot(m, w_ref[...], preferred_element_type=F32)
            ob_ref[b] = p.astype(BF16)
            for hd in range(N_HEADS):
                o5_ref[b, :, hd, :] = p[:, hd * HEAD_DIM:(hd + 1) * HEAD_DIM]


def _memkv(mem, g_mem, w_k, w_v):
    bq = mem.shape[0]
    nb = 2
    out5 = pl.BlockSpec((nb, N_MEM, N_HEADS, HEAD_DIM), lambda i: (i, 0, 0, 0))
    outb = pl.BlockSpec((nb, N_MEM, D_MODEL), lambda i: (i, 0, 0))
    return pl.pallas_call(
        functools.partial(_memkv_kernel, nb=nb),
        grid=(bq // nb,),
        in_specs=[
            pl.BlockSpec((nb, N_MEM, D_MODEL), lambda i: (i, 0, 0)),
            _const((1, D_MODEL)),
            _const((D_MODEL, D_MODEL)),
            _const((D_MODEL, D_MODEL)),
        ],
        out_specs=[out5, out5, outb, outb],
        out_shape=[jax.ShapeDtypeStruct((bq, N_MEM, N_HEADS, HEAD_DIM), F32)] * 2
        + [jax.ShapeDtypeStruct((bq, N_MEM, D_MODEL), BF16)] * 2,
        scratch_shapes=[pltpu.VMEM(w_k.shape, BF16), pltpu.VMEM(w_v.shape, BF16)],
        compiler_params=_params(1),
        name="memkv",
    )(mem, g_mem, w_k, w_v)


def _softmax_rows(s):
    s = s - jnp.max(s, axis=-1, keepdims=True)
    e = jnp.exp(s)
    return e / jnp.sum(e, axis=-1, keepdims=True)


def _attn_prompt_tile(x, kb, vb, wq_ref, wo_ref, gq_ref, go_ref):
    h = _rms(x, gq_ref[...])
    q = jnp.dot(h.astype(BF16), wq_ref[...], preferred_element_type=F32).astype(BF16)
    heads = [slice(hd * HEAD_DIM, (hd + 1) * HEAD_DIM) for hd in range(N_HEADS)]
    nt = (((1,), (1,)), ((), ()))
    sc = [lax.dot_general(q[:, sl], kb[:, sl], nt, preferred_element_type=F32) for sl in heads]
    p = [_softmax_rows(s * (HEAD_DIM ** -0.5)).astype(BF16) for s in sc]
    o = jnp.concatenate([jnp.dot(pp, vb[:, sl], preferred_element_type=F32) for pp, sl in zip(p, heads)], axis=-1)
    a = jnp.dot(o.astype(BF16), wo_ref[...], preferred_element_type=F32)
    return x + _rms(a, go_ref[...])


def _attn_sample_block(q2, k_ref, v_ref, *, bb):
    rows = q2.shape[0]
    qs = jnp.concatenate([q2[:, h * HEAD_DIM:(h + 1) * HEAD_DIM] for h in range(N_HEADS)], axis=0).astype(BF16)
    nr = N_HEADS * rows
    nc = N_MEM * N_HEADS
    row_head = lax.broadcasted_iota(jnp.int32, (nr, nc), 0) // rows
    col_head = lax.broadcasted_iota(jnp.int32, (nr, nc), 1) % N_HEADS
    same_head = row_head == col_head
    owner = lax.broadcasted_iota(jnp.int32, (nr, HEAD_DIM), 0) % bb
    acc = jnp.zeros((nr, HEAD_DIM), F32)
    for b in range(bb):
        ka = k_ref[b].reshape(nc, HEAD_DIM).astype(BF16)
        va = v_ref[b].reshape(nc, HEAD_DIM).astype(BF16)
        sc = lax.dot_general(qs, ka, (((1,), (1,)), ((), ())), preferred_element_type=F32)
        p = _softmax_rows(jnp.where(same_head, sc * (HEAD_DIM ** -0.5), -1e30))
        o = jnp.dot(p.astype(BF16), va, preferred_element_type=F32)
        acc = jnp.where(owner == b, o, acc)
    return jnp.concatenate([acc[h * rows:(h + 1) * rows] for h in range(N_HEADS)], axis=-1)


def _attn_kernel(x_ref, k_ref, v_ref, wq32_ref, wo32_ref, gq_ref, go_ref, qs_ref, ck_ref, cv_ref, o_ref, os_ref,
                 wq_ref, wo_ref, *, L, ct, bb):
    _cast_weights_once(1, (wq32_ref, wq_ref), (wo32_ref, wo_ref))
    x = x_ref[...].reshape(L * ct, D_MODEL)
    y = _attn_prompt_tile(x, k_ref[...], v_ref[...], wq_ref, wo_ref, gq_ref, go_ref)
    o_ref[...] = y.reshape(L, ct, D_MODEL)
    os_ref[...] = _attn_sample_block(qs_ref[...], ck_ref, cv_ref, bb=bb)


def _attn(x1, k, v, w_q, w_o, g2, g3, q_grp, cache_k, cache_v, *, C, ct):
    L, R, _ = x1.shape
    nc = C // ct
    n, rows_s, _ = q_grp.shape
    bb = cache_k.shape[1] // n
    assert n == R // ct and n * bb == cache_k.shape[1]
    kv_spec = pl.BlockSpec((None, bb, N_MEM, N_HEADS, HEAD_DIM), lambda i: (0, i, 0, 0, 0))
    grp_spec = pl.BlockSpec((None, rows_s, D_MODEL), lambda i: (i, 0, 0))
    return pl.pallas_call(
        functools.partial(_attn_kernel, L=L, ct=ct, bb=bb),
        grid=(n,),
        in_specs=[
            pl.BlockSpec((L, ct, D_MODEL), lambda i: (0, i, 0)),
            pl.BlockSpec((None, N_MEM, D_MODEL), lambda i: (i // nc, 0, 0)),
            pl.BlockSpec((None, N_MEM, D_MODEL), lambda i: (i // nc, 0, 0)),
            _const((D_MODEL, D_MODEL)),
            _const((D_MODEL, D_MODEL)),
            _const((1, D_MODEL)),
            _const((1, D_MODEL)),
            grp_spec, kv_spec, kv_spec,
        ],
        out_specs=[pl.BlockSpec((L, ct, D_MODEL), lambda i: (0, i, 0)), grp_spec],
        out_shape=[jax.ShapeDtypeStruct((L, R, D_MODEL), F32), jax.ShapeDtypeStruct(q_grp.shape, F32)],
        scratch_shapes=[pltpu.VMEM(w_q.shape, BF16), pltpu.VMEM(w_o.shape, BF16)],
        compiler_params=_params(1),
        name="attn",
    )(x1, k, v, w_q, w_o, g2, g3, q_grp, cache_k, cache_v)


def _qproj_kernel(x_ref, wq_ref, g_ref, q_ref):
    h = _rms(x_ref[...], g_ref[...])
    q_ref[...] = jnp.dot(h.astype(BF16), wq_ref[...].astype(BF16), preferred_element_type=F32)


def _qproj(x1f, w_q, g2):
    rows = x1f.shape[0]
    return pl.pallas_call(
        _qproj_kernel,
        grid=(1,),
        in_specs=[_const((rows, D_MODEL)), _const((D_MODEL, D_MODEL)), _const((1, D_MODEL))],
        out_specs=pl.BlockSpec((rows, D_MODEL), lambda i: (0, 0)),
        out_shape=jax.ShapeDtypeStruct((rows, D_MODEL), F32),
        compiler_params=_params(1),
        name="qproj_sample",
    )(x1f, w_q, g2)


def _oproj_kernel(o_ref, x_ref, wo_ref, g_ref, y_ref):
    a = jnp.dot(o_ref[...].astype(BF16), wo_ref[...].astype(BF16), preferred_element_type=F32)
    y_ref[...] = x_ref[...] + _rms(a, g_ref[...])


def _oproj(of, x1f, w_o, g3):
    rows = of.shape[0]
    return pl.pallas_call(
        _oproj_kernel,
        grid=(1,),
        in_specs=[_const((rows, D_MODEL)), _const((rows, D_MODEL)), _const((D_MODEL, D_MODEL)),
                  _const((1, D_MODEL))],
        out_specs=pl.BlockSpec((rows, D_MODEL), lambda i: (0, 0)),
        out_shape=jax.ShapeDtypeStruct((rows, D_MODEL), F32),
        compiler_params=_params(1),
        name="oproj_sample",
    )(of, x1f, w_o, g3)


FFN_WEIGHT_CHUNKS = 16
FFN_STAGE_SLOTS = 4
FFN_ROW_SPLIT = 2


def _stream_cast(w_hbm, w16_scr, stage, sem):
    slots, rpc = stage.shape[0], stage.shape[1]
    n = w_hbm.shape[0] // rpc

    def chunk(k):
        return pltpu.make_async_copy(w_hbm.at[pl.ds(k * rpc, rpc)], stage.at[k % slots], sem.at[k % slots])

    for k in range(min(slots - 1, n)):
        chunk(k).start()
    for k in range(n):
        if k + slots - 1 < n:
            chunk(k + slots - 1).start()
        chunk(k).wait()
        w16_scr[pl.ds(k * rpc, rpc), :] = stage[k % slots].astype(BF16)


def _ffn_tile(x_ref, o_ref, wg_scr, wu_scr, wd_scr, gi_ref, go_ref, *, L, ct):
    rows = L * ct
    x = x_ref[...].reshape(rows, D_MODEL)
    rs = rows // FFN_ROW_SPLIT
    xs = [x[i * rs:(i + 1) * rs] for i in range(FFN_ROW_SPLIT)]
    h = [_rms(xx, gi_ref[...]).astype(BF16) for xx in xs]
    gate = [jnp.dot(hh, wg_scr[...], preferred_element_type=F32) for hh in h]
    up = [jnp.dot(hh, wu_scr[...], preferred_element_type=F32) for hh in h]
    act = [(g * jax.nn.sigmoid(g) * u).astype(BF16) for g, u in zip(gate, up)]
    dn = [jnp.dot(a, wd_scr[...], preferred_element_type=F32) for a in act]
    y = jnp.concatenate([xx + _rms(d, go_ref[...]) for xx, d in zip(xs, dn)], axis=0)
    for s in range(L):
        o_ref[:, s, :] = y[s * ct:(s + 1) * ct]


def _ffn_kernel(xp_ref, xs_ref, wg_hbm, wu_hbm, wd_hbm, gi_ref, go_ref, op_ref, os_ref,
                wg_scr, wu_scr, wd_scr, stage_in, stage_out, sem, *, n_prompt, Lp, ctp, Ls, cts):
    i = pl.program_id(0)

    @pl.when(i == 0)
    def _():
        _stream_cast(wg_hbm, wg_scr, stage_in, sem)
        _stream_cast(wu_hbm, wu_scr, stage_in, sem)
        _stream_cast(wd_hbm, wd_scr, stage_out, sem)

    @pl.when(i < n_prompt)
    def _():
        _ffn_tile(xp_ref, op_ref, wg_scr, wu_scr, wd_scr, gi_ref, go_ref, L=Lp, ct=ctp)

    @pl.when(i == n_prompt)
    def _():
        _ffn_tile(xs_ref, os_ref, wg_scr, wu_scr, wd_scr, gi_ref, go_ref, L=Ls, ct=cts)


def _ffn(xp, xs, w_gate, w_up, w_down, g4, g5, *, bq, C, ct):
    Lp = xp.shape[0]
    Ls, rs, _ = xs.shape
    nc = C // ct
    n = bq * nc
    tile = lambda i: jnp.minimum(i, n - 1)
    hbm = pl.BlockSpec(memory_space=pl.ANY)
    return pl.pallas_call(
        functools.partial(_ffn_kernel, n_prompt=n, Lp=Lp, ctp=ct, Ls=Ls, cts=rs),
        grid=(n + 1,),
        in_specs=[
            pl.BlockSpec((Lp, ct, D_MODEL), lambda i: (0, tile(i), 0)),
            _const((Ls, rs, D_MODEL)),
            hbm, hbm, hbm,
            _const((1, D_MODEL)),
            _const((1, D_MODEL)),
        ],
        out_specs=[
            pl.BlockSpec((None, ct, Lp, D_MODEL), lambda i: (tile(i) // nc, tile(i) % nc, 0, 0)),
            pl.BlockSpec((None, rs, Ls, D_MODEL), lambda i: (0, 0, 0, 0)),
        ],
        out_shape=[
            jax.ShapeDtypeStruct((bq, C, Lp, D_MODEL), F32),
            jax.ShapeDtypeStruct((1, rs, Ls, D_MODEL), F32),
        ],
        scratch_shapes=[
            pltpu.VMEM((D_MODEL, D_FF), BF16),
            pltpu.VMEM((D_MODEL, D_FF), BF16),
            pltpu.VMEM((D_FF, D_MODEL), BF16),
            pltpu.VMEM((FFN_STAGE_SLOTS, D_MODEL // FFN_WEIGHT_CHUNKS, D_FF), F32),
            pltpu.VMEM((FFN_STAGE_SLOTS, D_FF // FFN_WEIGHT_CHUNKS, D_MODEL), F32),
            pltpu.SemaphoreType.DMA((FFN_STAGE_SLOTS,)),
        ],
        compiler_params=_params(1),
        name="ffn",
    )(xp, xs, w_gate, w_up, w_down, g4, g5)


def _mixer(x4, conv_state_t, h0_packed, wts, *, ct, sub, sc, bt, sample):
    bq, C, L, _ = x4.shape
    R = bq * C
    g = [wts["norm_g"][i:i + 1] for i in range(2)]

    v, u4 = _inproj(x4, g[0], wts["w_in"], L=L, ct=ct, sub=sub)
    sw = wts["ssm_L%d" % L]
    nb = R // (bt * sc)
    ssm_args = (u4, sw[0], sw[1], sw[2], wts["d_skip"], h0_packed.reshape(N_LANE_BLOCKS, nb, bt, STATE_W))
    conv_args = (wts["w_dw"], wts["b_dw"], wts["ln_g"], wts["ln_b"])
    y, hl = _ssm(*ssm_args, L=L, C=sc, bt=bt)
    if sample:
        cact, conv_new = _conv_sample(v, conv_state_t, *conv_args)
    else:
        cact = _conv_prompt(v, *conv_args, L=L, C=C)
        tail = v.reshape(L, bq, C, D_CONV)[:, :, C - 2:, :]
        tail = jnp.transpose(tail, (1, 2, 0, 3)).reshape(bq, 2 * L, D_CONV)
        conv_new = tail[:, 2 * L - (CONV_WIDTH - 1):, :]
    hl = hl.reshape(N_LANE_BLOCKS, nb * bt, STATE_W)

    x1 = _mixout(cact, y, x4, wts["w_glu"], wts["w_out"], g[1], L=L, ct=ct, sub=sub)
    return x1, conv_new, hl


PROMPT_L = 16
PROMPT_SUB = 32
PROMPT_CT = 64


def kernel(x_prompt, x_sample, mem_prompt, cache_mem_k, cache_mem_v, state_conv, state_ssm_re, state_ssm_im,
           norm_g, mem_norm_g, w_in, w_dw, b_dw, ln_g, ln_b, lam_re, lam_im, log_dt, b_re, b_im, c_re, c_im,
           d_skip, w_glu, w_out, w_q, w_k, w_v, w_o, w_gate, w_up, w_down):
    depth = w_in.shape[0]
    bp, tp, _ = x_prompt.shape
    bs, ts, _ = x_sample.shape
    assert tp % (PROMPT_L * PROMPT_CT) == 0 and tp >= CONV_WIDTH - 1

    yp = x_prompt.reshape(bp, tp // PROMPT_L, PROMPT_L, D_MODEL)
    ys = x_sample.reshape(1, bs, ts, D_MODEL)
    outs = [[] for _ in range(8)]
    for l in range(depth):
        wts = {
            "norm_g": norm_g[l],
            "w_in": w_in[l],
            "w_dw": w_dw[l],
            "b_dw": b_dw[l][None],
            "ln_g": ln_g[l][None],
            "ln_b": ln_b[l][None],
            "d_skip": d_skip[l].reshape(N_LANE_BLOCKS, 1, LANES),
            "w_glu": w_glu[l],
            "w_out": w_out[l],
        }
        ssm_args = (lam_re[l], lam_im[l], log_dt[l], b_re[l], b_im[l], c_re[l], c_im[l])
        assert ts <= PROMPT_L
        full = _ssm_weights(*ssm_args, PROMPT_L)
        wts["ssm_L%d" % PROMPT_L] = full[:2] + (full[2](PROMPT_L),)
        wts["ssm_L%d" % ts] = _ssm_weights_prefix(full, PROMPT_L, ts)

        kp, vp, kp16, vp16 = _memkv(mem_prompt, mem_norm_g[l][None], w_k[l], w_v[l])
        h0p = jnp.zeros((N_LANE_BLOCKS, bp, STATE_W), F32)
        cp_chunks = tp // PROMPT_L
        xp, cp, hp = _mixer(yp, None, h0p, wts, ct=PROMPT_CT, sub=PROMPT_SUB, sc=cp_chunks, bt=4, sample=False)
        h0s = _pack_state(state_ssm_re[l], state_ssm_im[l])
        xs, cs, hs = _mixer(ys, jnp.transpose(state_conv[l], (1, 0, 2)), h0s, wts, ct=bs, sub=bs, sc=1, bt=bs,
                            sample=True)
        cs = jnp.transpose(cs, (1, 0, 2))

        g2, g3 = norm_g[l][2:3], norm_g[l][3:4]
        n_tiles = bp * cp_chunks // PROMPT_SUB
        bb = bs // n_tiles
        xs_flat = xs.reshape(ts * bs, D_MODEL)
        q = _qproj(xs_flat, w_q[l], g2).reshape(ts, n_tiles, bb, D_MODEL)
        q = jnp.transpose(q, (1, 0, 2, 3)).reshape(n_tiles, ts * bb, D_MODEL)
        xp, o = _attn(xp, kp16, vp16, w_q[l], w_o[l], g2, g3, q, cache_mem_k[l:l + 1], cache_mem_v[l:l + 1],
                      C=cp_chunks, ct=PROMPT_SUB)
        o = jnp.transpose(o.reshape(n_tiles, ts, bb, D_MODEL), (1, 0, 2, 3)).reshape(ts * bs, D_MODEL)
        xs = _oproj(o, xs_flat, w_o[l], g3).reshape(ts, bs, D_MODEL)

        yp, ys = _ffn(xp, xs, w_gate[l], w_up[l], w_down[l], norm_g[l][4:5], norm_g[l][5:6],
                      bq=bp, C=cp_chunks, ct=PROMPT_SUB)

        hp_re, hp_im = _unpack_state(hp)
        hs_re, hs_im = _unpack_state(hs)
        for lst, val in zip(outs, (kp, vp, cp, hp_re, hp_im, cs, hs_re, hs_im)):
            lst.append(val)
    return (yp.reshape(bp, tp, D_MODEL), ys.reshape(bs, ts, D_MODEL)) + tuple(jnp.stack(o) for o in outs)
```

```python
import functools

import jax
import jax.numpy as jnp
from jax import lax
from jax.experimental import pallas as pl
from jax.experimental.pallas import tpu as pltpu

F32 = jnp.float32
BF16 = jnp.bfloat16

D_MODEL = 1024
D_CONV = 512
D_SSM = 512
CONV_WIDTH = 31
N_GROUPS = 32
P_GROUP = 16
N_STATE = 64
N_MEM = 256
N_HEADS = 4
HEAD_DIM = 256
D_FF = 2816
RMS_EPS = 1e-6
LN_EPS = 1e-5

LANES = 128
SUBLANES = 8
GROUPS_PER_BLOCK = LANES // P_GROUP
N_LANE_BLOCKS = D_SSM // LANES
STATE_HALF = GROUPS_PER_BLOCK * N_STATE
STATE_W = 2 * STATE_HALF
MXU_N = 256
VMEM_LIMIT = 56 * 1024 * 1024


def _params(n_axes, vmem=VMEM_LIMIT):
    return pltpu.CompilerParams(dimension_semantics=("arbitrary",) * n_axes, vmem_limit_bytes=vmem)


def _const(shape):
    nd = len(shape)
    return pl.BlockSpec(shape, lambda *_: (0,) * nd, pipeline_mode=pl.Buffered(1))


def _rms(x, g):
    return x * lax.rsqrt(jnp.mean(x * x, axis=-1, keepdims=True) + RMS_EPS) * g


def _cast_weights_once(n_axes, *pairs):
    first = pl.program_id(0) == 0
    for a in range(1, n_axes):
        first = jnp.logical_and(first, pl.program_id(a) == 0)

    @pl.when(first)
    def _():
        for src, dst in pairs:
            dst[...] = src[...].astype(BF16)


def _gather_rows(x_ref, L, c0, n):
    return jnp.concatenate([x_ref[c0:c0 + n, s, :] for s in range(L)], axis=0)


def _sub_tiles(ct, sub):
    return [(c0, sub) for c0 in range(0, ct, sub)]


def _nat_spec(ct, L):
    return pl.BlockSpec((None, ct, L, D_MODEL), lambda b, i: (b, i, 0, 0))


def _tp_spec(ct, L, width, nc):
    return pl.BlockSpec((L, ct, width), lambda b, i: (0, b * nc + i, 0))


def _inproj_kernel(x_ref, g_ref, w_ref, v_ref, u_ref, w16_scr, *, L, ct, sub):
    _cast_weights_once(2, (w_ref, w16_scr))
    for c0, n in _sub_tiles(ct, sub):
        x = _gather_rows(x_ref, L, c0, n)
        h = _rms(x, g_ref[...])
        z = jnp.dot(h.astype(BF16), w16_scr[...], preferred_element_type=F32)
        a = z[:, :D_CONV]
        g = z[:, D_CONV:2 * D_CONV]
        u = z[:, 2 * D_CONV:].astype(BF16)
        v_ref[:, c0:c0 + n, :] = (a * jax.nn.sigmoid(g)).reshape(L, n, D_CONV)
        for s in range(L):
            for j in range(N_LANE_BLOCKS):
                u_ref[j, c0:c0 + n, s * LANES:(s + 1) * LANES] = u[s * n:(s + 1) * n, j * LANES:(j + 1) * LANES]


def _inproj(x4, g0, w_in, *, L, ct, sub):
    bq, c = x4.shape[:2]
    nc = c // ct
    R = bq * c
    return pl.pallas_call(
        functools.partial(_inproj_kernel, L=L, ct=ct, sub=sub),
        grid=(bq, nc),
        in_specs=[_nat_spec(ct, L), _const((1, D_MODEL)), _const((D_MODEL, 2 * D_CONV + D_SSM))],
        out_specs=[
            _tp_spec(ct, L, D_CONV, nc),
            pl.BlockSpec((N_LANE_BLOCKS, ct, L * LANES), lambda b, i: (0, b * nc + i, 0)),
        ],
        out_shape=[
            jax.ShapeDtypeStruct((L, R, D_CONV), F32),
            jax.ShapeDtypeStruct((N_LANE_BLOCKS, R, L * LANES), BF16),
        ],
        scratch_shapes=[pltpu.VMEM(w_in.shape, BF16)],
        compiler_params=_params(2),
        name="inproj",
    )(x4, g0, w_in)


def _ln_silu(acc, g, b):
    mu = jnp.mean(acc, axis=-1, keepdims=True)
    xc = acc - mu
    var = jnp.mean(xc * xc, axis=-1, keepdims=True)
    y = xc * lax.rsqrt(var + LN_EPS) * g + b
    return y * jax.nn.sigmoid(y)


def _conv_shifted_copies(v_ref, v1_ref, v2_ref, *, L, C):
    row = lax.broadcasted_iota(jnp.int32, (C, D_CONV), 0)
    for s in range(L):
        x = v_ref[s]
        v1_ref[s] = jnp.where(row >= 1, pltpu.roll(x, 1, 0), 0.0)
        v2_ref[s] = jnp.where(row >= 2, pltpu.roll(x, 2, 0), 0.0)


def _conv_rows(srcs, w_ref, b_ref, g_ref, bb_ref, o_ref, r0, *, L, rc):
    sub8 = rc // SUBLANES
    for s in range(L):
        acc = jnp.broadcast_to(b_ref[...][None], (sub8, SUBLANES, D_CONV))
        for d in range(CONV_WIDTH):
            blk = (s - d) % L
            shift = (d - s + L - 1) // L if d > s else 0
            k = CONV_WIDTH - 1 - d
            src = srcs[shift][blk, pl.ds(r0, rc), :].reshape(sub8, SUBLANES, D_CONV)
            acc = acc + w_ref[k][None] * src
        y = _ln_silu(acc.reshape(rc, D_CONV), g_ref[...], bb_ref[...])
        o_ref[s, pl.ds(r0, rc), :] = y.astype(o_ref.dtype)


def _conv_prompt_kernel(v_ref, w_ref, b_ref, g_ref, bb_ref, o_ref, v1_ref, v2_ref, *, L, C, rc):
    _conv_shifted_copies(v_ref, v1_ref, v2_ref, L=L, C=C)
    srcs = (v_ref, v1_ref, v2_ref)

    def chunk(i, carry):
        _conv_rows(srcs, w_ref, b_ref, g_ref, bb_ref, o_ref, pl.multiple_of(i * rc, rc), L=L, rc=rc)
        return carry

    lax.fori_loop(0, C // rc, chunk, 0)


def _conv_prompt(v, w_dw, b_dw, ln_g, ln_b, *, L, C):
    R = v.shape[1]
    rc = 32
    return pl.pallas_call(
        functools.partial(_conv_prompt_kernel, L=L, C=C, rc=rc),
        grid=(R // C,),
        in_specs=[
            pl.BlockSpec((L, C, D_CONV), lambda b: (0, b, 0)),
            _const((CONV_WIDTH, SUBLANES, D_CONV)),
            _const((SUBLANES, D_CONV)),
            _const((1, D_CONV)),
            _const((1, D_CONV)),
        ],
        out_specs=pl.BlockSpec((L, C, D_CONV), lambda b: (0, b, 0)),
        out_shape=jax.ShapeDtypeStruct((L, R, D_CONV), BF16),
        scratch_shapes=[pltpu.VMEM((L, C, D_CONV), F32)] * 2,
        compiler_params=_params(1),
        name="conv_prompt",
    )(v, jnp.broadcast_to(w_dw[:, None, :], (CONV_WIDTH, SUBLANES, D_CONV)),
      jnp.broadcast_to(b_dw, (SUBLANES, D_CONV)), ln_g, ln_b)


def _conv_sample_kernel(v_ref, st_ref, w_ref, b_ref, g_ref, bb_ref, o_ref, new_ref, *, L, H):
    def ext(i):
        return st_ref[i] if i < H else v_ref[i - H]

    for t in range(L):
        acc = jnp.broadcast_to(b_ref[...], v_ref.shape[1:])
        for k in range(CONV_WIDTH):
            acc = acc + w_ref[k:k + 1, :] * ext(t + k)
        o_ref[t] = _ln_silu(acc, g_ref[...], bb_ref[...]).astype(o_ref.dtype)
    for i in range(H):
        new_ref[i] = ext(i + L)


def _conv_sample(v, state_t, w_dw, b_dw, ln_g, ln_b):
    L, R, _ = v.shape
    H = CONV_WIDTH - 1
    rt = 32
    rows = lambda n: pl.BlockSpec((n, rt, D_CONV), lambda i: (0, i, 0))
    return pl.pallas_call(
        functools.partial(_conv_sample_kernel, L=L, H=H),
        grid=(R // rt,),
        in_specs=[
            rows(L),
            rows(H),
            _const((CONV_WIDTH, D_CONV)),
            _const((1, D_CONV)),
            _const((1, D_CONV)),
            _const((1, D_CONV)),
        ],
        out_specs=[rows(L), rows(H)],
        out_shape=[
            jax.ShapeDtypeStruct((L, R, D_CONV), BF16),
            jax.ShapeDtypeStruct((H, R, D_CONV), F32),
        ],
        compiler_params=_params(1),
        name="conv_sample",
    )(v, state_t, w_dw, b_dw, ln_g, ln_b)


def _expand_block_diag(d):
    tiled = jnp.concatenate([d] * GROUPS_PER_BLOCK, axis=-1)
    r = lax.broadcasted_iota(jnp.int32, tiled.shape, 0) // P_GROUP
    c = lax.broadcasted_iota(jnp.int32, tiled.shape, 1) // N_STATE
    return jnp.where(r == c, tiled, jnp.zeros_like(tiled))


def _ssm_kernel(u_ref, wbc_ref, wcc_ref, a_ref, d_ref, h0_ref, y_ref, hl_ref,
                w_scr, wb_scr, wct_scr, s_scr, hp_scr, tap_scr, *, L, C, bt):
    per_tile = MXU_N // LANES

    @pl.when(pl.program_id(1) == 0)
    def _():
        for s in range(L):
            rows = slice(s * LANES, (s + 1) * LANES)
            for a in range(2):
                cols = slice(a * STATE_HALF, (a + 1) * STATE_HALF)
                wb_scr[rows, cols] = _expand_block_diag(wbc_ref[s, a])
                wct_scr[rows, cols] = _expand_block_diag(wcc_ref[s + 1, a])
        bbar = wb_scr[(L - 1) * LANES:L * LANES, :]
        nt = (((1,), (1,)), ((), ()))
        c0 = jnp.concatenate([_expand_block_diag(wcc_ref[0, a]) for a in range(2)], axis=-1)
        tap_scr[0] = lax.dot_general(bbar, c0, nt, preferred_element_type=F32).astype(BF16)
        for t in range(1, L):
            tap_scr[t] = lax.dot_general(bbar, wct_scr[(t - 1) * LANES:t * LANES, :], nt,
                                         preferred_element_type=F32).astype(BF16)
        for sp in range(L):
            for s in range(L):
                rows = slice(s * LANES, (s + 1) * LANES)
                cols = slice(sp * LANES, (sp + 1) * LANES)
                if s <= sp:
                    w_scr[rows, cols] = tap_scr[sp - s]
                elif s // per_tile == sp // per_tile:
                    w_scr[rows, cols] = jnp.zeros((LANES, LANES), BF16)

    nk = STATE_W // LANES
    half = nk // 2
    rows = bt * C
    inc_all = jnp.dot(u_ref[...], wb_scr[...], preferred_element_type=F32)
    for k in range(half):
        s_scr[k, 0:rows, :] = inc_all[:, k * LANES:(k + 1) * LANES]
        s_scr[k, rows:2 * rows, :] = inc_all[:, (half + k) * LANES:(half + k + 1) * LANES]

    d = d_ref[...]
    for n in range(L // per_tile):
        k = (n + 1) * MXU_N
        y = jnp.dot(u_ref[:, :k], w_scr[:k, n * MXU_N:(n + 1) * MXU_N], preferred_element_type=F32)
        for h in range(per_tile):
            s = n * per_tile + h
            us = u_ref[:, s * LANES:(s + 1) * LANES].astype(F32)
            y_ref[s] = y[:, h * LANES:(h + 1) * LANES] + d * us

    h0 = h0_ref[...]
    a = a_ref[...]

    def piece(x, k):
        return x[:, k * LANES:(k + 1) * LANES]

    def swap_parts(x):
        if 2 * bt == SUBLANES:
            return pltpu.roll(x, bt, 0)
        return jnp.concatenate([x[bt:], x[:bt]], axis=0)

    im_rows = lax.broadcasted_iota(jnp.int32, (2 * bt, LANES), 0) >= bt
    a_same = [jnp.broadcast_to(piece(a[0:1], k), (2 * bt, LANES)) for k in range(half)]
    a_cross = [jnp.where(im_rows, piece(a[1:2], k), -piece(a[1:2], k)) for k in range(half)]
    state = [jnp.concatenate([piece(h0, k), piece(h0, half + k)], axis=0) for k in range(half)]
    for c in range(C):
        sel = pl.ds(c, 2 * bt, stride=C) if C > 1 else pl.ds(0, 2 * bt)
        for k in range(half):
            hp_scr[k, sel, :] = state[k]
            state[k] = a_same[k] * state[k] + a_cross[k] * swap_parts(state[k]) + s_scr[k, sel, :]
    for k in range(half):
        hl_ref[:, k * LANES:(k + 1) * LANES] = state[k][:bt]
        hl_ref[:, (half + k) * LANES:(half + k + 1) * LANES] = state[k][bt:]

    hp = jnp.concatenate([hp_scr[k, 0:rows, :] for k in range(half)]
                         + [hp_scr[k, rows:2 * rows, :] for k in range(half)], axis=-1).astype(BF16)
    for n in range(L // per_tile):
        y = lax.dot_general(hp, wct_scr[n * MXU_N:(n + 1) * MXU_N, :], (((1,), (1,)), ((), ())),
                            preferred_element_type=F32)
        for h in range(per_tile):
            s = n * per_tile + h
            y_ref[s] += y[:, h * LANES:(h + 1) * LANES]


def _ssm(u4, wbc, wcc, a_pow, d_skip, h0, *, L, C, bt):
    R = u4.shape[1]
    rows = bt * C
    nb = R // rows
    lw = L * LANES
    in_specs = [
        pl.BlockSpec((None, rows, lw), lambda j, b: (j, b, 0)),
        pl.BlockSpec((None, L, 2, LANES, N_STATE), lambda j, b: (j, 0, 0, 0, 0)),
        pl.BlockSpec((None, L + 1, 2, LANES, N_STATE), lambda j, b: (j, 0, 0, 0, 0)),
        pl.BlockSpec((None, 2, STATE_HALF), lambda j, b: (j, 0, 0)),
        pl.BlockSpec((None, 1, LANES), lambda j, b: (j, 0, 0)),
        pl.BlockSpec((None, None, bt, STATE_W), lambda j, b: (j, b, 0, 0)),
    ]
    out_specs = [
        pl.BlockSpec((L, rows, LANES), lambda j, b: (0, b, j)),
        pl.BlockSpec((None, None, bt, STATE_W), lambda j, b: (j, b, 0, 0)),
    ]
    out_shape = [
        jax.ShapeDtypeStruct((L, R, D_SSM), F32),
        jax.ShapeDtypeStruct((N_LANE_BLOCKS, nb, bt, STATE_W), F32),
    ]
    scratch = [
        pltpu.VMEM((lw, lw), BF16),
        pltpu.VMEM((lw, STATE_W), BF16),
        pltpu.VMEM((lw, STATE_W), BF16),
        pltpu.VMEM((STATE_HALF // LANES, 2 * rows, LANES), F32),
        pltpu.VMEM((STATE_HALF // LANES, 2 * rows, LANES), F32),
        pltpu.VMEM((L, LANES, LANES), BF16),
    ]
    return pl.pallas_call(
        functools.partial(_ssm_kernel, L=L, C=C, bt=bt),
        grid=(N_LANE_BLOCKS, nb),
        in_specs=in_specs,
        out_specs=out_specs,
        out_shape=out_shape,
        scratch_shapes=scratch,
        compiler_params=_params(2),
        name="ssm",
    )(u4, wbc, wcc, a_pow, d_skip, h0)


def _ssm_weights(lam_re, lam_im, log_dt, b_re, b_im, c_re, c_im, L):
    dt = jnp.exp(log_dt)[:, None]
    zr = lam_re * dt
    zi = lam_im * dt
    n_pow = jnp.arange(L + 1, dtype=F32)[:, None, None]
    mag = jnp.exp(zr[None] * n_pow)
    pr = mag * jnp.cos(zi[None] * n_pow)
    pi = mag * jnp.sin(zi[None] * n_pow)
    a1r, a1i = pr[1], pi[1]
    den = lam_re * lam_re + lam_im * lam_im
    qr = ((a1r - 1.0) * lam_re + a1i * lam_im) / den
    qi = (a1i * lam_re - (a1r - 1.0) * lam_im) / den
    bbr = qr[:, :, None] * b_re - qi[:, :, None] * b_im
    bbi = qr[:, :, None] * b_im + qi[:, :, None] * b_re

    nj, g8 = N_LANE_BLOCKS, GROUPS_PER_BLOCK

    n_rev = (L - 1.0) - jnp.arange(L, dtype=F32)[:, None, None]
    mag_rev = jnp.exp(zr[None] * n_rev)
    rev = mag_rev * jnp.cos(zi[None] * n_rev), mag_rev * jnp.sin(zi[None] * n_rev)
    er = rev[0][:, :, :, None] * bbr[None] - rev[1][:, :, :, None] * bbi[None]
    ei = rev[0][:, :, :, None] * bbi[None] + rev[1][:, :, :, None] * bbr[None]
    e = jnp.stack([er, ei], 1).reshape(L, 2, nj, g8, N_STATE, P_GROUP)
    wbc = jnp.transpose(e, (2, 0, 1, 3, 5, 4)).reshape(nj, L, 2, LANES, N_STATE)

    mr = c_re[None] * pr[:, :, None, :] - c_im[None] * pi[:, :, None, :]
    mi = c_re[None] * pi[:, :, None, :] + c_im[None] * pr[:, :, None, :]
    m = jnp.stack([mr, -mi], 1).reshape(L + 1, 2, nj, g8, P_GROUP, N_STATE)
    wcc = jnp.transpose(m, (2, 0, 1, 3, 4, 5)).reshape(nj, L + 1, 2, LANES, N_STATE)

    def a_pow(n):
        return jnp.stack([pr[n].reshape(nj, STATE_HALF), pi[n].reshape(nj, STATE_HALF)], 1)

    return wbc.astype(BF16), wcc.astype(BF16), a_pow


def _ssm_weights_prefix(full, L_full, L):
    wbc, wcc, a_pow = full
    return wbc[:, L_full - L:], wcc[:, :L + 1], a_pow(L)


def _pack_state(re, im):
    b = re.shape[0]
    r = re.reshape(b, N_LANE_BLOCKS, STATE_HALF)
    i = im.reshape(b, N_LANE_BLOCKS, STATE_HALF)
    return jnp.transpose(jnp.concatenate([r, i], -1), (1, 0, 2))


def _unpack_state(h):
    b = h.shape[1]
    h = jnp.transpose(h, (1, 0, 2))
    re = h[:, :, :STATE_HALF].reshape(b, N_GROUPS, N_STATE)
    im = h[:, :, STATE_HALF:].reshape(b, N_GROUPS, N_STATE)
    return re, im


def _mixout_kernel(c_ref, y_ref, x_ref, wglu_ref, wout_ref, g_ref, o_ref, wglu_scr, wout_scr, *, L, ct, sub):
    _cast_weights_once(2, (wglu_ref, wglu_scr), (wout_ref, wout_scr))
    tiles = _sub_tiles(ct, sub)
    gy = [jax.nn.gelu(y_ref[:, c0:c0 + n, :].reshape(L * n, D_SSM)) for c0, n in tiles]
    gate = [jnp.dot(g.astype(BF16), wglu_scr[...], preferred_element_type=F32) for g in gy]
    m = [jnp.dot(c_ref[:, c0:c0 + n, :].reshape(L * n, D_CONV), wout_scr[0:D_CONV, :],
                 preferred_element_type=F32) for c0, n in tiles]
    sg = [g * jax.nn.sigmoid(t) for g, t in zip(gy, gate)]
    m = [a + jnp.dot(s.astype(BF16), wout_scr[D_CONV:, :], preferred_element_type=F32) for a, s in zip(m, sg)]
    for (c0, n), mm in zip(tiles, m):
        x = _gather_rows(x_ref, L, c0, n)
        o_ref[:, c0:c0 + n, :] = (x + _rms(mm, g_ref[...])).reshape(L, n, D_MODEL)


def _mixout(c, y, x4, w_glu, w_out, g1, *, L, ct, sub):
    bq, cc = x4.shape[:2]
    nc = cc // ct
    R = bq * cc
    return pl.pallas_call(
        functools.partial(_mixout_kernel, L=L, ct=ct, sub=sub),
        grid=(bq, nc),
        in_specs=[
            _tp_spec(ct, L, D_CONV, nc),
            _tp_spec(ct, L, D_SSM, nc),
            _nat_spec(ct, L),
            _const((D_SSM, D_SSM)),
            _const((D_CONV + D_SSM, D_MODEL)),
            _const((1, D_MODEL)),
        ],
        out_specs=_tp_spec(ct, L, D_MODEL, nc),
        out_shape=jax.ShapeDtypeStruct((L, R, D_MODEL), F32),
        scratch_shapes=[pltpu.VMEM(w_glu.shape, BF16), pltpu.VMEM(w_out.shape, BF16)],
        compiler_params=_params(2),
        name="mixout",
    )(c, y, x4, w_glu, w_out, g1)


def _memkv_kernel(m_ref, g_ref, wk_ref, wv_ref, k5_ref, v5_ref, kb_ref, vb_ref, wk_scr, wv_scr, *, nb):
    _cast_weights_once(1, (wk_ref, wk_scr), (wv_ref, wv_scr))
    ms = [_rms(m_ref[b], g_ref[...]).astype(BF16) for b in range(nb)]
    for w_ref, o5_ref, ob_ref in ((wk_scr, k5_ref, kb_ref), (wv_scr, v5_ref, vb_ref)):
        ps = [jnp.dot(m, w_ref[...], preferred_element_type=F32) for m in ms]
        for b, p in enumerate(ps):
            ob_ref[b] = p.astype(BF16)
            for hd in range(N_HEADS):
                o5_ref[b, :, hd, :] = p[:, hd * HEAD_DIM:(hd + 1) * HEAD_DIM]


def _memkv(mem, g_mem, w_k, w_v):
    bq = mem.shape[0]
    nb = 2
    out5 = pl.BlockSpec((nb, N_MEM, N_HEADS, HEAD_DIM), lambda i: (i, 0, 0, 0))
    outb = pl.BlockSpec((nb, N_MEM, D_MODEL), lambda i: (i, 0, 0))
    return pl.pallas_call(
        functools.partial(_memkv_kernel, nb=nb),
        grid=(bq // nb,),
        in_specs=[
            pl.BlockSpec((nb, N_MEM, D_MODEL), lambda i: (i, 0, 0)),
            _const((1, D_MODEL)),
            _const((D_MODEL, D_MODEL)),
            _const((D_MODEL, D_MODEL)),
        ],
        out_specs=[out5, out5, outb, outb],
        out_shape=[jax.ShapeDtypeStruct((bq, N_MEM, N_HEADS, HEAD_DIM), F32)] * 2
        + [jax.ShapeDtypeStruct((bq, N_MEM, D_MODEL), BF16)] * 2,
        scratch_shapes=[pltpu.VMEM(w_k.shape, BF16), pltpu.VMEM(w_v.shape, BF16)],
        compiler_params=_params(1),
        name="memkv",
    )(mem, g_mem, w_k, w_v)


def _softmax_rows(s):
    s = s - jnp.max(s, axis=-1, keepdims=True)
    e = jnp.exp(s)
    return e / jnp.sum(e, axis=-1, keepdims=True)


def _attn_kernel(x_ref, k_ref, v_ref, wq32_ref, wo32_ref, gq_ref, go_ref, qs_ref, ck_ref, cv_ref, o_ref, os_ref,
                 wq_ref, wo_ref, *, L, ct, bb):
    _cast_weights_once(1, (wq32_ref, wq_ref), (wo32_ref, wo_ref))
    nt = (((1,), (1,)), ((), ()))
    scale = HEAD_DIM ** -0.5
    heads = [slice(hd * HEAD_DIM, (hd + 1) * HEAD_DIM) for hd in range(N_HEADS)]
    x = x_ref[...].reshape(L * ct, D_MODEL)
    kb, vb = k_ref[...], v_ref[...]

    q2 = qs_ref[...]
    rows = q2.shape[0]
    qs = jnp.concatenate([q2[:, sl] for sl in heads], axis=0).astype(BF16)
    nr = N_HEADS * rows
    nc = N_MEM * N_HEADS
    same_head = (lax.broadcasted_iota(jnp.int32, (nr, nc), 0) // rows
                 == lax.broadcasted_iota(jnp.int32, (nr, nc), 1) % N_HEADS)
    owner = lax.broadcasted_iota(jnp.int32, (nr, HEAD_DIM), 0) % bb

    h = _rms(x, gq_ref[...])
    ka = [ck_ref[b].reshape(nc, HEAD_DIM).astype(BF16) for b in range(bb)]
    q = jnp.dot(h.astype(BF16), wq_ref[...], preferred_element_type=F32).astype(BF16)
    va = [cv_ref[b].reshape(nc, HEAD_DIM).astype(BF16) for b in range(bb)]
    sc_p = [lax.dot_general(q[:, sl], kb[:, sl], nt, preferred_element_type=F32) for sl in heads]
    sc_s = [lax.dot_general(qs, kk, nt, preferred_element_type=F32) for kk in ka]
    p_p = [_softmax_rows(s * scale).astype(BF16) for s in sc_p]
    p_s = [_softmax_rows(jnp.where(same_head, s * scale, -1e30)).astype(BF16) for s in sc_s]
    o_p = jnp.concatenate([jnp.dot(pp, vb[:, sl], preferred_element_type=F32) for pp, sl in zip(p_p, heads)], axis=-1)
    o_s = [jnp.dot(pp, vv, preferred_element_type=F32) for pp, vv in zip(p_s, va)]
    a = jnp.dot(o_p.astype(BF16), wo_ref[...], preferred_element_type=F32)
    acc = jnp.zeros((nr, HEAD_DIM), F32)
    for b, ob in enumerate(o_s):
        acc = jnp.where(owner == b, ob, acc)
    o_ref[...] = (x + _rms(a, go_ref[...])).reshape(L, ct, D_MODEL)
    os_ref[...] = jnp.concatenate([acc[hd * rows:(hd + 1) * rows] for hd in range(N_HEADS)], axis=-1)


def _attn(x1, k, v, w_q, w_o, g2, g3, q_grp, cache_k, cache_v, *, C, ct):
    L, R, _ = x1.shape
    nc = C // ct
    n, rows_s, _ = q_grp.shape
    bb = cache_k.shape[1] // n
    assert n == R // ct and n * bb == cache_k.shape[1]
    kv_spec = pl.BlockSpec((None, bb, N_MEM, N_HEADS, HEAD_DIM), lambda i: (0, i, 0, 0, 0))
    grp_spec = pl.BlockSpec((None, rows_s, D_MODEL), lambda i: (i, 0, 0))
    return pl.pallas_call(
        functools.partial(_attn_kernel, L=L, ct=ct, bb=bb),
        grid=(n,),
        in_specs=[
            pl.BlockSpec((L, ct, D_MODEL), lambda i: (0, i, 0)),
            pl.BlockSpec((None, N_MEM, D_MODEL), lambda i: (i // nc, 0, 0)),
            pl.BlockSpec((None, N_MEM, D_MODEL), lambda i: (i // nc, 0, 0)),
            _const((D_MODEL, D_MODEL)),
            _const((D_MODEL, D_MODEL)),
            _const((1, D_MODEL)),
            _const((1, D_MODEL)),
            grp_spec, kv_spec, kv_spec,
        ],
        out_specs=[pl.BlockSpec((L, ct, D_MODEL), lambda i: (0, i, 0)), grp_spec],
        out_shape=[jax.ShapeDtypeStruct((L, R, D_MODEL), F32), jax.ShapeDtypeStruct(q_grp.shape, F32)],
        scratch_shapes=[pltpu.VMEM(w_q.shape, BF16), pltpu.VMEM(w_o.shape, BF16)],
        compiler_params=_params(1),
        name="attn",
    )(x1, k, v, w_q, w_o, g2, g3, q_grp, cache_k, cache_v)


def _qproj_kernel(x_ref, wq_ref, g_ref, q_ref):
    h = _rms(x_ref[...], g_ref[...])
    q_ref[...] = jnp.dot(h.astype(BF16), wq_ref[...].astype(BF16), preferred_element_type=F32)


def _qproj(x1f, w_q, g2):
    rows = x1f.shape[0]
    return pl.pallas_call(
        _qproj_kernel,
        grid=(1,),
        in_specs=[_const((rows, D_MODEL)), _const((D_MODEL, D_MODEL)), _const((1, D_MODEL))],
        out_specs=pl.BlockSpec((rows, D_MODEL), lambda i: (0, 0)),
        out_shape=jax.ShapeDtypeStruct((rows, D_MODEL), F32),
        compiler_params=_params(1),
        name="qproj_sample",
    )(x1f, w_q, g2)


def _oproj_kernel(o_ref, x_ref, wo_ref, g_ref, y_ref):
    a = jnp.dot(o_ref[...].astype(BF16), wo_ref[...].astype(BF16), preferred_element_type=F32)
    y_ref[...] = x_ref[...] + _rms(a, g_ref[...])


def _oproj(of, x1f, w_o, g3):
    rows = of.shape[0]
    return pl.pallas_call(
        _oproj_kernel,
        grid=(1,),
        in_specs=[_const((rows, D_MODEL)), _const((rows, D_MODEL)), _const((D_MODEL, D_MODEL)),
                  _const((1, D_MODEL))],
        out_specs=pl.BlockSpec((rows, D_MODEL), lambda i: (0, 0)),
        out_shape=jax.ShapeDtypeStruct((rows, D_MODEL), F32),
        compiler_params=_params(1),
        name="oproj_sample",
    )(of, x1f, w_o, g3)


FFN_WEIGHT_CHUNKS = 16
FFN_STAGE_SLOTS = 4
FFN_ROW_SPLIT = 2


def _stream_cast(w_hbm, w16_scr, stage, sem):
    slots, rpc = stage.shape[0], stage.shape[1]
    n = w_hbm.shape[0] // rpc

    def chunk(k):
        return pltpu.make_async_copy(w_hbm.at[pl.ds(k * rpc, rpc)], stage.at[k % slots], sem.at[k % slots])

    for k in range(min(slots - 1, n)):
        chunk(k).start()
    for k in range(n):
        if k + slots - 1 < n:
            chunk(k + slots - 1).start()
        chunk(k).wait()
        w16_scr[pl.ds(k * rpc, rpc), :] = stage[k % slots].astype(BF16)


def _ffn_tile(x_ref, o_ref, wg_scr, wu_scr, wd_scr, gi_ref, go_ref, *, L, ct):
    rows = L * ct
    x = x_ref[...].reshape(rows, D_MODEL)
    rs = rows // FFN_ROW_SPLIT
    xs = [x[i * rs:(i + 1) * rs] for i in range(FFN_ROW_SPLIT)]
    h = [_rms(xx, gi_ref[...]).astype(BF16) for xx in xs]
    gate = [jnp.dot(hh, wg_scr[...], preferred_element_type=F32) for hh in h]
    up = [jnp.dot(hh, wu_scr[...], preferred_element_type=F32) for hh in h]
    act = [(g * jax.nn.sigmoid(g) * u).astype(BF16) for g, u in zip(gate, up)]
    dn = [jnp.dot(a, wd_scr[...], preferred_element_type=F32) for a in act]
    y = jnp.concatenate([xx + _rms(d, go_ref[...]) for xx, d in zip(xs, dn)], axis=0)
    for s in range(L):
        o_ref[:, s, :] = y[s * ct:(s + 1) * ct]


def _ffn_kernel(xp_ref, xs_ref, wg_hbm, wu_hbm, wd_hbm, gi_ref, go_ref, op_ref, os_ref,
                wg_scr, wu_scr, wd_scr, stage_in, stage_out, sem, *, n_prompt, Lp, ctp, Ls, cts):
    i = pl.program_id(0)

    @pl.when(i == 0)
    def _():
        _stream_cast(wg_hbm, wg_scr, stage_in, sem)
        _stream_cast(wu_hbm, wu_scr, stage_in, sem)
        _stream_cast(wd_hbm, wd_scr, stage_out, sem)

    @pl.when(i < n_prompt)
    def _():
        _ffn_tile(xp_ref, op_ref, wg_scr, wu_scr, wd_scr, gi_ref, go_ref, L=Lp, ct=ctp)

    @pl.when(i == n_prompt)
    def _():
        _ffn_tile(xs_ref, os_ref, wg_scr, wu_scr, wd_scr, gi_ref, go_ref, L=Ls, ct=cts)


def _ffn(xp, xs, w_gate, w_up, w_down, g4, g5, *, bq, C, ct):
    Lp = xp.shape[0]
    Ls, rs, _ = xs.shape
    nc = C // ct
    n = bq * nc
    tile = lambda i: jnp.minimum(i, n - 1)
    hbm = pl.BlockSpec(memory_space=pl.ANY)
    return pl.pallas_call(
        functools.partial(_ffn_kernel, n_prompt=n, Lp=Lp, ctp=ct, Ls=Ls, cts=rs),
        grid=(n + 1,),
        in_specs=[
            pl.BlockSpec((Lp, ct, D_MODEL), lambda i: (0, tile(i), 0)),
            _const((Ls, rs, D_MODEL)),
            hbm, hbm, hbm,
            _const((1, D_MODEL)),
            _const((1, D_MODEL)),
        ],
        out_specs=[
            pl.BlockSpec((None, ct, Lp, D_MODEL), lambda i: (tile(i) // nc, tile(i) % nc, 0, 0)),
            pl.BlockSpec((None, rs, Ls, D_MODEL), lambda i: (0, 0, 0, 0)),
        ],
        out_shape=[
            jax.ShapeDtypeStruct((bq, C, Lp, D_MODEL), F32),
            jax.ShapeDtypeStruct((1, rs, Ls, D_MODEL), F32),
        ],
        scratch_shapes=[
            pltpu.VMEM((D_MODEL, D_FF), BF16),
            pltpu.VMEM((D_MODEL, D_FF), BF16),
            pltpu.VMEM((D_FF, D_MODEL), BF16),
            pltpu.VMEM((FFN_STAGE_SLOTS, D_MODEL // FFN_WEIGHT_CHUNKS, D_FF), F32),
            pltpu.VMEM((FFN_STAGE_SLOTS, D_FF // FFN_WEIGHT_CHUNKS, D_MODEL), F32),
            pltpu.SemaphoreType.DMA((FFN_STAGE_SLOTS,)),
        ],
        compiler_params=_params(1),
        name="ffn",
    )(xp, xs, w_gate, w_up, w_down, g4, g5)


def _mixer(x4, conv_state_t, h0_packed, wts, *, ct, sub, sc, bt, sample):
    bq, C, L, _ = x4.shape
    R = bq * C
    g = [wts["norm_g"][i:i + 1] for i in range(2)]

    v, u4 = _inproj(x4, g[0], wts["w_in"], L=L, ct=ct, sub=sub)
    sw = wts["ssm_L%d" % L]
    nb = R // (bt * sc)
    ssm_args = (u4, sw[0], sw[1], sw[2], wts["d_skip"], h0_packed.reshape(N_LANE_BLOCKS, nb, bt, STATE_W))
    conv_args = (wts["w_dw"], wts["b_dw"], wts["ln_g"], wts["ln_b"])
    y, hl = _ssm(*ssm_args, L=L, C=sc, bt=bt)
    if sample:
        cact, conv_new = _conv_sample(v, conv_state_t, *conv_args)
    else:
        cact = _conv_prompt(v, *conv_args, L=L, C=C)
        tail = v.reshape(L, bq, C, D_CONV)[:, :, C - 2:, :]
        tail = jnp.transpose(tail, (1, 2, 0, 3)).reshape(bq, 2 * L, D_CONV)
        conv_new = tail[:, 2 * L - (CONV_WIDTH - 1):, :]
    hl = hl.reshape(N_LANE_BLOCKS, nb * bt, STATE_W)

    x1 = _mixout(cact, y, x4, wts["w_glu"], wts["w_out"], g[1], L=L, ct=ct, sub=sub)
    return x1, conv_new, hl


PROMPT_L = 16
PROMPT_SUB = 32
PROMPT_CT = 64


def kernel(x_prompt, x_sample, mem_prompt, cache_mem_k, cache_mem_v, state_conv, state_ssm_re, state_ssm_im,
           norm_g, mem_norm_g, w_in, w_dw, b_dw, ln_g, ln_b, lam_re, lam_im, log_dt, b_re, b_im, c_re, c_im,
           d_skip, w_glu, w_out, w_q, w_k, w_v, w_o, w_gate, w_up, w_down):
    depth = w_in.shape[0]
    bp, tp, _ = x_prompt.shape
    bs, ts, _ = x_sample.shape
    assert tp % (PROMPT_L * PROMPT_CT) == 0 and tp >= CONV_WIDTH - 1

    yp = x_prompt.reshape(bp, tp // PROMPT_L, PROMPT_L, D_MODEL)
    ys = x_sample.reshape(1, bs, ts, D_MODEL)
    outs = [[] for _ in range(8)]
    for l in range(depth):
        wts = {
            "norm_g": norm_g[l],
            "w_in": w_in[l],
            "w_dw": w_dw[l],
            "b_dw": b_dw[l][None],
            "ln_g": ln_g[l][None],
            "ln_b": ln_b[l][None],
            "d_skip": d_skip[l].reshape(N_LANE_BLOCKS, 1, LANES),
            "w_glu": w_glu[l],
            "w_out": w_out[l],
        }
        ssm_args = (lam_re[l], lam_im[l], log_dt[l], b_re[l], b_im[l], c_re[l], c_im[l])
        assert ts <= PROMPT_L
        full = _ssm_weights(*ssm_args, PROMPT_L)
        wts["ssm_L%d" % PROMPT_L] = full[:2] + (full[2](PROMPT_L),)
        wts["ssm_L%d" % ts] = _ssm_weights_prefix(full, PROMPT_L, ts)

        kp, vp, kp16, vp16 = _memkv(mem_prompt, mem_norm_g[l][None], w_k[l], w_v[l])
        h0p = jnp.zeros((N_LANE_BLOCKS, bp, STATE_W), F32)
        cp_chunks = tp // PROMPT_L
        xp, cp, hp = _mixer(yp, None, h0p, wts, ct=PROMPT_CT, sub=PROMPT_SUB, sc=cp_chunks, bt=4, sample=False)
        h0s = _pack_state(state_ssm_re[l], state_ssm_im[l])
        xs, cs, hs = _mixer(ys, jnp.transpose(state_conv[l], (1, 0, 2)), h0s, wts, ct=bs, sub=bs, sc=1, bt=bs,
                            sample=True)
        cs = jnp.transpose(cs, (1, 0, 2))

        g2, g3 = norm_g[l][2:3], norm_g[l][3:4]
        n_tiles = bp * cp_chunks // PROMPT_SUB
        bb = bs // n_tiles
        xs_flat = xs.reshape(ts * bs, D_MODEL)
        q = _qproj(xs_flat, w_q[l], g2).reshape(ts, n_tiles, bb, D_MODEL)
        q = jnp.transpose(q, (1, 0, 2, 3)).reshape(n_tiles, ts * bb, D_MODEL)
        xp, o = _attn(xp, kp16, vp16, w_q[l], w_o[l], g2, g3, q, cache_mem_k[l:l + 1], cache_mem_v[l:l + 1],
                      C=cp_chunks, ct=PROMPT_SUB)
        o = jnp.transpose(o.reshape(n_tiles, ts, bb, D_MODEL), (1, 0, 2, 3)).reshape(ts * bs, D_MODEL)
        xs = _oproj(o, xs_flat, w_o[l], g3).reshape(ts, bs, D_MODEL)

        yp, ys = _ffn(xp, xs, w_gate[l], w_up[l], w_down[l], norm_g[l][4:5], norm_g[l][5:6],
                      bq=bp, C=cp_chunks, ct=PROMPT_SUB)

        hp_re, hp_im = _unpack_state(hp)
        hs_re, hs_im = _unpack_state(hs)
        for lst, val in zip(outs, (kp, vp, cp, hp_re, hp_im, cs, hs_re, hs_im)):
            lst.append(val)
    return (yp.reshape(bp, tp, D_MODEL), ys.reshape(bs, ts, D_MODEL)) + tuple(jnp.stack(o) for o in outs)
```

```python
import functools

import jax
import jax.numpy as jnp
from jax import lax
from jax.experimental import pallas as pl
from jax.experimental.pallas import tpu as pltpu

F32 = jnp.float32
BF16 = jnp.bfloat16

D_MODEL = 1024
D_CONV = 512
D_SSM = 512
CONV_WIDTH = 31
N_GROUPS = 32
P_GROUP = 16
N_STATE = 64
N_MEM = 256
N_HEADS = 4
HEAD_DIM = 256
D_FF = 2816
RMS_EPS = 1e-6
LN_EPS = 1e-5

LANES = 128
SUBLANES = 8
GROUPS_PER_BLOCK = LANES // P_GROUP
N_LANE_BLOCKS = D_SSM // LANES
STATE_HALF = GROUPS_PER_BLOCK * N_STATE
STATE_W = 2 * STATE_HALF
MXU_N = 256
VMEM_LIMIT = 56 * 1024 * 1024


def _params(n_axes, vmem=VMEM_LIMIT):
    return pltpu.CompilerParams(dimension_semantics=("arbitrary",) * n_axes, vmem_limit_bytes=vmem)


def _const(shape):
    nd = len(shape)
    return pl.BlockSpec(shape, lambda *_: (0,) * nd, pipeline_mode=pl.Buffered(1))


def _rms(x, g):
    return x * lax.rsqrt(jnp.mean(x * x, axis=-1, keepdims=True) + RMS_EPS) * g


def _cast_weights_once(n_axes, *pairs):
    first = pl.program_id(0) == 0
    for a in range(1, n_axes):
        first = jnp.logical_and(first, pl.program_id(a) == 0)

    @pl.when(first)
    def _():
        for src, dst in pairs:
            dst[...] = src[...].astype(BF16)


def _gather_rows(x_ref, L, c0, n):
    return jnp.concatenate([x_ref[c0:c0 + n, s, :] for s in range(L)], axis=0)


def _sub_tiles(ct, sub):
    return [(c0, sub) for c0 in range(0, ct, sub)]


def _nat_spec(ct, L):
    return pl.BlockSpec((None, ct, L, D_MODEL), lambda b, i: (b, i, 0, 0))


def _tp_spec(ct, L, width, nc):
    return pl.BlockSpec((L, ct, width), lambda b, i: (0, b * nc + i, 0))


def _inproj_kernel(x_ref, g_ref, w_ref, v_ref, u_ref, w16_scr, *, L, ct, sub):
    _cast_weights_once(2, (w_ref, w16_scr))
    for c0, n in _sub_tiles(ct, sub):
        x = _gather_rows(x_ref, L, c0, n)
        h = _rms(x, g_ref[...])
        z = jnp.dot(h.astype(BF16), w16_scr[...], preferred_element_type=F32)
        a = z[:, :D_CONV]
        g = z[:, D_CONV:2 * D_CONV]
        u = z[:, 2 * D_CONV:].astype(BF16)
        v_ref[:, c0:c0 + n, :] = (a * jax.nn.sigmoid(g)).reshape(L, n, D_CONV)
        for s in range(L):
            for j in range(N_LANE_BLOCKS):
                u_ref[j, c0:c0 + n, s * LANES:(s + 1) * LANES] = u[s * n:(s + 1) * n, j * LANES:(j + 1) * LANES]


def _inproj(x4, g0, w_in, *, L, ct, sub):
    bq, c = x4.shape[:2]
    nc = c // ct
    R = bq * c
    return pl.pallas_call(
        functools.partial(_inproj_kernel, L=L, ct=ct, sub=sub),
        grid=(bq, nc),
        in_specs=[_nat_spec(ct, L), _const((1, D_MODEL)), _const((D_MODEL, 2 * D_CONV + D_SSM))],
        out_specs=[
            _tp_spec(ct, L, D_CONV, nc),
            pl.BlockSpec((N_LANE_BLOCKS, ct, L * LANES), lambda b, i: (0, b * nc + i, 0)),
        ],
        out_shape=[
            jax.ShapeDtypeStruct((L, R, D_CONV), F32),
            jax.ShapeDtypeStruct((N_LANE_BLOCKS, R, L * LANES), BF16),
        ],
        scratch_shapes=[pltpu.VMEM(w_in.shape, BF16)],
        compiler_params=_params(2),
        name="inproj",
    )(x4, g0, w_in)


def _ln_silu(acc, g, b):
    mu = jnp.mean(acc, axis=-1, keepdims=True)
    xc = acc - mu
    var = jnp.mean(xc * xc, axis=-1, keepdims=True)
    y = xc * lax.rsqrt(var + LN_EPS) * g + b
    return y * jax.nn.sigmoid(y)


def _conv_shifted_copies(v_ref, v1_ref, v2_ref, *, L, C):
    row = lax.broadcasted_iota(jnp.int32, (C, D_CONV), 0)
    for s in range(L):
        x = v_ref[s]
        v1_ref[s] = jnp.where(row >= 1, pltpu.roll(x, 1, 0), 0.0)
        v2_ref[s] = jnp.where(row >= 2, pltpu.roll(x, 2, 0), 0.0)


def _conv_rows(srcs, w_ref, b_ref, g_ref, bb_ref, o_ref, r0, out_r0, *, L, rc):
    sub8 = rc // SUBLANES
    for s in range(L):
        acc = jnp.broadcast_to(b_ref[...][None], (sub8, SUBLANES, D_CONV))
        for d in range(CONV_WIDTH):
            blk = (s - d) % L
            shift = (d - s + L - 1) // L if d > s else 0
            k = CONV_WIDTH - 1 - d
            src = srcs[shift][blk, pl.ds(r0, rc), :].reshape(sub8, SUBLANES, D_CONV)
            acc = acc + w_ref[k][None] * src
        y = _ln_silu(acc.reshape(rc, D_CONV), g_ref[...], bb_ref[...])
        o_ref[s, pl.ds(out_r0, rc), :] = y.astype(o_ref.dtype)


def _conv_sample_kernel(v_ref, st_ref, w_ref, b_ref, g_ref, bb_ref, o_ref, new_ref, *, L, H):
    def ext(i):
        return st_ref[i] if i < H else v_ref[i - H]

    for t in range(L):
        acc = jnp.broadcast_to(b_ref[...], v_ref.shape[1:])
        for k in range(CONV_WIDTH):
            acc = acc + w_ref[k:k + 1, :] * ext(t + k)
        o_ref[t] = _ln_silu(acc, g_ref[...], bb_ref[...]).astype(o_ref.dtype)
    for i in range(H):
        new_ref[i] = ext(i + L)


def _conv_sample(v, state_t, w_dw, b_dw, ln_g, ln_b):
    L, R, _ = v.shape
    H = CONV_WIDTH - 1
    rt = 32
    rows = lambda n: pl.BlockSpec((n, rt, D_CONV), lambda i: (0, i, 0))
    return pl.pallas_call(
        functools.partial(_conv_sample_kernel, L=L, H=H),
        grid=(R // rt,),
        in_specs=[
            rows(L),
            rows(H),
            _const((CONV_WIDTH, D_CONV)),
            _const((1, D_CONV)),
            _const((1, D_CONV)),
            _const((1, D_CONV)),
        ],
        out_specs=[rows(L), rows(H)],
        out_shape=[
            jax.ShapeDtypeStruct((L, R, D_CONV), BF16),
            jax.ShapeDtypeStruct((H, R, D_CONV), F32),
        ],
        compiler_params=_params(1),
        name="conv_sample",
    )(v, state_t, w_dw, b_dw, ln_g, ln_b)


def _expand_block_diag(d):
    tiled = jnp.concatenate([d] * GROUPS_PER_BLOCK, axis=-1)
    r = lax.broadcasted_iota(jnp.int32, tiled.shape, 0) // P_GROUP
    c = lax.broadcasted_iota(jnp.int32, tiled.shape, 1) // N_STATE
    return jnp.where(r == c, tiled, jnp.zeros_like(tiled))


def _ssm_kernel(u_ref, wbc_ref, wcc_ref, a_ref, d_ref, h0_ref, y_ref, hl_ref,
                w_scr, wb_scr, wct_scr, s_scr, hp_scr, tap_scr, *, L, C, bt):
    per_tile = MXU_N // LANES

    @pl.when(pl.program_id(1) == 0)
    def _():
        for s in range(L):
            rows = slice(s * LANES, (s + 1) * LANES)
            for a in range(2):
                cols = slice(a * STATE_HALF, (a + 1) * STATE_HALF)
                wb_scr[rows, cols] = _expand_block_diag(wbc_ref[s, a])
                wct_scr[rows, cols] = _expand_block_diag(wcc_ref[s + 1, a])
        bbar = wb_scr[(L - 1) * LANES:L * LANES, :]
        nt = (((1,), (1,)), ((), ()))
        c0 = jnp.concatenate([_expand_block_diag(wcc_ref[0, a]) for a in range(2)], axis=-1)
        tap_scr[0] = lax.dot_general(bbar, c0, nt, preferred_element_type=F32).astype(BF16)
        for t in range(1, L):
            tap_scr[t] = lax.dot_general(bbar, wct_scr[(t - 1) * LANES:t * LANES, :], nt,
                                         preferred_element_type=F32).astype(BF16)
        for sp in range(L):
            for s in range(L):
                rows = slice(s * LANES, (s + 1) * LANES)
                cols = slice(sp * LANES, (sp + 1) * LANES)
                if s <= sp:
                    w_scr[rows, cols] = tap_scr[sp - s]
                elif s // per_tile == sp // per_tile:
                    w_scr[rows, cols] = jnp.zeros((LANES, LANES), BF16)

    nk = STATE_W // LANES
    half = nk // 2
    rows = bt * C
    inc_all = jnp.dot(u_ref[...], wb_scr[...], preferred_element_type=F32)
    for k in range(half):
        s_scr[k, 0:rows, :] = inc_all[:, k * LANES:(k + 1) * LANES]
        s_scr[k, rows:2 * rows, :] = inc_all[:, (half + k) * LANES:(half + k + 1) * LANES]

    d = d_ref[...]
    for n in range(L // per_tile):
        k = (n + 1) * MXU_N
        y = jnp.dot(u_ref[:, :k], w_scr[:k, n * MXU_N:(n + 1) * MXU_N], preferred_element_type=F32)
        for h in range(per_tile):
            s = n * per_tile + h
            us = u_ref[:, s * LANES:(s + 1) * LANES].astype(F32)
            y_ref[s] = y[:, h * LANES:(h + 1) * LANES] + d * us

    h0 = h0_ref[...]
    a = a_ref[...]

    def piece(x, k):
        return x[:, k * LANES:(k + 1) * LANES]

    def swap_parts(x):
        if 2 * bt == SUBLANES:
            return pltpu.roll(x, bt, 0)
        return jnp.concatenate([x[bt:], x[:bt]], axis=0)

    im_rows = lax.broadcasted_iota(jnp.int32, (2 * bt, LANES), 0) >= bt
    a_same = [jnp.broadcast_to(piece(a[0:1], k), (2 * bt, LANES)) for k in range(half)]
    a_cross = [jnp.where(im_rows, piece(a[1:2], k), -piece(a[1:2], k)) for k in range(half)]
    state = [jnp.concatenate([piece(h0, k), piece(h0, half + k)], axis=0) for k in range(half)]
    for c in range(C):
        sel = pl.ds(c, 2 * bt, stride=C) if C > 1 else pl.ds(0, 2 * bt)
        for k in range(half):
            hp_scr[k, sel, :] = state[k]
            state[k] = a_same[k] * state[k] + a_cross[k] * swap_parts(state[k]) + s_scr[k, sel, :]
    for k in range(half):
        hl_ref[:, k * LANES:(k + 1) * LANES] = state[k][:bt]
        hl_ref[:, (half + k) * LANES:(half + k + 1) * LANES] = state[k][bt:]

    hp = jnp.concatenate([hp_scr[k, 0:rows, :] for k in range(half)]
                         + [hp_scr[k, rows:2 * rows, :] for k in range(half)], axis=-1).astype(BF16)
    for n in range(L // per_tile):
        y = lax.dot_general(hp, wct_scr[n * MXU_N:(n + 1) * MXU_N, :], (((1,), (1,)), ((), ())),
                            preferred_element_type=F32)
        for h in range(per_tile):
            s = n * per_tile + h
            y_ref[s] += y[:, h * LANES:(h + 1) * LANES]


def _ssm(u4, wbc, wcc, a_pow, d_skip, h0, *, L, C, bt):
    R = u4.shape[1]
    rows = bt * C
    nb = R // rows
    lw = L * LANES
    in_specs = [
        pl.BlockSpec((None, rows, lw), lambda j, b: (j, b, 0)),
        pl.BlockSpec((None, L, 2, LANES, N_STATE), lambda j, b: (j, 0, 0, 0, 0)),
        pl.BlockSpec((None, L + 1, 2, LANES, N_STATE), lambda j, b: (j, 0, 0, 0, 0)),
        pl.BlockSpec((None, 2, STATE_HALF), lambda j, b: (j, 0, 0)),
        pl.BlockSpec((None, 1, LANES), lambda j, b: (j, 0, 0)),
        pl.BlockSpec((None, None, bt, STATE_W), lambda j, b: (j, b, 0, 0)),
    ]
    out_specs = [
        pl.BlockSpec((L, rows, LANES), lambda j, b: (0, b, j)),
        pl.BlockSpec((None, None, bt, STATE_W), lambda j, b: (j, b, 0, 0)),
    ]
    out_shape = [
        jax.ShapeDtypeStruct((L, R, D_SSM), F32),
        jax.ShapeDtypeStruct((N_LANE_BLOCKS, nb, bt, STATE_W), F32),
    ]
    scratch = [
        pltpu.VMEM((lw, lw), BF16),
        pltpu.VMEM((lw, STATE_W), BF16),
        pltpu.VMEM((lw, STATE_W), BF16),
        pltpu.VMEM((STATE_HALF // LANES, 2 * rows, LANES), F32),
        pltpu.VMEM((STATE_HALF // LANES, 2 * rows, LANES), F32),
        pltpu.VMEM((L, LANES, LANES), BF16),
    ]
    return pl.pallas_call(
        functools.partial(_ssm_kernel, L=L, C=C, bt=bt),
        grid=(N_LANE_BLOCKS, nb),
        in_specs=in_specs,
        out_specs=out_specs,
        out_shape=out_shape,
        scratch_shapes=scratch,
        compiler_params=_params(2),
        name="ssm",
    )(u4, wbc, wcc, a_pow, d_skip, h0)


def _ssm_weights(lam_re, lam_im, log_dt, b_re, b_im, c_re, c_im, L):
    dt = jnp.exp(log_dt)[:, None]
    zr = lam_re * dt
    zi = lam_im * dt
    n_pow = jnp.arange(L + 1, dtype=F32)[:, None, None]
    mag = jnp.exp(zr[None] * n_pow)
    pr = mag * jnp.cos(zi[None] * n_pow)
    pi = mag * jnp.sin(zi[None] * n_pow)
    a1r, a1i = pr[1], pi[1]
    den = lam_re * lam_re + lam_im * lam_im
    qr = ((a1r - 1.0) * lam_re + a1i * lam_im) / den
    qi = (a1i * lam_re - (a1r - 1.0) * lam_im) / den
    bbr = qr[:, :, None] * b_re - qi[:, :, None] * b_im
    bbi = qr[:, :, None] * b_im + qi[:, :, None] * b_re

    nj, g8 = N_LANE_BLOCKS, GROUPS_PER_BLOCK

    n_rev = (L - 1.0) - jnp.arange(L, dtype=F32)[:, None, None]
    mag_rev = jnp.exp(zr[None] * n_rev)
    rev = mag_rev * jnp.cos(zi[None] * n_rev), mag_rev * jnp.sin(zi[None] * n_rev)
    er = rev[0][:, :, :, None] * bbr[None] - rev[1][:, :, :, None] * bbi[None]
    ei = rev[0][:, :, :, None] * bbi[None] + rev[1][:, :, :, None] * bbr[None]
    e = jnp.stack([er, ei], 1).reshape(L, 2, nj, g8, N_STATE, P_GROUP)
    wbc = jnp.transpose(e, (2, 0, 1, 3, 5, 4)).reshape(nj, L, 2, LANES, N_STATE)

    mr = c_re[None] * pr[:, :, None, :] - c_im[None] * pi[:, :, None, :]
    mi = c_re[None] * pi[:, :, None, :] + c_im[None] * pr[:, :, None, :]
    m = jnp.stack([mr, -mi], 1).reshape(L + 1, 2, nj, g8, P_GROUP, N_STATE)
    wcc = jnp.transpose(m, (2, 0, 1, 3, 4, 5)).reshape(nj, L + 1, 2, LANES, N_STATE)

    def a_pow(n):
        return jnp.stack([pr[n].reshape(nj, STATE_HALF), pi[n].reshape(nj, STATE_HALF)], 1)

    return wbc.astype(BF16), wcc.astype(BF16), a_pow


def _ssm_weights_prefix(full, L_full, L):
    wbc, wcc, a_pow = full
    return wbc[:, L_full - L:], wcc[:, :L + 1], a_pow(L)


def _pack_state(re, im):
    b = re.shape[0]
    r = re.reshape(b, N_LANE_BLOCKS, STATE_HALF)
    i = im.reshape(b, N_LANE_BLOCKS, STATE_HALF)
    return jnp.transpose(jnp.concatenate([r, i], -1), (1, 0, 2))


def _unpack_state(h):
    b = h.shape[1]
    h = jnp.transpose(h, (1, 0, 2))
    re = h[:, :, :STATE_HALF].reshape(b, N_GROUPS, N_STATE)
    im = h[:, :, STATE_HALF:].reshape(b, N_GROUPS, N_STATE)
    return re, im


def _mixout_kernel(*refs, L, ct, sub, conv_c):
    if conv_c:
        (v_ref, cw_ref, cb_ref, lg_ref, lb_ref, y_ref, x_ref, wglu_ref, wout_ref, g_ref, o_ref,
         wglu_scr, wout_scr, v1_scr, v2_scr, c_ref) = refs
    else:
        c_ref, y_ref, x_ref, wglu_ref, wout_ref, g_ref, o_ref, wglu_scr, wout_scr = refs
    _cast_weights_once(2, (wglu_ref, wglu_scr), (wout_ref, wout_scr))
    tiles = _sub_tiles(ct, sub)
    gy = [jax.nn.gelu(y_ref[:, c0:c0 + n, :].reshape(L * n, D_SSM)) for c0, n in tiles]
    gate = [jnp.dot(g.astype(BF16), wglu_scr[...], preferred_element_type=F32) for g in gy]
    if conv_c:
        step = pl.program_id(1)

        @pl.when(step == 0)
        def _():
            _conv_shifted_copies(v_ref, v1_scr, v2_scr, L=L, C=conv_c)

        base = pl.multiple_of(step * ct, ct)
        for c0, n in tiles:
            _conv_rows((v_ref, v1_scr, v2_scr), cw_ref, cb_ref, lg_ref, lb_ref, c_ref, base + c0, c0, L=L, rc=n)
    m = [jnp.dot(c_ref[:, c0:c0 + n, :].reshape(L * n, D_CONV), wout_scr[0:D_CONV, :],
                 preferred_element_type=F32) for c0, n in tiles]
    sg = [g * jax.nn.sigmoid(t) for g, t in zip(gy, gate)]
    m = [a + jnp.dot(s.astype(BF16), wout_scr[D_CONV:, :], preferred_element_type=F32) for a, s in zip(m, sg)]
    for (c0, n), mm in zip(tiles, m):
        x = _gather_rows(x_ref, L, c0, n)
        o_ref[:, c0:c0 + n, :] = (x + _rms(mm, g_ref[...])).reshape(L, n, D_MODEL)


def _mixout(c, y, x4, w_glu, w_out, g1, *, L, ct, sub, conv=None):
    bq, cc = x4.shape[:2]
    nc = cc // ct
    R = bq * cc
    in_specs = [
        _tp_spec(ct, L, D_SSM, nc),
        _nat_spec(ct, L),
        _const((D_SSM, D_SSM)),
        _const((D_CONV + D_SSM, D_MODEL)),
        _const((1, D_MODEL)),
    ]
    args = [y, x4, w_glu, w_out, g1]
    scratch = [pltpu.VMEM(w_glu.shape, BF16), pltpu.VMEM(w_out.shape, BF16)]
    if conv is None:
        in_specs = [_tp_spec(ct, L, D_CONV, nc)] + in_specs
        args = [c] + args
    else:
        v, w_dw, b_dw, ln_g, ln_b = conv
        in_specs = [
            pl.BlockSpec((L, cc, D_CONV), lambda b, i: (0, b, 0)),
            _const((CONV_WIDTH, SUBLANES, D_CONV)),
            _const((SUBLANES, D_CONV)),
            _const((1, D_CONV)),
            _const((1, D_CONV)),
        ] + in_specs
        args = [v, jnp.broadcast_to(w_dw[:, None, :], (CONV_WIDTH, SUBLANES, D_CONV)),
                jnp.broadcast_to(b_dw, (SUBLANES, D_CONV)), ln_g, ln_b] + args
        scratch += [pltpu.VMEM((L, cc, D_CONV), F32), pltpu.VMEM((L, cc, D_CONV), F32),
                    pltpu.VMEM((L, ct, D_CONV), BF16)]
    return pl.pallas_call(
        functools.partial(_mixout_kernel, L=L, ct=ct, sub=sub, conv_c=0 if conv is None else cc),
        grid=(bq, nc),
        in_specs=in_specs,
        out_specs=_tp_spec(ct, L, D_MODEL, nc),
        out_shape=jax.ShapeDtypeStruct((L, R, D_MODEL), F32),
        scratch_shapes=scratch,
        compiler_params=_params(2),
        name="mixout",
    )(*args)


def _memkv_kernel(m_ref, g_ref, wk_ref, wv_ref, k5_ref, v5_ref, kb_ref, vb_ref, wk_scr, wv_scr, *, nb):
    _cast_weights_once(1, (wk_ref, wk_scr), (wv_ref, wv_scr))
    ms = [_rms(m_ref[b], g_ref[...]).astype(BF16) for b in range(nb)]
    for w_ref, o5_ref, ob_ref in ((wk_scr, k5_ref, kb_ref), (wv_scr, v5_ref, vb_ref)):
        ps = [jnp.dot(m, w_ref[...], preferred_element_type=F32) for m in ms]
        for b, p in enumerate(ps):
            ob_ref[b] = p.astype(BF16)
            for hd in range(N_HEADS):
                o5_ref[b, :, hd, :] = p[:, hd * HEAD_DIM:(hd + 1) * HEAD_DIM]


def _memkv(mem, g_mem, w_k, w_v):
    bq = mem.shape[0]
    nb = 2
    out5 = pl.BlockSpec((nb, N_MEM, N_HEADS, HEAD_DIM), lambda i: (i, 0, 0, 0))
    outb = pl.BlockSpec((nb, N_MEM, D_MODEL), lambda i: (i, 0, 0))
    return pl.pallas_call(
        functools.partial(_memkv_kernel, nb=nb),
        grid=(bq // nb,),
        in_specs=[
            pl.BlockSpec((nb, N_MEM, D_MODEL), lambda i: (i, 0, 0)),
            _const((1, D_MODEL)),
            _const((D_MODEL, D_MODEL)),
            _const((D_MODEL, D_MODEL)),
        ],
        out_specs=[out5, out5, outb, outb],
        out_shape=[jax.ShapeDtypeStruct((bq, N_MEM, N_HEADS, HEAD_DIM), F32)] * 2
        + [jax.ShapeDtypeStruct((bq, N_MEM, D_MODEL), BF16)] * 2,
        scratch_shapes=[pltpu.VMEM(w_k.shape, BF16), pltpu.VMEM(w_v.shape, BF16)],
        compiler_params=_params(1),
        name="memkv",
    )(mem, g_mem, w_k, w_v)


def _softmax_rows(s):
    s = s - jnp.max(s, axis=-1, keepdims=True)
    e = jnp.exp(s)
    return e / jnp.sum(e, axis=-1, keepdims=True)


def _attn_kernel(x_ref, k_ref, v_ref, wq32_ref, wo32_ref, gq_ref, go_ref, qs_ref, ck_ref, cv_ref, o_ref, os_ref,
                 wq_ref, wo_ref, *, L, ct, bb):
    _cast_weights_once(1, (wq32_ref, wq_ref), (wo32_ref, wo_ref))
    nt = (((1,), (1,)), ((), ()))
    scale = HEAD_DIM ** -0.5
    heads = [slice(hd * HEAD_DIM, (hd + 1) * HEAD_DIM) for hd in range(N_HEADS)]
    x = x_ref[...].reshape(L * ct, D_MODEL)
    kb, vb = k_ref[...], v_ref[...]

    q2 = qs_ref[...]
    rows = q2.shape[0]
    qs = jnp.concatenate([q2[:, sl] for sl in heads], axis=0).astype(BF16)
    nr = N_HEADS * rows
    nc = N_MEM * N_HEADS
    same_head = (lax.broadcasted_iota(jnp.int32, (nr, nc), 0) // rows
                 == lax.broadcasted_iota(jnp.int32, (nr, nc), 1) % N_HEADS)
    owner = lax.broadcasted_iota(jnp.int32, (nr, HEAD_DIM), 0) % bb

    h = _rms(x, gq_ref[...])
    ka = [ck_ref[b].reshape(nc, HEAD_DIM).astype(BF16) for b in range(bb)]
    q = jnp.dot(h.astype(BF16), wq_ref[...], preferred_element_type=F32).astype(BF16)
    va = [cv_ref[b].reshape(nc, HEAD_DIM).astype(BF16) for b in range(bb)]
    sc_p = [lax.dot_general(q[:, sl], kb[:, sl], nt, preferred_element_type=F32) for sl in heads]
    sc_s = [lax.dot_general(qs, kk, nt, preferred_element_type=F32) for kk in ka]
    p_p = [_softmax_rows(s * scale).astype(BF16) for s in sc_p]
    p_s = [_softmax_rows(jnp.where(same_head, s * scale, -1e30)).astype(BF16) for s in sc_s]
    o_p = jnp.concatenate([jnp.dot(pp, vb[:, sl], preferred_element_type=F32) for pp, sl in zip(p_p, heads)], axis=-1)
    o_s = [jnp.dot(pp, vv, preferred_element_type=F32) for pp, vv in zip(p_s, va)]
    a = jnp.dot(o_p.astype(BF16), wo_ref[...], preferred_element_type=F32)
    acc = jnp.zeros((nr, HEAD_DIM), F32)
    for b, ob in enumerate(o_s):
        acc = jnp.where(owner == b, ob, acc)
    o_ref[...] = (x + _rms(a, go_ref[...])).reshape(L, ct, D_MODEL)
    os_ref[...] = jnp.concatenate([acc[hd * rows:(hd + 1) * rows] for hd in range(N_HEADS)], axis=-1)


def _attn(x1, k, v, w_q, w_o, g2, g3, q_grp, cache_k, cache_v, *, C, ct):
    L, R, _ = x1.shape
    nc = C // ct
    n, rows_s, _ = q_grp.shape
    bb = cache_k.shape[1] // n
    assert n == R // ct and n * bb == cache_k.shape[1]
    kv_spec = pl.BlockSpec((None, bb, N_MEM, N_HEADS, HEAD_DIM), lambda i: (0, i, 0, 0, 0))
    grp_spec = pl.BlockSpec((None, rows_s, D_MODEL), lambda i: (i, 0, 0))
    return pl.pallas_call(
        functools.partial(_attn_kernel, L=L, ct=ct, bb=bb),
        grid=(n,),
        in_specs=[
            pl.BlockSpec((L, ct, D_MODEL), lambda i: (0, i, 0)),
            pl.BlockSpec((None, N_MEM, D_MODEL), lambda i: (i // nc, 0, 0)),
            pl.BlockSpec((None, N_MEM, D_MODEL), lambda i: (i // nc, 0, 0)),
            _const((D_MODEL, D_MODEL)),
            _const((D_MODEL, D_MODEL)),
            _const((1, D_MODEL)),
            _const((1, D_MODEL)),
            grp_spec, kv_spec, kv_spec,
        ],
        out_specs=[pl.BlockSpec((L, ct, D_MODEL), lambda i: (0, i, 0)), grp_spec],
        out_shape=[jax.ShapeDtypeStruct((L, R, D_MODEL), F32), jax.ShapeDtypeStruct(q_grp.shape, F32)],
        scratch_shapes=[pltpu.VMEM(w_q.shape, BF16), pltpu.VMEM(w_o.shape, BF16)],
        compiler_params=_params(1),
        name="attn",
    )(x1, k, v, w_q, w_o, g2, g3, q_grp, cache_k, cache_v)


def _qproj_kernel(x_ref, wq_ref, g_ref, q_ref):
    h = _rms(x_ref[...], g_ref[...])
    q_ref[...] = jnp.dot(h.astype(BF16), wq_ref[...].astype(BF16), preferred_element_type=F32)


def _qproj(x1f, w_q, g2):
    rows = x1f.shape[0]
    return pl.pallas_call(
        _qproj_kernel,
        grid=(1,),
        in_specs=[_const((rows, D_MODEL)), _const((D_MODEL, D_MODEL)), _const((1, D_MODEL))],
        out_specs=pl.BlockSpec((rows, D_MODEL), lambda i: (0, 0)),
        out_shape=jax.ShapeDtypeStruct((rows, D_MODEL), F32),
        compiler_params=_params(1),
        name="qproj_sample",
    )(x1f, w_q, g2)


def _oproj_kernel(o_ref, x_ref, wo_ref, g_ref, y_ref):
    a = jnp.dot(o_ref[...].astype(BF16), wo_ref[...].astype(BF16), preferred_element_type=F32)
    y_ref[...] = x_ref[...] + _rms(a, g_ref[...])


def _oproj(of, x1f, w_o, g3):
    rows = of.shape[0]
    return pl.pallas_call(
        _oproj_kernel,
        grid=(1,),
        in_specs=[_const((rows, D_MODEL)), _const((rows, D_MODEL)), _const((D_MODEL, D_MODEL)),
                  _const((1, D_MODEL))],
        out_specs=pl.BlockSpec((rows, D_MODEL), lambda i: (0, 0)),
        out_shape=jax.ShapeDtypeStruct((rows, D_MODEL), F32),
        compiler_params=_params(1),
        name="oproj_sample",
    )(of, x1f, w_o, g3)


FFN_WEIGHT_CHUNKS = 16
FFN_STAGE_SLOTS = 4
FFN_ROW_SPLIT = 2


def _stream_cast(w_hbm, w16_scr, stage, sem):
    slots, rpc = stage.shape[0], stage.shape[1]
    n = w_hbm.shape[0] // rpc

    def chunk(k):
        return pltpu.make_async_copy(w_hbm.at[pl.ds(k * rpc, rpc)], stage.at[k % slots], sem.at[k % slots])

    for k in range(min(slots - 1, n)):
        chunk(k).start()
    for k in range(n):
        if k + slots - 1 < n:
            chunk(k + slots - 1).start()
        chunk(k).wait()
        w16_scr[pl.ds(k * rpc, rpc), :] = stage[k % slots].astype(BF16)


def _ffn_tile(x_ref, o_ref, wg_scr, wu_scr, wd_scr, gi_ref, go_ref, *, L, ct):
    rows = L * ct
    x = x_ref[...].reshape(rows, D_MODEL)
    rs = rows // FFN_ROW_SPLIT
    xs = [x[i * rs:(i + 1) * rs] for i in range(FFN_ROW_SPLIT)]
    h = [_rms(xx, gi_ref[...]).astype(BF16) for xx in xs]
    gate = [jnp.dot(hh, wg_scr[...], preferred_element_type=F32) for hh in h]
    up = [jnp.dot(hh, wu_scr[...], preferred_element_type=F32) for hh in h]
    act = [(g * jax.nn.sigmoid(g) * u).astype(BF16) for g, u in zip(gate, up)]
    dn = [jnp.dot(a, wd_scr[...], preferred_element_type=F32) for a in act]
    y = jnp.concatenate([xx + _rms(d, go_ref[...]) for xx, d in zip(xs, dn)], axis=0)
    for s in range(L):
        o_ref[:, s, :] = y[s * ct:(s + 1) * ct]


def _ffn_kernel(xp_ref, xs_ref, wg_hbm, wu_hbm, wd_hbm, gi_ref, go_ref, op_ref, os_ref,
                wg_scr, wu_scr, wd_scr, stage_in, stage_out, sem, *, n_prompt, Lp, ctp, Ls, cts):
    i = pl.program_id(0)

    @pl.when(i == 0)
    def _():
        _stream_cast(wg_hbm, wg_scr, stage_in, sem)
        _stream_cast(wu_hbm, wu_scr, stage_in, sem)
        _stream_cast(wd_hbm, wd_scr, stage_out, sem)

    @pl.when(i < n_prompt)
    def _():
        _ffn_tile(xp_ref, op_ref, wg_scr, wu_scr, wd_scr, gi_ref, go_ref, L=Lp, ct=ctp)

    @pl.when(i == n_prompt)
    def _():
        _ffn_tile(xs_ref, os_ref, wg_scr, wu_scr, wd_scr, gi_ref, go_ref, L=Ls, ct=cts)


def _ffn(xp, xs, w_gate, w_up, w_down, g4, g5, *, bq, C, ct):
    Lp = xp.shape[0]
    Ls, rs, _ = xs.shape
    nc = C // ct
    n = bq * nc
    tile = lambda i: jnp.minimum(i, n - 1)
    hbm = pl.BlockSpec(memory_space=pl.ANY)
    return pl.pallas_call(
        functools.partial(_ffn_kernel, n_prompt=n, Lp=Lp, ctp=ct, Ls=Ls, cts=rs),
        grid=(n + 1,),
        in_specs=[
            pl.BlockSpec((Lp, ct, D_MODEL), lambda i: (0, tile(i), 0)),
            _const((Ls, rs, D_MODEL)),
            hbm, hbm, hbm,
            _const((1, D_MODEL)),
            _const((1, D_MODEL)),
        ],
        out_specs=[
            pl.BlockSpec((None, ct, Lp, D_MODEL), lambda i: (tile(i) // nc, tile(i) % nc, 0, 0)),
            pl.BlockSpec((None, rs, Ls, D_MODEL), lambda i: (0, 0, 0, 0)),
        ],
        out_shape=[
            jax.ShapeDtypeStruct((bq, C, Lp, D_MODEL), F32),
            jax.ShapeDtypeStruct((1, rs, Ls, D_MODEL), F32),
        ],
        scratch_shapes=[
            pltpu.VMEM((D_MODEL, D_FF), BF16),
            pltpu.VMEM((D_MODEL, D_FF), BF16),
            pltpu.VMEM((D_FF, D_MODEL), BF16),
            pltpu.VMEM((FFN_STAGE_SLOTS, D_MODEL // FFN_WEIGHT_CHUNKS, D_FF), F32),
            pltpu.VMEM((FFN_STAGE_SLOTS, D_FF // FFN_WEIGHT_CHUNKS, D_MODEL), F32),
            pltpu.SemaphoreType.DMA((FFN_STAGE_SLOTS,)),
        ],
        compiler_params=_params(1),
        name="ffn",
    )(xp, xs, w_gate, w_up, w_down, g4, g5)


def _mixer(x4, conv_state_t, h0_packed, wts, *, ct, sub, sc, bt, sample):
    bq, C, L, _ = x4.shape
    R = bq * C
    g = [wts["norm_g"][i:i + 1] for i in range(2)]

    v, u4 = _inproj(x4, g[0], wts["w_in"], L=L, ct=ct, sub=sub)
    sw = wts["ssm_L%d" % L]
    nb = R // (bt * sc)
    ssm_args = (u4, sw[0], sw[1], sw[2], wts["d_skip"], h0_packed.reshape(N_LANE_BLOCKS, nb, bt, STATE_W))
    conv_args = (wts["w_dw"], wts["b_dw"], wts["ln_g"], wts["ln_b"])
    y, hl = _ssm(*ssm_args, L=L, C=sc, bt=bt)
    hl = hl.reshape(N_LANE_BLOCKS, nb * bt, STATE_W)
    out_args = (y, x4, wts["w_glu"], wts["w_out"], g[1])
    if sample:
        cact, conv_new = _conv_sample(v, conv_state_t, *conv_args)
        x1 = _mixout(cact, *out_args, L=L, ct=ct, sub=sub)
    else:
        x1 = _mixout(None, *out_args, L=L, ct=ct, sub=sub, conv=(v,) + conv_args)
        tail = v.reshape(L, bq, C, D_CONV)[:, :, C - 2:, :]
        tail = jnp.transpose(tail, (1, 2, 0, 3)).reshape(bq, 2 * L, D_CONV)
        conv_new = tail[:, 2 * L - (CONV_WIDTH - 1):, :]
    return x1, conv_new, hl


PROMPT_L = 16
PROMPT_SUB = 32
PROMPT_CT = 64


def kernel(x_prompt, x_sample, mem_prompt, cache_mem_k, cache_mem_v, state_conv, state_ssm_re, state_ssm_im,
           norm_g, mem_norm_g, w_in, w_dw, b_dw, ln_g, ln_b, lam_re, lam_im, log_dt, b_re, b_im, c_re, c_im,
           d_skip, w_glu, w_out, w_q, w_k, w_v, w_o, w_gate, w_up, w_down):
    depth = w_in.shape[0]
    bp, tp, _ = x_prompt.shape
    bs, ts, _ = x_sample.shape
    assert tp % (PROMPT_L * PROMPT_CT) == 0 and tp >= CONV_WIDTH - 1

    yp = x_prompt.reshape(bp, tp // PROMPT_L, PROMPT_L, D_MODEL)
    ys = x_sample.reshape(1, bs, ts, D_MODEL)
    outs = [[] for _ in range(8)]
    for l in range(depth):
        wts = {
            "norm_g": norm_g[l],
            "w_in": w_in[l],
            "w_dw": w_dw[l],
            "b_dw": b_dw[l][None],
            "ln_g": ln_g[l][None],
            "ln_b": ln_b[l][None],
            "d_skip": d_skip[l].reshape(N_LANE_BLOCKS, 1, LANES),
            "w_glu": w_glu[l],
            "w_out": w_out[l],
        }
        ssm_args = (lam_re[l], lam_im[l], log_dt[l], b_re[l], b_im[l], c_re[l], c_im[l])
        assert ts <= PROMPT_L
        full = _ssm_weights(*ssm_args, PROMPT_L)
        wts["ssm_L%d" % PROMPT_L] = full[:2] + (full[2](PROMPT_L),)
        wts["ssm_L%d" % ts] = _ssm_weights_prefix(full, PROMPT_L, ts)

        kp, vp, kp16, vp16 = _memkv(mem_prompt, mem_norm_g[l][None], w_k[l], w_v[l])
        h0p = jnp.zeros((N_LANE_BLOCKS, bp, STATE_W), F32)
        cp_chunks = tp // PROMPT_L
        xp, cp, hp = _mixer(yp, None, h0p, wts, ct=PROMPT_CT, sub=PROMPT_SUB, sc=cp_chunks, bt=4, sample=False)
        h0s = _pack_state(state_ssm_re[l], state_ssm_im[l])
        xs, cs, hs = _mixer(ys, jnp.transpose(state_conv[l], (1, 0, 2)), h0s, wts, ct=bs, sub=bs, sc=1, bt=bs,
                            sample=True)
        cs = jnp.transpose(cs, (1, 0, 2))

        g2, g3 = norm_g[l][2:3], norm_g[l][3:4]
        n_tiles = bp * cp_chunks // PROMPT_SUB
        bb = bs // n_tiles
        xs_flat = xs.reshape(ts * bs, D_MODEL)
        q = _qproj(xs_flat, w_q[l], g2).reshape(ts, n_tiles, bb, D_MODEL)
        q = jnp.transpose(q, (1, 0, 2, 3)).reshape(n_tiles, ts * bb, D_MODEL)
        xp, o = _attn(xp, kp16, vp16, w_q[l], w_o[l], g2, g3, q, cache_mem_k[l:l + 1], cache_mem_v[l:l + 1],
                      C=cp_chunks, ct=PROMPT_SUB)
        o = jnp.transpose(o.reshape(n_tiles, ts, bb, D_MODEL), (1, 0, 2, 3)).reshape(ts * bs, D_MODEL)
        xs = _oproj(o, xs_flat, w_o[l], g3).reshape(ts, bs, D_MODEL)

        yp, ys = _ffn(xp, xs, w_gate[l], w_up[l], w_down[l], norm_g[l][4:5], norm_g[l][5:6],
                      bq=bp, C=cp_chunks, ct=PROMPT_SUB)

        hp_re, hp_im = _unpack_state(hp)
        hs_re, hs_im = _unpack_state(hs)
        for lst, val in zip(outs, (kp, vp, cp, hp_re, hp_im, cs, hs_re, hs_im)):
            lst.append(val)
    return (yp.reshape(bp, tp, D_MODEL), ys.reshape(bs, ts, D_MODEL)) + tuple(jnp.stack(o) for o in outs)
```

```python
import functools

import jax
import jax.numpy as jnp
from jax import lax
from jax.experimental import pallas as pl
from jax.experimental.pallas import tpu as pltpu

F32 = jnp.float32
BF16 = jnp.bfloat16

D_MODEL = 1024
D_CONV = 512
D_SSM = 512
CONV_WIDTH = 31
N_GROUPS = 32
P_GROUP = 16
N_STATE = 64
N_MEM = 256
N_HEADS = 4
HEAD_DIM = 256
D_FF = 2816
RMS_EPS = 1e-6
LN_EPS = 1e-5

LANES = 128
SUBLANES = 8
GROUPS_PER_BLOCK = LANES // P_GROUP
N_LANE_BLOCKS = D_SSM // LANES
STATE_HALF = GROUPS_PER_BLOCK * N_STATE
STATE_W = 2 * STATE_HALF
MXU_N = 256
VMEM_LIMIT = 56 * 1024 * 1024


def _params(n_axes, vmem=VMEM_LIMIT):
    return pltpu.CompilerParams(dimension_semantics=("arbitrary",) * n_axes, vmem_limit_bytes=vmem)


def _const(shape):
    nd = len(shape)
    return pl.BlockSpec(shape, lambda *_: (0,) * nd, pipeline_mode=pl.Buffered(1))


def _rms(x, g):
    return x * lax.rsqrt(jnp.mean(x * x, axis=-1, keepdims=True) + RMS_EPS) * g


def _cast_weights_once(n_axes, *pairs):
    first = pl.program_id(0) == 0
    for a in range(1, n_axes):
        first = jnp.logical_and(first, pl.program_id(a) == 0)

    @pl.when(first)
    def _():
        for src, dst in pairs:
            dst[...] = src[...].astype(BF16)


def _gather_rows(x_ref, L, c0, n):
    return jnp.concatenate([x_ref[c0:c0 + n, s, :] for s in range(L)], axis=0)


def _sub_tiles(ct, sub):
    return [(c0, sub) for c0 in range(0, ct, sub)]


def _nat_spec(ct, L):
    return pl.BlockSpec((None, ct, L, D_MODEL), lambda b, i: (b, i, 0, 0))


def _tp_spec(ct, L, width, nc):
    return pl.BlockSpec((L, ct, width), lambda b, i: (0, b * nc + i, 0))


def _inproj_kernel(x_ref, g_ref, w_ref, v_ref, u_ref, w16_scr, *, L, ct, sub):
    _cast_weights_once(2, (w_ref, w16_scr))
    for c0, n in _sub_tiles(ct, sub):
        x = _gather_rows(x_ref, L, c0, n)
        h = _rms(x, g_ref[...])
        z = jnp.dot(h.astype(BF16), w16_scr[...], preferred_element_type=F32)
        a = z[:, :D_CONV]
        g = z[:, D_CONV:2 * D_CONV]
        u = z[:, 2 * D_CONV:].astype(BF16)
        v_ref[:, c0:c0 + n, :] = (a * jax.nn.sigmoid(g)).reshape(L, n, D_CONV)
        for s in range(L):
            for j in range(N_LANE_BLOCKS):
                u_ref[j, c0:c0 + n, s * LANES:(s + 1) * LANES] = u[s * n:(s + 1) * n, j * LANES:(j + 1) * LANES]


def _inproj(x4, g0, w_in, *, L, ct, sub):
    bq, c = x4.shape[:2]
    nc = c // ct
    R = bq * c
    return pl.pallas_call(
        functools.partial(_inproj_kernel, L=L, ct=ct, sub=sub),
        grid=(bq, nc),
        in_specs=[_nat_spec(ct, L), _const((1, D_MODEL)), _const((D_MODEL, 2 * D_CONV + D_SSM))],
        out_specs=[
            _tp_spec(ct, L, D_CONV, nc),
            pl.BlockSpec((N_LANE_BLOCKS, ct, L * LANES), lambda b, i: (0, b * nc + i, 0)),
        ],
        out_shape=[
            jax.ShapeDtypeStruct((L, R, D_CONV), F32),
            jax.ShapeDtypeStruct((N_LANE_BLOCKS, R, L * LANES), BF16),
        ],
        scratch_shapes=[pltpu.VMEM(w_in.shape, BF16)],
        compiler_params=_params(2),
        name="inproj",
    )(x4, g0, w_in)


def _ln_silu(acc, g, b):
    mu = jnp.mean(acc, axis=-1, keepdims=True)
    xc = acc - mu
    var = jnp.mean(xc * xc, axis=-1, keepdims=True)
    y = xc * lax.rsqrt(var + LN_EPS) * g + b
    return y * jax.nn.sigmoid(y)


def _conv_shifted_copies(v_ref, v1_ref, v2_ref, *, L, C):
    row = lax.broadcasted_iota(jnp.int32, (C, D_CONV), 0)
    for s in range(L):
        x = v_ref[s]
        v1_ref[s] = jnp.where(row >= 1, pltpu.roll(x, 1, 0), 0.0)
        v2_ref[s] = jnp.where(row >= 2, pltpu.roll(x, 2, 0), 0.0)


def _conv_rows(srcs, w_ref, b_ref, g_ref, bb_ref, o_ref, r0, out_r0, *, L, rc):
    sub8 = rc // SUBLANES
    for s in range(L):
        acc = jnp.broadcast_to(b_ref[...][None], (sub8, SUBLANES, D_CONV))
        for d in range(CONV_WIDTH):
            blk = (s - d) % L
            shift = (d - s + L - 1) // L if d > s else 0
            k = CONV_WIDTH - 1 - d
            src = srcs[shift][blk, pl.ds(r0, rc), :].reshape(sub8, SUBLANES, D_CONV)
            acc = acc + w_ref[k][None] * src
        y = _ln_silu(acc.reshape(rc, D_CONV), g_ref[...], bb_ref[...])
        o_ref[s, pl.ds(out_r0, rc), :] = y.astype(o_ref.dtype)


def _conv_sample_kernel(v_ref, st_ref, w_ref, b_ref, g_ref, bb_ref, o_ref, new_ref, *, L, H):
    def ext(i):
        return st_ref[i] if i < H else v_ref[i - H]

    for t in range(L):
        acc = jnp.broadcast_to(b_ref[...], v_ref.shape[1:])
        for k in range(CONV_WIDTH):
            acc = acc + w_ref[k:k + 1, :] * ext(t + k)
        o_ref[t] = _ln_silu(acc, g_ref[...], bb_ref[...]).astype(o_ref.dtype)
    for i in range(H):
        new_ref[i] = ext(i + L)


def _conv_sample(v, state_t, w_dw, b_dw, ln_g, ln_b):
    L, R, _ = v.shape
    H = CONV_WIDTH - 1
    rt = 32
    rows = lambda n: pl.BlockSpec((n, rt, D_CONV), lambda i: (0, i, 0))
    return pl.pallas_call(
        functools.partial(_conv_sample_kernel, L=L, H=H),
        grid=(R // rt,),
        in_specs=[
            rows(L),
            rows(H),
            _const((CONV_WIDTH, D_CONV)),
            _const((1, D_CONV)),
            _const((1, D_CONV)),
            _const((1, D_CONV)),
        ],
        out_specs=[rows(L), rows(H)],
        out_shape=[
            jax.ShapeDtypeStruct((L, R, D_CONV), BF16),
            jax.ShapeDtypeStruct((H, R, D_CONV), F32),
        ],
        compiler_params=_params(1),
        name="conv_sample",
    )(v, state_t, w_dw, b_dw, ln_g, ln_b)


def _expand_block_diag(d):
    tiled = jnp.concatenate([d] * GROUPS_PER_BLOCK, axis=-1)
    r = lax.broadcasted_iota(jnp.int32, tiled.shape, 0) // P_GROUP
    c = lax.broadcasted_iota(jnp.int32, tiled.shape, 1) // N_STATE
    return jnp.where(r == c, tiled, jnp.zeros_like(tiled))


def _ssm_kernel(*refs, L, C, bt, conv):
    if conv:
        (u_ref, wbc_ref, wcc_ref, a_ref, d_ref, h0_ref, v_ref, cw_ref, cb_ref, lg_ref, lb_ref, y_ref, hl_ref, c_ref,
         w_scr, wb_scr, wct_scr, s_scr, hp_scr, tap_scr, v1_scr, v2_scr) = refs
        _conv_shifted_copies(v_ref, v1_scr, v2_scr, L=L, C=C)
        for r0 in range(0, C, CONV_ROWS):
            _conv_rows((v_ref, v1_scr, v2_scr), cw_ref, cb_ref, lg_ref, lb_ref, c_ref, r0, r0, L=L, rc=CONV_ROWS)
    else:
        (u_ref, wbc_ref, wcc_ref, a_ref, d_ref, h0_ref, y_ref, hl_ref,
         w_scr, wb_scr, wct_scr, s_scr, hp_scr, tap_scr) = refs
    per_tile = MXU_N // LANES

    @pl.when(pl.program_id(1) == 0)
    def _():
        for s in range(L):
            rows = slice(s * LANES, (s + 1) * LANES)
            for a in range(2):
                cols = slice(a * STATE_HALF, (a + 1) * STATE_HALF)
                wb_scr[rows, cols] = _expand_block_diag(wbc_ref[s, a])
                wct_scr[rows, cols] = _expand_block_diag(wcc_ref[s + 1, a])
        bbar = wb_scr[(L - 1) * LANES:L * LANES, :]
        nt = (((1,), (1,)), ((), ()))
        c0 = jnp.concatenate([_expand_block_diag(wcc_ref[0, a]) for a in range(2)], axis=-1)
        tap_scr[0] = lax.dot_general(bbar, c0, nt, preferred_element_type=F32).astype(BF16)
        for t in range(1, L):
            tap_scr[t] = lax.dot_general(bbar, wct_scr[(t - 1) * LANES:t * LANES, :], nt,
                                         preferred_element_type=F32).astype(BF16)
        for sp in range(L):
            for s in range(L):
                rows = slice(s * LANES, (s + 1) * LANES)
                cols = slice(sp * LANES, (sp + 1) * LANES)
                if s <= sp:
                    w_scr[rows, cols] = tap_scr[sp - s]
                elif s // per_tile == sp // per_tile:
                    w_scr[rows, cols] = jnp.zeros((LANES, LANES), BF16)

    nk = STATE_W // LANES
    half = nk // 2
    rows = bt * C
    inc_all = jnp.dot(u_ref[...], wb_scr[...], preferred_element_type=F32)
    for k in range(half):
        s_scr[k, 0:rows, :] = inc_all[:, k * LANES:(k + 1) * LANES]
        s_scr[k, rows:2 * rows, :] = inc_all[:, (half + k) * LANES:(half + k + 1) * LANES]

    d = d_ref[...]
    for n in range(L // per_tile):
        k = (n + 1) * MXU_N
        y = jnp.dot(u_ref[:, :k], w_scr[:k, n * MXU_N:(n + 1) * MXU_N], preferred_element_type=F32)
        for h in range(per_tile):
            s = n * per_tile + h
            us = u_ref[:, s * LANES:(s + 1) * LANES].astype(F32)
            y_ref[s] = y[:, h * LANES:(h + 1) * LANES] + d * us

    h0 = h0_ref[...]
    a = a_ref[...]

    def piece(x, k):
        return x[:, k * LANES:(k + 1) * LANES]

    def swap_parts(x):
        if 2 * bt == SUBLANES:
            return pltpu.roll(x, bt, 0)
        return jnp.concatenate([x[bt:], x[:bt]], axis=0)

    im_rows = lax.broadcasted_iota(jnp.int32, (2 * bt, LANES), 0) >= bt
    a_same = [jnp.broadcast_to(piece(a[0:1], k), (2 * bt, LANES)) for k in range(half)]
    a_cross = [jnp.where(im_rows, piece(a[1:2], k), -piece(a[1:2], k)) for k in range(half)]
    state = [jnp.concatenate([piece(h0, k), piece(h0, half + k)], axis=0) for k in range(half)]
    for c in range(C):
        sel = pl.ds(c, 2 * bt, stride=C) if C > 1 else pl.ds(0, 2 * bt)
        for k in range(half):
            hp_scr[k, sel, :] = state[k]
            state[k] = a_same[k] * state[k] + a_cross[k] * swap_parts(state[k]) + s_scr[k, sel, :]
    for k in range(half):
        hl_ref[:, k * LANES:(k + 1) * LANES] = state[k][:bt]
        hl_ref[:, (half + k) * LANES:(half + k + 1) * LANES] = state[k][bt:]

    hp = jnp.concatenate([hp_scr[k, 0:rows, :] for k in range(half)]
                         + [hp_scr[k, rows:2 * rows, :] for k in range(half)], axis=-1).astype(BF16)
    for n in range(L // per_tile):
        y = lax.dot_general(hp, wct_scr[n * MXU_N:(n + 1) * MXU_N, :], (((1,), (1,)), ((), ())),
                            preferred_element_type=F32)
        for h in range(per_tile):
            s = n * per_tile + h
            y_ref[s] += y[:, h * LANES:(h + 1) * LANES]


CONV_ROWS = 32


def _ssm(u4, wbc, wcc, a_pow, d_skip, h0, *, L, C, bt, conv=None):
    R = u4.shape[1]
    rows = bt * C
    nb = R // rows
    lw = L * LANES
    in_specs = [
        pl.BlockSpec((None, rows, lw), lambda j, b: (j, b, 0)),
        pl.BlockSpec((None, L, 2, LANES, N_STATE), lambda j, b: (j, 0, 0, 0, 0)),
        pl.BlockSpec((None, L + 1, 2, LANES, N_STATE), lambda j, b: (j, 0, 0, 0, 0)),
        pl.BlockSpec((None, 2, STATE_HALF), lambda j, b: (j, 0, 0)),
        pl.BlockSpec((None, 1, LANES), lambda j, b: (j, 0, 0)),
        pl.BlockSpec((None, None, bt, STATE_W), lambda j, b: (j, b, 0, 0)),
    ]
    out_specs = [
        pl.BlockSpec((L, rows, LANES), lambda j, b: (0, b, j)),
        pl.BlockSpec((None, None, bt, STATE_W), lambda j, b: (j, b, 0, 0)),
    ]
    out_shape = [
        jax.ShapeDtypeStruct((L, R, D_SSM), F32),
        jax.ShapeDtypeStruct((N_LANE_BLOCKS, nb, bt, STATE_W), F32),
    ]
    scratch = [
        pltpu.VMEM((lw, lw), BF16),
        pltpu.VMEM((lw, STATE_W), BF16),
        pltpu.VMEM((lw, STATE_W), BF16),
        pltpu.VMEM((STATE_HALF // LANES, 2 * rows, LANES), F32),
        pltpu.VMEM((STATE_HALF // LANES, 2 * rows, LANES), F32),
        pltpu.VMEM((L, LANES, LANES), BF16),
    ]
    args = [u4, wbc, wcc, a_pow, d_skip, h0]
    if conv is not None:
        v, w_dw, b_dw, ln_g, ln_b = conv
        assert N_LANE_BLOCKS * nb == R // C
        seq = lambda j, b: (0, j * nb + b, 0)
        in_specs += [
            pl.BlockSpec((L, C, D_CONV), seq, pipeline_mode=pl.Buffered(1)),
            _const((CONV_WIDTH, SUBLANES, D_CONV)),
            _const((SUBLANES, D_CONV)),
            _const((1, D_CONV)),
            _const((1, D_CONV)),
        ]
        out_specs.append(pl.BlockSpec((L, C, D_CONV), seq))
        out_shape.append(jax.ShapeDtypeStruct((L, R, D_CONV), BF16))
        scratch += [pltpu.VMEM((L, C, D_CONV), F32), pltpu.VMEM((L, C, D_CONV), F32)]
        args += [v, jnp.broadcast_to(w_dw[:, None, :], (CONV_WIDTH, SUBLANES, D_CONV)),
                 jnp.broadcast_to(b_dw, (SUBLANES, D_CONV)), ln_g, ln_b]
    return pl.pallas_call(
        functools.partial(_ssm_kernel, L=L, C=C, bt=bt, conv=conv is not None),
        grid=(N_LANE_BLOCKS, nb),
        in_specs=in_specs,
        out_specs=out_specs,
        out_shape=out_shape,
        scratch_shapes=scratch,
        compiler_params=_params(2),
        name="ssm",
    )(*args)


def _ssm_weights(lam_re, lam_im, log_dt, b_re, b_im, c_re, c_im, L):
    dt = jnp.exp(log_dt)[:, None]
    zr = lam_re * dt
    zi = lam_im * dt
    n_pow = jnp.arange(L + 1, dtype=F32)[:, None, None]
    mag = jnp.exp(zr[None] * n_pow)
    pr = mag * jnp.cos(zi[None] * n_pow)
    pi = mag * jnp.sin(zi[None] * n_pow)
    a1r, a1i = pr[1], pi[1]
    den = lam_re * lam_re + lam_im * lam_im
    qr = ((a1r - 1.0) * lam_re + a1i * lam_im) / den
    qi = (a1i * lam_re - (a1r - 1.0) * lam_im) / den
    bbr = qr[:, :, None] * b_re - qi[:, :, None] * b_im
    bbi = qr[:, :, None] * b_im + qi[:, :, None] * b_re

    nj, g8 = N_LANE_BLOCKS, GROUPS_PER_BLOCK

    n_rev = (L - 1.0) - jnp.arange(L, dtype=F32)[:, None, None]
    mag_rev = jnp.exp(zr[None] * n_rev)
    rev = mag_rev * jnp.cos(zi[None] * n_rev), mag_rev * jnp.sin(zi[None] * n_rev)
    er = rev[0][:, :, :, None] * bbr[None] - rev[1][:, :, :, None] * bbi[None]
    ei = rev[0][:, :, :, None] * bbi[None] + rev[1][:, :, :, None] * bbr[None]
    e = jnp.stack([er, ei], 1).reshape(L, 2, nj, g8, N_STATE, P_GROUP)
    wbc = jnp.transpose(e, (2, 0, 1, 3, 5, 4)).reshape(nj, L, 2, LANES, N_STATE)

    mr = c_re[None] * pr[:, :, None, :] - c_im[None] * pi[:, :, None, :]
    mi = c_re[None] * pi[:, :, None, :] + c_im[None] * pr[:, :, None, :]
    m = jnp.stack([mr, -mi], 1).reshape(L + 1, 2, nj, g8, P_GROUP, N_STATE)
    wcc = jnp.transpose(m, (2, 0, 1, 3, 4, 5)).reshape(nj, L + 1, 2, LANES, N_STATE)

    def a_pow(n):
        return jnp.stack([pr[n].reshape(nj, STATE_HALF), pi[n].reshape(nj, STATE_HALF)], 1)

    return wbc.astype(BF16), wcc.astype(BF16), a_pow


def _ssm_weights_prefix(full, L_full, L):
    wbc, wcc, a_pow = full
    return wbc[:, L_full - L:], wcc[:, :L + 1], a_pow(L)


def _pack_state(re, im):
    b = re.shape[0]
    r = re.reshape(b, N_LANE_BLOCKS, STATE_HALF)
    i = im.reshape(b, N_LANE_BLOCKS, STATE_HALF)
    return jnp.transpose(jnp.concatenate([r, i], -1), (1, 0, 2))


def _unpack_state(h):
    b = h.shape[1]
    h = jnp.transpose(h, (1, 0, 2))
    re = h[:, :, :STATE_HALF].reshape(b, N_GROUPS, N_STATE)
    im = h[:, :, STATE_HALF:].reshape(b, N_GROUPS, N_STATE)
    return re, im


def _mixout_kernel(*refs, L, ct, sub, conv_c):
    if conv_c:
        (v_ref, cw_ref, cb_ref, lg_ref, lb_ref, y_ref, x_ref, wglu_ref, wout_ref, g_ref, o_ref,
         wglu_scr, wout_scr, v1_scr, v2_scr, c_ref) = refs
    else:
        c_ref, y_ref, x_ref, wglu_ref, wout_ref, g_ref, o_ref, wglu_scr, wout_scr = refs
    _cast_weights_once(2, (wglu_ref, wglu_scr), (wout_ref, wout_scr))
    tiles = _sub_tiles(ct, sub)
    gy = [jax.nn.gelu(y_ref[:, c0:c0 + n, :].reshape(L * n, D_SSM)) for c0, n in tiles]
    gate = [jnp.dot(g.astype(BF16), wglu_scr[...], preferred_element_type=F32) for g in gy]
    if conv_c:
        step = pl.program_id(1)

        @pl.when(step == 0)
        def _():
            _conv_shifted_copies(v_ref, v1_scr, v2_scr, L=L, C=conv_c)

        base = pl.multiple_of(step * ct, ct)
        for c0, n in tiles:
            _conv_rows((v_ref, v1_scr, v2_scr), cw_ref, cb_ref, lg_ref, lb_ref, c_ref, base + c0, c0, L=L, rc=n)
    m = [jnp.dot(c_ref[:, c0:c0 + n, :].reshape(L * n, D_CONV), wout_scr[0:D_CONV, :],
                 preferred_element_type=F32) for c0, n in tiles]
    sg = [g * jax.nn.sigmoid(t) for g, t in zip(gy, gate)]
    m = [a + jnp.dot(s.astype(BF16), wout_scr[D_CONV:, :], preferred_element_type=F32) for a, s in zip(m, sg)]
    for (c0, n), mm in zip(tiles, m):
        x = _gather_rows(x_ref, L, c0, n)
        o_ref[:, c0:c0 + n, :] = (x + _rms(mm, g_ref[...])).reshape(L, n, D_MODEL)


def _mixout(c, y, x4, w_glu, w_out, g1, *, L, ct, sub, conv=None):
    bq, cc = x4.shape[:2]
    nc = cc // ct
    R = bq * cc
    in_specs = [
        _tp_spec(ct, L, D_SSM, nc),
        _nat_spec(ct, L),
        _const((D_SSM, D_SSM)),
        _const((D_CONV + D_SSM, D_MODEL)),
        _const((1, D_MODEL)),
    ]
    args = [y, x4, w_glu, w_out, g1]
    scratch = [pltpu.VMEM(w_glu.shape, BF16), pltpu.VMEM(w_out.shape, BF16)]
    if conv is None:
        in_specs = [_tp_spec(ct, L, D_CONV, nc)] + in_specs
        args = [c] + args
    else:
        v, w_dw, b_dw, ln_g, ln_b = conv
        in_specs = [
            pl.BlockSpec((L, cc, D_CONV), lambda b, i: (0, b, 0)),
            _const((CONV_WIDTH, SUBLANES, D_CONV)),
            _const((SUBLANES, D_CONV)),
            _const((1, D_CONV)),
            _const((1, D_CONV)),
        ] + in_specs
        args = [v, jnp.broadcast_to(w_dw[:, None, :], (CONV_WIDTH, SUBLANES, D_CONV)),
                jnp.broadcast_to(b_dw, (SUBLANES, D_CONV)), ln_g, ln_b] + args
        scratch += [pltpu.VMEM((L, cc, D_CONV), F32), pltpu.VMEM((L, cc, D_CONV), F32),
                    pltpu.VMEM((L, ct, D_CONV), BF16)]
    return pl.pallas_call(
        functools.partial(_mixout_kernel, L=L, ct=ct, sub=sub, conv_c=0 if conv is None else cc),
        grid=(bq, nc),
        in_specs=in_specs,
        out_specs=_tp_spec(ct, L, D_MODEL, nc),
        out_shape=jax.ShapeDtypeStruct((L, R, D_MODEL), F32),
        scratch_shapes=scratch,
        compiler_params=_params(2),
        name="mixout",
    )(*args)


def _memkv_kernel(m_ref, g_ref, wk_ref, wv_ref, k5_ref, v5_ref, kb_ref, vb_ref, wk_scr, wv_scr, *, nb):
    _cast_weights_once(1, (wk_ref, wk_scr), (wv_ref, wv_scr))
    ms = [_rms(m_ref[b], g_ref[...]).astype(BF16) for b in range(nb)]
    for w_ref, o5_ref, ob_ref in ((wk_scr, k5_ref, kb_ref), (wv_scr, v5_ref, vb_ref)):
        ps = [jnp.dot(m, w_ref[...], preferred_element_type=F32) for m in ms]
        for b, p in enumerate(ps):
            ob_ref[b] = p.astype(BF16)
            for hd in range(N_HEADS):
                o5_ref[b, :, hd, :] = p[:, hd * HEAD_DIM:(hd + 1) * HEAD_DIM]


def _memkv(mem, g_mem, w_k, w_v):
    bq = mem.shape[0]
    nb = 2
    out5 = pl.BlockSpec((nb, N_MEM, N_HEADS, HEAD_DIM), lambda i: (i, 0, 0, 0))
    outb = pl.BlockSpec((nb, N_MEM, D_MODEL), lambda i: (i, 0, 0))
    return pl.pallas_call(
        functools.partial(_memkv_kernel, nb=nb),
        grid=(bq // nb,),
        in_specs=[
            pl.BlockSpec((nb, N_MEM, D_MODEL), lambda i: (i, 0, 0)),
            _const((1, D_MODEL)),
            _const((D_MODEL, D_MODEL)),
            _const((D_MODEL, D_MODEL)),
        ],
        out_specs=[out5, out5, outb, outb],
        out_shape=[jax.ShapeDtypeStruct((bq, N_MEM, N_HEADS, HEAD_DIM), F32)] * 2
        + [jax.ShapeDtypeStruct((bq, N_MEM, D_MODEL), BF16)] * 2,
        scratch_shapes=[pltpu.VMEM(w_k.shape, BF16), pltpu.VMEM(w_v.shape, BF16)],
        compiler_params=_params(1),
        name="memkv",
    )(mem, g_mem, w_k, w_v)


def _softmax_rows(s):
    s = s - jnp.max(s, axis=-1, keepdims=True)
    e = jnp.exp(s)
    return e / jnp.sum(e, axis=-1, keepdims=True)


def _attn_kernel(x_ref, k_ref, v_ref, wq32_ref, wo32_ref, gq_ref, go_ref, qs_ref, ck_ref, cv_ref, o_ref, os_ref,
                 wq_ref, wo_ref, *, L, ct, bb):
    _cast_weights_once(1, (wq32_ref, wq_ref), (wo32_ref, wo_ref))
    nt = (((1,), (1,)), ((), ()))
    scale = HEAD_DIM ** -0.5
    heads = [slice(hd * HEAD_DIM, (hd + 1) * HEAD_DIM) for hd in range(N_HEADS)]
    x = x_ref[...].reshape(L * ct, D_MODEL)
    kb, vb = k_ref[...], v_ref[...]

    q2 = qs_ref[...]
    rows = q2.shape[0]
    qs = jnp.concatenate([q2[:, sl] for sl in heads], axis=0).astype(BF16)
    nr = N_HEADS * rows
    nc = N_MEM * N_HEADS
    same_head = (lax.broadcasted_iota(jnp.int32, (nr, nc), 0) // rows
                 == lax.broadcasted_iota(jnp.int32, (nr, nc), 1) % N_HEADS)
    owner = lax.broadcasted_iota(jnp.int32, (nr, HEAD_DIM), 0) % bb

    h = _rms(x, gq_ref[...])
    ka = [ck_ref[b].reshape(nc, HEAD_DIM).astype(BF16) for b in range(bb)]
    q = jnp.dot(h.astype(BF16), wq_ref[...], preferred_element_type=F32).astype(BF16)
    va = [cv_ref[b].reshape(nc, HEAD_DIM).astype(BF16) for b in range(bb)]
    sc_p = [lax.dot_general(q[:, sl], kb[:, sl], nt, preferred_element_type=F32) for sl in heads]
    sc_s = [lax.dot_general(qs, kk, nt, preferred_element_type=F32) for kk in ka]
    p_p = [_softmax_rows(s * scale).astype(BF16) for s in sc_p]
    p_s = [_softmax_rows(jnp.where(same_head, s * scale, -1e30)).astype(BF16) for s in sc_s]
    o_p = jnp.concatenate([jnp.dot(pp, vb[:, sl], preferred_element_type=F32) for pp, sl in zip(p_p, heads)], axis=-1)
    o_s = [jnp.dot(pp, vv, preferred_element_type=F32) for pp, vv in zip(p_s, va)]
    a = jnp.dot(o_p.astype(BF16), wo_ref[...], preferred_element_type=F32)
    acc = jnp.zeros((nr, HEAD_DIM), F32)
    for b, ob in enumerate(o_s):
        acc = jnp.where(owner == b, ob, acc)
    o_ref[...] = (x + _rms(a, go_ref[...])).reshape(L, ct, D_MODEL)
    os_ref[...] = jnp.concatenate([acc[hd * rows:(hd + 1) * rows] for hd in range(N_HEADS)], axis=-1)


def _attn(x1, k, v, w_q, w_o, g2, g3, q_grp, cache_k, cache_v, *, C, ct):
    L, R, _ = x1.shape
    nc = C // ct
    n, rows_s, _ = q_grp.shape
    bb = cache_k.shape[1] // n
    assert n == R // ct and n * bb == cache_k.shape[1]
    kv_spec = pl.BlockSpec((None, bb, N_MEM, N_HEADS, HEAD_DIM), lambda i: (0, i, 0, 0, 0))
    grp_spec = pl.BlockSpec((None, rows_s, D_MODEL), lambda i: (i, 0, 0))
    return pl.pallas_call(
        functools.partial(_attn_kernel, L=L, ct=ct, bb=bb),
        grid=(n,),
        in_specs=[
            pl.BlockSpec((L, ct, D_MODEL), lambda i: (0, i, 0)),
            pl.BlockSpec((None, N_MEM, D_MODEL), lambda i: (i // nc, 0, 0)),
            pl.BlockSpec((None, N_MEM, D_MODEL), lambda i: (i // nc, 0, 0)),
            _const((D_MODEL, D_MODEL)),
            _const((D_MODEL, D_MODEL)),
            _const((1, D_MODEL)),
            _const((1, D_MODEL)),
            grp_spec, kv_spec, kv_spec,
        ],
        out_specs=[pl.BlockSpec((L, ct, D_MODEL), lambda i: (0, i, 0)), grp_spec],
        out_shape=[jax.ShapeDtypeStruct((L, R, D_MODEL), F32), jax.ShapeDtypeStruct(q_grp.shape, F32)],
        scratch_shapes=[pltpu.VMEM(w_q.shape, BF16), pltpu.VMEM(w_o.shape, BF16)],
        compiler_params=_params(1),
        name="attn",
    )(x1, k, v, w_q, w_o, g2, g3, q_grp, cache_k, cache_v)


def _qproj_kernel(x_ref, wq_ref, g_ref, q_ref):
    h = _rms(x_ref[...], g_ref[...])
    q_ref[...] = jnp.dot(h.astype(BF16), wq_ref[...].astype(BF16), preferred_element_type=F32)


def _qproj(x1f, w_q, g2):
    rows = x1f.shape[0]
    return pl.pallas_call(
        _qproj_kernel,
        grid=(1,),
        in_specs=[_const((rows, D_MODEL)), _const((D_MODEL, D_MODEL)), _const((1, D_MODEL))],
        out_specs=pl.BlockSpec((rows, D_MODEL), lambda i: (0, 0)),
        out_shape=jax.ShapeDtypeStruct((rows, D_MODEL), F32),
        compiler_params=_params(1),
        name="qproj_sample",
    )(x1f, w_q, g2)


def _oproj_kernel(o_ref, x_ref, wo_ref, g_ref, y_ref):
    a = jnp.dot(o_ref[...].astype(BF16), wo_ref[...].astype(BF16), preferred_element_type=F32)
    y_ref[...] = x_ref[...] + _rms(a, g_ref[...])


def _oproj(of, x1f, w_o, g3):
    rows = of.shape[0]
    return pl.pallas_call(
        _oproj_kernel,
        grid=(1,),
        in_specs=[_const((rows, D_MODEL)), _const((rows, D_MODEL)), _const((D_MODEL, D_MODEL)),
                  _const((1, D_MODEL))],
        out_specs=pl.BlockSpec((rows, D_MODEL), lambda i: (0, 0)),
        out_shape=jax.ShapeDtypeStruct((rows, D_MODEL), F32),
        compiler_params=_params(1),
        name="oproj_sample",
    )(of, x1f, w_o, g3)


FFN_WEIGHT_CHUNKS = 16
FFN_STAGE_SLOTS = 4
FFN_ROW_SPLIT = 2


def _stream_cast(w_hbm, w16_scr, stage, sem):
    slots, rpc = stage.shape[0], stage.shape[1]
    n = w_hbm.shape[0] // rpc

    def chunk(k):
        return pltpu.make_async_copy(w_hbm.at[pl.ds(k * rpc, rpc)], stage.at[k % slots], sem.at[k % slots])

    for k in range(min(slots - 1, n)):
        chunk(k).start()
    for k in range(n):
        if k + slots - 1 < n:
            chunk(k + slots - 1).start()
        chunk(k).wait()
        w16_scr[pl.ds(k * rpc, rpc), :] = stage[k % slots].astype(BF16)


def _ffn_tile(x_ref, o_ref, wg_scr, wu_scr, wd_scr, gi_ref, go_ref, *, L, ct):
    rows = L * ct
    x = x_ref[...].reshape(rows, D_MODEL)
    rs = rows // FFN_ROW_SPLIT
    xs = [x[i * rs:(i + 1) * rs] for i in range(FFN_ROW_SPLIT)]
    h = [_rms(xx, gi_ref[...]).astype(BF16) for xx in xs]
    gate = [jnp.dot(hh, wg_scr[...], preferred_element_type=F32) for hh in h]
    up = [jnp.dot(hh, wu_scr[...], preferred_element_type=F32) for hh in h]
    act = [(g * jax.nn.sigmoid(g) * u).astype(BF16) for g, u in zip(gate, up)]
    dn = [jnp.dot(a, wd_scr[...], preferred_element_type=F32) for a in act]
    y = jnp.concatenate([xx + _rms(d, go_ref[...]) for xx, d in zip(xs, dn)], axis=0)
    for s in range(L):
        o_ref[:, s, :] = y[s * ct:(s + 1) * ct]


def _ffn_kernel(xp_ref, xs_ref, wg_hbm, wu_hbm, wd_hbm, gi_ref, go_ref, op_ref, os_ref,
                wg_scr, wu_scr, wd_scr, stage_in, stage_out, sem, *, n_prompt, Lp, ctp, Ls, cts):
    i = pl.program_id(0)

    @pl.when(i == 0)
    def _():
        _stream_cast(wg_hbm, wg_scr, stage_in, sem)
        _stream_cast(wu_hbm, wu_scr, stage_in, sem)
        _stream_cast(wd_hbm, wd_scr, stage_out, sem)

    @pl.when(i < n_prompt)
    def _():
        _ffn_tile(xp_ref, op_ref, wg_scr, wu_scr, wd_scr, gi_ref, go_ref, L=Lp, ct=ctp)

    @pl.when(i == n_prompt)
    def _():
        _ffn_tile(xs_ref, os_ref, wg_scr, wu_scr, wd_scr, gi_ref, go_ref, L=Ls, ct=cts)


def _ffn(xp, xs, w_gate, w_up, w_down, g4, g5, *, bq, C, ct):
    Lp = xp.shape[0]
    Ls, rs, _ = xs.shape
    nc = C // ct
    n = bq * nc
    tile = lambda i: jnp.minimum(i, n - 1)
    hbm = pl.BlockSpec(memory_space=pl.ANY)
    return pl.pallas_call(
        functools.partial(_ffn_kernel, n_prompt=n, Lp=Lp, ctp=ct, Ls=Ls, cts=rs),
        grid=(n + 1,),
        in_specs=[
            pl.BlockSpec((Lp, ct, D_MODEL), lambda i: (0, tile(i), 0)),
            _const((Ls, rs, D_MODEL)),
            hbm, hbm, hbm,
            _const((1, D_MODEL)),
            _const((1, D_MODEL)),
        ],
        out_specs=[
            pl.BlockSpec((None, ct, Lp, D_MODEL), lambda i: (tile(i) // nc, tile(i) % nc, 0, 0)),
            pl.BlockSpec((None, rs, Ls, D_MODEL), lambda i: (0, 0, 0, 0)),
        ],
        out_shape=[
            jax.ShapeDtypeStruct((bq, C, Lp, D_MODEL), F32),
            jax.ShapeDtypeStruct((1, rs, Ls, D_MODEL), F32),
        ],
        scratch_shapes=[
            pltpu.VMEM((D_MODEL, D_FF), BF16),
            pltpu.VMEM((D_MODEL, D_FF), BF16),
            pltpu.VMEM((D_FF, D_MODEL), BF16),
            pltpu.VMEM((FFN_STAGE_SLOTS, D_MODEL // FFN_WEIGHT_CHUNKS, D_FF), F32),
            pltpu.VMEM((FFN_STAGE_SLOTS, D_FF // FFN_WEIGHT_CHUNKS, D_MODEL), F32),
            pltpu.SemaphoreType.DMA((FFN_STAGE_SLOTS,)),
        ],
        compiler_params=_params(1),
        name="ffn",
    )(xp, xs, w_gate, w_up, w_down, g4, g5)


def _mixer(x4, conv_state_t, h0_packed, wts, *, ct, sub, sc, bt, sample):
    bq, C, L, _ = x4.shape
    R = bq * C
    g = [wts["norm_g"][i:i + 1] for i in range(2)]

    v, u4 = _inproj(x4, g[0], wts["w_in"], L=L, ct=ct, sub=sub)
    sw = wts["ssm_L%d" % L]
    nb = R // (bt * sc)
    ssm_args = (u4, sw[0], sw[1], sw[2], wts["d_skip"], h0_packed.reshape(N_LANE_BLOCKS, nb, bt, STATE_W))
    conv_args = (wts["w_dw"], wts["b_dw"], wts["ln_g"], wts["ln_b"])
    if sample:
        y, hl = _ssm(*ssm_args, L=L, C=sc, bt=bt)
        cact, conv_new = _conv_sample(v, conv_state_t, *conv_args)
    else:
        y, hl, cact = _ssm(*ssm_args, L=L, C=sc, bt=bt, conv=(v,) + conv_args)
    hl = hl.reshape(N_LANE_BLOCKS, nb * bt, STATE_W)
    x1 = _mixout(cact, y, x4, wts["w_glu"], wts["w_out"], g[1], L=L, ct=ct, sub=sub)
    if not sample:
        tail = v.reshape(L, bq, C, D_CONV)[:, :, C - 2:, :]
        tail = jnp.transpose(tail, (1, 2, 0, 3)).reshape(bq, 2 * L, D_CONV)
        conv_new = tail[:, 2 * L - (CONV_WIDTH - 1):, :]
    return x1, conv_new, hl


PROMPT_L = 16
PROMPT_SUB = 32
PROMPT_CT = 64


def kernel(x_prompt, x_sample, mem_prompt, cache_mem_k, cache_mem_v, state_conv, state_ssm_re, state_ssm_im,
           norm_g, mem_norm_g, w_in, w_dw, b_dw, ln_g, ln_b, lam_re, lam_im, log_dt, b_re, b_im, c_re, c_im,
           d_skip, w_glu, w_out, w_q, w_k, w_v, w_o, w_gate, w_up, w_down):
    depth = w_in.shape[0]
    bp, tp, _ = x_prompt.shape
    bs, ts, _ = x_sample.shape
    assert tp % (PROMPT_L * PROMPT_CT) == 0 and tp >= CONV_WIDTH - 1

    yp = x_prompt.reshape(bp, tp // PROMPT_L, PROMPT_L, D_MODEL)
    ys = x_sample.reshape(1, bs, ts, D_MODEL)
    outs = [[] for _ in range(8)]
    for l in range(depth):
        wts = {
            "norm_g": norm_g[l],
            "w_in": w_in[l],
            "w_dw": w_dw[l],
            "b_dw": b_dw[l][None],
            "ln_g": ln_g[l][None],
            "ln_b": ln_b[l][None],
            "d_skip": d_skip[l].reshape(N_LANE_BLOCKS, 1, LANES),
            "w_glu": w_glu[l],
            "w_out": w_out[l],
        }
        ssm_args = (lam_re[l], lam_im[l], log_dt[l], b_re[l], b_im[l], c_re[l], c_im[l])
        assert ts <= PROMPT_L
        full = _ssm_weights(*ssm_args, PROMPT_L)
        wts["ssm_L%d" % PROMPT_L] = full[:2] + (full[2](PROMPT_L),)
        wts["ssm_L%d" % ts] = _ssm_weights_prefix(full, PROMPT_L, ts)

        kp, vp, kp16, vp16 = _memkv(mem_prompt, mem_norm_g[l][None], w_k[l], w_v[l])
        h0p = jnp.zeros((N_LANE_BLOCKS, bp, STATE_W), F32)
        cp_chunks = tp // PROMPT_L
        xp, cp, hp = _mixer(yp, None, h0p, wts, ct=PROMPT_CT, sub=PROMPT_SUB, sc=cp_chunks, bt=4, sample=False)
        h0s = _pack_state(state_ssm_re[l], state_ssm_im[l])
        xs, cs, hs = _mixer(ys, jnp.transpose(state_conv[l], (1, 0, 2)), h0s, wts, ct=bs, sub=bs, sc=1, bt=bs,
                            sample=True)
        cs = jnp.transpose(cs, (1, 0, 2))

        g2, g3 = norm_g[l][2:3], norm_g[l][3:4]
        n_tiles = bp * cp_chunks // PROMPT_SUB
        bb = bs // n_tiles
        xs_flat = xs.reshape(ts * bs, D_MODEL)
        q = _qproj(xs_flat, w_q[l], g2).reshape(ts, n_tiles, bb, D_MODEL)
        q = jnp.transpose(q, (1, 0, 2, 3)).reshape(n_tiles, ts * bb, D_MODEL)
        xp, o = _attn(xp, kp16, vp16, w_q[l], w_o[l], g2, g3, q, cache_mem_k[l:l + 1], cache_mem_v[l:l + 1],
                      C=cp_chunks, ct=PROMPT_SUB)
        o = jnp.transpose(o.reshape(n_tiles, ts, bb, D_MODEL), (1, 0, 2, 3)).reshape(ts * bs, D_MODEL)
        xs = _oproj(o, xs_flat, w_o[l], g3).reshape(ts, bs, D_MODEL)

        yp, ys = _ffn(xp, xs, w_gate[l], w_up[l], w_down[l], norm_g[l][4:5], norm_g[l][5:6],
                      bq=bp, C=cp_chunks, ct=PROMPT_SUB)

        hp_re, hp_im = _unpack_state(hp)
        hs_re, hs_im = _unpack_state(hs)
        for lst, val in zip(outs, (kp, vp, cp, hp_re, hp_im, cs, hs_re, hs_im)):
            lst.append(val)
    return (yp.reshape(bp, tp, D_MODEL), ys.reshape(bs, ts, D_MODEL)) + tuple(jnp.stack(o) for o in outs)
```

```python
import functools

import jax
import jax.numpy as jnp
from jax import lax
from jax.experimental import pallas as pl
from jax.experimental.pallas import tpu as pltpu

F32 = jnp.float32
BF16 = jnp.bfloat16

D_MODEL = 1024
D_CONV = 512
D_SSM = 512
CONV_WIDTH = 31
N_GROUPS = 32
P_GROUP = 16
N_STATE = 64
N_MEM = 256
N_HEADS = 4
HEAD_DIM = 256
D_FF = 2816
RMS_EPS = 1e-6
LN_EPS = 1e-5

LANES = 128
SUBLANES = 8
GROUPS_PER_BLOCK = LANES // P_GROUP
N_LANE_BLOCKS = D_SSM // LANES
STATE_HALF = GROUPS_PER_BLOCK * N_STATE
STATE_W = 2 * STATE_HALF
MXU_N = 256
VMEM_LIMIT = 56 * 1024 * 1024


def _params(n_axes, vmem=VMEM_LIMIT):
    return pltpu.CompilerParams(dimension_semantics=("arbitrary",) * n_axes, vmem_limit_bytes=vmem)


def _const(shape):
    nd = len(shape)
    return pl.BlockSpec(shape, lambda *_: (0,) * nd, pipeline_mode=pl.Buffered(1))


def _rms(x, g):
    return x * lax.rsqrt(jnp.mean(x * x, axis=-1, keepdims=True) + RMS_EPS) * g


def _cast_weights_once(n_axes, *pairs):
    first = pl.program_id(0) == 0
    for a in range(1, n_axes):
        first = jnp.logical_and(first, pl.program_id(a) == 0)

    @pl.when(first)
    def _():
        for src, dst in pairs:
            dst[...] = src[...].astype(BF16)


def _gather_rows(x_ref, L, c0, n):
    return jnp.concatenate([x_ref[c0:c0 + n, s, :] for s in range(L)], axis=0)


def _sub_tiles(ct, sub):
    return [(c0, sub) for c0 in range(0, ct, sub)]


def _nat_spec(ct, L):
    return pl.BlockSpec((None, ct, L, D_MODEL), lambda b, i: (b, i, 0, 0))


def _tp_spec(ct, L, width, nc):
    return pl.BlockSpec((L, ct, width), lambda b, i: (0, b * nc + i, 0))


def _inproj_kernel(x_ref, g_ref, w_ref, v_ref, u_ref, w16_scr, *, L, ct, sub):
    _cast_weights_once(2, (w_ref, w16_scr))
    for c0, n in _sub_tiles(ct, sub):
        x = _gather_rows(x_ref, L, c0, n)
        h = _rms(x, g_ref[...])
        z = jnp.dot(h.astype(BF16), w16_scr[...], preferred_element_type=F32)
        a = z[:, :D_CONV]
        g = z[:, D_CONV:2 * D_CONV]
        u = z[:, 2 * D_CONV:].astype(BF16)
        v_ref[:, c0:c0 + n, :] = (a * jax.nn.sigmoid(g)).reshape(L, n, D_CONV)
        for s in range(L):
            for j in range(N_LANE_BLOCKS):
                u_ref[j, c0:c0 + n, s * LANES:(s + 1) * LANES] = u[s * n:(s + 1) * n, j * LANES:(j + 1) * LANES]


def _inproj(x4, g0, w_in, *, L, ct, sub):
    bq, c = x4.shape[:2]
    nc = c // ct
    R = bq * c
    return pl.pallas_call(
        functools.partial(_inproj_kernel, L=L, ct=ct, sub=sub),
        grid=(bq, nc),
        in_specs=[_nat_spec(ct, L), _const((1, D_MODEL)), _const((D_MODEL, 2 * D_CONV + D_SSM))],
        out_specs=[
            _tp_spec(ct, L, D_CONV, nc),
            pl.BlockSpec((N_LANE_BLOCKS, ct, L * LANES), lambda b, i: (0, b * nc + i, 0)),
        ],
        out_shape=[
            jax.ShapeDtypeStruct((L, R, D_CONV), F32),
            jax.ShapeDtypeStruct((N_LANE_BLOCKS, R, L * LANES), BF16),
        ],
        scratch_shapes=[pltpu.VMEM(w_in.shape, BF16)],
        compiler_params=_params(2),
        name="inproj",
    )(x4, g0, w_in)


def _ln_silu(acc, g, b):
    mu = jnp.mean(acc, axis=-1, keepdims=True)
    xc = acc - mu
    var = jnp.mean(xc * xc, axis=-1, keepdims=True)
    y = xc * lax.rsqrt(var + LN_EPS) * g + b
    return y * jax.nn.sigmoid(y)


def _conv_shifted_copies(v_ref, v1_ref, v2_ref, *, L, C):
    row = lax.broadcasted_iota(jnp.int32, (C, D_CONV), 0)
    for s in range(L):
        x = v_ref[s]
        v1_ref[s] = jnp.where(row >= 1, pltpu.roll(x, 1, 0), 0.0)
        v2_ref[s] = jnp.where(row >= 2, pltpu.roll(x, 2, 0), 0.0)


def _conv_rows(srcs, w_ref, b_ref, g_ref, bb_ref, o_ref, r0, out_r0, *, L, rc):
    sub8 = rc // SUBLANES
    for s in range(L):
        acc = jnp.broadcast_to(b_ref[...][None], (sub8, SUBLANES, D_CONV))
        for d in range(CONV_WIDTH):
            blk = (s - d) % L
            shift = (d - s + L - 1) // L if d > s else 0
            k = CONV_WIDTH - 1 - d
            src = srcs[shift][blk, pl.ds(r0, rc), :].reshape(sub8, SUBLANES, D_CONV)
            acc = acc + w_ref[k][None] * src
        y = _ln_silu(acc.reshape(rc, D_CONV), g_ref[...], bb_ref[...])
        o_ref[s, pl.ds(out_r0, rc), :] = y.astype(o_ref.dtype)


def _conv_sample_kernel(v_ref, st_ref, w_ref, b_ref, g_ref, bb_ref, o_ref, new_ref, *, L, H):
    def ext(i):
        return st_ref[i] if i < H else v_ref[i - H]

    for t in range(L):
        acc = jnp.broadcast_to(b_ref[...], v_ref.shape[1:])
        for k in range(CONV_WIDTH):
            acc = acc + w_ref[k:k + 1, :] * ext(t + k)
        o_ref[t] = _ln_silu(acc, g_ref[...], bb_ref[...]).astype(o_ref.dtype)
    for i in range(H):
        new_ref[i] = ext(i + L)


def _conv_sample(v, state_t, w_dw, b_dw, ln_g, ln_b):
    L, R, _ = v.shape
    H = CONV_WIDTH - 1
    rt = 32
    rows = lambda n: pl.BlockSpec((n, rt, D_CONV), lambda i: (0, i, 0))
    return pl.pallas_call(
        functools.partial(_conv_sample_kernel, L=L, H=H),
        grid=(R // rt,),
        in_specs=[
            rows(L),
            rows(H),
            _const((CONV_WIDTH, D_CONV)),
            _const((1, D_CONV)),
            _const((1, D_CONV)),
            _const((1, D_CONV)),
        ],
        out_specs=[rows(L), rows(H)],
        out_shape=[
            jax.ShapeDtypeStruct((L, R, D_CONV), BF16),
            jax.ShapeDtypeStruct((H, R, D_CONV), F32),
        ],
        compiler_params=_params(1),
        name="conv_sample",
    )(v, state_t, w_dw, b_dw, ln_g, ln_b)


def _expand_block_diag(d):
    tiled = jnp.concatenate([d] * GROUPS_PER_BLOCK, axis=-1)
    r = lax.broadcasted_iota(jnp.int32, tiled.shape, 0) // P_GROUP
    c = lax.broadcasted_iota(jnp.int32, tiled.shape, 1) // N_STATE
    return jnp.where(r == c, tiled, jnp.zeros_like(tiled))


def _ssm_kernel(u_ref, wbc_ref, wcc_ref, a_ref, d_ref, h0_ref, y_ref, hl_ref,
                w_scr, wb_scr, wct_scr, s_scr, hp_scr, tap_scr, *, L, C, bt):
    per_tile = MXU_N // LANES

    @pl.when(pl.program_id(1) == 0)
    def _():
        for s in range(L):
            rows = slice(s * LANES, (s + 1) * LANES)
            for a in range(2):
                cols = slice(a * STATE_HALF, (a + 1) * STATE_HALF)
                wb_scr[rows, cols] = _expand_block_diag(wbc_ref[s, a])
                wct_scr[rows, cols] = _expand_block_diag(wcc_ref[s + 1, a])
        bbar = wb_scr[(L - 1) * LANES:L * LANES, :]
        nt = (((1,), (1,)), ((), ()))
        c0 = jnp.concatenate([_expand_block_diag(wcc_ref[0, a]) for a in range(2)], axis=-1)
        tap_scr[0] = lax.dot_general(bbar, c0, nt, preferred_element_type=F32).astype(BF16)
        for t in range(1, L):
            tap_scr[t] = lax.dot_general(bbar, wct_scr[(t - 1) * LANES:t * LANES, :], nt,
                                         preferred_element_type=F32).astype(BF16)
        for sp in range(L):
            for s in range(L):
                rows = slice(s * LANES, (s + 1) * LANES)
                cols = slice(sp * LANES, (sp + 1) * LANES)
                if s <= sp:
                    w_scr[rows, cols] = tap_scr[sp - s]
                elif s // per_tile == sp // per_tile:
                    w_scr[rows, cols] = jnp.zeros((LANES, LANES), BF16)

    nk = STATE_W // LANES
    half = nk // 2
    rows = bt * C
    inc_all = jnp.dot(u_ref[...], wb_scr[...], preferred_element_type=F32)
    for k in range(half):
        s_scr[k, 0:rows, :] = inc_all[:, k * LANES:(k + 1) * LANES]
        s_scr[k, rows:2 * rows, :] = inc_all[:, (half + k) * LANES:(half + k + 1) * LANES]

    d = d_ref[...]
    for n in range(L // per_tile):
        k = (n + 1) * MXU_N
        y = jnp.dot(u_ref[:, :k], w_scr[:k, n * MXU_N:(n + 1) * MXU_N], preferred_element_type=F32)
        for h in range(per_tile):
            s = n * per_tile + h
            us = u_ref[:, s * LANES:(s + 1) * LANES].astype(F32)
            y_ref[s] = y[:, h * LANES:(h + 1) * LANES] + d * us

    h0 = h0_ref[...]
    a = a_ref[...]

    def piece(x, k):
        return x[:, k * LANES:(k + 1) * LANES]

    def swap_parts(x):
        if 2 * bt == SUBLANES:
            return pltpu.roll(x, bt, 0)
        return jnp.concatenate([x[bt:], x[:bt]], axis=0)

    im_rows = lax.broadcasted_iota(jnp.int32, (2 * bt, LANES), 0) >= bt
    a_same = [jnp.broadcast_to(piece(a[0:1], k), (2 * bt, LANES)) for k in range(half)]
    a_cross = [jnp.where(im_rows, piece(a[1:2], k), -piece(a[1:2], k)) for k in range(half)]
    state = [jnp.concatenate([piece(h0, k), piece(h0, half + k)], axis=0) for k in range(half)]
    for c in range(C):
        sel = pl.ds(c, 2 * bt, stride=C) if C > 1 else pl.ds(0, 2 * bt)
        for k in range(half):
            hp_scr[k, sel, :] = state[k]
            state[k] = a_same[k] * state[k] + a_cross[k] * swap_parts(state[k]) + s_scr[k, sel, :]
    for k in range(half):
        hl_ref[:, k * LANES:(k + 1) * LANES] = state[k][:bt]
        hl_ref[:, (half + k) * LANES:(half + k + 1) * LANES] = state[k][bt:]

    hp = jnp.concatenate([hp_scr[k, 0:rows, :] for k in range(half)]
                         + [hp_scr[k, rows:2 * rows, :] for k in range(half)], axis=-1).astype(BF16)
    for n in range(L // per_tile):
        y = lax.dot_general(hp, wct_scr[n * MXU_N:(n + 1) * MXU_N, :], (((1,), (1,)), ((), ())),
                            preferred_element_type=F32)
        for h in range(per_tile):
            s = n * per_tile + h
            y_ref[s] += y[:, h * LANES:(h + 1) * LANES]


def _ssm(u4, wbc, wcc, a_pow, d_skip, h0, *, L, C, bt):
    R = u4.shape[1]
    rows = bt * C
    nb = R // rows
    lw = L * LANES
    in_specs = [
        pl.BlockSpec((None, rows, lw), lambda j, b: (j, b, 0)),
        pl.BlockSpec((None, L, 2, LANES, N_STATE), lambda j, b: (j, 0, 0, 0, 0)),
        pl.BlockSpec((None, L + 1, 2, LANES, N_STATE), lambda j, b: (j, 0, 0, 0, 0)),
        pl.BlockSpec((None, 2, STATE_HALF), lambda j, b: (j, 0, 0)),
        pl.BlockSpec((None, 1, LANES), lambda j, b: (j, 0, 0)),
        pl.BlockSpec((None, None, bt, STATE_W), lambda j, b: (j, b, 0, 0)),
    ]
    out_specs = [
        pl.BlockSpec((L, rows, LANES), lambda j, b: (0, b, j)),
        pl.BlockSpec((None, None, bt, STATE_W), lambda j, b: (j, b, 0, 0)),
    ]
    out_shape = [
        jax.ShapeDtypeStruct((L, R, D_SSM), F32),
        jax.ShapeDtypeStruct((N_LANE_BLOCKS, nb, bt, STATE_W), F32),
    ]
    scratch = [
        pltpu.VMEM((lw, lw), BF16),
        pltpu.VMEM((lw, STATE_W), BF16),
        pltpu.VMEM((lw, STATE_W), BF16),
        pltpu.VMEM((STATE_HALF // LANES, 2 * rows, LANES), F32),
        pltpu.VMEM((STATE_HALF // LANES, 2 * rows, LANES), F32),
        pltpu.VMEM((L, LANES, LANES), BF16),
    ]
    return pl.pallas_call(
        functools.partial(_ssm_kernel, L=L, C=C, bt=bt),
        grid=(N_LANE_BLOCKS, nb),
        in_specs=in_specs,
        out_specs=out_specs,
        out_shape=out_shape,
        scratch_shapes=scratch,
        compiler_params=_params(2),
        name="ssm",
    )(u4, wbc, wcc, a_pow, d_skip, h0)


def _ssm_weights(lam_re, lam_im, log_dt, b_re, b_im, c_re, c_im, L):
    dt = jnp.exp(log_dt)[:, None]
    zr = lam_re * dt
    zi = lam_im * dt
    n_pow = jnp.arange(L + 1, dtype=F32)[:, None, None]
    mag = jnp.exp(zr[None] * n_pow)
    pr = mag * jnp.cos(zi[None] * n_pow)
    pi = mag * jnp.sin(zi[None] * n_pow)
    a1r, a1i = pr[1], pi[1]
    den = lam_re * lam_re + lam_im * lam_im
    qr = ((a1r - 1.0) * lam_re + a1i * lam_im) / den
    qi = (a1i * lam_re - (a1r - 1.0) * lam_im) / den
    bbr = qr[:, :, None] * b_re - qi[:, :, None] * b_im
    bbi = qr[:, :, None] * b_im + qi[:, :, None] * b_re

    nj, g8 = N_LANE_BLOCKS, GROUPS_PER_BLOCK

    n_rev = (L - 1.0) - jnp.arange(L, dtype=F32)[:, None, None]
    mag_rev = jnp.exp(zr[None] * n_rev)
    rev = mag_rev * jnp.cos(zi[None] * n_rev), mag_rev * jnp.sin(zi[None] * n_rev)
    er = rev[0][:, :, :, None] * bbr[None] - rev[1][:, :, :, None] * bbi[None]
    ei = rev[0][:, :, :, None] * bbi[None] + rev[1][:, :, :, None] * bbr[None]
    e = jnp.stack([er, ei], 1).reshape(L, 2, nj, g8, N_STATE, P_GROUP)
    wbc = jnp.transpose(e, (2, 0, 1, 3, 5, 4)).reshape(nj, L, 2, LANES, N_STATE)

    mr = c_re[None] * pr[:, :, None, :] - c_im[None] * pi[:, :, None, :]
    mi = c_re[None] * pi[:, :, None, :] + c_im[None] * pr[:, :, None, :]
    m = jnp.stack([mr, -mi], 1).reshape(L + 1, 2, nj, g8, P_GROUP, N_STATE)
    wcc = jnp.transpose(m, (2, 0, 1, 3, 4, 5)).reshape(nj, L + 1, 2, LANES, N_STATE)

    def a_pow(n):
        return jnp.stack([pr[n].reshape(nj, STATE_HALF), pi[n].reshape(nj, STATE_HALF)], 1)

    return wbc.astype(BF16), wcc.astype(BF16), a_pow


def _ssm_weights_prefix(full, L_full, L):
    wbc, wcc, a_pow = full
    return wbc[:, L_full - L:], wcc[:, :L + 1], a_pow(L)


def _pack_state(re, im):
    b = re.shape[0]
    r = re.reshape(b, N_LANE_BLOCKS, STATE_HALF)
    i = im.reshape(b, N_LANE_BLOCKS, STATE_HALF)
    return jnp.transpose(jnp.concatenate([r, i], -1), (1, 0, 2))


def _unpack_state(h):
    b = h.shape[1]
    h = jnp.transpose(h, (1, 0, 2))
    re = h[:, :, :STATE_HALF].reshape(b, N_GROUPS, N_STATE)
    im = h[:, :, STATE_HALF:].reshape(b, N_GROUPS, N_STATE)
    return re, im


def _mixout_kernel(*refs, L, ct, sub, conv_c):
    if conv_c:
        (v_ref, cw_ref, cb_ref, lg_ref, lb_ref, y_ref, x_ref, wglu_ref, wout_ref, g_ref, o_ref,
         wglu_scr, wout_scr, v1_scr, v2_scr, c_ref) = refs
    else:
        c_ref, y_ref, x_ref, wglu_ref, wout_ref, g_ref, o_ref, wglu_scr, wout_scr = refs
    _cast_weights_once(2, (wglu_ref, wglu_scr), (wout_ref, wout_scr))
    tiles = _sub_tiles(ct, sub)
    gy = [jax.nn.gelu(y_ref[:, c0:c0 + n, :].reshape(L * n, D_SSM)) for c0, n in tiles]
    gate = [jnp.dot(g.astype(BF16), wglu_scr[...], preferred_element_type=F32) for g in gy]
    if conv_c:
        step = pl.program_id(1)

        @pl.when(step == 0)
        def _():
            _conv_shifted_copies(v_ref, v1_scr, v2_scr, L=L, C=conv_c)

        base = pl.multiple_of(step * ct, ct)
        for c0, n in tiles:
            _conv_rows((v_ref, v1_scr, v2_scr), cw_ref, cb_ref, lg_ref, lb_ref, c_ref, base + c0, c0, L=L, rc=n)
    m = [jnp.dot(c_ref[:, c0:c0 + n, :].reshape(L * n, D_CONV), wout_scr[0:D_CONV, :],
                 preferred_element_type=F32) for c0, n in tiles]
    sg = [g * jax.nn.sigmoid(t) for g, t in zip(gy, gate)]
    m = [a + jnp.dot(s.astype(BF16), wout_scr[D_CONV:, :], preferred_element_type=F32) for a, s in zip(m, sg)]
    for (c0, n), mm in zip(tiles, m):
        x = _gather_rows(x_ref, L, c0, n)
        o_ref[:, c0:c0 + n, :] = (x + _rms(mm, g_ref[...])).reshape(L, n, D_MODEL)


def _mixout(c, y, x4, w_glu, w_out, g1, *, L, ct, sub, conv=None):
    bq, cc = x4.shape[:2]
    nc = cc // ct
    R = bq * cc
    in_specs = [
        _tp_spec(ct, L, D_SSM, nc),
        _nat_spec(ct, L),
        _const((D_SSM, D_SSM)),
        _const((D_CONV + D_SSM, D_MODEL)),
        _const((1, D_MODEL)),
    ]
    args = [y, x4, w_glu, w_out, g1]
    scratch = [pltpu.VMEM(w_glu.shape, BF16), pltpu.VMEM(w_out.shape, BF16)]
    if conv is None:
        in_specs = [_tp_spec(ct, L, D_CONV, nc)] + in_specs
        args = [c] + args
    else:
        v, w_dw, b_dw, ln_g, ln_b = conv
        in_specs = [
            pl.BlockSpec((L, cc, D_CONV), lambda b, i: (0, b, 0)),
            _const((CONV_WIDTH, SUBLANES, D_CONV)),
            _const((SUBLANES, D_CONV)),
            _const((1, D_CONV)),
            _const((1, D_CONV)),
        ] + in_specs
        args = [v, jnp.broadcast_to(w_dw[:, None, :], (CONV_WIDTH, SUBLANES, D_CONV)),
                jnp.broadcast_to(b_dw, (SUBLANES, D_CONV)), ln_g, ln_b] + args
        scratch += [pltpu.VMEM((L, cc, D_CONV), F32), pltpu.VMEM((L, cc, D_CONV), F32),
                    pltpu.VMEM((L, ct, D_CONV), BF16)]
    return pl.pallas_call(
        functools.partial(_mixout_kernel, L=L, ct=ct, sub=sub, conv_c=0 if conv is None else cc),
        grid=(bq, nc),
        in_specs=in_specs,
        out_specs=_tp_spec(ct, L, D_MODEL, nc),
        out_shape=jax.ShapeDtypeStruct((L, R, D_MODEL), F32),
        scratch_shapes=scratch,
        compiler_params=_params(2),
        name="mixout",
    )(*args)


def _memkv_kernel(m_ref, g_ref, wk_ref, wv_ref, k5_ref, v5_ref, kb_ref, vb_ref, wk_scr, wv_scr, *, nb):
    _cast_weights_once(1, (wk_ref, wk_scr), (wv_ref, wv_scr))
    ms = [_rms(m_ref[b], g_ref[...]).astype(BF16) for b in range(nb)]
    for w_ref, o5_ref, ob_ref in ((wk_scr, k5_ref, kb_ref), (wv_scr, v5_ref, vb_ref)):
        ps = [jnp.dot(m, w_ref[...], preferred_element_type=F32) for m in ms]
        for b, p in enumerate(ps):
            ob_ref[b] = p.astype(BF16)
            for hd in range(N_HEADS):
                o5_ref[b, :, hd, :] = p[:, hd * HEAD_DIM:(hd + 1) * HEAD_DIM]


def _memkv(mem, g_mem, w_k, w_v):
    bq = mem.shape[0]
    nb = 2
    out5 = pl.BlockSpec((nb, N_MEM, N_HEADS, HEAD_DIM), lambda i: (i, 0, 0, 0))
    outb = pl.BlockSpec((nb, N_MEM, D_MODEL), lambda i: (i, 0, 0))
    return pl.pallas_call(
        functools.partial(_memkv_kernel, nb=nb),
        grid=(bq // nb,),
        in_specs=[
            pl.BlockSpec((nb, N_MEM, D_MODEL), lambda i: (i, 0, 0)),
            _const((1, D_MODEL)),
            _const((D_MODEL, D_MODEL)),
            _const((D_MODEL, D_MODEL)),
        ],
        out_specs=[out5, out5, outb, outb],
        out_shape=[jax.ShapeDtypeStruct((bq, N_MEM, N_HEADS, HEAD_DIM), F32)] * 2
        + [jax.ShapeDtypeStruct((bq, N_MEM, D_MODEL), BF16)] * 2,
        scratch_shapes=[pltpu.VMEM(w_k.shape, BF16), pltpu.VMEM(w_v.shape, BF16)],
        compiler_params=_params(1),
        name="memkv",
    )(mem, g_mem, w_k, w_v)


def _softmax_rows(s):
    s = s - jnp.max(s, axis=-1, keepdims=True)
    e = jnp.exp(s)
    return e / jnp.sum(e, axis=-1, keepdims=True)


def _attn_kernel(x_ref, k_ref, v_ref, wq32_ref, wo32_ref, gq_ref, go_ref, xs_ref, ck_ref, cv_ref, o_ref, os_ref,
                 wq_ref, wo_ref, *, L, ct, bb):
    _cast_weights_once(1, (wq32_ref, wq_ref), (wo32_ref, wo_ref))
    nt = (((1,), (1,)), ((), ()))
    scale = HEAD_DIM ** -0.5
    heads = [slice(hd * HEAD_DIM, (hd + 1) * HEAD_DIM) for hd in range(N_HEADS)]
    np_rows = L * ct
    x = jnp.concatenate([x_ref[...].reshape(np_rows, D_MODEL), xs_ref[...]], axis=0)
    kb, vb = k_ref[...], v_ref[...]

    rows = xs_ref.shape[0]
    nr = N_HEADS * rows
    nc = N_MEM * N_HEADS
    same_head = (lax.broadcasted_iota(jnp.int32, (nr, nc), 0) // rows
                 == lax.broadcasted_iota(jnp.int32, (nr, nc), 1) % N_HEADS)
    owner = lax.broadcasted_iota(jnp.int32, (nr, HEAD_DIM), 0) % bb

    h = _rms(x, gq_ref[...])
    ka = [ck_ref[b].reshape(nc, HEAD_DIM).astype(BF16) for b in range(bb)]
    q_all = jnp.dot(h.astype(BF16), wq_ref[...], preferred_element_type=F32).astype(BF16)
    q, q2 = q_all[:np_rows], q_all[np_rows:]
    qs = jnp.concatenate([q2[:, sl] for sl in heads], axis=0)
    va = [cv_ref[b].reshape(nc, HEAD_DIM).astype(BF16) for b in range(bb)]
    sc_p = [lax.dot_general(q[:, sl], kb[:, sl], nt, preferred_element_type=F32) for sl in heads]
    sc_s = [lax.dot_general(qs, kk, nt, preferred_element_type=F32) for kk in ka]
    p_p = [_softmax_rows(s * scale).astype(BF16) for s in sc_p]
    p_s = [_softmax_rows(jnp.where(same_head, s * scale, -1e30)).astype(BF16) for s in sc_s]
    o_p = jnp.concatenate([jnp.dot(pp, vb[:, sl], preferred_element_type=F32) for pp, sl in zip(p_p, heads)], axis=-1)
    o_s = [jnp.dot(pp, vv, preferred_element_type=F32) for pp, vv in zip(p_s, va)]
    acc = jnp.zeros((nr, HEAD_DIM), F32)
    for b, ob in enumerate(o_s):
        acc = jnp.where(owner == b, ob, acc)
    o_all = jnp.concatenate(
        [o_p, jnp.concatenate([acc[hd * rows:(hd + 1) * rows] for hd in range(N_HEADS)], axis=-1)], axis=0)
    a = jnp.dot(o_all.astype(BF16), wo_ref[...], preferred_element_type=F32)
    y = x + _rms(a, go_ref[...])
    o_ref[...] = y[:np_rows].reshape(L, ct, D_MODEL)
    os_ref[...] = y[np_rows:]


def _attn(x1, k, v, w_q, w_o, g2, g3, xs_grp, cache_k, cache_v, *, C, ct):
    L, R, _ = x1.shape
    nc = C // ct
    n, rows_s, _ = xs_grp.shape
    bb = cache_k.shape[1] // n
    assert n == R // ct and n * bb == cache_k.shape[1]
    kv_spec = pl.BlockSpec((None, bb, N_MEM, N_HEADS, HEAD_DIM), lambda i: (0, i, 0, 0, 0))
    grp_spec = pl.BlockSpec((None, rows_s, D_MODEL), lambda i: (i, 0, 0))
    return pl.pallas_call(
        functools.partial(_attn_kernel, L=L, ct=ct, bb=bb),
        grid=(n,),
        in_specs=[
            pl.BlockSpec((L, ct, D_MODEL), lambda i: (0, i, 0)),
            pl.BlockSpec((None, N_MEM, D_MODEL), lambda i: (i // nc, 0, 0)),
            pl.BlockSpec((None, N_MEM, D_MODEL), lambda i: (i // nc, 0, 0)),
            _const((D_MODEL, D_MODEL)),
            _const((D_MODEL, D_MODEL)),
            _const((1, D_MODEL)),
            _const((1, D_MODEL)),
            grp_spec, kv_spec, kv_spec,
        ],
        out_specs=[pl.BlockSpec((L, ct, D_MODEL), lambda i: (0, i, 0)), grp_spec],
        out_shape=[jax.ShapeDtypeStruct((L, R, D_MODEL), F32), jax.ShapeDtypeStruct(xs_grp.shape, F32)],
        scratch_shapes=[pltpu.VMEM(w_q.shape, BF16), pltpu.VMEM(w_o.shape, BF16)],
        compiler_params=_params(1),
        name="attn",
    )(x1, k, v, w_q, w_o, g2, g3, xs_grp, cache_k, cache_v)


FFN_WEIGHT_CHUNKS = 16
FFN_STAGE_SLOTS = 4
FFN_ROW_SPLIT = 2


def _stream_cast(w_hbm, w16_scr, stage, sem):
    slots, rpc = stage.shape[0], stage.shape[1]
    n = w_hbm.shape[0] // rpc

    def chunk(k):
        return pltpu.make_async_copy(w_hbm.at[pl.ds(k * rpc, rpc)], stage.at[k % slots], sem.at[k % slots])

    for k in range(min(slots - 1, n)):
        chunk(k).start()
    for k in range(n):
        if k + slots - 1 < n:
            chunk(k + slots - 1).start()
        chunk(k).wait()
        w16_scr[pl.ds(k * rpc, rpc), :] = stage[k % slots].astype(BF16)


def _ffn_tile(x_ref, o_ref, wg_scr, wu_scr, wd_scr, gi_ref, go_ref, *, L, ct):
    rows = L * ct
    x = x_ref[...].reshape(rows, D_MODEL)
    rs = rows // FFN_ROW_SPLIT
    xs = [x[i * rs:(i + 1) * rs] for i in range(FFN_ROW_SPLIT)]
    h = [_rms(xx, gi_ref[...]).astype(BF16) for xx in xs]
    gate = [jnp.dot(hh, wg_scr[...], preferred_element_type=F32) for hh in h]
    up = [jnp.dot(hh, wu_scr[...], preferred_element_type=F32) for hh in h]
    act = [(g * jax.nn.sigmoid(g) * u).astype(BF16) for g, u in zip(gate, up)]
    dn = [jnp.dot(a, wd_scr[...], preferred_element_type=F32) for a in act]
    y = jnp.concatenate([xx + _rms(d, go_ref[...]) for xx, d in zip(xs, dn)], axis=0)
    for s in range(L):
        o_ref[:, s, :] = y[s * ct:(s + 1) * ct]


def _ffn_kernel(xp_ref, xs_ref, wg_hbm, wu_hbm, wd_hbm, gi_ref, go_ref, op_ref, os_ref,
                wg_scr, wu_scr, wd_scr, stage_in, stage_out, sem, *, n_prompt, Lp, ctp, Ls, cts):
    i = pl.program_id(0)

    @pl.when(i == 0)
    def _():
        _stream_cast(wg_hbm, wg_scr, stage_in, sem)
        _stream_cast(wu_hbm, wu_scr, stage_in, sem)
        _stream_cast(wd_hbm, wd_scr, stage_out, sem)

    @pl.when(i < n_prompt)
    def _():
        _ffn_tile(xp_ref, op_ref, wg_scr, wu_scr, wd_scr, gi_ref, go_ref, L=Lp, ct=ctp)

    @pl.when(i == n_prompt)
    def _():
        _ffn_tile(xs_ref, os_ref, wg_scr, wu_scr, wd_scr, gi_ref, go_ref, L=Ls, ct=cts)


def _ffn(xp, xs, w_gate, w_up, w_down, g4, g5, *, bq, C, ct):
    Lp = xp.shape[0]
    Ls, rs, _ = xs.shape
    nc = C // ct
    n = bq * nc
    tile = lambda i: jnp.minimum(i, n - 1)
    hbm = pl.BlockSpec(memory_space=pl.ANY)
    return pl.pallas_call(
        functools.partial(_ffn_kernel, n_prompt=n, Lp=Lp, ctp=ct, Ls=Ls, cts=rs),
        grid=(n + 1,),
        in_specs=[
            pl.BlockSpec((Lp, ct, D_MODEL), lambda i: (0, tile(i), 0)),
            _const((Ls, rs, D_MODEL)),
            hbm, hbm, hbm,
            _const((1, D_MODEL)),
            _const((1, D_MODEL)),
        ],
        out_specs=[
            pl.BlockSpec((None, ct, Lp, D_MODEL), lambda i: (tile(i) // nc, tile(i) % nc, 0, 0)),
            pl.BlockSpec((None, rs, Ls, D_MODEL), lambda i: (0, 0, 0, 0)),
        ],
        out_shape=[
            jax.ShapeDtypeStruct((bq, C, Lp, D_MODEL), F32),
            jax.ShapeDtypeStruct((1, rs, Ls, D_MODEL), F32),
        ],
        scratch_shapes=[
            pltpu.VMEM((D_MODEL, D_FF), BF16),
            pltpu.VMEM((D_MODEL, D_FF), BF16),
            pltpu.VMEM((D_FF, D_MODEL), BF16),
            pltpu.VMEM((FFN_STAGE_SLOTS, D_MODEL // FFN_WEIGHT_CHUNKS, D_FF), F32),
            pltpu.VMEM((FFN_STAGE_SLOTS, D_FF // FFN_WEIGHT_CHUNKS, D_MODEL), F32),
            pltpu.SemaphoreType.DMA((FFN_STAGE_SLOTS,)),
        ],
        compiler_params=_params(1),
        name="ffn",
    )(xp, xs, w_gate, w_up, w_down, g4, g5)


def _mixer(x4, conv_state_t, h0_packed, wts, *, ct, sub, sc, bt, sample):
    bq, C, L, _ = x4.shape
    R = bq * C
    g = [wts["norm_g"][i:i + 1] for i in range(2)]

    v, u4 = _inproj(x4, g[0], wts["w_in"], L=L, ct=ct, sub=sub)
    sw = wts["ssm_L%d" % L]
    nb = R // (bt * sc)
    ssm_args = (u4, sw[0], sw[1], sw[2], wts["d_skip"], h0_packed.reshape(N_LANE_BLOCKS, nb, bt, STATE_W))
    conv_args = (wts["w_dw"], wts["b_dw"], wts["ln_g"], wts["ln_b"])
    y, hl = _ssm(*ssm_args, L=L, C=sc, bt=bt)
    hl = hl.reshape(N_LANE_BLOCKS, nb * bt, STATE_W)
    out_args = (y, x4, wts["w_glu"], wts["w_out"], g[1])
    if sample:
        cact, conv_new = _conv_sample(v, conv_state_t, *conv_args)
        x1 = _mixout(cact, *out_args, L=L, ct=ct, sub=sub)
    else:
        x1 = _mixout(None, *out_args, L=L, ct=ct, sub=sub, conv=(v,) + conv_args)
        tail = v.reshape(L, bq, C, D_CONV)[:, :, C - 2:, :]
        tail = jnp.transpose(tail, (1, 2, 0, 3)).reshape(bq, 2 * L, D_CONV)
        conv_new = tail[:, 2 * L - (CONV_WIDTH - 1):, :]
    return x1, conv_new, hl


PROMPT_L = 16
PROMPT_SUB = 32
PROMPT_CT = 64


def kernel(x_prompt, x_sample, mem_prompt, cache_mem_k, cache_mem_v, state_conv, state_ssm_re, state_ssm_im,
           norm_g, mem_norm_g, w_in, w_dw, b_dw, ln_g, ln_b, lam_re, lam_im, log_dt, b_re, b_im, c_re, c_im,
           d_skip, w_glu, w_out, w_q, w_k, w_v, w_o, w_gate, w_up, w_down):
    depth = w_in.shape[0]
    bp, tp, _ = x_prompt.shape
    bs, ts, _ = x_sample.shape
    assert tp % (PROMPT_L * PROMPT_CT) == 0 and tp >= CONV_WIDTH - 1

    yp = x_prompt.reshape(bp, tp // PROMPT_L, PROMPT_L, D_MODEL)
    ys = x_sample.reshape(1, bs, ts, D_MODEL)
    outs = [[] for _ in range(8)]
    for l in range(depth):
        wts = {
            "norm_g": norm_g[l],
            "w_in": w_in[l],
            "w_dw": w_dw[l],
            "b_dw": b_dw[l][None],
            "ln_g": ln_g[l][None],
            "ln_b": ln_b[l][None],
            "d_skip": d_skip[l].reshape(N_LANE_BLOCKS, 1, LANES),
            "w_glu": w_glu[l],
            "w_out": w_out[l],
        }
        ssm_args = (lam_re[l], lam_im[l], log_dt[l], b_re[l], b_im[l], c_re[l], c_im[l])
        assert ts <= PROMPT_L
        full = _ssm_weights(*ssm_args, PROMPT_L)
        wts["ssm_L%d" % PROMPT_L] = full[:2] + (full[2](PROMPT_L),)
        wts["ssm_L%d" % ts] = _ssm_weights_prefix(full, PROMPT_L, ts)

        kp, vp, kp16, vp16 = _memkv(mem_prompt, mem_norm_g[l][None], w_k[l], w_v[l])
        h0p = jnp.zeros((N_LANE_BLOCKS, bp, STATE_W), F32)
        cp_chunks = tp // PROMPT_L
        xp, cp, hp = _mixer(yp, None, h0p, wts, ct=PROMPT_CT, sub=PROMPT_SUB, sc=cp_chunks, bt=4, sample=False)
        h0s = _pack_state(state_ssm_re[l], state_ssm_im[l])
        xs, cs, hs = _mixer(ys, jnp.transpose(state_conv[l], (1, 0, 2)), h0s, wts, ct=bs, sub=bs, sc=1, bt=bs,
                            sample=True)
        cs = jnp.transpose(cs, (1, 0, 2))

        g2, g3 = norm_g[l][2:3], norm_g[l][3:4]
        n_tiles = bp * cp_chunks // PROMPT_SUB
        bb = bs // n_tiles
        xs = jnp.transpose(xs.reshape(ts, n_tiles, bb, D_MODEL), (1, 0, 2, 3)).reshape(n_tiles, ts * bb, D_MODEL)
        xp, xs = _attn(xp, kp16, vp16, w_q[l], w_o[l], g2, g3, xs, cache_mem_k[l:l + 1], cache_mem_v[l:l + 1],
                       C=cp_chunks, ct=PROMPT_SUB)
        xs = jnp.transpose(xs.reshape(n_tiles, ts, bb, D_MODEL), (1, 0, 2, 3)).reshape(ts, bs, D_MODEL)

        yp, ys = _ffn(xp, xs, w_gate[l], w_up[l], w_down[l], norm_g[l][4:5], norm_g[l][5:6],
                      bq=bp, C=cp_chunks, ct=PROMPT_SUB)

        hp_re, hp_im = _unpack_state(hp)
        hs_re, hs_im = _unpack_state(hs)
        for lst, val in zip(outs, (kp, vp, cp, hp_re, hp_im, cs, hs_re, hs_im)):
            lst.append(val)
    return (yp.reshape(bp, tp, D_MODEL), ys.reshape(bs, ts, D_MODEL)) + tuple(jnp.stack(o) for o in outs)
```

```python
import functools

import jax
import jax.numpy as jnp
from jax import lax
from jax.experimental import pallas as pl
from jax.experimental.pallas import tpu as pltpu

F32 = jnp.float32
BF16 = jnp.bfloat16

D_MODEL = 1024
D_CONV = 512
D_SSM = 512
CONV_WIDTH = 31
N_GROUPS = 32
P_GROUP = 16
N_STATE = 64
N_MEM = 256
N_HEADS = 4
HEAD_DIM = 256
D_FF = 2816
RMS_EPS = 1e-6
LN_EPS = 1e-5

LANES = 128
SUBLANES = 8
GROUPS_PER_BLOCK = LANES // P_GROUP
N_LANE_BLOCKS = D_SSM // LANES
STATE_HALF = GROUPS_PER_BLOCK * N_STATE
STATE_W = 2 * STATE_HALF
MXU_N = 256
VMEM_LIMIT = 56 * 1024 * 1024


def _params(n_axes, vmem=VMEM_LIMIT):
    return pltpu.CompilerParams(dimension_semantics=("arbitrary",) * n_axes, vmem_limit_bytes=vmem)


def _const(shape):
    nd = len(shape)
    return pl.BlockSpec(shape, lambda *_: (0,) * nd, pipeline_mode=pl.Buffered(1))


def _rms(x, g):
    return x * lax.rsqrt(jnp.mean(x * x, axis=-1, keepdims=True) + RMS_EPS) * g


def _cast_weights_once(n_axes, *pairs):
    first = pl.program_id(0) == 0
    for a in range(1, n_axes):
        first = jnp.logical_and(first, pl.program_id(a) == 0)

    @pl.when(first)
    def _():
        for src, dst in pairs:
            dst[...] = src[...].astype(BF16)


def _gather_rows(x_ref, L, c0, n):
    return jnp.concatenate([x_ref[c0:c0 + n, s, :] for s in range(L)], axis=0)


def _sub_tiles(ct, sub):
    return [(c0, sub) for c0 in range(0, ct, sub)]


def _nat_spec(ct, L):
    return pl.BlockSpec((None, ct, L, D_MODEL), lambda b, i: (b, i, 0, 0))


def _tp_spec(ct, L, width, nc):
    return pl.BlockSpec((L, ct, width), lambda b, i: (0, b * nc + i, 0))


def _inproj_kernel(x_ref, g_ref, w_ref, v_ref, u_ref, w16_scr, *, L, ct, sub):
    _cast_weights_once(2, (w_ref, w16_scr))
    for c0, n in _sub_tiles(ct, sub):
        x = _gather_rows(x_ref, L, c0, n)
        h = _rms(x, g_ref[...])
        z = jnp.dot(h.astype(BF16), w16_scr[...], preferred_element_type=F32)
        a = z[:, :D_CONV]
        g = z[:, D_CONV:2 * D_CONV]
        u = z[:, 2 * D_CONV:].astype(BF16)
        v_ref[:, c0:c0 + n, :] = (a * jax.nn.sigmoid(g)).reshape(L, n, D_CONV)
        for s in range(L):
            for j in range(N_LANE_BLOCKS):
                u_ref[j, c0:c0 + n, s * LANES:(s + 1) * LANES] = u[s * n:(s + 1) * n, j * LANES:(j + 1) * LANES]


def _inproj(x4, g0, w_in, *, L, ct, sub):
    bq, c = x4.shape[:2]
    nc = c // ct
    R = bq * c
    return pl.pallas_call(
        functools.partial(_inproj_kernel, L=L, ct=ct, sub=sub),
        grid=(bq, nc),
        in_specs=[_nat_spec(ct, L), _const((1, D_MODEL)), _const((D_MODEL, 2 * D_CONV + D_SSM))],
        out_specs=[
            _tp_spec(ct, L, D_CONV, nc),
            pl.BlockSpec((N_LANE_BLOCKS, ct, L * LANES), lambda b, i: (0, b * nc + i, 0)),
        ],
        out_shape=[
            jax.ShapeDtypeStruct((L, R, D_CONV), F32),
            jax.ShapeDtypeStruct((N_LANE_BLOCKS, R, L * LANES), BF16),
        ],
        scratch_shapes=[pltpu.VMEM(w_in.shape, BF16)],
        compiler_params=_params(2),
        name="inproj",
    )(x4, g0, w_in)


def _ln_silu(acc, g, b):
    mu = jnp.mean(acc, axis=-1, keepdims=True)
    xc = acc - mu
    var = jnp.mean(xc * xc, axis=-1, keepdims=True)
    y = xc * lax.rsqrt(var + LN_EPS) * g + b
    return y * jax.nn.sigmoid(y)


def _conv_shifted_copies(v_ref, v1_ref, v2_ref, *, L, C):
    row = lax.broadcasted_iota(jnp.int32, (C, D_CONV), 0)
    for s in range(L):
        x = v_ref[s]
        v1_ref[s] = jnp.where(row >= 1, pltpu.roll(x, 1, 0), 0.0)
        v2_ref[s] = jnp.where(row >= 2, pltpu.roll(x, 2, 0), 0.0)


def _conv_rows(srcs, w_ref, b_ref, g_ref, bb_ref, o_ref, r0, out_r0, *, L, rc):
    sub8 = rc // SUBLANES
    for s in range(L):
        acc = jnp.broadcast_to(b_ref[...][None], (sub8, SUBLANES, D_CONV))
        for d in range(CONV_WIDTH):
            blk = (s - d) % L
            shift = (d - s + L - 1) // L if d > s else 0
            k = CONV_WIDTH - 1 - d
            src = srcs[shift][blk, pl.ds(r0, rc), :].reshape(sub8, SUBLANES, D_CONV)
            acc = acc + w_ref[k][None] * src
        y = _ln_silu(acc.reshape(rc, D_CONV), g_ref[...], bb_ref[...])
        o_ref[s, pl.ds(out_r0, rc), :] = y.astype(o_ref.dtype)


def _conv_sample_kernel(v_ref, st_ref, w_ref, b_ref, g_ref, bb_ref, o_ref, new_ref, *, L, H):
    def ext(i):
        return st_ref[i] if i < H else v_ref[i - H]

    for t in range(L):
        acc = jnp.broadcast_to(b_ref[...], v_ref.shape[1:])
        for k in range(CONV_WIDTH):
            acc = acc + w_ref[k:k + 1, :] * ext(t + k)
        o_ref[t] = _ln_silu(acc, g_ref[...], bb_ref[...]).astype(o_ref.dtype)
    for i in range(H):
        new_ref[i] = ext(i + L)


def _conv_sample(v, state_t, w_dw, b_dw, ln_g, ln_b):
    L, R, _ = v.shape
    H = CONV_WIDTH - 1
    rt = 32
    rows = lambda n: pl.BlockSpec((n, rt, D_CONV), lambda i: (0, i, 0))
    return pl.pallas_call(
        functools.partial(_conv_sample_kernel, L=L, H=H),
        grid=(R // rt,),
        in_specs=[
            rows(L),
            rows(H),
            _const((CONV_WIDTH, D_CONV)),
            _const((1, D_CONV)),
            _const((1, D_CONV)),
            _const((1, D_CONV)),
        ],
        out_specs=[rows(L), rows(H)],
        out_shape=[
            jax.ShapeDtypeStruct((L, R, D_CONV), BF16),
            jax.ShapeDtypeStruct((H, R, D_CONV), F32),
        ],
        compiler_params=_params(1),
        name="conv_sample",
    )(v, state_t, w_dw, b_dw, ln_g, ln_b)


def _expand_block_diag(d):
    tiled = jnp.concatenate([d] * GROUPS_PER_BLOCK, axis=-1)
    r = lax.broadcasted_iota(jnp.int32, tiled.shape, 0) // P_GROUP
    c = lax.broadcasted_iota(jnp.int32, tiled.shape, 1) // N_STATE
    return jnp.where(r == c, tiled, jnp.zeros_like(tiled))


def _ssm_kernel(u_ref, wbc_ref, wcc_ref, a_ref, d_ref, h0_ref, y_ref, hl_ref,
                w_scr, wb_scr, wct_scr, s_scr, hp_scr, tap_scr, *, L, C, bt):
    per_tile = MXU_N // LANES

    @pl.when(pl.program_id(1) == 0)
    def _():
        for s in range(L):
            rows = slice(s * LANES, (s + 1) * LANES)
            for a in range(2):
                cols = slice(a * STATE_HALF, (a + 1) * STATE_HALF)
                wb_scr[rows, cols] = _expand_block_diag(wbc_ref[s, a])
                wct_scr[rows, cols] = _expand_block_diag(wcc_ref[s + 1, a])
        bbar = wb_scr[(L - 1) * LANES:L * LANES, :]
        nt = (((1,), (1,)), ((), ()))
        c0 = jnp.concatenate([_expand_block_diag(wcc_ref[0, a]) for a in range(2)], axis=-1)
        tap_scr[0] = lax.dot_general(bbar, c0, nt, preferred_element_type=F32).astype(BF16)
        for t in range(1, L):
            tap_scr[t] = lax.dot_general(bbar, wct_scr[(t - 1) * LANES:t * LANES, :], nt,
                                         preferred_element_type=F32).astype(BF16)
        for sp in range(L):
            for s in range(L):
                rows = slice(s * LANES, (s + 1) * LANES)
                cols = slice(sp * LANES, (sp + 1) * LANES)
                if s <= sp:
                    w_scr[rows, cols] = tap_scr[sp - s]
                elif s // per_tile == sp // per_tile:
                    w_scr[rows, cols] = jnp.zeros((LANES, LANES), BF16)

    nk = STATE_W // LANES
    half = nk // 2
    rows = bt * C
    inc_all = jnp.dot(u_ref[...], wb_scr[...], preferred_element_type=F32)
    for k in range(half):
        s_scr[k, 0:rows, :] = inc_all[:, k * LANES:(k + 1) * LANES]
        s_scr[k, rows:2 * rows, :] = inc_all[:, (half + k) * LANES:(half + k + 1) * LANES]

    d = d_ref[...]
    for n in range(L // per_tile):
        k = (n + 1) * MXU_N
        y = jnp.dot(u_ref[:, :k], w_scr[:k, n * MXU_N:(n + 1) * MXU_N], preferred_element_type=F32)
        for h in range(per_tile):
            s = n * per_tile + h
            us = u_ref[:, s * LANES:(s + 1) * LANES].astype(F32)
            y_ref[s] = y[:, h * LANES:(h + 1) * LANES] + d * us

    h0 = h0_ref[...]
    a = a_ref[...]

    def piece(x, k):
        return x[:, k * LANES:(k + 1) * LANES]

    def swap_parts(x):
        if 2 * bt == SUBLANES:
            return pltpu.roll(x, bt, 0)
        return jnp.concatenate([x[bt:], x[:bt]], axis=0)

    im_rows = lax.broadcasted_iota(jnp.int32, (2 * bt, LANES), 0) >= bt
    a_same = [jnp.broadcast_to(piece(a[0:1], k), (2 * bt, LANES)) for k in range(half)]
    a_cross = [jnp.where(im_rows, piece(a[1:2], k), -piece(a[1:2], k)) for k in range(half)]
    state = [jnp.concatenate([piece(h0, k), piece(h0, half + k)], axis=0) for k in range(half)]
    for c in range(C):
        sel = pl.ds(c, 2 * bt, stride=C) if C > 1 else pl.ds(0, 2 * bt)
        for k in range(half):
            hp_scr[k, sel, :] = state[k]
            state[k] = a_same[k] * state[k] + a_cross[k] * swap_parts(state[k]) + s_scr[k, sel, :]
    for k in range(half):
        hl_ref[:, k * LANES:(k + 1) * LANES] = state[k][:bt]
        hl_ref[:, (half + k) * LANES:(half + k + 1) * LANES] = state[k][bt:]

    hp = jnp.concatenate([hp_scr[k, 0:rows, :] for k in range(half)]
                         + [hp_scr[k, rows:2 * rows, :] for k in range(half)], axis=-1).astype(BF16)
    for n in range(L // per_tile):
        y = lax.dot_general(hp, wct_scr[n * MXU_N:(n + 1) * MXU_N, :], (((1,), (1,)), ((), ())),
                            preferred_element_type=F32)
        for h in range(per_tile):
            s = n * per_tile + h
            y_ref[s] += y[:, h * LANES:(h + 1) * LANES]


def _ssm(u4, wbc, wcc, a_pow, d_skip, h0, *, L, C, bt):
    R = u4.shape[1]
    rows = bt * C
    nb = R // rows
    lw = L * LANES
    in_specs = [
        pl.BlockSpec((None, rows, lw), lambda j, b: (j, b, 0)),
        pl.BlockSpec((None, L, 2, LANES, N_STATE), lambda j, b: (j, 0, 0, 0, 0)),
        pl.BlockSpec((None, L + 1, 2, LANES, N_STATE), lambda j, b: (j, 0, 0, 0, 0)),
        pl.BlockSpec((None, 2, STATE_HALF), lambda j, b: (j, 0, 0)),
        pl.BlockSpec((None, 1, LANES), lambda j, b: (j, 0, 0)),
        pl.BlockSpec((None, None, bt, STATE_W), lambda j, b: (j, b, 0, 0)),
    ]
    out_specs = [
        pl.BlockSpec((L, rows, LANES), lambda j, b: (0, b, j)),
        pl.BlockSpec((None, None, bt, STATE_W), lambda j, b: (j, b, 0, 0)),
    ]
    out_shape = [
        jax.ShapeDtypeStruct((L, R, D_SSM), F32),
        jax.ShapeDtypeStruct((N_LANE_BLOCKS, nb, bt, STATE_W), F32),
    ]
    scratch = [
        pltpu.VMEM((lw, lw), BF16),
        pltpu.VMEM((lw, STATE_W), BF16),
        pltpu.VMEM((lw, STATE_W), BF16),
        pltpu.VMEM((STATE_HALF // LANES, 2 * rows, LANES), F32),
        pltpu.VMEM((STATE_HALF // LANES, 2 * rows, LANES), F32),
        pltpu.VMEM((L, LANES, LANES), BF16),
    ]
    return pl.pallas_call(
        functools.partial(_ssm_kernel, L=L, C=C, bt=bt),
        grid=(N_LANE_BLOCKS, nb),
        in_specs=in_specs,
        out_specs=out_specs,
        out_shape=out_shape,
        scratch_shapes=scratch,
        compiler_params=_params(2),
        name="ssm",
    )(u4, wbc, wcc, a_pow, d_skip, h0)


def _ssm_weights(lam_re, lam_im, log_dt, b_re, b_im, c_re, c_im, L):
    dt = jnp.exp(log_dt)[:, None]
    zr = lam_re * dt
    zi = lam_im * dt
    n_pow = jnp.arange(L + 1, dtype=F32)[:, None, None]
    mag = jnp.exp(zr[None] * n_pow)
    pr = mag * jnp.cos(zi[None] * n_pow)
    pi = mag * jnp.sin(zi[None] * n_pow)
    a1r, a1i = pr[1], pi[1]
    den = lam_re * lam_re + lam_im * lam_im
    qr = ((a1r - 1.0) * lam_re + a1i * lam_im) / den
    qi = (a1i * lam_re - (a1r - 1.0) * lam_im) / den
    bbr = qr[:, :, None] * b_re - qi[:, :, None] * b_im
    bbi = qr[:, :, None] * b_im + qi[:, :, None] * b_re

    nj, g8 = N_LANE_BLOCKS, GROUPS_PER_BLOCK

    n_rev = (L - 1.0) - jnp.arange(L, dtype=F32)[:, None, None]
    mag_rev = jnp.exp(zr[None] * n_rev)
    rev = mag_rev * jnp.cos(zi[None] * n_rev), mag_rev * jnp.sin(zi[None] * n_rev)
    er = rev[0][:, :, :, None] * bbr[None] - rev[1][:, :, :, None] * bbi[None]
    ei = rev[0][:, :, :, None] * bbi[None] + rev[1][:, :, :, None] * bbr[None]
    e = jnp.stack([er, ei], 1).reshape(L, 2, nj, g8, N_STATE, P_GROUP)
    wbc = jnp.transpose(e, (2, 0, 1, 3, 5, 4)).reshape(nj, L, 2, LANES, N_STATE)

    mr = c_re[None] * pr[:, :, None, :] - c_im[None] * pi[:, :, None, :]
    mi = c_re[None] * pi[:, :, None, :] + c_im[None] * pr[:, :, None, :]
    m = jnp.stack([mr, -mi], 1).reshape(L + 1, 2, nj, g8, P_GROUP, N_STATE)
    wcc = jnp.transpose(m, (2, 0, 1, 3, 4, 5)).reshape(nj, L + 1, 2, LANES, N_STATE)

    def a_pow(n):
        return jnp.stack([pr[n].reshape(nj, STATE_HALF), pi[n].reshape(nj, STATE_HALF)], 1)

    return wbc.astype(BF16), wcc.astype(BF16), a_pow


def _ssm_weights_prefix(full, L_full, L):
    wbc, wcc, a_pow = full
    return wbc[:, L_full - L:], wcc[:, :L + 1], a_pow(L)


def _pack_state(re, im):
    b = re.shape[0]
    r = re.reshape(b, N_LANE_BLOCKS, STATE_HALF)
    i = im.reshape(b, N_LANE_BLOCKS, STATE_HALF)
    return jnp.transpose(jnp.concatenate([r, i], -1), (1, 0, 2))


def _unpack_state(h):
    b = h.shape[1]
    h = jnp.transpose(h, (1, 0, 2))
    re = h[:, :, :STATE_HALF].reshape(b, N_GROUPS, N_STATE)
    im = h[:, :, STATE_HALF:].reshape(b, N_GROUPS, N_STATE)
    return re, im


def _mixout_kernel(*refs, L, ct, sub, conv_c):
    if conv_c:
        (v_ref, cw_ref, cb_ref, lg_ref, lb_ref, y_ref, x_ref, wglu_ref, wout_ref, g_ref, o_ref,
         wglu_scr, wout_scr, v1_scr, v2_scr, c_ref) = refs
    else:
        c_ref, y_ref, x_ref, wglu_ref, wout_ref, g_ref, o_ref, wglu_scr, wout_scr = refs
    _cast_weights_once(2, (wglu_ref, wglu_scr), (wout_ref, wout_scr))
    tiles = _sub_tiles(ct, sub)
    gy = [jax.nn.gelu(y_ref[:, c0:c0 + n, :].reshape(L * n, D_SSM)) for c0, n in tiles]
    gate = [jnp.dot(g.astype(BF16), wglu_scr[...], preferred_element_type=F32) for g in gy]
    if conv_c:
        step = pl.program_id(1)

        @pl.when(step == 0)
        def _():
            _conv_shifted_copies(v_ref, v1_scr, v2_scr, L=L, C=conv_c)

        base = pl.multiple_of(step * ct, ct)
        for c0, n in tiles:
            _conv_rows((v_ref, v1_scr, v2_scr), cw_ref, cb_ref, lg_ref, lb_ref, c_ref, base + c0, c0, L=L, rc=n)
    m = [jnp.dot(c_ref[:, c0:c0 + n, :].reshape(L * n, D_CONV), wout_scr[0:D_CONV, :],
                 preferred_element_type=F32) for c0, n in tiles]
    sg = [g * jax.nn.sigmoid(t) for g, t in zip(gy, gate)]
    m = [a + jnp.dot(s.astype(BF16), wout_scr[D_CONV:, :], preferred_element_type=F32) for a, s in zip(m, sg)]
    for (c0, n), mm in zip(tiles, m):
        x = _gather_rows(x_ref, L, c0, n)
        o_ref[:, c0:c0 + n, :] = (x + _rms(mm, g_ref[...])).reshape(L, n, D_MODEL)


def _mixout(c, y, x4, w_glu, w_out, g1, *, L, ct, sub, conv=None):
    bq, cc = x4.shape[:2]
    nc = cc // ct
    R = bq * cc
    in_specs = [
        _tp_spec(ct, L, D_SSM, nc),
        _nat_spec(ct, L),
        _const((D_SSM, D_SSM)),
        _const((D_CONV + D_SSM, D_MODEL)),
        _const((1, D_MODEL)),
    ]
    args = [y, x4, w_glu, w_out, g1]
    scratch = [pltpu.VMEM(w_glu.shape, BF16), pltpu.VMEM(w_out.shape, BF16)]
    if conv is None:
        in_specs = [_tp_spec(ct, L, D_CONV, nc)] + in_specs
        args = [c] + args
    else:
        v, w_dw, b_dw, ln_g, ln_b = conv
        in_specs = [
            pl.BlockSpec((L, cc, D_CONV), lambda b, i: (0, b, 0)),
            _const((CONV_WIDTH, SUBLANES, D_CONV)),
            _const((SUBLANES, D_CONV)),
            _const((1, D_CONV)),
            _const((1, D_CONV)),
        ] + in_specs
        args = [v, jnp.broadcast_to(w_dw[:, None, :], (CONV_WIDTH, SUBLANES, D_CONV)),
                jnp.broadcast_to(b_dw, (SUBLANES, D_CONV)), ln_g, ln_b] + args
        scratch += [pltpu.VMEM((L, cc, D_CONV), F32), pltpu.VMEM((L, cc, D_CONV), F32),
                    pltpu.VMEM((L, ct, D_CONV), BF16)]
    return pl.pallas_call(
        functools.partial(_mixout_kernel, L=L, ct=ct, sub=sub, conv_c=0 if conv is None else cc),
        grid=(bq, nc),
        in_specs=in_specs,
        out_specs=_tp_spec(ct, L, D_MODEL, nc),
        out_shape=jax.ShapeDtypeStruct((L, R, D_MODEL), F32),
        scratch_shapes=scratch,
        compiler_params=_params(2),
        name="mixout",
    )(*args)


def _memkv_kernel(m_ref, g_ref, wk_ref, wv_ref, k5_ref, v5_ref, kb_ref, vb_ref, wk_scr, wv_scr, *, nb):
    _cast_weights_once(1, (wk_ref, wk_scr), (wv_ref, wv_scr))
    ms = [_rms(m_ref[b], g_ref[...]).astype(BF16) for b in range(nb)]
    for w_ref, o5_ref, ob_ref in ((wk_scr, k5_ref, kb_ref), (wv_scr, v5_ref, vb_ref)):
        ps = [jnp.dot(m, w_ref[...], preferred_element_type=F32) for m in ms]
        for b, p in enumerate(ps):
            ob_ref[b] = p.astype(BF16)
            for hd in range(N_HEADS):
                o5_ref[b, :, hd, :] = p[:, hd * HEAD_DIM:(hd + 1) * HEAD_DIM]


def _memkv(mem, g_mem, w_k, w_v):
    bq = mem.shape[0]
    nb = 2
    out5 = pl.BlockSpec((nb, N_MEM, N_HEADS, HEAD_DIM), lambda i: (i, 0, 0, 0))
    outb = pl.BlockSpec((nb, N_MEM, D_MODEL), lambda i: (i, 0, 0))
    return pl.pallas_call(
        functools.partial(_memkv_kernel, nb=nb),
        grid=(bq // nb,),
        in_specs=[
            pl.BlockSpec((nb, N_MEM, D_MODEL), lambda i: (i, 0, 0)),
            _const((1, D_MODEL)),
            _const((D_MODEL, D_MODEL)),
            _const((D_MODEL, D_MODEL)),
        ],
        out_specs=[out5, out5, outb, outb],
        out_shape=[jax.ShapeDtypeStruct((bq, N_MEM, N_HEADS, HEAD_DIM), F32)] * 2
        + [jax.ShapeDtypeStruct((bq, N_MEM, D_MODEL), BF16)] * 2,
        scratch_shapes=[pltpu.VMEM(w_k.shape, BF16), pltpu.VMEM(w_v.shape, BF16)],
        compiler_params=_params(1),
        name="memkv",
    )(mem, g_mem, w_k, w_v)


def _softmax_rows(s):
    s = s - jnp.max(s, axis=-1, keepdims=True)
    e = jnp.exp(s)
    return e / jnp.sum(e, axis=-1, keepdims=True)


CACHE_SLOTS = 3


def _attn_kernel(x_ref, k_ref, v_ref, wq32_ref, wo32_ref, gq_ref, go_ref, xs_ref, ck_ref, cv_ref, o_ref, os_ref,
                 wq_ref, wo_ref, kbuf, vbuf, sem, *, L, ct, bb, n):
    _cast_weights_once(1, (wq32_ref, wq_ref), (wo32_ref, wo_ref))

    i = pl.program_id(0)

    def fetch(t):
        slot = lax.rem(t, CACHE_SLOTS)
        return (pltpu.make_async_copy(ck_ref.at[0, pl.ds(t * bb, bb)], kbuf.at[slot], sem.at[0, slot]),
                pltpu.make_async_copy(cv_ref.at[0, pl.ds(t * bb, bb)], vbuf.at[slot], sem.at[1, slot]))

    def start(t):
        for c in fetch(t):
            c.start()

    @pl.when(i == 0)
    def _():
        start(i)
        start(i + 1)

    @pl.when(i + 2 < n)
    def _():
        start(i + 2)

    for c in fetch(i):
        c.wait()
    cur = lax.rem(i, CACHE_SLOTS)

    nt = (((1,), (1,)), ((), ()))
    scale = HEAD_DIM ** -0.5
    heads = [slice(hd * HEAD_DIM, (hd + 1) * HEAD_DIM) for hd in range(N_HEADS)]
    np_rows = L * ct
    x = jnp.concatenate([x_ref[...].reshape(np_rows, D_MODEL), xs_ref[...]], axis=0)
    kb, vb = k_ref[...], v_ref[...]

    rows = xs_ref.shape[0]
    nr = N_HEADS * rows
    nc = N_MEM * N_HEADS
    same_head = (lax.broadcasted_iota(jnp.int32, (nr, nc), 0) // rows
                 == lax.broadcasted_iota(jnp.int32, (nr, nc), 1) % N_HEADS)
    owner = lax.broadcasted_iota(jnp.int32, (nr, HEAD_DIM), 0) % bb

    h = _rms(x, gq_ref[...])
    ka = [kbuf[cur, b].reshape(nc, HEAD_DIM).astype(BF16) for b in range(bb)]
    q_all = jnp.dot(h.astype(BF16), wq_ref[...], preferred_element_type=F32).astype(BF16)
    q, q2 = q_all[:np_rows], q_all[np_rows:]
    qs = jnp.concatenate([q2[:, sl] for sl in heads], axis=0)
    va = [vbuf[cur, b].reshape(nc, HEAD_DIM).astype(BF16) for b in range(bb)]
    sc_p = [lax.dot_general(q[:, sl], kb[:, sl], nt, preferred_element_type=F32) for sl in heads]
    sc_s = [lax.dot_general(qs, kk, nt, preferred_element_type=F32) for kk in ka]
    p_p = [_softmax_rows(s * scale).astype(BF16) for s in sc_p]
    p_s = [_softmax_rows(jnp.where(same_head, s * scale, -1e30)).astype(BF16) for s in sc_s]
    o_p = jnp.concatenate([jnp.dot(pp, vb[:, sl], preferred_element_type=F32) for pp, sl in zip(p_p, heads)], axis=-1)
    o_s = [jnp.dot(pp, vv, preferred_element_type=F32) for pp, vv in zip(p_s, va)]
    acc = jnp.zeros((nr, HEAD_DIM), F32)
    for b, ob in enumerate(o_s):
        acc = jnp.where(owner == b, ob, acc)
    o_all = jnp.concatenate(
        [o_p, jnp.concatenate([acc[hd * rows:(hd + 1) * rows] for hd in range(N_HEADS)], axis=-1)], axis=0)
    a = jnp.dot(o_all.astype(BF16), wo_ref[...], preferred_element_type=F32)
    y = x + _rms(a, go_ref[...])
    o_ref[...] = y[:np_rows].reshape(L, ct, D_MODEL)
    os_ref[...] = y[np_rows:]


def _attn(x1, k, v, w_q, w_o, g2, g3, xs_grp, cache_k, cache_v, *, C, ct):
    L, R, _ = x1.shape
    nc = C // ct
    n, rows_s, _ = xs_grp.shape
    bb = cache_k.shape[1] // n
    assert n == R // ct and n * bb == cache_k.shape[1]
    kv_spec = pl.BlockSpec(memory_space=pl.ANY)
    grp_spec = pl.BlockSpec((None, rows_s, D_MODEL), lambda i: (i, 0, 0))
    return pl.pallas_call(
        functools.partial(_attn_kernel, L=L, ct=ct, bb=bb, n=n),
        grid=(n,),
        in_specs=[
            pl.BlockSpec((L, ct, D_MODEL), lambda i: (0, i, 0)),
            pl.BlockSpec((None, N_MEM, D_MODEL), lambda i: (i // nc, 0, 0)),
            pl.BlockSpec((None, N_MEM, D_MODEL), lambda i: (i // nc, 0, 0)),
            _const((D_MODEL, D_MODEL)),
            _const((D_MODEL, D_MODEL)),
            _const((1, D_MODEL)),
            _const((1, D_MODEL)),
            grp_spec, kv_spec, kv_spec,
        ],
        out_specs=[pl.BlockSpec((L, ct, D_MODEL), lambda i: (0, i, 0)), grp_spec],
        out_shape=[jax.ShapeDtypeStruct((L, R, D_MODEL), F32), jax.ShapeDtypeStruct(xs_grp.shape, F32)],
        scratch_shapes=[
            pltpu.VMEM(w_q.shape, BF16),
            pltpu.VMEM(w_o.shape, BF16),
            pltpu.VMEM((CACHE_SLOTS, bb, N_MEM, N_HEADS, HEAD_DIM), F32),
            pltpu.VMEM((CACHE_SLOTS, bb, N_MEM, N_HEADS, HEAD_DIM), F32),
            pltpu.SemaphoreType.DMA((2, CACHE_SLOTS)),
        ],
        compiler_params=_params(1),
        name="attn",
    )(x1, k, v, w_q, w_o, g2, g3, xs_grp, cache_k, cache_v)


FFN_WEIGHT_CHUNKS = 16
FFN_STAGE_SLOTS = 4
FFN_ROW_SPLIT = 2


def _stream_cast(w_hbm, w16_scr, stage, sem):
    slots, rpc = stage.shape[0], stage.shape[1]
    n = w_hbm.shape[0] // rpc

    def chunk(k):
        return pltpu.make_async_copy(w_hbm.at[pl.ds(k * rpc, rpc)], stage.at[k % slots], sem.at[k % slots])

    for k in range(min(slots - 1, n)):
        chunk(k).start()
    for k in range(n):
        if k + slots - 1 < n:
            chunk(k + slots - 1).start()
        chunk(k).wait()
        w16_scr[pl.ds(k * rpc, rpc), :] = stage[k % slots].astype(BF16)


def _ffn_tile(x_ref, o_ref, wg_scr, wu_scr, wd_scr, gi_ref, go_ref, *, L, ct):
    rows = L * ct
    x = x_ref[...].reshape(rows, D_MODEL)
    rs = rows // FFN_ROW_SPLIT
    xs = [x[i * rs:(i + 1) * rs] for i in range(FFN_ROW_SPLIT)]
    h = [_rms(xx, gi_ref[...]).astype(BF16) for xx in xs]
    gate = [jnp.dot(hh, wg_scr[...], preferred_element_type=F32) for hh in h]
    up = [jnp.dot(hh, wu_scr[...], preferred_element_type=F32) for hh in h]
    act = [(g * jax.nn.sigmoid(g) * u).astype(BF16) for g, u in zip(gate, up)]
    dn = [jnp.dot(a, wd_scr[...], preferred_element_type=F32) for a in act]
    y = jnp.concatenate([xx + _rms(d, go_ref[...]) for xx, d in zip(xs, dn)], axis=0)
    for s in range(L):
        o_ref[:, s, :] = y[s * ct:(s + 1) * ct]


def _ffn_kernel(xp_ref, xs_ref, wg_hbm, wu_hbm, wd_hbm, gi_ref, go_ref, op_ref, os_ref,
                wg_scr, wu_scr, wd_scr, stage_in, stage_out, sem, *, n_prompt, Lp, ctp, Ls, cts):
    i = pl.program_id(0)

    @pl.when(i == 0)
    def _():
        _stream_cast(wg_hbm, wg_scr, stage_in, sem)
        _stream_cast(wu_hbm, wu_scr, stage_in, sem)
        _stream_cast(wd_hbm, wd_scr, stage_out, sem)

    @pl.when(i < n_prompt)
    def _():
        _ffn_tile(xp_ref, op_ref, wg_scr, wu_scr, wd_scr, gi_ref, go_ref, L=Lp, ct=ctp)

    @pl.when(i == n_prompt)
    def _():
        _ffn_tile(xs_ref, os_ref, wg_scr, wu_scr, wd_scr, gi_ref, go_ref, L=Ls, ct=cts)


def _ffn(xp, xs, w_gate, w_up, w_down, g4, g5, *, bq, C, ct):
    Lp = xp.shape[0]
    Ls, rs, _ = xs.shape
    nc = C // ct
    n = bq * nc
    tile = lambda i: jnp.minimum(i, n - 1)
    hbm = pl.BlockSpec(memory_space=pl.ANY)
    return pl.pallas_call(
        functools.partial(_ffn_kernel, n_prompt=n, Lp=Lp, ctp=ct, Ls=Ls, cts=rs),
        grid=(n + 1,),
        in_specs=[
            pl.BlockSpec((Lp, ct, D_MODEL), lambda i: (0, tile(i), 0)),
            _const((Ls, rs, D_MODEL)),
            hbm, hbm, hbm,
            _const((1, D_MODEL)),
            _const((1, D_MODEL)),
        ],
        out_specs=[
            pl.BlockSpec((None, ct, Lp, D_MODEL), lambda i: (tile(i) // nc, tile(i) % nc, 0, 0)),
            pl.BlockSpec((None, rs, Ls, D_MODEL), lambda i: (0, 0, 0, 0)),
        ],
        out_shape=[
            jax.ShapeDtypeStruct((bq, C, Lp, D_MODEL), F32),
            jax.ShapeDtypeStruct((1, rs, Ls, D_MODEL), F32),
        ],
        scratch_shapes=[
            pltpu.VMEM((D_MODEL, D_FF), BF16),
            pltpu.VMEM((D_MODEL, D_FF), BF16),
            pltpu.VMEM((D_FF, D_MODEL), BF16),
            pltpu.VMEM((FFN_STAGE_SLOTS, D_MODEL // FFN_WEIGHT_CHUNKS, D_FF), F32),
            pltpu.VMEM((FFN_STAGE_SLOTS, D_FF // FFN_WEIGHT_CHUNKS, D_MODEL), F32),
            pltpu.SemaphoreType.DMA((FFN_STAGE_SLOTS,)),
        ],
        compiler_params=_params(1),
        name="ffn",
    )(xp, xs, w_gate, w_up, w_down, g4, g5)


def _mixer(x4, conv_state_t, h0_packed, wts, *, ct, sub, sc, bt, sample):
    bq, C, L, _ = x4.shape
    R = bq * C
    g = [wts["norm_g"][i:i + 1] for i in range(2)]

    v, u4 = _inproj(x4, g[0], wts["w_in"], L=L, ct=ct, sub=sub)
    sw = wts["ssm_L%d" % L]
    nb = R // (bt * sc)
    ssm_args = (u4, sw[0], sw[1], sw[2], wts["d_skip"], h0_packed.reshape(N_LANE_BLOCKS, nb, bt, STATE_W))
    conv_args = (wts["w_dw"], wts["b_dw"], wts["ln_g"], wts["ln_b"])
    y, hl = _ssm(*ssm_args, L=L, C=sc, bt=bt)
    hl = hl.reshape(N_LANE_BLOCKS, nb * bt, STATE_W)
    out_args = (y, x4, wts["w_glu"], wts["w_out"], g[1])
    if sample:
        cact, conv_new = _conv_sample(v, conv_state_t, *conv_args)
        x1 = _mixout(cact, *out_args, L=L, ct=ct, sub=sub)
    else:
        x1 = _mixout(None, *out_args, L=L, ct=ct, sub=sub, conv=(v,) + conv_args)
        tail = v.reshape(L, bq, C, D_CONV)[:, :, C - 2:, :]
        tail = jnp.transpose(tail, (1, 2, 0, 3)).reshape(bq, 2 * L, D_CONV)
        conv_new = tail[:, 2 * L - (CONV_WIDTH - 1):, :]
    return x1, conv_new, hl


PROMPT_L = 16
PROMPT_SUB = 32
PROMPT_CT = 64


def kernel(x_prompt, x_sample, mem_prompt, cache_mem_k, cache_mem_v, state_conv, state_ssm_re, state_ssm_im,
           norm_g, mem_norm_g, w_in, w_dw, b_dw, ln_g, ln_b, lam_re, lam_im, log_dt, b_re, b_im, c_re, c_im,
           d_skip, w_glu, w_out, w_q, w_k, w_v, w_o, w_gate, w_up, w_down):
    depth = w_in.shape[0]
    bp, tp, _ = x_prompt.shape
    bs, ts, _ = x_sample.shape
    assert tp % (PROMPT_L * PROMPT_CT) == 0 and tp >= CONV_WIDTH - 1

    yp = x_prompt.reshape(bp, tp // PROMPT_L, PROMPT_L, D_MODEL)
    ys = x_sample.reshape(1, bs, ts, D_MODEL)
    outs = [[] for _ in range(8)]
    for l in range(depth):
        wts = {
            "norm_g": norm_g[l],
            "w_in": w_in[l],
            "w_dw": w_dw[l],
            "b_dw": b_dw[l][None],
            "ln_g": ln_g[l][None],
            "ln_b": ln_b[l][None],
            "d_skip": d_skip[l].reshape(N_LANE_BLOCKS, 1, LANES),
            "w_glu": w_glu[l],
            "w_out": w_out[l],
        }
        ssm_args = (lam_re[l], lam_im[l], log_dt[l], b_re[l], b_im[l], c_re[l], c_im[l])
        assert ts <= PROMPT_L
        full = _ssm_weights(*ssm_args, PROMPT_L)
        wts["ssm_L%d" % PROMPT_L] = full[:2] + (full[2](PROMPT_L),)
        wts["ssm_L%d" % ts] = _ssm_weights_prefix(full, PROMPT_L, ts)

        kp, vp, kp16, vp16 = _memkv(mem_prompt, mem_norm_g[l][None], w_k[l], w_v[l])
        h0p = jnp.zeros((N_LANE_BLOCKS, bp, STATE_W), F32)
        cp_chunks = tp // PROMPT_L
        xp, cp, hp = _mixer(yp, None, h0p, wts, ct=PROMPT_CT, sub=PROMPT_SUB, sc=cp_chunks, bt=4, sample=False)
        h0s = _pack_state(state_ssm_re[l], state_ssm_im[l])
        xs, cs, hs = _mixer(ys, jnp.transpose(state_conv[l], (1, 0, 2)), h0s, wts, ct=bs, sub=bs, sc=1, bt=bs,
                            sample=True)
        cs = jnp.transpose(cs, (1, 0, 2))

        g2, g3 = norm_g[l][2:3], norm_g[l][3:4]
        n_tiles = bp * cp_chunks // PROMPT_SUB
        bb = bs // n_tiles
        xs = jnp.transpose(xs.reshape(ts, n_tiles, bb, D_MODEL), (1, 0, 2, 3)).reshape(n_tiles, ts * bb, D_MODEL)
        xp, xs = _attn(xp, kp16, vp16, w_q[l], w_o[l], g2, g3, xs, cache_mem_k[l:l + 1], cache_mem_v[l:l + 1],
                       C=cp_chunks, ct=PROMPT_SUB)
        xs = jnp.transpose(xs.reshape(n_tiles, ts, bb, D_MODEL), (1, 0, 2, 3)).reshape(ts, bs, D_MODEL)

        yp, ys = _ffn(xp, xs, w_gate[l], w_up[l], w_down[l], norm_g[l][4:5], norm_g[l][5:6],
                      bq=bp, C=cp_chunks, ct=PROMPT_SUB)

        hp_re, hp_im = _unpack_state(hp)
        hs_re, hs_im = _unpack_state(hs)
        for lst, val in zip(outs, (kp, vp, cp, hp_re, hp_im, cs, hs_re, hs_im)):
            lst.append(val)
    return (yp.reshape(bp, tp, D_MODEL), ys.reshape(bs, ts, D_MODEL)) + tuple(jnp.stack(o) for o in outs)
```

```python
import functools

import jax
import jax.numpy as jnp
from jax import lax
from jax.experimental import pallas as pl
from jax.experimental.pallas import tpu as pltpu

F32 = jnp.float32
BF16 = jnp.bfloat16

D_MODEL = 1024
D_CONV = 512
D_SSM = 512
CONV_WIDTH = 31
N_GROUPS = 32
P_GROUP = 16
N_STATE = 64
N_MEM = 256
N_HEADS = 4
HEAD_DIM = 256
D_FF = 2816
RMS_EPS = 1e-6
LN_EPS = 1e-5

LANES = 128
SUBLANES = 8
GROUPS_PER_BLOCK = LANES // P_GROUP
N_LANE_BLOCKS = D_SSM // LANES
STATE_HALF = GROUPS_PER_BLOCK * N_STATE
STATE_W = 2 * STATE_HALF
MXU_N = 256
VMEM_LIMIT = 56 * 1024 * 1024


def _params(n_axes, vmem=VMEM_LIMIT):
    return pltpu.CompilerParams(dimension_semantics=("arbitrary",) * n_axes, vmem_limit_bytes=vmem)


def _const(shape):
    nd = len(shape)
    return pl.BlockSpec(shape, lambda *_: (0,) * nd, pipeline_mode=pl.Buffered(1))


def _rms(x, g):
    return x * lax.rsqrt(jnp.mean(x * x, axis=-1, keepdims=True) + RMS_EPS) * g


def _cast_weights_once(n_axes, *pairs):
    first = pl.program_id(0) == 0
    for a in range(1, n_axes):
        first = jnp.logical_and(first, pl.program_id(a) == 0)

    @pl.when(first)
    def _():
        for src, dst in pairs:
            dst[...] = src[...].astype(BF16)


def _gather_rows(x_ref, L, c0, n):
    return jnp.concatenate([x_ref[c0:c0 + n, s, :] for s in range(L)], axis=0)


def _sub_tiles(ct, sub):
    return [(c0, sub) for c0 in range(0, ct, sub)]


def _nat_spec(ct, L):
    return pl.BlockSpec((None, ct, L, D_MODEL), lambda b, i: (b, i, 0, 0))


def _tp_spec(ct, L, width, nc):
    return pl.BlockSpec((L, ct, width), lambda b, i: (0, b * nc + i, 0))


def _inproj_kernel(x_ref, g_ref, w_ref, v_ref, u_ref, w16_scr, *, L, ct, sub):
    _cast_weights_once(2, (w_ref, w16_scr))
    for c0, n in _sub_tiles(ct, sub):
        x = _gather_rows(x_ref, L, c0, n)
        h = _rms(x, g_ref[...])
        z = jnp.dot(h.astype(BF16), w16_scr[...], preferred_element_type=F32)
        a = z[:, :D_CONV]
        g = z[:, D_CONV:2 * D_CONV]
        u = z[:, 2 * D_CONV:].astype(BF16)
        v_ref[:, c0:c0 + n, :] = (a * jax.nn.sigmoid(g)).reshape(L, n, D_CONV)
        for s in range(L):
            for j in range(N_LANE_BLOCKS):
                u_ref[j, c0:c0 + n, s * LANES:(s + 1) * LANES] = u[s * n:(s + 1) * n, j * LANES:(j + 1) * LANES]


def _inproj(x4, g0, w_in, *, L, ct, sub):
    bq, c = x4.shape[:2]
    nc = c // ct
    R = bq * c
    return pl.pallas_call(
        functools.partial(_inproj_kernel, L=L, ct=ct, sub=sub),
        grid=(bq, nc),
        in_specs=[_nat_spec(ct, L), _const((1, D_MODEL)), _const((D_MODEL, 2 * D_CONV + D_SSM))],
        out_specs=[
            _tp_spec(ct, L, D_CONV, nc),
            pl.BlockSpec((N_LANE_BLOCKS, ct, L * LANES), lambda b, i: (0, b * nc + i, 0)),
        ],
        out_shape=[
            jax.ShapeDtypeStruct((L, R, D_CONV), F32),
            jax.ShapeDtypeStruct((N_LANE_BLOCKS, R, L * LANES), BF16),
        ],
        scratch_shapes=[pltpu.VMEM(w_in.shape, BF16)],
        compiler_params=_params(2),
        name="inproj",
    )(x4, g0, w_in)


def _ln_silu(acc, g, b):
    mu = jnp.mean(acc, axis=-1, keepdims=True)
    xc = acc - mu
    var = jnp.mean(xc * xc, axis=-1, keepdims=True)
    y = xc * lax.rsqrt(var + LN_EPS) * g + b
    return y * jax.nn.sigmoid(y)


def _conv_shifted_copies(v_ref, v1_ref, v2_ref, *, L, C):
    row = lax.broadcasted_iota(jnp.int32, (C, D_CONV), 0)
    for s in range(L):
        x = v_ref[s]
        v1_ref[s] = jnp.where(row >= 1, pltpu.roll(x, 1, 0), 0.0)
        v2_ref[s] = jnp.where(row >= 2, pltpu.roll(x, 2, 0), 0.0)


def _conv_rows(srcs, w_ref, b_ref, g_ref, bb_ref, o_ref, r0, out_r0, *, L, rc):
    sub8 = rc // SUBLANES
    for s in range(L):
        acc = jnp.broadcast_to(b_ref[...][None], (sub8, SUBLANES, D_CONV))
        for d in range(CONV_WIDTH):
            blk = (s - d) % L
            shift = (d - s + L - 1) // L if d > s else 0
            k = CONV_WIDTH - 1 - d
            src = srcs[shift][blk, pl.ds(r0, rc), :].reshape(sub8, SUBLANES, D_CONV)
            acc = acc + w_ref[k][None] * src
        y = _ln_silu(acc.reshape(rc, D_CONV), g_ref[...], bb_ref[...])
        o_ref[s, pl.ds(out_r0, rc), :] = y.astype(o_ref.dtype)


def _conv_sample_kernel(v_ref, st_ref, w_ref, b_ref, g_ref, bb_ref, o_ref, new_ref, *, L, H):
    def ext(i):
        return st_ref[i] if i < H else v_ref[i - H]

    for t in range(L):
        acc = jnp.broadcast_to(b_ref[...], v_ref.shape[1:])
        for k in range(CONV_WIDTH):
            acc = acc + w_ref[k:k + 1, :] * ext(t + k)
        o_ref[t] = _ln_silu(acc, g_ref[...], bb_ref[...]).astype(o_ref.dtype)
    for i in range(H):
        new_ref[i] = ext(i + L)


def _conv_sample(v, state_t, w_dw, b_dw, ln_g, ln_b):
    L, R, _ = v.shape
    H = CONV_WIDTH - 1
    rt = 32
    rows = lambda n: pl.BlockSpec((n, rt, D_CONV), lambda i: (0, i, 0))
    return pl.pallas_call(
        functools.partial(_conv_sample_kernel, L=L, H=H),
        grid=(R // rt,),
        in_specs=[
            rows(L),
            rows(H),
            _const((CONV_WIDTH, D_CONV)),
            _const((1, D_CONV)),
            _const((1, D_CONV)),
            _const((1, D_CONV)),
        ],
        out_specs=[rows(L), rows(H)],
        out_shape=[
            jax.ShapeDtypeStruct((L, R, D_CONV), BF16),
            jax.ShapeDtypeStruct((H, R, D_CONV), F32),
        ],
        compiler_params=_params(1),
        name="conv_sample",
    )(v, state_t, w_dw, b_dw, ln_g, ln_b)


def _expand_block_diag(d):
    tiled = jnp.concatenate([d] * GROUPS_PER_BLOCK, axis=-1)
    r = lax.broadcasted_iota(jnp.int32, tiled.shape, 0) // P_GROUP
    c = lax.broadcasted_iota(jnp.int32, tiled.shape, 1) // N_STATE
    return jnp.where(r == c, tiled, jnp.zeros_like(tiled))


def _ssm_kernel(u_ref, wbc_ref, wcc_ref, a_ref, d_ref, h0_ref, y_ref, hl_ref,
                w_scr, wb_scr, wct_scr, s_scr, hp_scr, tap_scr, *, L, C, bt):
    per_tile = MXU_N // LANES

    @pl.when(pl.program_id(1) == 0)
    def _():
        for s in range(L):
            rows = slice(s * LANES, (s + 1) * LANES)
            for a in range(2):
                cols = slice(a * STATE_HALF, (a + 1) * STATE_HALF)
                wb_scr[rows, cols] = _expand_block_diag(wbc_ref[s, a])
                wct_scr[rows, cols] = _expand_block_diag(wcc_ref[s + 1, a])
        bbar = wb_scr[(L - 1) * LANES:L * LANES, :]
        nt = (((1,), (1,)), ((), ()))
        c0 = jnp.concatenate([_expand_block_diag(wcc_ref[0, a]) for a in range(2)], axis=-1)
        tap_scr[0] = lax.dot_general(bbar, c0, nt, preferred_element_type=F32).astype(BF16)
        for t in range(1, L):
            tap_scr[t] = lax.dot_general(bbar, wct_scr[(t - 1) * LANES:t * LANES, :], nt,
                                         preferred_element_type=F32).astype(BF16)
        for sp in range(L):
            for s in range(L):
                rows = slice(s * LANES, (s + 1) * LANES)
                cols = slice(sp * LANES, (sp + 1) * LANES)
                if s <= sp:
                    w_scr[rows, cols] = tap_scr[sp - s]
                elif s // per_tile == sp // per_tile:
                    w_scr[rows, cols] = jnp.zeros((LANES, LANES), BF16)

    nk = STATE_W // LANES
    half = nk // 2
    rows = bt * C
    inc_all = jnp.dot(u_ref[...], wb_scr[...], preferred_element_type=F32)
    for k in range(half):
        s_scr[k, 0:rows, :] = inc_all[:, k * LANES:(k + 1) * LANES]
        s_scr[k, rows:2 * rows, :] = inc_all[:, (half + k) * LANES:(half + k + 1) * LANES]

    d = d_ref[...]
    for n in range(L // per_tile):
        k = (n + 1) * MXU_N
        y = jnp.dot(u_ref[:, :k], w_scr[:k, n * MXU_N:(n + 1) * MXU_N], preferred_element_type=F32)
        for h in range(per_tile):
            s = n * per_tile + h
            us = u_ref[:, s * LANES:(s + 1) * LANES].astype(F32)
            y_ref[s] = y[:, h * LANES:(h + 1) * LANES] + d * us

    h0 = h0_ref[...]
    a = a_ref[...]

    def piece(x, k):
        return x[:, k * LANES:(k + 1) * LANES]

    def swap_parts(x):
        if 2 * bt == SUBLANES:
            return pltpu.roll(x, bt, 0)
        return jnp.concatenate([x[bt:], x[:bt]], axis=0)

    im_rows = lax.broadcasted_iota(jnp.int32, (2 * bt, LANES), 0) >= bt
    a_same = [jnp.broadcast_to(piece(a[0:1], k), (2 * bt, LANES)) for k in range(half)]
    a_cross = [jnp.where(im_rows, piece(a[1:2], k), -piece(a[1:2], k)) for k in range(half)]
    state = [jnp.concatenate([piece(h0, k), piece(h0, half + k)], axis=0) for k in range(half)]
    for c in range(C):
        sel = pl.ds(c, 2 * bt, stride=C) if C > 1 else pl.ds(0, 2 * bt)
        for k in range(half):
            hp_scr[k, sel, :] = state[k]
            state[k] = a_same[k] * state[k] + a_cross[k] * swap_parts(state[k]) + s_scr[k, sel, :]
    for k in range(half):
        hl_ref[:, k * LANES:(k + 1) * LANES] = state[k][:bt]
        hl_ref[:, (half + k) * LANES:(half + k + 1) * LANES] = state[k][bt:]

    hp = jnp.concatenate([hp_scr[k, 0:rows, :] for k in range(half)]
                         + [hp_scr[k, rows:2 * rows, :] for k in range(half)], axis=-1).astype(BF16)
    for n in range(L // per_tile):
        y = lax.dot_general(hp, wct_scr[n * MXU_N:(n + 1) * MXU_N, :], (((1,), (1,)), ((), ())),
                            preferred_element_type=F32)
        for h in range(per_tile):
            s = n * per_tile + h
            y_ref[s] += y[:, h * LANES:(h + 1) * LANES]


def _ssm(u4, wbc, wcc, a_pow, d_skip, h0, *, L, C, bt):
    R = u4.shape[1]
    rows = bt * C
    nb = R // rows
    lw = L * LANES
    in_specs = [
        pl.BlockSpec((None, rows, lw), lambda j, b: (j, b, 0)),
        pl.BlockSpec((None, L, 2, LANES, N_STATE), lambda j, b: (j, 0, 0, 0, 0)),
        pl.BlockSpec((None, L + 1, 2, LANES, N_STATE), lambda j, b: (j, 0, 0, 0, 0)),
        pl.BlockSpec((None, 2, STATE_HALF), lambda j, b: (j, 0, 0)),
        pl.BlockSpec((None, 1, LANES), lambda j, b: (j, 0, 0)),
        pl.BlockSpec((None, None, bt, STATE_W), lambda j, b: (j, b, 0, 0)),
    ]
    out_specs = [
        pl.BlockSpec((L, rows, LANES), lambda j, b: (0, b, j)),
        pl.BlockSpec((None, None, bt, STATE_W), lambda j, b: (j, b, 0, 0)),
    ]
    out_shape = [
        jax.ShapeDtypeStruct((L, R, D_SSM), F32),
        jax.ShapeDtypeStruct((N_LANE_BLOCKS, nb, bt, STATE_W), F32),
    ]
    scratch = [
        pltpu.VMEM((lw, lw), BF16),
        pltpu.VMEM((lw, STATE_W), BF16),
        pltpu.VMEM((lw, STATE_W), BF16),
        pltpu.VMEM((STATE_HALF // LANES, 2 * rows, LANES), F32),
        pltpu.VMEM((STATE_HALF // LANES, 2 * rows, LANES), F32),
        pltpu.VMEM((L, LANES, LANES), BF16),
    ]
    return pl.pallas_call(
        functools.partial(_ssm_kernel, L=L, C=C, bt=bt),
        grid=(N_LANE_BLOCKS, nb),
        in_specs=in_specs,
        out_specs=out_specs,
        out_shape=out_shape,
        scratch_shapes=scratch,
        compiler_params=_params(2),
        name="ssm",
    )(u4, wbc, wcc, a_pow, d_skip, h0)


def _ssm_weights(lam_re, lam_im, log_dt, b_re, b_im, c_re, c_im, L):
    dt = jnp.exp(log_dt)[:, None]
    zr = lam_re * dt
    zi = lam_im * dt
    n_pow = jnp.arange(L + 1, dtype=F32)[:, None, None]
    mag = jnp.exp(zr[None] * n_pow)
    pr = mag * jnp.cos(zi[None] * n_pow)
    pi = mag * jnp.sin(zi[None] * n_pow)
    a1r, a1i = pr[1], pi[1]
    den = lam_re * lam_re + lam_im * lam_im
    qr = ((a1r - 1.0) * lam_re + a1i * lam_im) / den
    qi = (a1i * lam_re - (a1r - 1.0) * lam_im) / den
    bbr = qr[:, :, None] * b_re - qi[:, :, None] * b_im
    bbi = qr[:, :, None] * b_im + qi[:, :, None] * b_re

    nj, g8 = N_LANE_BLOCKS, GROUPS_PER_BLOCK

    n_rev = (L - 1.0) - jnp.arange(L, dtype=F32)[:, None, None]
    mag_rev = jnp.exp(zr[None] * n_rev)
    rev = mag_rev * jnp.cos(zi[None] * n_rev), mag_rev * jnp.sin(zi[None] * n_rev)
    er = rev[0][:, :, :, None] * bbr[None] - rev[1][:, :, :, None] * bbi[None]
    ei = rev[0][:, :, :, None] * bbi[None] + rev[1][:, :, :, None] * bbr[None]
    e = jnp.stack([er, ei], 1).reshape(L, 2, nj, g8, N_STATE, P_GROUP)
    wbc = jnp.transpose(e, (2, 0, 1, 3, 5, 4)).reshape(nj, L, 2, LANES, N_STATE)

    mr = c_re[None] * pr[:, :, None, :] - c_im[None] * pi[:, :, None, :]
    mi = c_re[None] * pi[:, :, None, :] + c_im[None] * pr[:, :, None, :]
    m = jnp.stack([mr, -mi], 1).reshape(L + 1, 2, nj, g8, P_GROUP, N_STATE)
    wcc = jnp.transpose(m, (2, 0, 1, 3, 4, 5)).reshape(nj, L + 1, 2, LANES, N_STATE)

    def a_pow(n):
        return jnp.stack([pr[n].reshape(nj, STATE_HALF), pi[n].reshape(nj, STATE_HALF)], 1)

    return wbc.astype(BF16), wcc.astype(BF16), a_pow


def _ssm_weights_prefix(full, L_full, L):
    wbc, wcc, a_pow = full
    return wbc[:, L_full - L:], wcc[:, :L + 1], a_pow(L)


def _pack_state(re, im):
    b = re.shape[0]
    r = re.reshape(b, N_LANE_BLOCKS, STATE_HALF)
    i = im.reshape(b, N_LANE_BLOCKS, STATE_HALF)
    return jnp.transpose(jnp.concatenate([r, i], -1), (1, 0, 2))


def _unpack_state(h):
    b = h.shape[1]
    h = jnp.transpose(h, (1, 0, 2))
    re = h[:, :, :STATE_HALF].reshape(b, N_GROUPS, N_STATE)
    im = h[:, :, STATE_HALF:].reshape(b, N_GROUPS, N_STATE)
    return re, im


def _mixout_kernel(*refs, L, ct, sub, conv_c):
    if conv_c:
        (v_ref, cw_ref, cb_ref, lg_ref, lb_ref, y_ref, x_ref, wglu_ref, wout_ref, g_ref, o_ref,
         wglu_scr, wout_scr, v1_scr, v2_scr, c_ref) = refs
    else:
        c_ref, y_ref, x_ref, wglu_ref, wout_ref, g_ref, o_ref, wglu_scr, wout_scr = refs
    _cast_weights_once(2, (wglu_ref, wglu_scr), (wout_ref, wout_scr))
    tiles = _sub_tiles(ct, sub)
    gy = [jax.nn.gelu(y_ref[:, c0:c0 + n, :].reshape(L * n, D_SSM)) for c0, n in tiles]
    gate = [jnp.dot(g.astype(BF16), wglu_scr[...], preferred_element_type=F32) for g in gy]
    if conv_c:
        step = pl.program_id(1)

        @pl.when(step == 0)
        def _():
            _conv_shifted_copies(v_ref, v1_scr, v2_scr, L=L, C=conv_c)

        base = pl.multiple_of(step * ct, ct)
        for c0, n in tiles:
            _conv_rows((v_ref, v1_scr, v2_scr), cw_ref, cb_ref, lg_ref, lb_ref, c_ref, base + c0, c0, L=L, rc=n)
    m = [jnp.dot(c_ref[:, c0:c0 + n, :].reshape(L * n, D_CONV), wout_scr[0:D_CONV, :],
                 preferred_element_type=F32) for c0, n in tiles]
    sg = [g * jax.nn.sigmoid(t) for g, t in zip(gy, gate)]
    m = [a + jnp.dot(s.astype(BF16), wout_scr[D_CONV:, :], preferred_element_type=F32) for a, s in zip(m, sg)]
    for (c0, n), mm in zip(tiles, m):
        x = _gather_rows(x_ref, L, c0, n)
        o_ref[:, c0:c0 + n, :] = (x + _rms(mm, g_ref[...])).reshape(L, n, D_MODEL)


def _mixout(c, y, x4, w_glu, w_out, g1, *, L, ct, sub, conv=None):
    bq, cc = x4.shape[:2]
    nc = cc // ct
    R = bq * cc
    in_specs = [
        _tp_spec(ct, L, D_SSM, nc),
        _nat_spec(ct, L),
        _const((D_SSM, D_SSM)),
        _const((D_CONV + D_SSM, D_MODEL)),
        _const((1, D_MODEL)),
    ]
    args = [y, x4, w_glu, w_out, g1]
    scratch = [pltpu.VMEM(w_glu.shape, BF16), pltpu.VMEM(w_out.shape, BF16)]
    if conv is None:
        in_specs = [_tp_spec(ct, L, D_CONV, nc)] + in_specs
        args = [c] + args
    else:
        v, w_dw, b_dw, ln_g, ln_b = conv
        in_specs = [
            pl.BlockSpec((L, cc, D_CONV), lambda b, i: (0, b, 0)),
            _const((CONV_WIDTH, SUBLANES, D_CONV)),
            _const((SUBLANES, D_CONV)),
            _const((1, D_CONV)),
            _const((1, D_CONV)),
        ] + in_specs
        args = [v, jnp.broadcast_to(w_dw[:, None, :], (CONV_WIDTH, SUBLANES, D_CONV)),
                jnp.broadcast_to(b_dw, (SUBLANES, D_CONV)), ln_g, ln_b] + args
        scratch += [pltpu.VMEM((L, cc, D_CONV), F32), pltpu.VMEM((L, cc, D_CONV), F32),
                    pltpu.VMEM((L, ct, D_CONV), BF16)]
    return pl.pallas_call(
        functools.partial(_mixout_kernel, L=L, ct=ct, sub=sub, conv_c=0 if conv is None else cc),
        grid=(bq, nc),
        in_specs=in_specs,
        out_specs=_tp_spec(ct, L, D_MODEL, nc),
        out_shape=jax.ShapeDtypeStruct((L, R, D_MODEL), F32),
        scratch_shapes=scratch,
        compiler_params=_params(2),
        name="mixout",
    )(*args)


def _memkv_kernel(m_ref, g_ref, wk_ref, wv_ref, k5_ref, v5_ref, kb_ref, vb_ref, wk_scr, wv_scr, *, nb):
    _cast_weights_once(1, (wk_ref, wk_scr), (wv_ref, wv_scr))
    ms = [_rms(m_ref[b], g_ref[...]).astype(BF16) for b in range(nb)]
    for w_ref, o5_ref, ob_ref in ((wk_scr, k5_ref, kb_ref), (wv_scr, v5_ref, vb_ref)):
        ps = [jnp.dot(m, w_ref[...], preferred_element_type=F32) for m in ms]
        for b, p in enumerate(ps):
            ob_ref[b] = p.astype(BF16)
            for hd in range(N_HEADS):
                o5_ref[b, :, hd, :] = p[:, hd * HEAD_DIM:(hd + 1) * HEAD_DIM]


def _memkv(mem, g_mem, w_k, w_v):
    bq = mem.shape[0]
    nb = 4
    out5 = pl.BlockSpec((nb, N_MEM, N_HEADS, HEAD_DIM), lambda i: (i, 0, 0, 0))
    outb = pl.BlockSpec((nb, N_MEM, D_MODEL), lambda i: (i, 0, 0))
    return pl.pallas_call(
        functools.partial(_memkv_kernel, nb=nb),
        grid=(bq // nb,),
        in_specs=[
            pl.BlockSpec((nb, N_MEM, D_MODEL), lambda i: (i, 0, 0)),
            _const((1, D_MODEL)),
            _const((D_MODEL, D_MODEL)),
            _const((D_MODEL, D_MODEL)),
        ],
        out_specs=[out5, out5, outb, outb],
        out_shape=[jax.ShapeDtypeStruct((bq, N_MEM, N_HEADS, HEAD_DIM), F32)] * 2
        + [jax.ShapeDtypeStruct((bq, N_MEM, D_MODEL), BF16)] * 2,
        scratch_shapes=[pltpu.VMEM(w_k.shape, BF16), pltpu.VMEM(w_v.shape, BF16)],
        compiler_params=_params(1),
        name="memkv",
    )(mem, g_mem, w_k, w_v)


def _softmax_rows(s):
    s = s - jnp.max(s, axis=-1, keepdims=True)
    e = jnp.exp(s)
    return e / jnp.sum(e, axis=-1, keepdims=True)


def _attn_kernel(x_ref, k_ref, v_ref, wq32_ref, wo32_ref, gq_ref, go_ref, xs_ref, ck_ref, cv_ref, o_ref, os_ref,
                 wq_ref, wo_ref, *, L, ct, bb):
    _cast_weights_once(1, (wq32_ref, wq_ref), (wo32_ref, wo_ref))
    nt = (((1,), (1,)), ((), ()))
    scale = HEAD_DIM ** -0.5
    heads = [slice(hd * HEAD_DIM, (hd + 1) * HEAD_DIM) for hd in range(N_HEADS)]
    np_rows = L * ct
    x = jnp.concatenate([x_ref[...].reshape(np_rows, D_MODEL), xs_ref[...]], axis=0)
    kb, vb = k_ref[...], v_ref[...]

    rows = xs_ref.shape[0]
    nr = N_HEADS * rows
    nc = N_MEM * N_HEADS
    same_head = (lax.broadcasted_iota(jnp.int32, (nr, nc), 0) // rows
                 == lax.broadcasted_iota(jnp.int32, (nr, nc), 1) % N_HEADS)
    owner = lax.broadcasted_iota(jnp.int32, (nr, HEAD_DIM), 0) % bb

    h = _rms(x, gq_ref[...])
    ka = [ck_ref[b].reshape(nc, HEAD_DIM).astype(BF16) for b in range(bb)]
    q_all = jnp.dot(h.astype(BF16), wq_ref[...], preferred_element_type=F32).astype(BF16)
    q, q2 = q_all[:np_rows], q_all[np_rows:]
    qs = jnp.concatenate([q2[:, sl] for sl in heads], axis=0)
    va = [cv_ref[b].reshape(nc, HEAD_DIM).astype(BF16) for b in range(bb)]
    sc_p = [lax.dot_general(q[:, sl], kb[:, sl], nt, preferred_element_type=F32) for sl in heads]
    sc_s = [lax.dot_general(qs, kk, nt, preferred_element_type=F32) for kk in ka]
    p_p = [_softmax_rows(s * scale).astype(BF16) for s in sc_p]
    p_s = [_softmax_rows(jnp.where(same_head, s * scale, -1e30)).astype(BF16) for s in sc_s]
    o_p = jnp.concatenate([jnp.dot(pp, vb[:, sl], preferred_element_type=F32) for pp, sl in zip(p_p, heads)], axis=-1)
    o_s = [jnp.dot(pp, vv, preferred_element_type=F32) for pp, vv in zip(p_s, va)]
    acc = jnp.zeros((nr, HEAD_DIM), F32)
    for b, ob in enumerate(o_s):
        acc = jnp.where(owner == b, ob, acc)
    o_all = jnp.concatenate(
        [o_p, jnp.concatenate([acc[hd * rows:(hd + 1) * rows] for hd in range(N_HEADS)], axis=-1)], axis=0)
    a = jnp.dot(o_all.astype(BF16), wo_ref[...], preferred_element_type=F32)
    y = x + _rms(a, go_ref[...])
    o_ref[...] = y[:np_rows].reshape(L, ct, D_MODEL)
    os_ref[...] = y[np_rows:]


def _attn(x1, k, v, w_q, w_o, g2, g3, xs_grp, cache_k, cache_v, *, C, ct):
    L, R, _ = x1.shape
    nc = C // ct
    n, rows_s, _ = xs_grp.shape
    bb = cache_k.shape[1] // n
    assert n == R // ct and n * bb == cache_k.shape[1]
    kv_spec = pl.BlockSpec((None, bb, N_MEM, N_HEADS, HEAD_DIM), lambda i: (0, i, 0, 0, 0))
    grp_spec = pl.BlockSpec((None, rows_s, D_MODEL), lambda i: (i, 0, 0))
    return pl.pallas_call(
        functools.partial(_attn_kernel, L=L, ct=ct, bb=bb),
        grid=(n,),
        in_specs=[
            pl.BlockSpec((L, ct, D_MODEL), lambda i: (0, i, 0)),
            pl.BlockSpec((None, N_MEM, D_MODEL), lambda i: (i // nc, 0, 0)),
            pl.BlockSpec((None, N_MEM, D_MODEL), lambda i: (i // nc, 0, 0)),
            _const((D_MODEL, D_MODEL)),
            _const((D_MODEL, D_MODEL)),
            _const((1, D_MODEL)),
            _const((1, D_MODEL)),
            grp_spec, kv_spec, kv_spec,
        ],
        out_specs=[pl.BlockSpec((L, ct, D_MODEL), lambda i: (0, i, 0)), grp_spec],
        out_shape=[jax.ShapeDtypeStruct((L, R, D_MODEL), F32), jax.ShapeDtypeStruct(xs_grp.shape, F32)],
        scratch_shapes=[pltpu.VMEM(w_q.shape, BF16), pltpu.VMEM(w_o.shape, BF16)],
        compiler_params=_params(1),
        name="attn",
    )(x1, k, v, w_q, w_o, g2, g3, xs_grp, cache_k, cache_v)


FFN_WEIGHT_CHUNKS = 16
FFN_STAGE_SLOTS = 4
FFN_ROW_SPLIT = 2


def _stream_cast(w_hbm, w16_scr, stage, sem):
    slots, rpc = stage.shape[0], stage.shape[1]
    n = w_hbm.shape[0] // rpc

    def chunk(k):
        return pltpu.make_async_copy(w_hbm.at[pl.ds(k * rpc, rpc)], stage.at[k % slots], sem.at[k % slots])

    for k in range(min(slots - 1, n)):
        chunk(k).start()
    for k in range(n):
        if k + slots - 1 < n:
            chunk(k + slots - 1).start()
        chunk(k).wait()
        w16_scr[pl.ds(k * rpc, rpc), :] = stage[k % slots].astype(BF16)


def _ffn_tile(x_ref, o_ref, wg_scr, wu_scr, wd_scr, gi_ref, go_ref, *, L, ct):
    rows = L * ct
    x = x_ref[...].reshape(rows, D_MODEL)
    rs = rows // FFN_ROW_SPLIT
    xs = [x[i * rs:(i + 1) * rs] for i in range(FFN_ROW_SPLIT)]
    h = [_rms(xx, gi_ref[...]).astype(BF16) for xx in xs]
    gate = [jnp.dot(hh, wg_scr[...], preferred_element_type=F32) for hh in h]
    up = [jnp.dot(hh, wu_scr[...], preferred_element_type=F32) for hh in h]
    act = [(g * jax.nn.sigmoid(g) * u).astype(BF16) for g, u in zip(gate, up)]
    dn = [jnp.dot(a, wd_scr[...], preferred_element_type=F32) for a in act]
    y = jnp.concatenate([xx + _rms(d, go_ref[...]) for xx, d in zip(xs, dn)], axis=0)
    for s in range(L):
        o_ref[:, s, :] = y[s * ct:(s + 1) * ct]


def _ffn_kernel(xp_ref, xs_ref, wg_hbm, wu_hbm, wd_hbm, gi_ref, go_ref, op_ref, os_ref,
                wg_scr, wu_scr, wd_scr, stage_in, stage_out, sem, *, n_prompt, Lp, ctp, Ls, cts):
    i = pl.program_id(0)

    @pl.when(i == 0)
    def _():
        _stream_cast(wg_hbm, wg_scr, stage_in, sem)
        _stream_cast(wu_hbm, wu_scr, stage_in, sem)
        _stream_cast(wd_hbm, wd_scr, stage_out, sem)

    @pl.when(i < n_prompt)
    def _():
        _ffn_tile(xp_ref, op_ref, wg_scr, wu_scr, wd_scr, gi_ref, go_ref, L=Lp, ct=ctp)

    @pl.when(i == n_prompt)
    def _():
        _ffn_tile(xs_ref, os_ref, wg_scr, wu_scr, wd_scr, gi_ref, go_ref, L=Ls, ct=cts)


def _ffn(xp, xs, w_gate, w_up, w_down, g4, g5, *, bq, C, ct):
    Lp = xp.shape[0]
    Ls, rs, _ = xs.shape
    nc = C // ct
    n = bq * nc
    tile = lambda i: jnp.minimum(i, n - 1)
    hbm = pl.BlockSpec(memory_space=pl.ANY)
    return pl.pallas_call(
        functools.partial(_ffn_kernel, n_prompt=n, Lp=Lp, ctp=ct, Ls=Ls, cts=rs),
        grid=(n + 1,),
        in_specs=[
            pl.BlockSpec((Lp, ct, D_MODEL), lambda i: (0, tile(i), 0)),
            _const((Ls, rs, D_MODEL)),
            hbm, hbm, hbm,
            _const((1, D_MODEL)),
            _const((1, D_MODEL)),
        ],
        out_specs=[
            pl.BlockSpec((None, ct, Lp, D_MODEL), lambda i: (tile(i) // nc, tile(i) % nc, 0, 0)),
            pl.BlockSpec((None, rs, Ls, D_MODEL), lambda i: (0, 0, 0, 0)),
        ],
        out_shape=[
            jax.ShapeDtypeStruct((bq, C, Lp, D_MODEL), F32),
            jax.ShapeDtypeStruct((1, rs, Ls, D_MODEL), F32),
        ],
        scratch_shapes=[
            pltpu.VMEM((D_MODEL, D_FF), BF16),
            pltpu.VMEM((D_MODEL, D_FF), BF16),
            pltpu.VMEM((D_FF, D_MODEL), BF16),
            pltpu.VMEM((FFN_STAGE_SLOTS, D_MODEL // FFN_WEIGHT_CHUNKS, D_FF), F32),
            pltpu.VMEM((FFN_STAGE_SLOTS, D_FF // FFN_WEIGHT_CHUNKS, D_MODEL), F32),
            pltpu.SemaphoreType.DMA((FFN_STAGE_SLOTS,)),
        ],
        compiler_params=_params(1),
        name="ffn",
    )(xp, xs, w_gate, w_up, w_down, g4, g5)


def _mixer(x4, conv_state_t, h0_packed, wts, *, ct, sub, sc, bt, sample):
    bq, C, L, _ = x4.shape
    R = bq * C
    g = [wts["norm_g"][i:i + 1] for i in range(2)]

    v, u4 = _inproj(x4, g[0], wts["w_in"], L=L, ct=ct, sub=sub)
    sw = wts["ssm_L%d" % L]
    nb = R // (bt * sc)
    ssm_args = (u4, sw[0], sw[1], sw[2], wts["d_skip"], h0_packed.reshape(N_LANE_BLOCKS, nb, bt, STATE_W))
    conv_args = (wts["w_dw"], wts["b_dw"], wts["ln_g"], wts["ln_b"])
    y, hl = _ssm(*ssm_args, L=L, C=sc, bt=bt)
    hl = hl.reshape(N_LANE_BLOCKS, nb * bt, STATE_W)
    out_args = (y, x4, wts["w_glu"], wts["w_out"], g[1])
    if sample:
        cact, conv_new = _conv_sample(v, conv_state_t, *conv_args)
        x1 = _mixout(cact, *out_args, L=L, ct=ct, sub=sub)
    else:
        x1 = _mixout(None, *out_args, L=L, ct=ct, sub=sub, conv=(v,) + conv_args)
        tail = v.reshape(L, bq, C, D_CONV)[:, :, C - 2:, :]
        tail = jnp.transpose(tail, (1, 2, 0, 3)).reshape(bq, 2 * L, D_CONV)
        conv_new = tail[:, 2 * L - (CONV_WIDTH - 1):, :]
    return x1, conv_new, hl


PROMPT_L = 16
PROMPT_SUB = 32
PROMPT_CT = 64


def kernel(x_prompt, x_sample, mem_prompt, cache_mem_k, cache_mem_v, state_conv, state_ssm_re, state_ssm_im,
           norm_g, mem_norm_g, w_in, w_dw, b_dw, ln_g, ln_b, lam_re, lam_im, log_dt, b_re, b_im, c_re, c_im,
           d_skip, w_glu, w_out, w_q, w_k, w_v, w_o, w_gate, w_up, w_down):
    depth = w_in.shape[0]
    bp, tp, _ = x_prompt.shape
    bs, ts, _ = x_sample.shape
    assert tp % (PROMPT_L * PROMPT_CT) == 0 and tp >= CONV_WIDTH - 1

    yp = x_prompt.reshape(bp, tp // PROMPT_L, PROMPT_L, D_MODEL)
    ys = x_sample.reshape(1, bs, ts, D_MODEL)
    outs = [[] for _ in range(8)]
    for l in range(depth):
        wts = {
            "norm_g": norm_g[l],
            "w_in": w_in[l],
            "w_dw": w_dw[l],
            "b_dw": b_dw[l][None],
            "ln_g": ln_g[l][None],
            "ln_b": ln_b[l][None],
            "d_skip": d_skip[l].reshape(N_LANE_BLOCKS, 1, LANES),
            "w_glu": w_glu[l],
            "w_out": w_out[l],
        }
        ssm_args = (lam_re[l], lam_im[l], log_dt[l], b_re[l], b_im[l], c_re[l], c_im[l])
        assert ts <= PROMPT_L
        full = _ssm_weights(*ssm_args, PROMPT_L)
        wts["ssm_L%d" % PROMPT_L] = full[:2] + (full[2](PROMPT_L),)
        wts["ssm_L%d" % ts] = _ssm_weights_prefix(full, PROMPT_L, ts)

        kp, vp, kp16, vp16 = _memkv(mem_prompt, mem_norm_g[l][None], w_k[l], w_v[l])
        h0p = jnp.zeros((N_LANE_BLOCKS, bp, STATE_W), F32)
        cp_chunks = tp // PROMPT_L
        xp, cp, hp = _mixer(yp, None, h0p, wts, ct=PROMPT_CT, sub=PROMPT_SUB, sc=cp_chunks, bt=4, sample=False)
        h0s = _pack_state(state_ssm_re[l], state_ssm_im[l])
        xs, cs, hs = _mixer(ys, jnp.transpose(state_conv[l], (1, 0, 2)), h0s, wts, ct=bs, sub=bs, sc=1, bt=bs,
                            sample=True)
        cs = jnp.transpose(cs, (1, 0, 2))

        g2, g3 = norm_g[l][2:3], norm_g[l][3:4]
        n_tiles = bp * cp_chunks // PROMPT_SUB
        bb = bs // n_tiles
        xs = jnp.transpose(xs.reshape(ts, n_tiles, bb, D_MODEL), (1, 0, 2, 3)).reshape(n_tiles, ts * bb, D_MODEL)
        xp, xs = _attn(xp, kp16, vp16, w_q[l], w_o[l], g2, g3, xs, cache_mem_k[l:l + 1], cache_mem_v[l:l + 1],
                       C=cp_chunks, ct=PROMPT_SUB)
        xs = jnp.transpose(xs.reshape(n_tiles, ts, bb, D_MODEL), (1, 0, 2, 3)).reshape(ts, bs, D_MODEL)

        yp, ys = _ffn(xp, xs, w_gate[l], w_up[l], w_down[l], norm_g[l][4:5], norm_g[l][5:6],
                      bq=bp, C=cp_chunks, ct=PROMPT_SUB)

        hp_re, hp_im = _unpack_state(hp)
        hs_re, hs_im = _unpack_state(hs)
        for lst, val in zip(outs, (kp, vp, cp, hp_re, hp_im, cs, hs_re, hs_im)):
            lst.append(val)
    return (yp.reshape(bp, tp, D_MODEL), ys.reshape(bs, ts, D_MODEL)) + tuple(jnp.stack(o) for o in outs)
```
